```python
import jax, jax.numpy as jnp
from jax import lax
import numpy as np

D_MODEL = 2048
BATCH = 8
SEQ = 2048
DEPTH = 2

HEAD_DIM = 128
Q_BLOCK = 128
NEG = -1e30
FORCE = 1e9
EPS = 1e-6

MOBA_HEADS = 8
MOBA_BLOCK = 256
MOBA_TOPK = 3
MOBA_WIDTH = MOBA_HEADS * HEAD_DIM

NSA_HEADS = 8
NSA_KV_HEADS = 2
NSA_GROUP = NSA_HEADS // NSA_KV_HEADS
NSA_CMP_LEN = 32
NSA_CMP_STRIDE = 16
NSA_SLC_BLOCK = 64
NSA_TOPN = 8
NSA_WINDOW = 512
NSA_WIDTH = NSA_HEADS * HEAD_DIM
NSA_KV_WIDTH = NSA_KV_HEADS * HEAD_DIM

GLA_HEADS = 4
GLA_DK = 128
GLA_DV = 256
GLA_RANK = 16
GLA_TAU = 16.0
GLA_CHUNK = 64
GLA_KEY_WIDTH = GLA_HEADS * GLA_DK
GLA_WIDTH = GLA_HEADS * GLA_DV

N_BRANCH = 3
IN_SPLITS = (MOBA_WIDTH, MOBA_WIDTH, MOBA_WIDTH,
             NSA_WIDTH, 6 * NSA_KV_WIDTH, 3 * NSA_HEADS,
             GLA_KEY_WIDTH, GLA_KEY_WIDTH, GLA_WIDTH,
             GLA_RANK, GLA_WIDTH,
             N_BRANCH * D_MODEL)
IN_WIDTH = sum(IN_SPLITS)

MOE_GROUPS = 4
MOE_EXPERTS_PER_GROUP = 8
MOE_EXPERTS = MOE_GROUPS * MOE_EXPERTS_PER_GROUP
MOE_TOPK = 2
MOE_FF = D_MODEL // 4
MOE_ROW_BLOCK = 128

kernel_name = 'hybrid_moba_nsa_gla_hmoe_block'


def rmsnorm(x, g):
    xf = x.astype(jnp.float32)
    y = xf * lax.rsqrt(jnp.mean(xf * xf, axis=-1, keepdims=True) + EPS)
    return (y * g.astype(jnp.float32)).astype(x.dtype)


def moba_attention(q, k, v):
    B, S = q.shape[0], q.shape[1]
    Sp = -(-S // MOBA_BLOCK) * MOBA_BLOCK
    pad = ((0, 0), (0, Sp - S), (0, 0), (0, 0))
    q, k, v = jnp.pad(q, pad), jnp.pad(k, pad), jnp.pad(v, pad)
    nb = Sp // MOBA_BLOCK
    topk = min(MOBA_TOPK, nb)
    scale = HEAD_DIM ** -0.5
    heads = jnp.arange(MOBA_HEADS)[:, None, None]

    def per_example(args):
        qe, ke, ve = args
        qh = qe.transpose(1, 0, 2)
        kb = ke.transpose(1, 0, 2).reshape(MOBA_HEADS, nb, MOBA_BLOCK, HEAD_DIM)
        vb = ve.transpose(1, 0, 2).reshape(MOBA_HEADS, nb, MOBA_BLOCK, HEAD_DIM)
        kmean = jnp.mean(kb.astype(jnp.float32), axis=2)

        def per_qblock(i):
            start = i * Q_BLOCK
            qb = lax.dynamic_slice_in_dim(qh, start, Q_BLOCK, axis=1)
            pos = start + jnp.arange(Q_BLOCK)
            cur = start // MOBA_BLOCK
            gate = jnp.einsum('hqd,hnd->hqn', qb.astype(jnp.float32), kmean)
            gate = jnp.where(jnp.arange(nb) < cur, gate, NEG)
            _, sel = lax.top_k(gate, topk)
            k_sel = kb[heads, sel]
            v_sel = vb[heads, sel]
            k_own = lax.dynamic_index_in_dim(kb, cur, axis=1, keepdims=False)
            v_own = lax.dynamic_index_in_dim(vb, cur, axis=1, keepdims=False)
            s_sel = jnp.einsum('hqd,hqnkd->hqnk', qb, k_sel, preferred_element_type=jnp.float32) * scale
            s_sel = jnp.where((sel < cur)[..., None], s_sel, NEG)
            s_own = jnp.einsum('hqd,hkd->hqk', qb, k_own, preferred_element_type=jnp.float32) * scale
            own_pos = cur * MOBA_BLOCK + jnp.arange(MOBA_BLOCK)
            s_own = jnp.where(own_pos[None, None, :] <= pos[None, :, None], s_own, NEG)
            s = jnp.concatenate([s_sel.reshape(MOBA_HEADS, Q_BLOCK, topk * MOBA_BLOCK), s_own], axis=-1)
            p = jax.nn.softmax(s, axis=-1).astype(ve.dtype)
            p_sel = p[..., :topk * MOBA_BLOCK].reshape(MOBA_HEADS, Q_BLOCK, topk, MOBA_BLOCK)
            p_own = p[..., topk * MOBA_BLOCK:]
            return (jnp.einsum('hqnk,hqnkd->qhd', p_sel, v_sel)
                    + jnp.einsum('hqk,hkd->qhd', p_own, v_own))

        out = lax.map(per_qblock, jnp.arange(Sp // Q_BLOCK))
        return out.reshape(Sp, MOBA_HEADS, HEAD_DIM)

    out = lax.map(per_example, (q, k, v))
    return out[:, :S]


def nsa_attention(q, kv, gates, cmp_pe, cmp_w1, cmp_w2):
    B, S = q.shape[0], q.shape[1]
    G, R, dh = NSA_KV_HEADS, NSA_GROUP, HEAD_DIM
    scale = dh ** -0.5
    ncmp = (S - NSA_CMP_LEN) // NSA_CMP_STRIDE + 1
    win_idx = jnp.arange(ncmp)[:, None] * NSA_CMP_STRIDE + jnp.arange(NSA_CMP_LEN)[None, :]
    blocks = kv[:, :, 0:2][:, win_idx]
    blocks = blocks + cmp_pe.transpose(1, 0, 2)[None, None, :, :, None, :]
    flat = blocks.transpose(0, 1, 3, 4, 2, 5).reshape(B, ncmp, 2, G, NSA_CMP_LEN * dh)
    hid = jax.nn.silu(jnp.einsum('bncgi,cio->bncgo', flat, cmp_w1))
    kv_c = jnp.einsum('bncgi,cio->bncgo', hid, cmp_w2)
    k_c, v_c = kv_c[:, :, 0], kv_c[:, :, 1]
    cmp_start = jnp.arange(ncmp) * NSA_CMP_STRIDE
    cmp_end = cmp_start + NSA_CMP_LEN - 1
    nslc = S // NSA_SLC_BLOCK
    slc_start = jnp.arange(nslc) * NSA_SLC_BLOCK
    overlap = ((cmp_start[:, None] < slc_start[None, :] + NSA_SLC_BLOCK)
               & (cmp_end[:, None] >= slc_start[None, :])).astype(jnp.float32)
    topn = min(NSA_TOPN, nslc)
    groups = jnp.arange(G)[:, None, None]

    def per_example(args):
        qe, kve, ge, kce, vce = args
        qg = qe.reshape(S, G, R, dh).transpose(1, 2, 0, 3)
        pos = jnp.arange(S)
        s_c = jnp.einsum('grqd,ngd->grqn', qg, kce, preferred_element_type=jnp.float32) * scale
        cmask = cmp_end[None, :] <= pos[:, None]
        p_c = jax.nn.softmax(jnp.where(cmask, s_c, NEG), axis=-1) * cmask
        o_c = jnp.einsum('grqn,ngd->qgrd', p_c.astype(vce.dtype), vce)
        imp = jnp.einsum('gqn,nj->gqj', p_c.sum(axis=1), overlap)
        cur_blk = pos // NSA_SLC_BLOCK
        j = jnp.arange(nslc)[None, :]
        forced = (j == 0) | (j == cur_blk[:, None]) | (j == cur_blk[:, None] - 1)
        imp = jnp.where(forced, FORCE, imp)
        imp = jnp.where(j > cur_blk[:, None], NEG, imp)
        _, sel = lax.top_k(imp, topn)
        ksb = kve[:, 2].transpose(1, 0, 2).reshape(G, nslc, NSA_SLC_BLOCK, dh)
        vsb = kve[:, 3].transpose(1, 0, 2).reshape(G, nslc, NSA_SLC_BLOCK, dh)
        kw = jnp.pad(kve[:, 4], ((NSA_WINDOW, 0), (0, 0), (0, 0)))
        vw = jnp.pad(kve[:, 5], ((NSA_WINDOW, 0), (0, 0), (0, 0)))

        def per_qblock(i):
            start = i * Q_BLOCK
            qb = lax.dynamic_slice_in_dim(qg, start, Q_BLOCK, axis=2)
            qpos = start + jnp.arange(Q_BLOCK)
            selb = lax.dynamic_slice_in_dim(sel, start, Q_BLOCK, axis=1)
            k_sel = ksb[groups, selb]
            v_sel = vsb[groups, selb]
            s_s = jnp.einsum('grqd,gqnkd->grqnk', qb, k_sel, preferred_element_type=jnp.float32) * scale
            key_pos = selb[..., None] * NSA_SLC_BLOCK + jnp.arange(NSA_SLC_BLOCK)
            s_s = jnp.where((key_pos <= qpos[None, :, None, None])[:, None], s_s, NEG)
            p_s = jax.nn.softmax(s_s.reshape(G, R, Q_BLOCK, topn * NSA_SLC_BLOCK), axis=-1)
            p_s = p_s.reshape(G, R, Q_BLOCK, topn, NSA_SLC_BLOCK).astype(v_sel.dtype)
            o_s = jnp.einsum('grqnk,gqnkd->qgrd', p_s, v_sel)
            k_w = lax.dynamic_slice_in_dim(kw, start, NSA_WINDOW + Q_BLOCK, axis=0)
            v_w = lax.dynamic_slice_in_dim(vw, start, NSA_WINDOW + Q_BLOCK, axis=0)
            wpos = start - NSA_WINDOW + jnp.arange(NSA_WINDOW + Q_BLOCK)
            wmask = ((wpos[None, :] <= qpos[:, None]) & (wpos[None, :] > qpos[:, None] - NSA_WINDOW)
                     & (wpos[None, :] >= 0))
            s_w = jnp.einsum('grqd,kgd->grqk', qb, k_w, preferred_element_type=jnp.float32) * scale
            p_w = jax.nn.softmax(jnp.where(wmask, s_w, NEG), axis=-1).astype(v_w.dtype)
            o_w = jnp.einsum('grqk,kgd->qgrd', p_w, v_w)
            return o_s, o_w

        o_s, o_w = lax.map(per_qblock, jnp.arange(S // Q_BLOCK))
        o_s = o_s.reshape(S, G, R, dh)
        o_w = o_w.reshape(S, G, R, dh)
        gg = ge.reshape(S, G, R, 3)
        o = gg[..., 0:1] * o_c + gg[..., 1:2] * o_s + gg[..., 2:3] * o_w
        return o.reshape(S, NSA_HEADS, dh)

    return lax.map(per_example, (q, kv, gates, k_c, v_c))


def gla_attention(q, k, v, log_a):
    B, S, H = q.shape[0], q.shape[1], q.shape[2]
    nc = S // GLA_CHUNK

    def chunks(t):
        return t.reshape(B, nc, GLA_CHUNK, H, t.shape[-1]).transpose(1, 0, 3, 2, 4)

    qc, kc, vc, ac = chunks(q * (GLA_DK ** -0.5)), chunks(k), chunks(v), chunks(log_a)
    causal = jnp.tril(jnp.ones((GLA_CHUNK, GLA_CHUNK), dtype=bool))

    def step(state, inp):
        qi, ki, vi, ai = inp
        qf, kf, vf = qi.astype(jnp.float32), ki.astype(jnp.float32), vi.astype(jnp.float32)
        b = jnp.cumsum(ai.astype(jnp.float32), axis=2)
        diff = b[:, :, :, None, :] - b[:, :, None, :, :]
        decay = jnp.exp(jnp.where(causal[None, None, :, :, None], diff, -jnp.inf))
        attn = jnp.einsum('bhtd,bhsd,bhtsd->bhts', qf, kf, decay)
        o = (jnp.einsum('bhts,bhse->bhte', attn, vf)
             + jnp.einsum('bhtd,bhde->bhte', qf * jnp.exp(b), state))
        b_last = b[:, :, -1:, :]
        state = (jnp.exp(b_last[:, :, 0, :, None]) * state
                 + jnp.einsum('bhsd,bhse->bhde', kf * jnp.exp(b_last - b), vf))
        return state, o

    state0 = jnp.zeros((B, H, GLA_DK, GLA_DV), jnp.float32)
    _, o = lax.scan(step, state0, (qc, kc, vc, ac))
    return o.transpose(1, 0, 3, 2, 4).reshape(B, S, H, GLA_DV).astype(v.dtype)


def hier_moe(h, rg_w, rg_b, re_w, re_b, w_gate, w_up, w_down):
    B, S, D = h.shape
    T = B * S
    ht = h.reshape(T, D)
    g_logits = (ht @ rg_w).astype(jnp.float32) + rg_b.astype(jnp.float32)
    grp = jnp.argmax(g_logits, axis=-1)
    p_grp = jnp.take_along_axis(jax.nn.softmax(g_logits, axis=-1), grp[:, None], axis=1)[:, 0]
    e_logits = ((ht @ re_w).astype(jnp.float32) + re_b.astype(jnp.float32)).reshape(T, MOE_GROUPS, MOE_EXPERTS_PER_GROUP)
    e_in = jnp.take_along_axis(e_logits, grp[:, None, None], axis=1)[:, 0]
    top_v, top_i = lax.top_k(e_in, MOE_TOPK)
    w = jax.nn.softmax(top_v, axis=-1) * p_grp[:, None]
    expert = grp[:, None] * MOE_EXPERTS_PER_GROUP + top_i
    A = T * MOE_TOPK
    e_flat = expert.reshape(A)
    w_flat = w.reshape(A)
    tok = jnp.arange(A) // MOE_TOPK
    order = jnp.argsort(e_flat)
    e_sorted = e_flat[order]
    counts = jax.ops.segment_sum(jnp.ones((A,), jnp.int32), e_flat, num_segments=MOE_EXPERTS)
    starts = jnp.cumsum(counts) - counts
    padded = (counts + MOE_ROW_BLOCK - 1) // MOE_ROW_BLOCK * MOE_ROW_BLOCK
    pends = jnp.cumsum(padded)
    pstarts = pends - padded
    dest = pstarts[e_sorted] + (jnp.arange(A) - starts[e_sorted])
    P = (A + MOE_EXPERTS * (MOE_ROW_BLOCK - 1) + MOE_ROW_BLOCK - 1) // MOE_ROW_BLOCK * MOE_ROW_BLOCK
    nblk = P // MOE_ROW_BLOCK
    row_tok = jnp.zeros((P,), jnp.int32).at[dest].set(tok[order])
    row_w = jnp.zeros((P,), jnp.float32).at[dest].set(w_flat[order])
    blk_e = jnp.clip(jnp.searchsorted(pends, jnp.arange(nblk) * MOE_ROW_BLOCK, side='right'), 0, MOE_EXPERTS - 1)

    def expert_block(args):
        toks, e = args
        xb = ht[toks]
        return (jax.nn.silu(xb @ w_gate[e]) * (xb @ w_up[e])) @ w_down[e]

    y_rows = lax.map(expert_block, (row_tok.reshape(nblk, MOE_ROW_BLOCK), blk_e)).reshape(P, D)
    y_rows = y_rows * row_w[:, None].astype(y_rows.dtype)
    y = jax.ops.segment_sum(y_rows, row_tok, num_segments=T)
    return y.reshape(B, S, D)


def hybrid_layer(x, norm1_g, w_in, nsa_cmp_pe, nsa_cmp_w1, nsa_cmp_w2, gla_wa, gla_ba, gla_norm_g,
                 w_br_moba, w_br_nsa, w_br_gla, w_out, norm2_g, router_group_w, router_group_b,
                 router_expert_w, router_expert_b, expert_w_gate, expert_w_up, expert_w_down):
    B, S = x.shape[0], x.shape[1]
    h = rmsnorm(x, norm1_g)
    proj = h @ w_in
    offsets = [int(o) for o in np.cumsum(IN_SPLITS)[:-1]]
    (mq, mk, mv, nq, nkv, ng, gq, gk, gv, ga, gg, mg) = jnp.split(proj, offsets, axis=-1)
    o_m = moba_attention(mq.reshape(B, S, MOBA_HEADS, HEAD_DIM), mk.reshape(B, S, MOBA_HEADS, HEAD_DIM),
                         mv.reshape(B, S, MOBA_HEADS, HEAD_DIM)).reshape(B, S, MOBA_WIDTH)
    o_n = nsa_attention(nq.reshape(B, S, NSA_HEADS, HEAD_DIM), nkv.reshape(B, S, 6, NSA_KV_HEADS, HEAD_DIM),
                        jax.nn.sigmoid(ng.reshape(B, S, NSA_HEADS, 3)),
                        nsa_cmp_pe, nsa_cmp_w1, nsa_cmp_w2).reshape(B, S, NSA_WIDTH)
    log_a = jax.nn.log_sigmoid((ga @ gla_wa + gla_ba).astype(jnp.float32)) / GLA_TAU
    o_g = gla_attention(gq.reshape(B, S, GLA_HEADS, GLA_DK), gk.reshape(B, S, GLA_HEADS, GLA_DK),
                        gv.reshape(B, S, GLA_HEADS, GLA_DV), log_a.reshape(B, S, GLA_HEADS, GLA_DK))
    o_g = (rmsnorm(o_g, gla_norm_g) * jax.nn.silu(gg.reshape(B, S, GLA_HEADS, GLA_DV))).reshape(B, S, GLA_WIDTH)
    gates = jax.nn.sigmoid(mg.reshape(B, S, N_BRANCH, D_MODEL))
    merged = (gates[:, :, 0] * (o_m @ w_br_moba) + gates[:, :, 1] * (o_n @ w_br_nsa)
              + gates[:, :, 2] * (o_g @ w_br_gla))
    x = x + merged @ w_out
    x = x + hier_moe(rmsnorm(x, norm2_g), router_group_w, router_group_b, router_expert_w, router_expert_b,
                     expert_w_gate, expert_w_up, expert_w_down)
    return x


def setup_inputs(seed: int = 0) -> dict:
    key = jax.random.key(seed)
    ks = jax.random.split(key, 24)
    L = DEPTH

    def nrm(k, shape, scale):
        return jax.random.normal(k, shape, jnp.float32) * scale

    return {
        'x': nrm(ks[0], (BATCH, SEQ, D_MODEL), 1.0),
        'norm1_g': 1.0 + nrm(ks[1], (L, D_MODEL), 0.02),
        'w_in': nrm(ks[2], (L, D_MODEL, IN_WIDTH), D_MODEL ** -0.5),
        'nsa_cmp_pe': nrm(ks[3], (L, 2, NSA_CMP_LEN, HEAD_DIM), 0.1),
        'nsa_cmp_w1': nrm(ks[4], (L, 2, NSA_CMP_LEN * HEAD_DIM, HEAD_DIM), (NSA_CMP_LEN * HEAD_DIM) ** -0.5),
        'nsa_cmp_w2': nrm(ks[5], (L, 2, HEAD_DIM, HEAD_DIM), HEAD_DIM ** -0.5),
        'gla_wa': nrm(ks[6], (L, GLA_RANK, GLA_KEY_WIDTH), GLA_RANK ** -0.5),
        'gla_ba': nrm(ks[7], (L, GLA_KEY_WIDTH), 0.02),
        'gla_norm_g': 1.0 + nrm(ks[8], (L, GLA_DV), 0.02),
        'w_br_moba': nrm(ks[9], (L, MOBA_WIDTH, D_MODEL), MOBA_WIDTH ** -0.5),
        'w_br_nsa': nrm(ks[10], (L, NSA_WIDTH, D_MODEL), NSA_WIDTH ** -0.5),
        'w_br_gla': nrm(ks[11], (L, GLA_WIDTH, D_MODEL), GLA_WIDTH ** -0.5),
        'w_out': nrm(ks[12], (L, D_MODEL, D_MODEL), D_MODEL ** -0.5),
        'norm2_g': 1.0 + nrm(ks[13], (L, D_MODEL), 0.02),
        'router_group_w': nrm(ks[14], (L, D_MODEL, MOE_GROUPS), D_MODEL ** -0.5),
        'router_group_b': nrm(ks[15], (L, MOE_GROUPS), 0.01),
        'router_expert_w': nrm(ks[16], (L, D_MODEL, MOE_EXPERTS), D_MODEL ** -0.5),
        'router_expert_b': nrm(ks[17], (L, MOE_EXPERTS), 0.01),
        'expert_w_gate': nrm(ks[18], (L, MOE_EXPERTS, D_MODEL, MOE_FF), D_MODEL ** -0.5),
        'expert_w_up': nrm(ks[19], (L, MOE_EXPERTS, D_MODEL, MOE_FF), D_MODEL ** -0.5),
        'expert_w_down': nrm(ks[20], (L, MOE_EXPERTS, MOE_FF, D_MODEL), MOE_FF ** -0.5),
        'final_norm_g': 1.0 + nrm(ks[21], (D_MODEL,), 0.02),
    }


def reference(x, norm1_g, w_in, nsa_cmp_pe, nsa_cmp_w1, nsa_cmp_w2, gla_wa, gla_ba, gla_norm_g,
              w_br_moba, w_br_nsa, w_br_gla, w_out, norm2_g, router_group_w, router_group_b,
              router_expert_w, router_expert_b, expert_w_gate, expert_w_up, expert_w_down, final_norm_g):
    for l in range(DEPTH):
        x = hybrid_layer(x, norm1_g[l], w_in[l], nsa_cmp_pe[l], nsa_cmp_w1[l], nsa_cmp_w2[l],
                         gla_wa[l], gla_ba[l], gla_norm_g[l], w_br_moba[l], w_br_nsa[l], w_br_gla[l],
                         w_out[l], norm2_g[l], router_group_w[l], router_group_b[l],
                         router_expert_w[l], router_expert_b[l], expert_w_gate[l], expert_w_up[l],
                         expert_w_down[l])
    return rmsnorm(x, final_norm_g)
```

```python
import functools

import jax
import jax.numpy as jnp
import numpy as np
from jax import lax
from jax.experimental import pallas as pl
from jax.experimental.pallas import tpu as pltpu

F32 = jnp.float32
BF16 = jnp.bfloat16
I32 = jnp.int32

D_MODEL = 2048
DEPTH = 2
HEAD_DIM = 128
NEG = -1e30
FORCE = 1e9
EPS = 1e-6

MOBA_HEADS = 8
MOBA_BLOCK = 256
MOBA_TOPK = 3
MOBA_WIDTH = MOBA_HEADS * HEAD_DIM

NSA_HEADS = 8
NSA_KV_HEADS = 2
NSA_GROUP = NSA_HEADS // NSA_KV_HEADS
NSA_CMP_LEN = 32
NSA_CMP_STRIDE = 16
NSA_SLC_BLOCK = 64
NSA_TOPN = 8
NSA_WINDOW = 512
NSA_WIDTH = NSA_HEADS * HEAD_DIM
NSA_KV_WIDTH = NSA_KV_HEADS * HEAD_DIM

GLA_HEADS = 4
GLA_DK = 128
GLA_DV = 256
GLA_RANK = 16
GLA_TAU = 16.0
GLA_KEY_WIDTH = GLA_HEADS * GLA_DK
GLA_WIDTH = GLA_HEADS * GLA_DV

N_BRANCH = 3
MOE_GROUPS = 4
MOE_EXPERTS_PER_GROUP = 8
MOE_EXPERTS = MOE_GROUPS * MOE_EXPERTS_PER_GROUP
MOE_TOPK = 2
MOE_FF = D_MODEL // 4

LANES = 128
VMEM_LIMIT_BYTES = 56 * 1024 * 1024

_SRC_NG = MOBA_WIDTH * 3 + NSA_WIDTH + 6 * NSA_KV_WIDTH
_SRC_GQ = _SRC_NG + 3 * NSA_HEADS
_SRC_GA = _SRC_GQ + 2 * GLA_KEY_WIDTH + GLA_WIDTH
_SRC_GG = _SRC_GA + GLA_RANK
_SRC_END = _SRC_GG + GLA_WIDTH + N_BRANCH * D_MODEL

C_MQ = 0
C_MK = C_MQ + MOBA_WIDTH
C_MV = C_MK + MOBA_WIDTH
C_NQ = C_MV + MOBA_WIDTH
C_NKV = C_NQ + NSA_WIDTH
C_GQ = C_NKV + 6 * NSA_KV_WIDTH
C_GK = C_GQ + GLA_KEY_WIDTH
C_GV = C_GK + GLA_KEY_WIDTH
C_GG = C_GV + GLA_WIDTH
C_MG = C_GG + GLA_WIDTH
C_SMALL = C_MG + N_BRANCH * D_MODEL
PROJ_WIDTH = C_SMALL + LANES
SMALL_GA_LANE = 0
SMALL_NG_LANE = GLA_RANK

ROUTER_LANES = LANES
MOE_ROWS = 256


def _params(*semantics):
    return pltpu.CompilerParams(dimension_semantics=semantics, vmem_limit_bytes=VMEM_LIMIT_BYTES)


def _dot(a, b):
    return jnp.dot(a, b, preferred_element_type=F32)


def _dot_nt(a, b):
    return lax.dot_general(a, b, (((1,), (1,)), ((), ())), preferred_element_type=F32)


def _dot_split3(a01, x):
    x1 = x.astype(BF16)
    r1 = x - x1.astype(F32)
    x2 = r1.astype(BF16)
    x3 = (r1 - x2.astype(F32)).astype(BF16)
    return _dot(a01, x1) + _dot(a01, x2) + _dot(a01, x3)


def _rms(x, g):
    return x * lax.rsqrt(jnp.mean(x * x, axis=-1, keepdims=True) + EPS) * g


def _rmsnorm_kernel(x_ref, g_ref, o_ref):
    o_ref[...] = _rms(x_ref[...].astype(F32), g_ref[...]).astype(o_ref.dtype)


def rmsnorm(x2d, g, out_dtype, tm=512):
    m, d = x2d.shape
    tm = min(tm, m)
    return pl.pallas_call(
        _rmsnorm_kernel,
        grid=(m // tm,),
        in_specs=[pl.BlockSpec((tm, d), lambda i: (i, 0)), pl.BlockSpec((1, d), lambda i: (0, 0))],
        out_specs=pl.BlockSpec((tm, d), lambda i: (i, 0)),
        out_shape=jax.ShapeDtypeStruct((m, d), out_dtype),
        compiler_params=_params("parallel"),
        name="rmsnorm",
    )(x2d, g.reshape(1, d).astype(F32))


def _matmul_kernel(a_ref, b_ref, o_ref):
    o_ref[...] = _dot(a_ref[...], b_ref[...]).astype(o_ref.dtype)


def matmul(a, b, out_dtype, tm=1024, tn=1152):
    m, k = a.shape
    n = b.shape[1]
    tm, tn = min(tm, m), min(tn, n)
    return pl.pallas_call(
        _matmul_kernel,
        grid=(m // tm, n // tn),
        in_specs=[pl.BlockSpec((tm, k), lambda i, j: (i, 0)), pl.BlockSpec((k, tn), lambda i, j: (0, j))],
        out_specs=pl.BlockSpec((tm, tn), lambda i, j: (i, j)),
        out_shape=jax.ShapeDtypeStruct((m, n), out_dtype),
        compiler_params=_params("parallel", "parallel"),
        name="matmul",
    )(a, b)


def _matmul_residual_kernel(a_ref, b_ref, r_ref, o_ref):
    o_ref[...] = r_ref[...] + _dot(a_ref[...], b_ref[...])


def matmul_residual(a, b, res, tm=1024, tn=1024):
    m, k = a.shape
    n = b.shape[1]
    tm, tn = min(tm, m), min(tn, n)
    return pl.pallas_call(
        _matmul_residual_kernel,
        grid=(m // tm, n // tn),
        in_specs=[pl.BlockSpec((tm, k), lambda i, j: (i, 0)), pl.BlockSpec((k, tn), lambda i, j: (0, j)),
                  pl.BlockSpec((tm, tn), lambda i, j: (i, j))],
        out_specs=pl.BlockSpec((tm, tn), lambda i, j: (i, j)),
        out_shape=jax.ShapeDtypeStruct((m, n), F32),
        compiler_params=_params("parallel", "parallel"),
        name="matmul_residual",
    )(a, b, res)


def _softmax_step(carry, q, k, v, mask, scale):
    m_i, l_i, acc = carry
    s = jnp.where(mask, _dot_nt(q, k) * scale, NEG)
    m_new = jnp.maximum(m_i, jnp.max(s, axis=-1, keepdims=True))
    alpha = jnp.exp(m_i - m_new)
    p = jnp.where(mask, jnp.exp(s - m_new), 0.0)
    l_new = alpha * l_i + jnp.sum(p, axis=-1, keepdims=True)
    acc_new = alpha * acc + _dot(p.astype(v.dtype), v)
    return m_new, l_new, acc_new


def _softmax_init(m, e):
    return jnp.full((m, 1), NEG, F32), jnp.zeros((m, 1), F32), jnp.zeros((m, e), F32)


def _rank_desc(vals, n_candidates):
    lane = lax.broadcasted_iota(I32, vals.shape, 1)
    rank = jnp.zeros(vals.shape, F32)
    for m in range(n_candidates):
        vm = vals[:, m:m + 1]
        beats = (vm > vals) | ((vm == vals) & (lane > m))
        rank = rank + beats.astype(F32)
    return rank


def _lane_column(x, idx):
    lane = lax.broadcasted_iota(I32, x.shape, 1)
    return jnp.sum(jnp.where(lane == idx, x, 0.0), axis=-1, keepdims=True)


def _moba_kernel(q_ref, k_ref, v_ref, o_ref, *, n_blocks, topk):
    blk = MOBA_BLOCK
    cur = pl.program_id(2)
    q = q_ref[...]
    seq = k_ref.shape[0]
    scale = HEAD_DIM ** -0.5

    row = lax.broadcasted_iota(I32, (LANES, seq), 0)
    col = lax.broadcasted_iota(I32, (LANES, seq), 1)
    indicator = (col // blk == row).astype(BF16)
    kmean = _dot(indicator, k_ref[...]) * (1.0 / blk)
    gate = _dot_nt(q, kmean.astype(BF16))
    lane = lax.broadcasted_iota(I32, gate.shape, 1)
    valid = lane < cur
    gate = jnp.where(valid, gate, NEG)
    chosen = (valid & (_rank_desc(gate, n_blocks) < topk)).astype(F32)

    base = pl.multiple_of(cur * blk, blk)
    qi = lax.broadcasted_iota(I32, (blk, blk), 0)
    ki = lax.broadcasted_iota(I32, (blk, blk), 1)
    carry = _softmax_step(_softmax_init(blk, HEAD_DIM), q, k_ref[pl.ds(base, blk), :],
                          v_ref[pl.ds(base, blk), :], ki <= qi, scale)

    def past_block(n, carry):
        start = pl.multiple_of(n * blk, blk)
        mask = jnp.broadcast_to(_lane_column(chosen, n) > 0.5, (blk, blk))
        return _softmax_step(carry, q, k_ref[pl.ds(start, blk), :], v_ref[pl.ds(start, blk), :], mask, scale)

    _, l_i, acc = lax.fori_loop(0, cur, past_block, carry)
    o_ref[...] = (acc / l_i).astype(o_ref.dtype)


def moba_attention(proj, batch, seq):
    assert seq % MOBA_BLOCK == 0
    n_blocks = seq // MOBA_BLOCK
    qb, kb, vb = C_MQ // HEAD_DIM, C_MK // HEAD_DIM, C_MV // HEAD_DIM
    kern = functools.partial(_moba_kernel, n_blocks=n_blocks, topk=min(MOBA_TOPK, n_blocks))
    return pl.pallas_call(
        kern,
        grid=(batch, MOBA_HEADS, n_blocks),
        in_specs=[pl.BlockSpec((None, MOBA_BLOCK, HEAD_DIM), lambda b, h, i: (b, i, qb + h)),
                  pl.BlockSpec((None, seq, HEAD_DIM), lambda b, h, i: (b, 0, kb + h)),
                  pl.BlockSpec((None, seq, HEAD_DIM), lambda b, h, i: (b, 0, vb + h))],
        out_specs=pl.BlockSpec((None, MOBA_BLOCK, HEAD_DIM), lambda b, h, i: (b, i, h)),
        out_shape=jax.ShapeDtypeStruct((batch, seq, MOBA_WIDTH), BF16),
        compiler_params=_params("parallel", "parallel", "arbitrary"),
        name="moba",
    )(proj, proj, proj)


def _nsa_compress_kernel(x_ref, pe_ref, w1_ref, w2_ref, o_ref):
    x = x_ref[...].astype(F32)
    half = x.shape[1]
    lo = _dot((x + pe_ref[:, :half]).astype(BF16), w1_ref[:half, :])
    hi = _dot((x + pe_ref[:, half:]).astype(BF16), w1_ref[half:, :])
    pre = lo + pltpu.roll(hi, hi.shape[0] - 1, 0)
    hid = pre * jax.nn.sigmoid(pre)
    o_ref[...] = _dot(hid.astype(BF16), w2_ref[...]).astype(o_ref.dtype)


def nsa_compress(proj, cmp_pe, cmp_w1, cmp_w2, batch, seq):
    n16 = seq // NSA_CMP_STRIDE
    width = NSA_CMP_STRIDE * HEAD_DIM
    x = proj[:, :, C_NKV:C_NKV + 2 * NSA_KV_WIDTH].reshape(batch, seq, 2, NSA_KV_HEADS, HEAD_DIM)
    x = x.transpose(0, 2, 3, 1, 4).reshape(batch, 2, NSA_KV_HEADS, n16, width)
    pe = cmp_pe.reshape(2, 1, NSA_CMP_LEN * HEAD_DIM).astype(F32)
    return pl.pallas_call(
        _nsa_compress_kernel,
        grid=(batch, 2, NSA_KV_HEADS),
        in_specs=[pl.BlockSpec((None, None, None, n16, width), lambda b, c, g: (b, c, g, 0, 0)),
                  pl.BlockSpec((None, 1, 2 * width), lambda b, c, g: (c, 0, 0)),
                  pl.BlockSpec((None, 2 * width, HEAD_DIM), lambda b, c, g: (c, 0, 0)),
                  pl.BlockSpec((None, HEAD_DIM, HEAD_DIM), lambda b, c, g: (c, 0, 0))],
        out_specs=pl.BlockSpec((None, None, None, n16, HEAD_DIM), lambda b, c, g: (b, c, g, 0, 0)),
        out_shape=jax.ShapeDtypeStruct((batch, 2, NSA_KV_HEADS, n16, HEAD_DIM), BF16),
        compiler_params=_params("parallel", "parallel", "parallel"),
        name="nsa_compress",
    )(x, pe, cmp_w1.astype(BF16), cmp_w2.astype(BF16))


def _nsa_kernel(q_ref, kc_ref, vc_ref, ks_ref, vs_ref, kw_ref, vw_ref, small_ref, o_ref, *,
                tq, tk, n_cmp, n_slc, topn, win_len):
    g = pl.program_id(1)
    qi = pl.program_id(2)
    seq = ks_ref.shape[0]
    rows = NSA_GROUP * tq
    scale = HEAD_DIM ** -0.5
    start = qi * tq

    q = jnp.concatenate([q_ref[:, r * HEAD_DIM:(r + 1) * HEAD_DIM] for r in range(NSA_GROUP)], axis=0)
    pos1 = start + lax.broadcasted_iota(I32, (tq, 1), 0)
    pos = jnp.concatenate([pos1] * NSA_GROUP, axis=0)

    n16 = kc_ref.shape[0]
    n_idx = lax.broadcasted_iota(I32, (rows, n16), 1)
    in_range = n_idx < n_cmp
    cmask = (n_idx * NSA_CMP_STRIDE + (NSA_CMP_LEN - 1) <= pos) & in_range
    s_c = jnp.where(cmask, _dot_nt(q, kc_ref[...]) * scale, NEG)
    e_c = jnp.where(in_range, jnp.exp(s_c - jnp.max(s_c, axis=-1, keepdims=True)), 0.0)
    p_c = jnp.where(cmask, e_c / jnp.sum(e_c, axis=-1, keepdims=True), 0.0)
    o_c = _dot(p_c.astype(BF16), vc_ref[...])

    p_sum = p_c[0:tq]
    for r in range(1, NSA_GROUP):
        p_sum = p_sum + p_c[r * tq:(r + 1) * tq]
    cn = lax.broadcasted_iota(I32, (n16, LANES), 0)
    cj = lax.broadcasted_iota(I32, (n16, LANES), 1)
    overlap = ((cn * NSA_CMP_STRIDE < (cj + 1) * NSA_SLC_BLOCK)
               & (cn * NSA_CMP_STRIDE + (NSA_CMP_LEN - 1) >= cj * NSA_SLC_BLOCK)
               & (cn < n_cmp) & (cj < n_slc)).astype(BF16)
    p_hi = p_sum.astype(BF16)
    p_lo = (p_sum - p_hi.astype(F32)).astype(BF16)
    imp = _dot(p_hi, overlap) + _dot(p_lo, overlap)
    j_idx = lax.broadcasted_iota(I32, (tq, LANES), 1)
    cur_blk = pos1 // NSA_SLC_BLOCK
    forced = (j_idx == 0) | (j_idx == cur_blk) | (j_idx == cur_blk - 1)
    imp = jnp.where(forced, FORCE, imp)
    imp = jnp.where(j_idx > cur_blk, NEG, imp)
    chosen1 = ((_rank_desc(imp, n_slc) < topn) & (j_idx <= cur_blk)).astype(BF16)
    chosen = jnp.concatenate([chosen1] * NSA_GROUP, axis=0)

    def slc_chunk(c, carry):
        k0 = pl.multiple_of(c * tk, tk)
        ej = lax.broadcasted_iota(I32, (LANES, tk), 0)
        ekey = k0 + lax.broadcasted_iota(I32, (LANES, tk), 1)
        expand = (ekey // NSA_SLC_BLOCK == ej).astype(BF16)
        key = k0 + lax.broadcasted_iota(I32, (rows, tk), 1)
        mask = (_dot(chosen, expand) > 0.5) & (key <= pos)
        return _softmax_step(carry, q, ks_ref[pl.ds(k0, tk), :], vs_ref[pl.ds(k0, tk), :], mask, scale)

    n_chunks = (start + tq + tk - 1) // tk
    _, l_s, acc_s = lax.fori_loop(0, n_chunks, slc_chunk, _softmax_init(rows, HEAD_DIM))
    o_s = acc_s / l_s

    w0 = pl.multiple_of(jnp.maximum(start + tq - win_len, 0), tq)
    carry = _softmax_init(rows, HEAD_DIM)
    for c in range(win_len // tq):
        k0 = pl.multiple_of(w0 + c * tq, tq)
        key = k0 + lax.broadcasted_iota(I32, (rows, tq), 1)
        mask = (key <= pos) & (key > pos - NSA_WINDOW)
        carry = _softmax_step(carry, q, kw_ref[pl.ds(k0, tq), :], vw_ref[pl.ds(k0, tq), :], mask, scale)
    o_w = carry[2] / carry[1]

    small = small_ref[...].astype(F32)
    for r in range(NSA_GROUP):
        lane0 = SMALL_NG_LANE + (g * NSA_GROUP + r) * 3
        sl = slice(r * tq, (r + 1) * tq)
        out = (jax.nn.sigmoid(_lane_column(small, lane0)) * o_c[sl]
               + jax.nn.sigmoid(_lane_column(small, lane0 + 1)) * o_s[sl]
               + jax.nn.sigmoid(_lane_column(small, lane0 + 2)) * o_w[sl])
        o_ref[:, r * HEAD_DIM:(r + 1) * HEAD_DIM] = out.astype(o_ref.dtype)


def nsa_attention(proj, kv_c, batch, seq, tq=128, tk=256):
    tk = min(tk, seq)
    assert seq % tk == 0 and tk % tq == 0 and tk % NSA_SLC_BLOCK == 0
    n_cmp = (seq - NSA_CMP_LEN) // NSA_CMP_STRIDE + 1
    n_slc = seq // NSA_SLC_BLOCK
    assert n_cmp <= LANES and n_slc <= LANES
    win_len = min(NSA_WINDOW + tq, seq)
    n16 = seq // NSA_CMP_STRIDE
    gw = NSA_GROUP * HEAD_DIM
    nkv = C_NKV // HEAD_DIM

    def kv_spec(slot):
        return pl.BlockSpec((None, seq, HEAD_DIM),
                            lambda b, g, i, slot=slot: (b, 0, nkv + slot * NSA_KV_HEADS + g))

    def cmp_spec(c):
        return pl.BlockSpec((None, None, None, n16, HEAD_DIM), lambda b, g, i, c=c: (b, c, g, 0, 0))

    kern = functools.partial(_nsa_kernel, tq=tq, tk=tk, n_cmp=n_cmp, n_slc=n_slc,
                             topn=min(NSA_TOPN, n_slc), win_len=win_len)
    return pl.pallas_call(
        kern,
        grid=(batch, NSA_KV_HEADS, seq // tq),
        in_specs=[pl.BlockSpec((None, tq, gw), lambda b, g, i: (b, i, C_NQ // gw + g)),
                  cmp_spec(0), cmp_spec(1), kv_spec(2), kv_spec(3), kv_spec(4), kv_spec(5),
                  pl.BlockSpec((None, tq, LANES), lambda b, g, i: (b, i, C_SMALL // LANES))],
        out_specs=pl.BlockSpec((None, tq, gw), lambda b, g, i: (b, i, g)),
        out_shape=jax.ShapeDtypeStruct((batch, seq, NSA_WIDTH), BF16),
        compiler_params=_params("parallel", "parallel", "arbitrary"),
        name="nsa",
    )(proj, kv_c, kv_c, proj, proj, proj, proj, proj)


GLA_SUB = 16


def _gla_kernel(q_ref, k_ref, v_ref, gg_ref, small_ref, wa_ref, ba_ref, ng_ref, o_ref,
                state_ref, attn_ref, *, chunk):
    @pl.when(pl.program_id(2) == 0)
    def _():
        state_ref[...] = jnp.zeros_like(state_ref)

    q = q_ref[...].astype(F32) * (GLA_DK ** -0.5)
    k = k_ref[...].astype(F32)
    v = v_ref[...]
    z = _dot(small_ref[...], wa_ref[...]) + ba_ref[...]
    log_a = (jnp.minimum(z, 0.0) - jnp.log(1.0 + jnp.exp(-jnp.abs(z)))) * (1.0 / GLA_TAU)
    tri = (lax.broadcasted_iota(I32, (chunk, chunk), 1)
           <= lax.broadcasted_iota(I32, (chunk, chunk), 0)).astype(BF16)
    b = _dot_split3(tri, log_a)

    attn_ref[...] = jnp.zeros_like(attn_ref)
    t_idx = lax.broadcasted_iota(I32, (GLA_SUB, 1), 0)
    s_lane = lax.broadcasted_iota(I32, (GLA_SUB, GLA_SUB), 1)
    for i in range(chunk // GLA_SUB):
        r0 = i * GLA_SUB
        bi, qi, ki = b[r0:r0 + GLA_SUB], q[r0:r0 + GLA_SUB], k[r0:r0 + GLA_SUB]
        diag = jnp.zeros((GLA_SUB, GLA_SUB), F32)
        for s in range(GLA_SUB):
            decay = jnp.exp(jnp.where(t_idx >= s, bi - bi[s:s + 1], NEG))
            col = jnp.sum(qi * ki[s:s + 1] * decay, axis=-1, keepdims=True)
            diag = jnp.where(s_lane == s, col, diag)
        attn_ref[r0:r0 + GLA_SUB, r0:r0 + GLA_SUB] = diag
        if i > 0:
            ref_b = b[r0:r0 + 1]
            q_dec = (qi * jnp.exp(bi - ref_b)).astype(BF16)
            k_dec = (k[:r0] * jnp.exp(ref_b - b[:r0])).astype(BF16)
            attn_ref[r0:r0 + GLA_SUB, 0:r0] = _dot_nt(q_dec, k_dec)

    state_t = state_ref[...]
    o = _dot(attn_ref[...].astype(BF16), v) + _dot_nt((q * jnp.exp(b)).astype(BF16), state_t.astype(BF16))
    b_last = b[chunk - 1:chunk]
    k_dec = (k * jnp.exp(b_last - b)).astype(BF16)
    state_ref[...] = state_t * jnp.exp(b_last) + _dot(v.astype(F32).T.astype(BF16), k_dec)

    gate = gg_ref[...].astype(F32)
    o_ref[...] = (_rms(o, ng_ref[...]) * (gate * jax.nn.sigmoid(gate))).astype(o_ref.dtype)


def gla_attention(proj, gla_wa, gla_ba, gla_norm_g, batch, seq, chunk=128):
    chunk = min(chunk, seq)
    assert seq % chunk == 0 and chunk % GLA_SUB == 0
    wa = jnp.zeros((LANES, GLA_KEY_WIDTH), BF16).at[SMALL_GA_LANE:SMALL_GA_LANE + GLA_RANK].set(gla_wa.astype(BF16))
    kern = functools.partial(_gla_kernel, chunk=chunk)
    return pl.pallas_call(
        kern,
        grid=(batch, GLA_HEADS, seq // chunk),
        in_specs=[pl.BlockSpec((None, chunk, GLA_DK), lambda b, h, c: (b, c, C_GQ // GLA_DK + h)),
                  pl.BlockSpec((None, chunk, GLA_DK), lambda b, h, c: (b, c, C_GK // GLA_DK + h)),
                  pl.BlockSpec((None, chunk, GLA_DV), lambda b, h, c: (b, c, C_GV // GLA_DV + h)),
                  pl.BlockSpec((None, chunk, GLA_DV), lambda b, h, c: (b, c, C_GG // GLA_DV + h)),
                  pl.BlockSpec((None, chunk, LANES), lambda b, h, c: (b, c, C_SMALL // LANES)),
                  pl.BlockSpec((LANES, GLA_DK), lambda b, h, c: (0, h)),
                  pl.BlockSpec((1, GLA_DK), lambda b, h, c: (0, h)),
                  pl.BlockSpec((1, GLA_DV), lambda b, h, c: (0, 0))],
        out_specs=pl.BlockSpec((None, chunk, GLA_DV), lambda b, h, c: (b, c, h)),
        out_shape=jax.ShapeDtypeStruct((batch, seq, GLA_WIDTH), BF16),
        scratch_shapes=[pltpu.VMEM((GLA_DV, GLA_DK), F32), pltpu.VMEM((chunk, chunk), F32)],
        compiler_params=_params("parallel", "parallel", "arbitrary"),
        name="gla",
    )(proj, proj, proj, proj, proj, wa, gla_ba.reshape(1, GLA_KEY_WIDTH).astype(F32),
      gla_norm_g.reshape(1, GLA_DV).astype(F32))


def _merge_kernel(om_ref, on_ref, og_ref, wm_ref, wn_ref, wg_ref, gm_ref, gn_ref, gl_ref, o_ref):
    def gated(gate_ref, a_ref, w_ref):
        return jax.nn.sigmoid(gate_ref[...].astype(F32)) * _dot(a_ref[...], w_ref[...])

    o_ref[...] = (gated(gm_ref, om_ref, wm_ref) + gated(gn_ref, on_ref, wn_ref)
                  + gated(gl_ref, og_ref, wg_ref)).astype(o_ref.dtype)


def merge_branches(o_m, o_n, o_g, w_m, w_n, w_g, proj2d, tm=1024, tn=512):
    m = o_m.shape[0]
    tm = min(tm, m)
    assert C_MG % tn == 0 and D_MODEL % tn == 0

    def gate_spec(c):
        return pl.BlockSpec((tm, tn), lambda i, j, c=c: (i, (C_MG + c * D_MODEL) // tn + j))

    def act_spec(width):
        return pl.BlockSpec((tm, width), lambda i, j: (i, 0))

    def w_spec(width):
        return pl.BlockSpec((width, tn), lambda i, j: (0, j))

    return pl.pallas_call(
        _merge_kernel,
        grid=(m // tm, D_MODEL // tn),
        in_specs=[act_spec(MOBA_WIDTH), act_spec(NSA_WIDTH), act_spec(GLA_WIDTH),
                  w_spec(MOBA_WIDTH), w_spec(NSA_WIDTH), w_spec(GLA_WIDTH),
                  gate_spec(0), gate_spec(1), gate_spec(2)],
        out_specs=pl.BlockSpec((tm, tn), lambda i, j: (i, j)),
        out_shape=jax.ShapeDtypeStruct((m, D_MODEL), BF16),
        compiler_params=_params("parallel", "parallel"),
        name="merge",
    )(o_m, o_n, o_g, w_m, w_n, w_g, proj2d, proj2d, proj2d)


def _router_kernel(x_ref, g_ref, w_ref, b_ref, h_ref, logit_ref):
    h = _rms(x_ref[...], g_ref[...])
    h_ref[...] = h.astype(h_ref.dtype)
    h1 = h.astype(BF16)
    r1 = h - h1.astype(F32)
    h2 = r1.astype(BF16)
    h3 = (r1 - h2.astype(F32)).astype(BF16)
    w1 = w_ref[0]
    w2 = w_ref[1]
    logit_ref[...] = (_dot(h1, w1) + (_dot(h1, w2) + _dot(h2, w1)) + (_dot(h2, w2) + _dot(h3, w1))) + b_ref[...]


def router(x2d, norm_g, rg_w, rg_b, re_w, re_b, tm=512):
    m, d = x2d.shape
    tm = min(tm, m)
    n_real = MOE_GROUPS + MOE_EXPERTS
    w = jnp.zeros((d, ROUTER_LANES), F32).at[:, :n_real].set(jnp.concatenate([rg_w, re_w], axis=1))
    w_hi = w.astype(BF16)
    w_lo = (w - w_hi.astype(F32)).astype(BF16)
    bias = jnp.zeros((1, ROUTER_LANES), F32).at[0, :n_real].set(jnp.concatenate([rg_b, re_b]))
    return pl.pallas_call(
        _router_kernel,
        grid=(m // tm,),
        in_specs=[pl.BlockSpec((tm, d), lambda i: (i, 0)), pl.BlockSpec((1, d), lambda i: (0, 0)),
                  pl.BlockSpec((2, d, ROUTER_LANES), lambda i: (0, 0, 0)),
                  pl.BlockSpec((1, ROUTER_LANES), lambda i: (0, 0))],
        out_specs=[pl.BlockSpec((tm, d), lambda i: (i, 0)), pl.BlockSpec((tm, ROUTER_LANES), lambda i: (i, 0))],
        out_shape=[jax.ShapeDtypeStruct((m, d), BF16), jax.ShapeDtypeStruct((m, ROUTER_LANES), F32)],
        compiler_params=_params("parallel"),
        name="router",
    )(x2d, norm_g.reshape(1, d).astype(F32), jnp.stack([w_hi, w_lo]), bias)


def _expert_kernel(blk_e_ref, n_used_ref, x_ref, wg_ref, wu_ref, wd_ref, rw_ref, o_ref):
    del blk_e_ref

    @pl.when(pl.program_id(0) < n_used_ref[0])
    def _():
        x = x_ref[...]
        gate = _dot(x, wg_ref[...])
        hid = gate * jax.nn.sigmoid(gate) * _dot(x, wu_ref[...])
        o_ref[...] = _dot(hid.astype(BF16), wd_ref[...]) * rw_ref[...]

    @pl.when(pl.program_id(0) >= n_used_ref[0])
    def _():
        o_ref[...] = jnp.zeros_like(o_ref)


def expert_blocks(xs, row_w, blk_e, n_used, w_gate, w_up, w_down):
    p, d = xs.shape
    ff = w_gate.shape[2]
    n_blk = p // MOE_ROWS
    grid_spec = pltpu.PrefetchScalarGridSpec(
        num_scalar_prefetch=2,
        grid=(n_blk,),
        in_specs=[pl.BlockSpec((MOE_ROWS, d), lambda i, e, n: (i, 0)),
                  pl.BlockSpec((None, d, ff), lambda i, e, n: (e[i], 0, 0)),
                  pl.BlockSpec((None, d, ff), lambda i, e, n: (e[i], 0, 0)),
                  pl.BlockSpec((None, ff, d), lambda i, e, n: (e[i], 0, 0)),
                  pl.BlockSpec((MOE_ROWS, 1), lambda i, e, n: (i, 0))],
        out_specs=pl.BlockSpec((MOE_ROWS, d), lambda i, e, n: (i, 0)),
    )
    return pl.pallas_call(
        _expert_kernel,
        grid_spec=grid_spec,
        out_shape=jax.ShapeDtypeStruct((p, d), F32),
        compiler_params=_params("arbitrary"),
        name="experts",
    )(blk_e, n_used, xs, w_gate, w_up, w_down, row_w)


def _combine_kernel(x_ref, y0_ref, y1_ref, g_ref, o_ref, *, final_norm):
    x = x_ref[...] + (y0_ref[...] + y1_ref[...])
    o_ref[...] = _rms(x, g_ref[...]) if final_norm else x


def combine(x2d, y0, y1, norm_g, final_norm, tm=512):
    m, d = x2d.shape
    tm = min(tm, m)
    row = pl.BlockSpec((tm, d), lambda i: (i, 0))
    return pl.pallas_call(
        functools.partial(_combine_kernel, final_norm=final_norm),
        grid=(m // tm,),
        in_specs=[row, row, row, pl.BlockSpec((1, d), lambda i: (0, 0))],
        out_specs=row,
        out_shape=jax.ShapeDtypeStruct((m, d), F32),
        compiler_params=_params("parallel"),
        name="combine",
    )(x2d, y0, y1, norm_g.reshape(1, d).astype(F32))


def hier_moe(x2d, norm_g, rg_w, rg_b, re_w, re_b, w_gate, w_up, w_down, out_norm_g, final_norm):
    t = x2d.shape[0]
    h, logits = router(x2d, norm_g, rg_w, rg_b, re_w, re_b)
    g_logits = logits[:, :MOE_GROUPS]
    grp = jnp.argmax(g_logits, axis=-1)
    p_grp = jnp.take_along_axis(jax.nn.softmax(g_logits, axis=-1), grp[:, None], axis=1)[:, 0]
    e_logits = logits[:, MOE_GROUPS:MOE_GROUPS + MOE_EXPERTS].reshape(t, MOE_GROUPS, MOE_EXPERTS_PER_GROUP)
    e_in = jnp.take_along_axis(e_logits, grp[:, None, None], axis=1)[:, 0]
    top_v, top_i = lax.top_k(e_in, MOE_TOPK)
    w = jax.nn.softmax(top_v, axis=-1) * p_grp[:, None]
    expert = (grp[:, None] * MOE_EXPERTS_PER_GROUP + top_i).astype(I32)

    a = t * MOE_TOPK
    e_flat = expert.reshape(a)
    onehot = (e_flat[:, None] == jnp.arange(MOE_EXPERTS, dtype=I32)[None, :]).astype(I32)
    running = jnp.cumsum(onehot, axis=0)
    counts = running[-1]
    rank = jnp.take_along_axis(running, e_flat[:, None], axis=1)[:, 0] - 1
    padded = (counts + MOE_ROWS - 1) // MOE_ROWS * MOE_ROWS
    pends = jnp.cumsum(padded)
    dest = (pends - padded)[e_flat] + rank
    p_rows = (a + MOE_EXPERTS * (MOE_ROWS - 1)) // MOE_ROWS * MOE_ROWS
    n_blk = p_rows // MOE_ROWS
    row_tok = jnp.zeros((p_rows,), I32).at[dest].set(jnp.arange(a, dtype=I32) // MOE_TOPK)
    row_w = jnp.zeros((p_rows,), F32).at[dest].set(w.reshape(a))
    blk_start = jnp.arange(n_blk, dtype=I32) * MOE_ROWS
    blk_e = jnp.minimum(jnp.sum((pends[None, :] <= blk_start[:, None]).astype(I32), axis=1), MOE_EXPERTS - 1)
    n_used = (pends[-1] // MOE_ROWS).astype(I32).reshape(1)

    xs = jnp.take(h, row_tok, axis=0)
    y_rows = expert_blocks(xs, row_w[:, None], blk_e, n_used, w_gate.astype(BF16), w_up.astype(BF16),
                           w_down.astype(BF16))
    dest2 = dest.reshape(t, MOE_TOPK)
    y0 = jnp.take(y_rows, dest2[:, 0], axis=0)
    y1 = jnp.take(y_rows, dest2[:, 1], axis=0)
    return combine(x2d, y0, y1, out_norm_g, final_norm)


def _permute_w_in(w_in):
    d = w_in.shape[0]
    pad = jnp.zeros((d, PROJ_WIDTH - _SRC_END), w_in.dtype)
    return jnp.concatenate([w_in[:, :_SRC_NG], w_in[:, _SRC_GQ:_SRC_GA], w_in[:, _SRC_GG:_SRC_END],
                            w_in[:, _SRC_GA:_SRC_GG], w_in[:, _SRC_NG:_SRC_GQ], pad], axis=1).astype(BF16)


def hybrid_layer(x, norm1_g, w_in, nsa_cmp_pe, nsa_cmp_w1, nsa_cmp_w2, gla_wa, gla_ba, gla_norm_g,
                 w_br_moba, w_br_nsa, w_br_gla, w_out, norm2_g, router_group_w, router_group_b,
                 router_expert_w, router_expert_b, expert_w_gate, expert_w_up, expert_w_down,
                 out_norm_g, final_norm):
    batch, seq, d = x.shape
    t = batch * seq
    x2d = x.reshape(t, d)
    h = rmsnorm(x2d, norm1_g, BF16)
    proj2d = matmul(h, _permute_w_in(w_in), BF16)
    proj = proj2d.reshape(batch, seq, PROJ_WIDTH)
    o_m = moba_attention(proj, batch, seq)
    kv_c = nsa_compress(proj, nsa_cmp_pe, nsa_cmp_w1, nsa_cmp_w2, batch, seq)
    o_n = nsa_attention(proj, kv_c, batch, seq)
    o_g = gla_attention(proj, gla_wa, gla_ba, gla_norm_g, batch, seq)
    merged = merge_branches(o_m.reshape(t, -1), o_n.reshape(t, -1), o_g.reshape(t, -1),
                            w_br_moba.astype(BF16), w_br_nsa.astype(BF16), w_br_gla.astype(BF16), proj2d)
    x2d = matmul_residual(merged, w_out.astype(BF16), x2d)
    x2d = hier_moe(x2d, norm2_g, router_group_w, router_group_b, router_expert_w, router_expert_b,
                   expert_w_gate, expert_w_up, expert_w_down, out_norm_g, final_norm)
    return x2d.reshape(batch, seq, d)


def kernel(x, norm1_g, w_in, nsa_cmp_pe, nsa_cmp_w1, nsa_cmp_w2, gla_wa, gla_ba, gla_norm_g, w_br_moba,
           w_br_nsa, w_br_gla, w_out, norm2_g, router_group_w, router_group_b, router_expert_w,
           router_expert_b, expert_w_gate, expert_w_up, expert_w_down, final_norm_g):
    for l in range(DEPTH):
        x = hybrid_layer(x, norm1_g[l], w_in[l], nsa_cmp_pe[l], nsa_cmp_w1[l], nsa_cmp_w2[l], gla_wa[l],
                         gla_ba[l], gla_norm_g[l], w_br_moba[l], w_br_nsa[l], w_br_gla[l], w_out[l],
                         norm2_g[l], router_group_w[l], router_group_b[l], router_expert_w[l],
                         router_expert_b[l], expert_w_gate[l], expert_w_up[l], expert_w_down[l],
                         final_norm_g, l == DEPTH - 1)
    return x
```

```python
import functools

import jax
import jax.numpy as jnp
import numpy as np
from jax import lax
from jax.experimental import pallas as pl
from jax.experimental.pallas import tpu as pltpu

F32 = jnp.float32
BF16 = jnp.bfloat16
I32 = jnp.int32

D_MODEL = 2048
DEPTH = 2
HEAD_DIM = 128
NEG = -1e30
FORCE = 1e9
EPS = 1e-6
LOG2E = 1.4426950408889634

MOBA_HEADS = 8
MOBA_BLOCK = 256
MOBA_TOPK = 3
MOBA_WIDTH = MOBA_HEADS * HEAD_DIM

NSA_HEADS = 8
NSA_KV_HEADS = 2
NSA_GROUP = NSA_HEADS // NSA_KV_HEADS
NSA_CMP_LEN = 32
NSA_CMP_STRIDE = 16
NSA_SLC_BLOCK = 64
NSA_TOPN = 8
NSA_WINDOW = 512
NSA_WIDTH = NSA_HEADS * HEAD_DIM
NSA_KV_WIDTH = NSA_KV_HEADS * HEAD_DIM

GLA_HEADS = 4
GLA_DK = 128
GLA_DV = 256
GLA_RANK = 16
GLA_TAU = 16.0
GLA_KEY_WIDTH = GLA_HEADS * GLA_DK
GLA_WIDTH = GLA_HEADS * GLA_DV

N_BRANCH = 3
MOE_GROUPS = 4
MOE_EXPERTS_PER_GROUP = 8
MOE_EXPERTS = MOE_GROUPS * MOE_EXPERTS_PER_GROUP
MOE_TOPK = 2
MOE_FF = D_MODEL // 4

LANES = 128
SUBLANES = 8
BF16_SUBLANES = 16
VMEM_LIMIT_BYTES = 56 * 1024 * 1024

_SRC_NG = MOBA_WIDTH * 3 + NSA_WIDTH + 6 * NSA_KV_WIDTH
_SRC_GQ = _SRC_NG + 3 * NSA_HEADS
_SRC_GA = _SRC_GQ + 2 * GLA_KEY_WIDTH + GLA_WIDTH
_SRC_GG = _SRC_GA + GLA_RANK
_SRC_END = _SRC_GG + GLA_WIDTH + N_BRANCH * D_MODEL

C_MQ = 0
C_MK = C_MQ + MOBA_WIDTH
C_MV = C_MK + MOBA_WIDTH
C_NQ = C_MV + MOBA_WIDTH
C_NKV = C_NQ + NSA_WIDTH
C_GQ = C_NKV + 6 * NSA_KV_WIDTH
C_GK = C_GQ + GLA_KEY_WIDTH
C_GV = C_GK + GLA_KEY_WIDTH
C_GG = C_GV + GLA_WIDTH
C_MG = C_GG + GLA_WIDTH
C_SMALL = C_MG + N_BRANCH * D_MODEL
PROJ_WIDTH = C_SMALL + LANES
SMALL_GA_LANE = 0
SMALL_NG_LANE = GLA_RANK

ROUTER_LANES = LANES
MOE_ROWS = 256


def _params(*semantics):
    return pltpu.CompilerParams(dimension_semantics=semantics, vmem_limit_bytes=VMEM_LIMIT_BYTES)


def _dot(a, b):
    return jnp.dot(a, b, preferred_element_type=F32)


def _dot_nt(a, b):
    return lax.dot_general(a, b, (((1,), (1,)), ((), ())), preferred_element_type=F32)


def _dot_split3(a01, x):
    x1 = x.astype(BF16)
    r1 = x - x1.astype(F32)
    x2 = r1.astype(BF16)
    x3 = (r1 - x2.astype(F32)).astype(BF16)
    return _dot(a01, x1) + _dot(a01, x2) + _dot(a01, x3)


def _rms(x, g):
    return x * lax.rsqrt(jnp.mean(x * x, axis=-1, keepdims=True) + EPS) * g


def _rmsnorm_kernel(x_ref, g_ref, o_ref):
    o_ref[...] = _rms(x_ref[...].astype(F32), g_ref[...]).astype(o_ref.dtype)


def rmsnorm(x2d, g, out_dtype, tm=512):
    m, d = x2d.shape
    tm = min(tm, m)
    return pl.pallas_call(
        _rmsnorm_kernel,
        grid=(m // tm,),
        in_specs=[pl.BlockSpec((tm, d), lambda i: (i, 0)), pl.BlockSpec((1, d), lambda i: (0, 0))],
        out_specs=pl.BlockSpec((tm, d), lambda i: (i, 0)),
        out_shape=jax.ShapeDtypeStruct((m, d), out_dtype),
        compiler_params=_params("parallel"),
        name="rmsnorm",
    )(x2d, g.reshape(1, d).astype(F32))


def _matmul_kernel(a_ref, b_ref, o_ref):
    o_ref[...] = _dot(a_ref[...], b_ref[...]).astype(o_ref.dtype)


def matmul(a, b, out_dtype, tm=1024, tn=1152):
    m, k = a.shape
    n = b.shape[1]
    tm, tn = min(tm, m), min(tn, n)
    return pl.pallas_call(
        _matmul_kernel,
        grid=(m // tm, n // tn),
        in_specs=[pl.BlockSpec((tm, k), lambda i, j: (i, 0)), pl.BlockSpec((k, tn), lambda i, j: (0, j))],
        out_specs=pl.BlockSpec((tm, tn), lambda i, j: (i, j)),
        out_shape=jax.ShapeDtypeStruct((m, n), out_dtype),
        compiler_params=_params("parallel", "parallel"),
        name="matmul",
    )(a, b)


def _matmul_residual_kernel(a_ref, b_ref, r_ref, o_ref):
    o_ref[...] = r_ref[...] + _dot(a_ref[...], b_ref[...])


def matmul_residual(a, b, res, tm=1024, tn=1024):
    m, k = a.shape
    n = b.shape[1]
    tm, tn = min(tm, m), min(tn, n)
    return pl.pallas_call(
        _matmul_residual_kernel,
        grid=(m // tm, n // tn),
        in_specs=[pl.BlockSpec((tm, k), lambda i, j: (i, 0)), pl.BlockSpec((k, tn), lambda i, j: (0, j)),
                  pl.BlockSpec((tm, tn), lambda i, j: (i, j))],
        out_specs=pl.BlockSpec((tm, tn), lambda i, j: (i, j)),
        out_shape=jax.ShapeDtypeStruct((m, n), F32),
        compiler_params=_params("parallel", "parallel"),
        name="matmul_residual",
    )(a, b, res)


def _softmax_step(carry, q, k, v, mask, scale):
    m_i, l_i, acc = carry
    s = jnp.where(mask, _dot_nt(q, k) * scale, NEG)
    m_new = jnp.maximum(m_i, jnp.max(s, axis=-1, keepdims=True))
    alpha = jnp.exp(m_i - m_new)
    p = jnp.where(mask, jnp.exp(s - m_new), 0.0)
    l_new = alpha * l_i + jnp.sum(p, axis=-1, keepdims=True)
    acc_new = alpha * acc + _dot(p.astype(v.dtype), v)
    return m_new, l_new, acc_new


def _softmax_init(m, e):
    return jnp.full((m, 1), NEG, F32), jnp.zeros((m, 1), F32), jnp.zeros((m, e), F32)


def _rank_desc(vals, n_candidates):
    lane = lax.broadcasted_iota(I32, vals.shape, 1)
    rank = jnp.zeros(vals.shape, F32)
    for m in range(n_candidates):
        vm = vals[:, m:m + 1]
        beats = (vm > vals) | ((vm == vals) & (lane > m))
        rank = rank + beats.astype(F32)
    return rank


def _softmax_step_t(carry, q, k, v_t, bias, scale):
    m_i, l_i, acc = carry
    s = _dot_nt(k, q) * scale + bias
    m_new = jnp.maximum(m_i, jnp.max(s, axis=0, keepdims=True))
    alpha = jnp.exp(m_i - m_new)
    p = jnp.exp(s - m_new)
    l_new = alpha * l_i + jnp.sum(p, axis=0, keepdims=True)
    acc_new = alpha * acc + _dot(v_t, p.astype(v_t.dtype))
    return m_new, l_new, acc_new


def _softmax_init_t(m, e):
    return jnp.full((1, m), NEG, F32), jnp.zeros((1, m), F32), jnp.zeros((e, m), F32)


def _rank_desc_rows(vals, n_candidates):
    row = lax.broadcasted_iota(I32, vals.shape, 0)
    rank = jnp.zeros(vals.shape, F32)
    for m in range(n_candidates):
        vm = vals[m:m + 1, :]
        beats = (vm > vals) | ((vm == vals) & (row > m))
        rank = rank + beats.astype(F32)
    return rank


def _lane_column(x, idx):
    lane = lax.broadcasted_iota(I32, x.shape, 1)
    return jnp.sum(jnp.where(lane == idx, x, 0.0), axis=-1, keepdims=True)


def _moba_kernel(q_ref, k_ref, vt_ref, o_ref, kmean_ref, bias_ref, s_ref, p_ref, acc_ref, *, n_blocks, topk,
                 heads):
    blk = MOBA_BLOCK
    cur = pl.program_id(2)
    scale = HEAD_DIM ** -0.5
    cols = [slice(h * HEAD_DIM, (h + 1) * HEAD_DIM) for h in range(heads)]

    @pl.when(cur == 0)
    def _():
        kmean_ref[...] = jnp.zeros_like(kmean_ref)
        for h in range(heads):
            for n in range(n_blocks):
                k_blk = k_ref[n * blk:(n + 1) * blk, cols[h]].astype(F32)
                kmean_ref[h, n:n + 1, :] = jnp.sum(k_blk, axis=0, keepdims=True) * (1.0 / blk)

    for h in range(heads):
        gate = _dot_nt(kmean_ref[h].astype(BF16), q_ref[:, cols[h]])
        row = lax.broadcasted_iota(I32, gate.shape, 0)
        valid = row < cur
        gate = jnp.where(valid, gate, NEG)
        chosen = valid & (_rank_desc_rows(gate, n_blocks) < topk)
        bias_ref[h] = jnp.where(chosen, 0.0, NEG)

    seq = k_ref.shape[0]
    lo_blocks = (n_blocks + 1) // 2
    lo = lo_blocks * blk
    need_hi = cur >= lo_blocks

    for h in range(heads):
        s_ref[h, 0:lo, :] = _dot_nt(k_ref[0:lo, cols[h]], q_ref[:, cols[h]])

    if lo < seq:
        @pl.when(need_hi)
        def _():
            for h in range(heads):
                s_ref[h, lo:seq, :] = _dot_nt(k_ref[lo:seq, cols[h]], q_ref[:, cols[h]])

    sub = LANES
    base = pl.multiple_of(cur * blk, blk)
    qry_i = lax.broadcasted_iota(I32, (sub, blk), 1)
    key_i = [j * sub + lax.broadcasted_iota(I32, (sub, blk), 0) for j in range(blk // sub)]
    exp_scale = scale * LOG2E

    def fold_max(x):
        return jnp.max(x.reshape(sub // SUBLANES, SUBLANES, blk), axis=0)

    def fold_sum(x):
        return jnp.sum(x.reshape(sub // SUBLANES, SUBLANES, blk), axis=0)

    def tiles(h, start):
        return [s_ref[h, pl.ds(pl.multiple_of(start + j * sub, sub), sub), :] for j in range(blk // sub)]

    def max_body(n, m8):
        out = []
        for h in range(heads):
            blk_max = functools.reduce(jnp.maximum, [fold_max(t) for t in tiles(h, n * blk)])
            out.append(jnp.maximum(m8[h], blk_max * scale + bias_ref[h, pl.ds(n, 1), :]))
        return tuple(out)

    m8 = []
    for h in range(heads):
        own = [fold_max(jnp.where(key_i[j] <= qry_i, t, NEG)) for j, t in enumerate(tiles(h, base))]
        m8.append(functools.reduce(jnp.maximum, own) * scale)
    m8 = lax.fori_loop(0, cur, max_body, tuple(m8))
    m_log2 = [jnp.max(m, axis=0, keepdims=True) * LOG2E for m in m8]

    def store_p(h, start, j, p):
        p_ref[h, pl.ds(pl.multiple_of(start + j * sub, sub), sub), :] = p.astype(p_ref.dtype)

    def prob_body(n, l8):
        out = []
        for h in range(heads):
            shift = bias_ref[h, pl.ds(n, 1), :] * LOG2E - m_log2[h]
            acc = l8[h]
            for j, t in enumerate(tiles(h, n * blk)):
                p = jnp.exp2(t * exp_scale + shift)
                store_p(h, n * blk, j, p)
                acc = acc + fold_sum(p)
            out.append(acc)
        return tuple(out)

    l8 = []
    for h in range(heads):
        acc = jnp.zeros((SUBLANES, blk), F32)
        for j, t in enumerate(tiles(h, base)):
            p = jnp.where(key_i[j] <= qry_i, jnp.exp2(t * exp_scale - m_log2[h]), 0.0)
            store_p(h, base, j, p)
            acc = acc + fold_sum(p)
        l8.append(acc)
    l8 = lax.fori_loop(0, cur, prob_body, tuple(l8))

    def zero_body(n, carry):
        for h in range(heads):
            for j in range(blk // sub):
                store_p(h, n * blk, j, jnp.zeros((sub, blk), F32))
        return carry

    lax.fori_loop(cur + 1, jnp.where(need_hi, n_blocks, lo_blocks), zero_body, 0)
    for h in range(heads):
        acc_ref[h] = _dot(vt_ref[h, :, 0:lo], p_ref[h, 0:lo, :])

    if lo < seq:
        @pl.when(need_hi)
        def _():
            for h in range(heads):
                acc_ref[h] += _dot(vt_ref[h, :, lo:seq], p_ref[h, lo:seq, :])

    for h in range(heads):
        l_sum = jnp.sum(l8[h], axis=0, keepdims=True)
        o_ref[:, cols[h]] = (acc_ref[h] / l_sum).T.astype(o_ref.dtype)


def moba_attention(proj, batch, seq, heads=2):
    assert seq % MOBA_BLOCK == 0 and MOBA_HEADS % heads == 0
    n_blocks = seq // MOBA_BLOCK
    rows = -(-n_blocks // BF16_SUBLANES) * BF16_SUBLANES
    width = heads * HEAD_DIM
    qb, kb = C_MQ // width, C_MK // width
    v_t = proj[:, :, C_MV:C_MV + MOBA_WIDTH].reshape(batch, seq, MOBA_HEADS, HEAD_DIM).transpose(0, 2, 3, 1)
    kern = functools.partial(_moba_kernel, n_blocks=n_blocks, topk=min(MOBA_TOPK, n_blocks), heads=heads)
    return pl.pallas_call(
        kern,
        grid=(batch, MOBA_HEADS // heads, n_blocks),
        in_specs=[pl.BlockSpec((None, MOBA_BLOCK, width), lambda b, h, i: (b, i, qb + h)),
                  pl.BlockSpec((None, seq, width), lambda b, h, i: (b, 0, kb + h)),
                  pl.BlockSpec((None, heads, HEAD_DIM, seq), lambda b, h, i: (b, h, 0, 0))],
        out_specs=pl.BlockSpec((None, MOBA_BLOCK, width), lambda b, h, i: (b, i, h)),
        out_shape=jax.ShapeDtypeStruct((batch, seq, MOBA_WIDTH), BF16),
        scratch_shapes=[pltpu.VMEM((heads, rows, HEAD_DIM), F32), pltpu.VMEM((heads, rows, MOBA_BLOCK), F32),
                        pltpu.VMEM((heads, seq, MOBA_BLOCK), F32), pltpu.VMEM((heads, seq, MOBA_BLOCK), BF16),
                        pltpu.VMEM((heads, HEAD_DIM, MOBA_BLOCK), F32)],
        compiler_params=_params("parallel", "parallel", "arbitrary"),
        name="moba",
    )(proj, proj, v_t)


def _nsa_compress_kernel(x_ref, pe_ref, w1_ref, w2_ref, o_ref):
    x = x_ref[...].astype(F32)
    half = x.shape[1]
    lo = _dot((x + pe_ref[:, :half]).astype(BF16), w1_ref[:half, :])
    hi = _dot((x + pe_ref[:, half:]).astype(BF16), w1_ref[half:, :])
    pre = lo + pltpu.roll(hi, hi.shape[0] - 1, 0)
    hid = pre * jax.nn.sigmoid(pre)
    o_ref[...] = _dot(hid.astype(BF16), w2_ref[...]).astype(o_ref.dtype)


def nsa_compress(proj, cmp_pe, cmp_w1, cmp_w2, batch, seq):
    n16 = seq // NSA_CMP_STRIDE
    width = NSA_CMP_STRIDE * HEAD_DIM
    x = proj[:, :, C_NKV:C_NKV + 2 * NSA_KV_WIDTH].reshape(batch, seq, 2, NSA_KV_HEADS, HEAD_DIM)
    x = x.transpose(0, 2, 3, 1, 4).reshape(batch, 2, NSA_KV_HEADS, n16, width)
    pe = cmp_pe.reshape(2, 1, NSA_CMP_LEN * HEAD_DIM).astype(F32)
    return pl.pallas_call(
        _nsa_compress_kernel,
        grid=(batch, 2, NSA_KV_HEADS),
        in_specs=[pl.BlockSpec((None, None, None, n16, width), lambda b, c, g: (b, c, g, 0, 0)),
                  pl.BlockSpec((None, 1, 2 * width), lambda b, c, g: (c, 0, 0)),
                  pl.BlockSpec((None, 2 * width, HEAD_DIM), lambda b, c, g: (c, 0, 0)),
                  pl.BlockSpec((None, HEAD_DIM, HEAD_DIM), lambda b, c, g: (c, 0, 0))],
        out_specs=pl.BlockSpec((None, None, None, n16, HEAD_DIM), lambda b, c, g: (b, c, g, 0, 0)),
        out_shape=jax.ShapeDtypeStruct((batch, 2, NSA_KV_HEADS, n16, HEAD_DIM), BF16),
        compiler_params=_params("parallel", "parallel", "parallel"),
        name="nsa_compress",
    )(x, pe, cmp_w1.astype(BF16), cmp_w2.astype(BF16))


def _nsa_kernel(q_ref, kc_ref, vc_ref, ks_ref, vs_ref, kw_ref, vw_ref, small_ref, o_ref, *,
                tq, tk, n_cmp, n_slc, topn, win_len):
    g = pl.program_id(1)
    qi = pl.program_id(2)
    seq = ks_ref.shape[0]
    rows = NSA_GROUP * tq
    scale = HEAD_DIM ** -0.5
    start = qi * tq

    q = jnp.concatenate([q_ref[:, r * HEAD_DIM:(r + 1) * HEAD_DIM] for r in range(NSA_GROUP)], axis=0)
    pos1 = start + lax.broadcasted_iota(I32, (tq, 1), 0)
    pos = jnp.concatenate([pos1] * NSA_GROUP, axis=0)

    n16 = kc_ref.shape[0]
    n_idx = lax.broadcasted_iota(I32, (rows, n16), 1)
    in_range = n_idx < n_cmp
    cmask = (n_idx * NSA_CMP_STRIDE + (NSA_CMP_LEN - 1) <= pos) & in_range
    s_c = jnp.where(cmask, _dot_nt(q, kc_ref[...]) * scale, NEG)
    e_c = jnp.where(in_range, jnp.exp(s_c - jnp.max(s_c, axis=-1, keepdims=True)), 0.0)
    p_c = jnp.where(cmask, e_c / jnp.sum(e_c, axis=-1, keepdims=True), 0.0)
    o_c = _dot(p_c.astype(BF16), vc_ref[...])

    p_sum = p_c[0:tq]
    for r in range(1, NSA_GROUP):
        p_sum = p_sum + p_c[r * tq:(r + 1) * tq]
    cn = lax.broadcasted_iota(I32, (n16, LANES), 0)
    cj = lax.broadcasted_iota(I32, (n16, LANES), 1)
    overlap = ((cn * NSA_CMP_STRIDE < (cj + 1) * NSA_SLC_BLOCK)
               & (cn * NSA_CMP_STRIDE + (NSA_CMP_LEN - 1) >= cj * NSA_SLC_BLOCK)
               & (cn < n_cmp) & (cj < n_slc)).astype(BF16)
    p_hi = p_sum.astype(BF16)
    p_lo = (p_sum - p_hi.astype(F32)).astype(BF16)
    imp = _dot(p_hi, overlap) + _dot(p_lo, overlap)
    j_idx = lax.broadcasted_iota(I32, (tq, LANES), 1)
    cur_blk = pos1 // NSA_SLC_BLOCK
    forced = (j_idx == 0) | (j_idx == cur_blk) | (j_idx == cur_blk - 1)
    imp = jnp.where(forced, FORCE, imp)
    imp = jnp.where(j_idx > cur_blk, NEG, imp)
    chosen1 = ((_rank_desc(imp, n_slc) < topn) & (j_idx <= cur_blk)).astype(BF16)
    chosen = jnp.concatenate([chosen1] * NSA_GROUP, axis=0)

    def slc_chunk(c, carry):
        k0 = pl.multiple_of(c * tk, tk)
        ej = lax.broadcasted_iota(I32, (LANES, tk), 0)
        ekey = k0 + lax.broadcasted_iota(I32, (LANES, tk), 1)
        expand = (ekey // NSA_SLC_BLOCK == ej).astype(BF16)
        key = k0 + lax.broadcasted_iota(I32, (rows, tk), 1)
        mask = (_dot(chosen, expand) > 0.5) & (key <= pos)
        return _softmax_step(carry, q, ks_ref[pl.ds(k0, tk), :], vs_ref[pl.ds(k0, tk), :], mask, scale)

    n_chunks = (start + tq + tk - 1) // tk
    _, l_s, acc_s = lax.fori_loop(0, n_chunks, slc_chunk, _softmax_init(rows, HEAD_DIM))
    o_s = acc_s / l_s

    w0 = pl.multiple_of(jnp.maximum(start + tq - win_len, 0), tq)
    carry = _softmax_init(rows, HEAD_DIM)
    for c in range(win_len // tq):
        k0 = pl.multiple_of(w0 + c * tq, tq)
        key = k0 + lax.broadcasted_iota(I32, (rows, tq), 1)
        mask = (key <= pos) & (key > pos - NSA_WINDOW)
        carry = _softmax_step(carry, q, kw_ref[pl.ds(k0, tq), :], vw_ref[pl.ds(k0, tq), :], mask, scale)
    o_w = carry[2] / carry[1]

    small = small_ref[...].astype(F32)
    for r in range(NSA_GROUP):
        lane0 = SMALL_NG_LANE + (g * NSA_GROUP + r) * 3
        sl = slice(r * tq, (r + 1) * tq)
        out = (jax.nn.sigmoid(_lane_column(small, lane0)) * o_c[sl]
               + jax.nn.sigmoid(_lane_column(small, lane0 + 1)) * o_s[sl]
               + jax.nn.sigmoid(_lane_column(small, lane0 + 2)) * o_w[sl])
        o_ref[:, r * HEAD_DIM:(r + 1) * HEAD_DIM] = out.astype(o_ref.dtype)


def nsa_attention(proj, kv_c, batch, seq, tq=128, tk=256):
    tk = min(tk, seq)
    assert seq % tk == 0 and tk % tq == 0 and tk % NSA_SLC_BLOCK == 0
    n_cmp = (seq - NSA_CMP_LEN) // NSA_CMP_STRIDE + 1
    n_slc = seq // NSA_SLC_BLOCK
    assert n_cmp <= LANES and n_slc <= LANES
    win_len = min(NSA_WINDOW + tq, seq)
    n16 = seq // NSA_CMP_STRIDE
    gw = NSA_GROUP * HEAD_DIM
    nkv = C_NKV // HEAD_DIM

    def kv_spec(slot):
        return pl.BlockSpec((None, seq, HEAD_DIM),
                            lambda b, g, i, slot=slot: (b, 0, nkv + slot * NSA_KV_HEADS + g))

    def cmp_spec(c):
        return pl.BlockSpec((None, None, None, n16, HEAD_DIM), lambda b, g, i, c=c: (b, c, g, 0, 0))

    kern = functools.partial(_nsa_kernel, tq=tq, tk=tk, n_cmp=n_cmp, n_slc=n_slc,
                             topn=min(NSA_TOPN, n_slc), win_len=win_len)
    return pl.pallas_call(
        kern,
        grid=(batch, NSA_KV_HEADS, seq // tq),
        in_specs=[pl.BlockSpec((None, tq, gw), lambda b, g, i: (b, i, C_NQ // gw + g)),
                  cmp_spec(0), cmp_spec(1), kv_spec(2), kv_spec(3), kv_spec(4), kv_spec(5),
                  pl.BlockSpec((None, tq, LANES), lambda b, g, i: (b, i, C_SMALL // LANES))],
        out_specs=pl.BlockSpec((None, tq, gw), lambda b, g, i: (b, i, g)),
        out_shape=jax.ShapeDtypeStruct((batch, seq, NSA_WIDTH), BF16),
        compiler_params=_params("parallel", "parallel", "arbitrary"),
        name="nsa",
    )(proj, kv_c, kv_c, proj, proj, proj, proj, proj)


GLA_SUB = 16


def _gla_kernel(q_ref, k_ref, v_ref, gg_ref, small_ref, wa_ref, ba_ref, ng_ref, o_ref,
                state_ref, attn_ref, *, chunk):
    @pl.when(pl.program_id(2) == 0)
    def _():
        state_ref[...] = jnp.zeros_like(state_ref)

    q = q_ref[...].astype(F32) * (GLA_DK ** -0.5)
    k = k_ref[...].astype(F32)
    v = v_ref[...]
    z = _dot(small_ref[...], wa_ref[...]) + ba_ref[...]
    log_a = (jnp.minimum(z, 0.0) - jnp.log(1.0 + jnp.exp(-jnp.abs(z)))) * (1.0 / GLA_TAU)
    tri = (lax.broadcasted_iota(I32, (chunk, chunk), 1)
           <= lax.broadcasted_iota(I32, (chunk, chunk), 0)).astype(BF16)
    b = _dot_split3(tri, log_a)

    attn_ref[...] = jnp.zeros_like(attn_ref)
    t_idx = lax.broadcasted_iota(I32, (GLA_SUB, 1), 0)
    s_lane = lax.broadcasted_iota(I32, (GLA_SUB, GLA_SUB), 1)
    for i in range(chunk // GLA_SUB):
        r0 = i * GLA_SUB
        bi, qi, ki = b[r0:r0 + GLA_SUB], q[r0:r0 + GLA_SUB], k[r0:r0 + GLA_SUB]
        diag = jnp.zeros((GLA_SUB, GLA_SUB), F32)
        for s in range(GLA_SUB):
            decay = jnp.exp(jnp.where(t_idx >= s, bi - bi[s:s + 1], NEG))
            col = jnp.sum(qi * ki[s:s + 1] * decay, axis=-1, keepdims=True)
            diag = jnp.where(s_lane == s, col, diag)
        attn_ref[r0:r0 + GLA_SUB, r0:r0 + GLA_SUB] = diag
        if i > 0:
            ref_b = b[r0:r0 + 1]
            q_dec = (qi * jnp.exp(bi - ref_b)).astype(BF16)
            k_dec = (k[:r0] * jnp.exp(ref_b - b[:r0])).astype(BF16)
            attn_ref[r0:r0 + GLA_SUB, 0:r0] = _dot_nt(q_dec, k_dec)

    state_t = state_ref[...]
    o = _dot(attn_ref[...].astype(BF16), v) + _dot_nt((q * jnp.exp(b)).astype(BF16), state_t.astype(BF16))
    b_last = b[chunk - 1:chunk]
    k_dec = (k * jnp.exp(b_last - b)).astype(BF16)
    state_ref[...] = state_t * jnp.exp(b_last) + _dot(v.astype(F32).T.astype(BF16), k_dec)

    gate = gg_ref[...].astype(F32)
    o_ref[...] = (_rms(o, ng_ref[...]) * (gate * jax.nn.sigmoid(gate))).astype(o_ref.dtype)


def gla_attention(proj, gla_wa, gla_ba, gla_norm_g, batch, seq, chunk=128):
    chunk = min(chunk, seq)
    assert seq % chunk == 0 and chunk % GLA_SUB == 0
    wa = jnp.zeros((LANES, GLA_KEY_WIDTH), BF16).at[SMALL_GA_LANE:SMALL_GA_LANE + GLA_RANK].set(gla_wa.astype(BF16))
    kern = functools.partial(_gla_kernel, chunk=chunk)
    return pl.pallas_call(
        kern,
        grid=(batch, GLA_HEADS, seq // chunk),
        in_specs=[pl.BlockSpec((None, chunk, GLA_DK), lambda b, h, c: (b, c, C_GQ // GLA_DK + h)),
                  pl.BlockSpec((None, chunk, GLA_DK), lambda b, h, c: (b, c, C_GK // GLA_DK + h)),
                  pl.BlockSpec((None, chunk, GLA_DV), lambda b, h, c: (b, c, C_GV // GLA_DV + h)),
                  pl.BlockSpec((None, chunk, GLA_DV), lambda b, h, c: (b, c, C_GG // GLA_DV + h)),
                  pl.BlockSpec((None, chunk, LANES), lambda b, h, c: (b, c, C_SMALL // LANES)),
                  pl.BlockSpec((LANES, GLA_DK), lambda b, h, c: (0, h)),
                  pl.BlockSpec((1, GLA_DK), lambda b, h, c: (0, h)),
                  pl.BlockSpec((1, GLA_DV), lambda b, h, c: (0, 0))],
        out_specs=pl.BlockSpec((None, chunk, GLA_DV), lambda b, h, c: (b, c, h)),
        out_shape=jax.ShapeDtypeStruct((batch, seq, GLA_WIDTH), BF16),
        scratch_shapes=[pltpu.VMEM((GLA_DV, GLA_DK), F32), pltpu.VMEM((chunk, chunk), F32)],
        compiler_params=_params("parallel", "parallel", "arbitrary"),
        name="gla",
    )(proj, proj, proj, proj, proj, wa, gla_ba.reshape(1, GLA_KEY_WIDTH).astype(F32),
      gla_norm_g.reshape(1, GLA_DV).astype(F32))


def _merge_kernel(om_ref, on_ref, og_ref, wm_ref, wn_ref, wg_ref, gm_ref, gn_ref, gl_ref, o_ref):
    def gated(gate_ref, a_ref, w_ref):
        return jax.nn.sigmoid(gate_ref[...].astype(F32)) * _dot(a_ref[...], w_ref[...])

    o_ref[...] = (gated(gm_ref, om_ref, wm_ref) + gated(gn_ref, on_ref, wn_ref)
                  + gated(gl_ref, og_ref, wg_ref)).astype(o_ref.dtype)


def merge_branches(o_m, o_n, o_g, w_m, w_n, w_g, proj2d, tm=1024, tn=512):
    m = o_m.shape[0]
    tm = min(tm, m)
    assert C_MG % tn == 0 and D_MODEL % tn == 0

    def gate_spec(c):
        return pl.BlockSpec((tm, tn), lambda i, j, c=c: (i, (C_MG + c * D_MODEL) // tn + j))

    def act_spec(width):
        return pl.BlockSpec((tm, width), lambda i, j: (i, 0))

    def w_spec(width):
        return pl.BlockSpec((width, tn), lambda i, j: (0, j))

    return pl.pallas_call(
        _merge_kernel,
        grid=(m // tm, D_MODEL // tn),
        in_specs=[act_spec(MOBA_WIDTH), act_spec(NSA_WIDTH), act_spec(GLA_WIDTH),
                  w_spec(MOBA_WIDTH), w_spec(NSA_WIDTH), w_spec(GLA_WIDTH),
                  gate_spec(0), gate_spec(1), gate_spec(2)],
        out_specs=pl.BlockSpec((tm, tn), lambda i, j: (i, j)),
        out_shape=jax.ShapeDtypeStruct((m, D_MODEL), BF16),
        compiler_params=_params("parallel", "parallel"),
        name="merge",
    )(o_m, o_n, o_g, w_m, w_n, w_g, proj2d, proj2d, proj2d)


def _router_kernel(x_ref, g_ref, w_ref, b_ref, h_ref, logit_ref):
    h = _rms(x_ref[...], g_ref[...])
    h_ref[...] = h.astype(h_ref.dtype)
    h1 = h.astype(BF16)
    r1 = h - h1.astype(F32)
    h2 = r1.astype(BF16)
    h3 = (r1 - h2.astype(F32)).astype(BF16)
    w1 = w_ref[0]
    w2 = w_ref[1]
    logit_ref[...] = (_dot(h1, w1) + (_dot(h1, w2) + _dot(h2, w1)) + (_dot(h2, w2) + _dot(h3, w1))) + b_ref[...]


def router(x2d, norm_g, rg_w, rg_b, re_w, re_b, tm=512):
    m, d = x2d.shape
    tm = min(tm, m)
    n_real = MOE_GROUPS + MOE_EXPERTS
    w = jnp.zeros((d, ROUTER_LANES), F32).at[:, :n_real].set(jnp.concatenate([rg_w, re_w], axis=1))
    w_hi = w.astype(BF16)
    w_lo = (w - w_hi.astype(F32)).astype(BF16)
    bias = jnp.zeros((1, ROUTER_LANES), F32).at[0, :n_real].set(jnp.concatenate([rg_b, re_b]))
    return pl.pallas_call(
        _router_kernel,
        grid=(m // tm,),
        in_specs=[pl.BlockSpec((tm, d), lambda i: (i, 0)), pl.BlockSpec((1, d), lambda i: (0, 0)),
                  pl.BlockSpec((2, d, ROUTER_LANES), lambda i: (0, 0, 0)),
                  pl.BlockSpec((1, ROUTER_LANES), lambda i: (0, 0))],
        out_specs=[pl.BlockSpec((tm, d), lambda i: (i, 0)), pl.BlockSpec((tm, ROUTER_LANES), lambda i: (i, 0))],
        out_shape=[jax.ShapeDtypeStruct((m, d), BF16), jax.ShapeDtypeStruct((m, ROUTER_LANES), F32)],
        compiler_params=_params("parallel"),
        name="router",
    )(x2d, norm_g.reshape(1, d).astype(F32), jnp.stack([w_hi, w_lo]), bias)


def _expert_kernel(blk_e_ref, n_used_ref, x_ref, wg_ref, wu_ref, wd_ref, o_ref, wg_bf, wu_bf, wd_bf):
    i = pl.program_id(0)
    new_expert = (i == 0) | (blk_e_ref[i] != blk_e_ref[jnp.maximum(i - 1, 0)])

    @pl.when(new_expert)
    def _():
        wg_bf[...] = wg_ref[...].astype(BF16)
        wu_bf[...] = wu_ref[...].astype(BF16)
        wd_bf[...] = wd_ref[...].astype(BF16)

    @pl.when(i < n_used_ref[0])
    def _():
        x = x_ref[...]
        gate = _dot(x, wg_bf[...])
        hid = gate * jax.nn.sigmoid(gate) * _dot(x, wu_bf[...])
        o_ref[...] = _dot(hid.astype(BF16), wd_bf[...]).astype(o_ref.dtype)

    @pl.when(i >= n_used_ref[0])
    def _():
        o_ref[...] = jnp.zeros_like(o_ref)


def expert_blocks(xs, blk_e, n_used, w_gate, w_up, w_down):
    p, d = xs.shape
    ff = w_gate.shape[2]
    n_blk = p // MOE_ROWS
    grid_spec = pltpu.PrefetchScalarGridSpec(
        num_scalar_prefetch=2,
        grid=(n_blk,),
        in_specs=[pl.BlockSpec((MOE_ROWS, d), lambda i, e, n: (i, 0)),
                  pl.BlockSpec((None, d, ff), lambda i, e, n: (e[i], 0, 0)),
                  pl.BlockSpec((None, d, ff), lambda i, e, n: (e[i], 0, 0)),
                  pl.BlockSpec((None, ff, d), lambda i, e, n: (e[i], 0, 0))],
        out_specs=pl.BlockSpec((MOE_ROWS, d), lambda i, e, n: (i, 0)),
        scratch_shapes=[pltpu.VMEM((d, ff), BF16), pltpu.VMEM((d, ff), BF16), pltpu.VMEM((ff, d), BF16)],
    )
    return pl.pallas_call(
        _expert_kernel,
        grid_spec=grid_spec,
        out_shape=jax.ShapeDtypeStruct((p, d), BF16),
        compiler_params=_params("arbitrary"),
        name="experts",
    )(blk_e, n_used, xs, w_gate, w_up, w_down)


def _combine_kernel(x_ref, y0_ref, y1_ref, w_ref, g_ref, o_ref, *, final_norm):
    w = w_ref[...]
    x = x_ref[...] + (w[:, 0:1] * y0_ref[...].astype(F32) + w[:, 1:2] * y1_ref[...].astype(F32))
    o_ref[...] = _rms(x, g_ref[...]) if final_norm else x


def combine(x2d, y0, y1, w, norm_g, final_norm, tm=512):
    m, d = x2d.shape
    tm = min(tm, m)
    row = pl.BlockSpec((tm, d), lambda i: (i, 0))
    return pl.pallas_call(
        functools.partial(_combine_kernel, final_norm=final_norm),
        grid=(m // tm,),
        in_specs=[row, row, row, pl.BlockSpec((tm, MOE_TOPK), lambda i: (i, 0)),
                  pl.BlockSpec((1, d), lambda i: (0, 0))],
        out_specs=row,
        out_shape=jax.ShapeDtypeStruct((m, d), F32),
        compiler_params=_params("parallel"),
        name="combine",
    )(x2d, y0, y1, w, norm_g.reshape(1, d).astype(F32))


def hier_moe(x2d, norm_g, rg_w, rg_b, re_w, re_b, w_gate, w_up, w_down, out_norm_g, final_norm):
    t = x2d.shape[0]
    h, logits = router(x2d, norm_g, rg_w, rg_b, re_w, re_b)
    g_logits = logits[:, :MOE_GROUPS]
    grp = jnp.argmax(g_logits, axis=-1)
    p_grp = jnp.take_along_axis(jax.nn.softmax(g_logits, axis=-1), grp[:, None], axis=1)[:, 0]
    e_logits = logits[:, MOE_GROUPS:MOE_GROUPS + MOE_EXPERTS].reshape(t, MOE_GROUPS, MOE_EXPERTS_PER_GROUP)
    e_in = jnp.take_along_axis(e_logits, grp[:, None, None], axis=1)[:, 0]
    top_v, top_i = lax.top_k(e_in, MOE_TOPK)
    w = jax.nn.softmax(top_v, axis=-1) * p_grp[:, None]
    expert = (grp[:, None] * MOE_EXPERTS_PER_GROUP + top_i).astype(I32)

    a = t * MOE_TOPK
    e_flat = expert.reshape(a)
    onehot = (e_flat[:, None] == jnp.arange(MOE_EXPERTS, dtype=I32)[None, :]).astype(I32)
    running = jnp.cumsum(onehot, axis=0)
    counts = running[-1]
    rank = jnp.take_along_axis(running, e_flat[:, None], axis=1)[:, 0] - 1
    padded = (counts + MOE_ROWS - 1) // MOE_ROWS * MOE_ROWS
    pends = jnp.cumsum(padded)
    dest = (pends - padded)[e_flat] + rank
    p_rows = (a + MOE_EXPERTS * (MOE_ROWS - 1)) // MOE_ROWS * MOE_ROWS
    n_blk = p_rows // MOE_ROWS
    row_tok = jnp.zeros((p_rows,), I32).at[dest].set(jnp.arange(a, dtype=I32) // MOE_TOPK)
    blk_start = jnp.arange(n_blk, dtype=I32) * MOE_ROWS
    blk_e = jnp.minimum(jnp.sum((pends[None, :] <= blk_start[:, None]).astype(I32), axis=1), MOE_EXPERTS - 1)
    n_used = (pends[-1] // MOE_ROWS).astype(I32).reshape(1)

    xs = jnp.take(h, row_tok, axis=0)
    y_rows = expert_blocks(xs, blk_e, n_used, w_gate, w_up, w_down)
    dest2 = dest.reshape(t, MOE_TOPK)
    y0 = jnp.take(y_rows, dest2[:, 0], axis=0)
    y1 = jnp.take(y_rows, dest2[:, 1], axis=0)
    return combine(x2d, y0, y1, w, out_norm_g, final_norm)


def _permute_w_in(w_in):
    d = w_in.shape[0]
    pad = jnp.zeros((d, PROJ_WIDTH - _SRC_END), w_in.dtype)
    return jnp.concatenate([w_in[:, :_SRC_NG], w_in[:, _SRC_GQ:_SRC_GA], w_in[:, _SRC_GG:_SRC_END],
                            w_in[:, _SRC_GA:_SRC_GG], w_in[:, _SRC_NG:_SRC_GQ], pad], axis=1).astype(BF16)


def hybrid_layer(x, norm1_g, w_in, nsa_cmp_pe, nsa_cmp_w1, nsa_cmp_w2, gla_wa, gla_ba, gla_norm_g,
                 w_br_moba, w_br_nsa, w_br_gla, w_out, norm2_g, router_group_w, router_group_b,
                 router_expert_w, router_expert_b, expert_w_gate, expert_w_up, expert_w_down,
                 out_norm_g, final_norm):
    batch, seq, d = x.shape
    t = batch * seq
    x2d = x.reshape(t, d)
    h = rmsnorm(x2d, norm1_g, BF16)
    proj2d = matmul(h, _permute_w_in(w_in), BF16)
    proj = proj2d.reshape(batch, seq, PROJ_WIDTH)
    o_m = moba_attention(proj, batch, seq)
    kv_c = nsa_compress(proj, nsa_cmp_pe, nsa_cmp_w1, nsa_cmp_w2, batch, seq)
    o_n = nsa_attention(proj, kv_c, batch, seq)
    o_g = gla_attention(proj, gla_wa, gla_ba, gla_norm_g, batch, seq)
    merged = merge_branches(o_m.reshape(t, -1), o_n.reshape(t, -1), o_g.reshape(t, -1),
                            w_br_moba.astype(BF16), w_br_nsa.astype(BF16), w_br_gla.astype(BF16), proj2d)
    x2d = matmul_residual(merged, w_out.astype(BF16), x2d)
    x2d = hier_moe(x2d, norm2_g, router_group_w, router_group_b, router_expert_w, router_expert_b,
                   expert_w_gate, expert_w_up, expert_w_down, out_norm_g, final_norm)
    return x2d.reshape(batch, seq, d)


def kernel(x, norm1_g, w_in, nsa_cmp_pe, nsa_cmp_w1, nsa_cmp_w2, gla_wa, gla_ba, gla_norm_g, w_br_moba,
           w_br_nsa, w_br_gla, w_out, norm2_g, router_group_w, router_group_b, router_expert_w,
           router_expert_b, expert_w_gate, expert_w_up, expert_w_down, final_norm_g):
    for l in range(DEPTH):
        x = hybrid_layer(x, norm1_g[l], w_in[l], nsa_cmp_pe[l], nsa_cmp_w1[l], nsa_cmp_w2[l], gla_wa[l],
                         gla_ba[l], gla_norm_g[l], w_br_moba[l], w_br_nsa[l], w_br_gla[l], w_out[l],
                         norm2_g[l], router_group_w[l], router_group_b[l], router_expert_w[l],
                         router_expert_b[l], expert_w_gate[l], expert_w_up[l], expert_w_down[l],
                         final_norm_g, l == DEPTH - 1)
    return x
```

```python
import functools

import jax
import jax.numpy as jnp
import numpy as np
from jax import lax
from jax.experimental import pallas as pl
from jax.experimental.pallas import tpu as pltpu

F32 = jnp.float32
BF16 = jnp.bfloat16
I32 = jnp.int32

D_MODEL = 2048
DEPTH = 2
HEAD_DIM = 128
NEG = -1e30
FORCE = 1e9
EPS = 1e-6
LOG2E = 1.4426950408889634

MOBA_HEADS = 8
MOBA_BLOCK = 256
MOBA_TOPK = 3
MOBA_WIDTH = MOBA_HEADS * HEAD_DIM

NSA_HEADS = 8
NSA_KV_HEADS = 2
NSA_GROUP = NSA_HEADS // NSA_KV_HEADS
NSA_CMP_LEN = 32
NSA_CMP_STRIDE = 16
NSA_SLC_BLOCK = 64
NSA_TOPN = 8
NSA_WINDOW = 512
NSA_WIDTH = NSA_HEADS * HEAD_DIM
NSA_KV_WIDTH = NSA_KV_HEADS * HEAD_DIM

GLA_HEADS = 4
GLA_DK = 128
GLA_DV = 256
GLA_RANK = 16
GLA_TAU = 16.0
GLA_KEY_WIDTH = GLA_HEADS * GLA_DK
GLA_WIDTH = GLA_HEADS * GLA_DV

N_BRANCH = 3
MOE_GROUPS = 4
MOE_EXPERTS_PER_GROUP = 8
MOE_EXPERTS = MOE_GROUPS * MOE_EXPERTS_PER_GROUP
MOE_TOPK = 2
MOE_FF = D_MODEL // 4

LANES = 128
SUBLANES = 8
BF16_SUBLANES = 16
VMEM_LIMIT_BYTES = 56 * 1024 * 1024

_SRC_NG = MOBA_WIDTH * 3 + NSA_WIDTH + 6 * NSA_KV_WIDTH
_SRC_GQ = _SRC_NG + 3 * NSA_HEADS
_SRC_GA = _SRC_GQ + 2 * GLA_KEY_WIDTH + GLA_WIDTH
_SRC_GG = _SRC_GA + GLA_RANK
_SRC_END = _SRC_GG + GLA_WIDTH + N_BRANCH * D_MODEL

C_MQ = 0
C_MK = C_MQ + MOBA_WIDTH
C_MV = C_MK + MOBA_WIDTH
C_NQ = C_MV + MOBA_WIDTH
C_NKV = C_NQ + NSA_WIDTH
C_GQ = C_NKV + 6 * NSA_KV_WIDTH
C_GK = C_GQ + GLA_KEY_WIDTH
C_GV = C_GK + GLA_KEY_WIDTH
C_GG = C_GV + GLA_WIDTH
C_MG = C_GG + GLA_WIDTH
C_SMALL = C_MG + N_BRANCH * D_MODEL
PROJ_WIDTH = C_SMALL + LANES
SMALL_GA_LANE = 0
SMALL_NG_LANE = GLA_RANK

ROUTER_LANES = LANES
MOE_ROWS = 256


def _params(*semantics):
    return pltpu.CompilerParams(dimension_semantics=semantics, vmem_limit_bytes=VMEM_LIMIT_BYTES)


def _dot(a, b):
    return jnp.dot(a, b, preferred_element_type=F32)


def _dot_nt(a, b):
    return lax.dot_general(a, b, (((1,), (1,)), ((), ())), preferred_element_type=F32)


def _dot_split3(a01, x):
    x1 = x.astype(BF16)
    r1 = x - x1.astype(F32)
    x2 = r1.astype(BF16)
    x3 = (r1 - x2.astype(F32)).astype(BF16)
    return _dot(a01, x1) + _dot(a01, x2) + _dot(a01, x3)


def _rms(x, g):
    return x * lax.rsqrt(jnp.mean(x * x, axis=-1, keepdims=True) + EPS) * g


def _rmsnorm_kernel(x_ref, g_ref, o_ref):
    o_ref[...] = _rms(x_ref[...].astype(F32), g_ref[...]).astype(o_ref.dtype)


def rmsnorm(x2d, g, out_dtype, tm=512):
    m, d = x2d.shape
    tm = min(tm, m)
    return pl.pallas_call(
        _rmsnorm_kernel,
        grid=(m // tm,),
        in_specs=[pl.BlockSpec((tm, d), lambda i: (i, 0)), pl.BlockSpec((1, d), lambda i: (0, 0))],
        out_specs=pl.BlockSpec((tm, d), lambda i: (i, 0)),
        out_shape=jax.ShapeDtypeStruct((m, d), out_dtype),
        compiler_params=_params("parallel"),
        name="rmsnorm",
    )(x2d, g.reshape(1, d).astype(F32))


def _matmul_kernel(a_ref, b_ref, o_ref):
    o_ref[...] = _dot(a_ref[...], b_ref[...]).astype(o_ref.dtype)


def matmul(a, b, out_dtype, tm=1024, tn=1152):
    m, k = a.shape
    n = b.shape[1]
    tm, tn = min(tm, m), min(tn, n)
    return pl.pallas_call(
        _matmul_kernel,
        grid=(m // tm, n // tn),
        in_specs=[pl.BlockSpec((tm, k), lambda i, j: (i, 0)), pl.BlockSpec((k, tn), lambda i, j: (0, j))],
        out_specs=pl.BlockSpec((tm, tn), lambda i, j: (i, j)),
        out_shape=jax.ShapeDtypeStruct((m, n), out_dtype),
        compiler_params=_params("parallel", "parallel"),
        name="matmul",
    )(a, b)


def _matmul_residual_kernel(a_ref, b_ref, r_ref, o_ref):
    o_ref[...] = r_ref[...] + _dot(a_ref[...], b_ref[...])


def matmul_residual(a, b, res, tm=1024, tn=1024):
    m, k = a.shape
    n = b.shape[1]
    tm, tn = min(tm, m), min(tn, n)
    return pl.pallas_call(
        _matmul_residual_kernel,
        grid=(m // tm, n // tn),
        in_specs=[pl.BlockSpec((tm, k), lambda i, j: (i, 0)), pl.BlockSpec((k, tn), lambda i, j: (0, j)),
                  pl.BlockSpec((tm, tn), lambda i, j: (i, j))],
        out_specs=pl.BlockSpec((tm, tn), lambda i, j: (i, j)),
        out_shape=jax.ShapeDtypeStruct((m, n), F32),
        compiler_params=_params("parallel", "parallel"),
        name="matmul_residual",
    )(a, b, res)


def _softmax_step(carry, q, k, v, mask, scale):
    m_i, l_i, acc = carry
    s = jnp.where(mask, _dot_nt(q, k) * scale, NEG)
    m_new = jnp.maximum(m_i, jnp.max(s, axis=-1, keepdims=True))
    alpha = jnp.exp(m_i - m_new)
    p = jnp.where(mask, jnp.exp(s - m_new), 0.0)
    l_new = alpha * l_i + jnp.sum(p, axis=-1, keepdims=True)
    acc_new = alpha * acc + _dot(p.astype(v.dtype), v)
    return m_new, l_new, acc_new


def _softmax_init(m, e):
    return jnp.full((m, 1), NEG, F32), jnp.zeros((m, 1), F32), jnp.zeros((m, e), F32)


def _rank_desc(vals, n_candidates):
    lane = lax.broadcasted_iota(I32, vals.shape, 1)
    rank = jnp.zeros(vals.shape, F32)
    for m in range(n_candidates):
        vm = vals[:, m:m + 1]
        beats = (vm > vals) | ((vm == vals) & (lane > m))
        rank = rank + beats.astype(F32)
    return rank


def _softmax_step_t(carry, q, k, v_t, bias, scale):
    m_i, l_i, acc = carry
    s = _dot_nt(k, q) * scale + bias
    m_new = jnp.maximum(m_i, jnp.max(s, axis=0, keepdims=True))
    alpha = jnp.exp(m_i - m_new)
    p = jnp.exp(s - m_new)
    l_new = alpha * l_i + jnp.sum(p, axis=0, keepdims=True)
    acc_new = alpha * acc + _dot(v_t, p.astype(v_t.dtype))
    return m_new, l_new, acc_new


def _softmax_init_t(m, e):
    return jnp.full((1, m), NEG, F32), jnp.zeros((1, m), F32), jnp.zeros((e, m), F32)


def _rank_desc_rows(vals, n_candidates):
    row = lax.broadcasted_iota(I32, vals.shape, 0)
    rank = jnp.zeros(vals.shape, F32)
    for m in range(n_candidates):
        vm = vals[m:m + 1, :]
        beats = (vm > vals) | ((vm == vals) & (row > m))
        rank = rank + beats.astype(F32)
    return rank


def _lane_column(x, idx):
    lane = lax.broadcasted_iota(I32, x.shape, 1)
    return jnp.sum(jnp.where(lane == idx, x, 0.0), axis=-1, keepdims=True)


def _moba_kernel(q_ref, k_ref, vt_ref, o_ref, kmean_ref, bias_ref, s_ref, p_ref, acc_ref, *, n_blocks, topk,
                 heads):
    blk = MOBA_BLOCK
    cur = pl.program_id(2)
    scale = HEAD_DIM ** -0.5
    cols = [slice(h * HEAD_DIM, (h + 1) * HEAD_DIM) for h in range(heads)]

    @pl.when(cur == 0)
    def _():
        kmean_ref[...] = jnp.zeros_like(kmean_ref)
        for h in range(heads):
            for n in range(n_blocks):
                k_blk = k_ref[n * blk:(n + 1) * blk, cols[h]].astype(F32)
                kmean_ref[h, n:n + 1, :] = jnp.sum(k_blk, axis=0, keepdims=True) * (1.0 / blk)

    for h in range(heads):
        gate = _dot_nt(kmean_ref[h].astype(BF16), q_ref[:, cols[h]])
        row = lax.broadcasted_iota(I32, gate.shape, 0)
        valid = row < cur
        gate = jnp.where(valid, gate, NEG)
        chosen = valid & (_rank_desc_rows(gate, n_blocks) < topk)
        bias_ref[h] = jnp.where(chosen, 0.0, NEG)

    seq = k_ref.shape[0]
    lo_blocks = (n_blocks + 1) // 2
    lo = lo_blocks * blk
    need_hi = cur >= lo_blocks

    for h in range(heads):
        s_ref[h, 0:lo, :] = _dot_nt(k_ref[0:lo, cols[h]], q_ref[:, cols[h]])

    if lo < seq:
        @pl.when(need_hi)
        def _():
            for h in range(heads):
                s_ref[h, lo:seq, :] = _dot_nt(k_ref[lo:seq, cols[h]], q_ref[:, cols[h]])

    sub = LANES
    base = pl.multiple_of(cur * blk, blk)
    qry_i = lax.broadcasted_iota(I32, (sub, blk), 1)
    key_i = [j * sub + lax.broadcasted_iota(I32, (sub, blk), 0) for j in range(blk // sub)]
    exp_scale = scale * LOG2E

    def fold_max(x):
        return jnp.max(x.reshape(sub // SUBLANES, SUBLANES, blk), axis=0)

    def fold_sum(x):
        return jnp.sum(x.reshape(sub // SUBLANES, SUBLANES, blk), axis=0)

    def tiles(h, start):
        return [s_ref[h, pl.ds(pl.multiple_of(start + j * sub, sub), sub), :] for j in range(blk // sub)]

    def max_body(n, m8):
        out = []
        for h in range(heads):
            blk_max = functools.reduce(jnp.maximum, [fold_max(t) for t in tiles(h, n * blk)])
            out.append(jnp.maximum(m8[h], blk_max * scale + bias_ref[h, pl.ds(n, 1), :]))
        return tuple(out)

    m8 = []
    for h in range(heads):
        own = [fold_max(jnp.where(key_i[j] <= qry_i, t, NEG)) for j, t in enumerate(tiles(h, base))]
        m8.append(functools.reduce(jnp.maximum, own) * scale)
    m8 = lax.fori_loop(0, cur, max_body, tuple(m8))
    m_log2 = [jnp.max(m, axis=0, keepdims=True) * LOG2E for m in m8]

    def store_p(h, start, j, p):
        p_ref[h, pl.ds(pl.multiple_of(start + j * sub, sub), sub), :] = p.astype(p_ref.dtype)

    def prob_body(n, l8):
        out = []
        for h in range(heads):
            shift = bias_ref[h, pl.ds(n, 1), :] * LOG2E - m_log2[h]
            acc = l8[h]
            for j, t in enumerate(tiles(h, n * blk)):
                p = jnp.exp2(t * exp_scale + shift)
                store_p(h, n * blk, j, p)
                acc = acc + fold_sum(p)
            out.append(acc)
        return tuple(out)

    l8 = []
    for h in range(heads):
        acc = jnp.zeros((SUBLANES, blk), F32)
        for j, t in enumerate(tiles(h, base)):
            p = jnp.where(key_i[j] <= qry_i, jnp.exp2(t * exp_scale - m_log2[h]), 0.0)
            store_p(h, base, j, p)
            acc = acc + fold_sum(p)
        l8.append(acc)
    l8 = lax.fori_loop(0, cur, prob_body, tuple(l8))

    def zero_body(n, carry):
        for h in range(heads):
            for j in range(blk // sub):
                store_p(h, n * blk, j, jnp.zeros((sub, blk), F32))
        return carry

    lax.fori_loop(cur + 1, jnp.where(need_hi, n_blocks, lo_blocks), zero_body, 0)
    for h in range(heads):
        acc_ref[h] = _dot(vt_ref[h, :, 0:lo], p_ref[h, 0:lo, :])

    if lo < seq:
        @pl.when(need_hi)
        def _():
            for h in range(heads):
                acc_ref[h] += _dot(vt_ref[h, :, lo:seq], p_ref[h, lo:seq, :])

    for h in range(heads):
        l_sum = jnp.sum(l8[h], axis=0, keepdims=True)
        o_ref[:, cols[h]] = (acc_ref[h] / l_sum).T.astype(o_ref.dtype)


def moba_attention(proj, batch, seq, heads=2):
    assert seq % MOBA_BLOCK == 0 and MOBA_HEADS % heads == 0
    n_blocks = seq // MOBA_BLOCK
    rows = -(-n_blocks // BF16_SUBLANES) * BF16_SUBLANES
    width = heads * HEAD_DIM
    qb, kb = C_MQ // width, C_MK // width
    v_t = proj[:, :, C_MV:C_MV + MOBA_WIDTH].reshape(batch, seq, MOBA_HEADS, HEAD_DIM).transpose(0, 2, 3, 1)
    kern = functools.partial(_moba_kernel, n_blocks=n_blocks, topk=min(MOBA_TOPK, n_blocks), heads=heads)
    return pl.pallas_call(
        kern,
        grid=(batch, MOBA_HEADS // heads, n_blocks),
        in_specs=[pl.BlockSpec((None, MOBA_BLOCK, width), lambda b, h, i: (b, i, qb + h)),
                  pl.BlockSpec((None, seq, width), lambda b, h, i: (b, 0, kb + h)),
                  pl.BlockSpec((None, heads, HEAD_DIM, seq), lambda b, h, i: (b, h, 0, 0))],
        out_specs=pl.BlockSpec((None, MOBA_BLOCK, width), lambda b, h, i: (b, i, h)),
        out_shape=jax.ShapeDtypeStruct((batch, seq, MOBA_WIDTH), BF16),
        scratch_shapes=[pltpu.VMEM((heads, rows, HEAD_DIM), F32), pltpu.VMEM((heads, rows, MOBA_BLOCK), F32),
                        pltpu.VMEM((heads, seq, MOBA_BLOCK), F32), pltpu.VMEM((heads, seq, MOBA_BLOCK), BF16),
                        pltpu.VMEM((heads, HEAD_DIM, MOBA_BLOCK), F32)],
        compiler_params=_params("parallel", "parallel", "arbitrary"),
        name="moba",
    )(proj, proj, v_t)


def _nsa_compress_kernel(x_ref, pe_ref, w1_ref, w2_ref, o_ref, ot_ref):
    x = x_ref[...].astype(F32)
    half = x.shape[1]
    lo = _dot((x + pe_ref[:, :half]).astype(BF16), w1_ref[:half, :])
    hi = _dot((x + pe_ref[:, half:]).astype(BF16), w1_ref[half:, :])
    pre = lo + pltpu.roll(hi, hi.shape[0] - 1, 0)
    hid = pre * jax.nn.sigmoid(pre)
    out = _dot(hid.astype(BF16), w2_ref[...])
    o_ref[...] = out.astype(o_ref.dtype)
    ot_ref[...] = out.T.astype(ot_ref.dtype)


def nsa_compress(proj, cmp_pe, cmp_w1, cmp_w2, batch, seq):
    n16 = seq // NSA_CMP_STRIDE
    width = NSA_CMP_STRIDE * HEAD_DIM
    x = proj[:, :, C_NKV:C_NKV + 2 * NSA_KV_WIDTH].reshape(batch, seq, 2, NSA_KV_HEADS, HEAD_DIM)
    x = x.transpose(0, 2, 3, 1, 4).reshape(batch, 2, NSA_KV_HEADS, n16, width)
    pe = cmp_pe.reshape(2, 1, NSA_CMP_LEN * HEAD_DIM).astype(F32)
    return pl.pallas_call(
        _nsa_compress_kernel,
        grid=(batch, 2, NSA_KV_HEADS),
        in_specs=[pl.BlockSpec((None, None, None, n16, width), lambda b, c, g: (b, c, g, 0, 0)),
                  pl.BlockSpec((None, 1, 2 * width), lambda b, c, g: (c, 0, 0)),
                  pl.BlockSpec((None, 2 * width, HEAD_DIM), lambda b, c, g: (c, 0, 0)),
                  pl.BlockSpec((None, HEAD_DIM, HEAD_DIM), lambda b, c, g: (c, 0, 0))],
        out_specs=[pl.BlockSpec((None, None, None, n16, HEAD_DIM), lambda b, c, g: (b, c, g, 0, 0)),
                   pl.BlockSpec((None, None, None, HEAD_DIM, n16), lambda b, c, g: (b, c, g, 0, 0))],
        out_shape=[jax.ShapeDtypeStruct((batch, 2, NSA_KV_HEADS, n16, HEAD_DIM), BF16),
                   jax.ShapeDtypeStruct((batch, 2, NSA_KV_HEADS, HEAD_DIM, n16), BF16)],
        compiler_params=_params("parallel", "parallel", "parallel"),
        name="nsa_compress",
    )(x, pe, cmp_w1.astype(BF16), cmp_w2.astype(BF16))


def _nsa_kernel(q_ref, kc_ref, vct_ref, ks_ref, vst_ref, kw_ref, vwt_ref, ngt_ref, o_ref,
                bias_ref, s_ref, p_ref, sw_ref, pw_ref, out_ref, *, tq, span, n_cmp, n_slc, topn, win_len):
    g = pl.program_id(1)
    qi = pl.program_id(2)
    seq = ks_ref.shape[0]
    n16 = kc_ref.shape[0]
    lanes = NSA_GROUP * tq
    blk = NSA_SLC_BLOCK
    scale = HEAD_DIM ** -0.5
    exp_scale = scale * LOG2E
    start = qi * tq
    heads = [slice(r * tq, (r + 1) * tq) for r in range(NSA_GROUP)]
    q = [q_ref[:, r * HEAD_DIM:(r + 1) * HEAD_DIM] for r in range(NSA_GROUP)]

    def per_group(row):
        return jnp.concatenate([row] * NSA_GROUP, axis=1)

    def scores(keys):
        return jnp.concatenate([_dot_nt(keys, q[r]) for r in range(NSA_GROUP)], axis=1)

    def fold(op, x):
        return op(x.reshape(x.shape[0] // SUBLANES, SUBLANES, lanes), axis=0)

    pos1 = start + lax.broadcasted_iota(I32, (1, tq), 1)
    pos = per_group(pos1)

    n_idx = lax.broadcasted_iota(I32, (n16, lanes), 0)
    in_range = n_idx < n_cmp
    cmask = (n_idx * NSA_CMP_STRIDE + (NSA_CMP_LEN - 1) <= pos) & in_range
    s_c = jnp.where(cmask, scores(kc_ref[...]) * scale, NEG)
    e_c = jnp.where(in_range, jnp.exp(s_c - jnp.max(s_c, axis=0, keepdims=True)), 0.0)
    p_c = jnp.where(cmask, e_c / jnp.sum(e_c, axis=0, keepdims=True), 0.0)
    out_ref[0] = _dot(vct_ref[...], p_c.astype(BF16))

    p_sum = functools.reduce(lambda a, b: a + b, [p_c[:, h] for h in heads])
    rows = bias_ref.shape[0]
    oj = lax.broadcasted_iota(I32, (rows, n16), 0)
    on = lax.broadcasted_iota(I32, (rows, n16), 1)
    overlap_t = ((on * NSA_CMP_STRIDE < (oj + 1) * blk) & (on * NSA_CMP_STRIDE + (NSA_CMP_LEN - 1) >= oj * blk)
                 & (on < n_cmp) & (oj < n_slc)).astype(BF16)
    p_hi = p_sum.astype(BF16)
    p_lo = (p_sum - p_hi.astype(F32)).astype(BF16)
    imp = _dot(overlap_t, p_hi) + _dot(overlap_t, p_lo)
    j_idx = lax.broadcasted_iota(I32, (rows, tq), 0)
    cur_blk = pos1 // blk
    forced = (j_idx == 0) | (j_idx == cur_blk) | (j_idx == cur_blk - 1)
    imp = jnp.where(forced, FORCE, imp)
    imp = jnp.where(j_idx > cur_blk, NEG, imp)
    chosen = (_rank_desc_rows(imp, n_slc) < topn) & (j_idx <= cur_blk)
    bias_ref[...] = jnp.where(chosen, 0.0, NEG)

    n_spans = seq // span
    for k in range(n_spans):
        def span_scores(k=k):
            s_ref[k * span:(k + 1) * span, :] = scores(ks_ref[k * span:(k + 1) * span, :])
        if k == 0:
            span_scores()
        else:
            pl.when(start >= k * span)(span_scores)

    per_tile = tq // blk
    first_own = qi * per_tile

    def block_bias(j):
        return per_group(bias_ref[pl.ds(j, 1), :])

    def block_rows(j):
        return pl.ds(pl.multiple_of(j * blk, blk), blk)

    def causal_mask(j):
        key = j * blk + lax.broadcasted_iota(I32, (blk, lanes), 0)
        return key <= pos

    def max_body(i, m8):
        for d in range(per_tile):
            j = i * per_tile + d
            m8 = jnp.maximum(m8, fold(jnp.max, s_ref[block_rows(j), :]) * scale + block_bias(j))
        return m8

    m8 = lax.fori_loop(0, qi, max_body, jnp.full((SUBLANES, lanes), NEG, F32))
    for d in range(per_tile):
        j = first_own + d
        own = jnp.where(causal_mask(j), s_ref[block_rows(j), :], NEG)
        m8 = jnp.maximum(m8, fold(jnp.max, own) * scale + block_bias(j))
    m_log2 = jnp.max(m8, axis=0, keepdims=True) * LOG2E

    def prob_body(i, l8):
        for d in range(per_tile):
            j = i * per_tile + d
            p = jnp.exp2(s_ref[block_rows(j), :] * exp_scale + (block_bias(j) * LOG2E - m_log2))
            p_ref[block_rows(j), :] = p.astype(p_ref.dtype)
            l8 = l8 + fold(jnp.sum, p)
        return l8

    l8 = lax.fori_loop(0, qi, prob_body, jnp.zeros((SUBLANES, lanes), F32))
    for d in range(per_tile):
        j = first_own + d
        p = jnp.exp2(s_ref[block_rows(j), :] * exp_scale + (block_bias(j) * LOG2E - m_log2))
        p = jnp.where(causal_mask(j), p, 0.0)
        p_ref[block_rows(j), :] = p.astype(p_ref.dtype)
        l8 = l8 + fold(jnp.sum, p)

    def zero_body(i, carry):
        p_ref[pl.ds(pl.multiple_of(i * tq, tq), tq), :] = jnp.zeros((tq, lanes), p_ref.dtype)
        return carry

    visible_tiles = (start // span + 1) * (span // tq)
    lax.fori_loop(qi + 1, visible_tiles, zero_body, 0)

    for k in range(n_spans):
        def span_pv(k=k):
            pv = _dot(vst_ref[:, k * span:(k + 1) * span], p_ref[k * span:(k + 1) * span, :])
            if k == 0:
                out_ref[1] = pv
            else:
                out_ref[1] += pv
        if k == 0:
            span_pv()
        else:
            pl.when(start >= k * span)(span_pv)
    out_ref[1] = out_ref[1] / jnp.sum(l8, axis=0, keepdims=True)

    w0 = pl.multiple_of(jnp.maximum(start + tq - win_len, 0), tq)
    sw_ref[...] = scores(kw_ref[pl.ds(w0, win_len), :])
    n_tiles = win_len // blk

    def win_tile(t):
        key = w0 + t * blk + lax.broadcasted_iota(I32, (blk, lanes), 0)
        return sw_ref[t * blk:(t + 1) * blk, :], (key <= pos) & (key > pos - NSA_WINDOW)

    m8 = jnp.full((SUBLANES, lanes), NEG, F32)
    for t in range(n_tiles):
        tile, mask = win_tile(t)
        m8 = jnp.maximum(m8, fold(jnp.max, jnp.where(mask, tile, NEG)))
    m_log2 = jnp.max(m8, axis=0, keepdims=True) * exp_scale
    l8 = jnp.zeros((SUBLANES, lanes), F32)
    for t in range(n_tiles):
        tile, mask = win_tile(t)
        p = jnp.where(mask, jnp.exp2(tile * exp_scale - m_log2), 0.0)
        pw_ref[t * blk:(t + 1) * blk, :] = p.astype(pw_ref.dtype)
        l8 = l8 + fold(jnp.sum, p)
    w_tile0 = w0 // tq
    pv = _dot(vwt_ref[w_tile0], pw_ref[0:tq, :])
    for c in range(1, win_len // tq):
        pv = pv + _dot(vwt_ref[w_tile0 + c], pw_ref[c * tq:(c + 1) * tq, :])
    out_ref[2] = pv / jnp.sum(l8, axis=0, keepdims=True)

    for r in range(NSA_GROUP):
        gate_row = (g * NSA_GROUP + r) * 3
        mix = jnp.zeros((HEAD_DIM, tq), F32)
        for c in range(3):
            mix = mix + jax.nn.sigmoid(ngt_ref[pl.ds(gate_row + c, 1), :]) * out_ref[c, :, heads[r]]
        o_ref[:, r * HEAD_DIM:(r + 1) * HEAD_DIM] = mix.T.astype(o_ref.dtype)


def nsa_attention(proj, kv_c, kv_ct, batch, seq, tq=128):
    assert tq == LANES and seq % tq == 0 and tq % NSA_SLC_BLOCK == 0
    n_cmp = (seq - NSA_CMP_LEN) // NSA_CMP_STRIDE + 1
    n_slc = seq // NSA_SLC_BLOCK
    rows = -(-n_slc // BF16_SUBLANES) * BF16_SUBLANES
    span = max(tq, seq // 4)
    win_len = min(NSA_WINDOW + tq, seq)
    n16 = seq // NSA_CMP_STRIDE
    gw = NSA_GROUP * HEAD_DIM
    lanes = NSA_GROUP * tq
    nkv = C_NKV // HEAD_DIM

    def kv_cols(slot):
        c0 = C_NKV + slot * NSA_KV_WIDTH
        return proj[:, :, c0:c0 + NSA_KV_WIDTH].reshape(batch, seq, NSA_KV_HEADS, HEAD_DIM)

    vs_t = kv_cols(3).transpose(0, 2, 3, 1)
    vw_t = kv_cols(5).reshape(batch, seq // tq, tq, NSA_KV_HEADS, HEAD_DIM).transpose(0, 3, 1, 4, 2)
    n_gates = 3 * NSA_HEADS
    ng_t = proj[:, :, C_SMALL + SMALL_NG_LANE:C_SMALL + SMALL_NG_LANE + n_gates].astype(F32).transpose(0, 2, 1)

    def k_spec(slot):
        return pl.BlockSpec((None, seq, HEAD_DIM),
                            lambda b, g, i, slot=slot: (b, 0, nkv + slot * NSA_KV_HEADS + g))

    kern = functools.partial(_nsa_kernel, tq=tq, span=span, n_cmp=n_cmp, n_slc=n_slc,
                             topn=min(NSA_TOPN, n_slc), win_len=win_len)
    return pl.pallas_call(
        kern,
        grid=(batch, NSA_KV_HEADS, seq // tq),
        in_specs=[pl.BlockSpec((None, tq, gw), lambda b, g, i: (b, i, C_NQ // gw + g)),
                  pl.BlockSpec((None, None, None, n16, HEAD_DIM), lambda b, g, i: (b, 0, g, 0, 0)),
                  pl.BlockSpec((None, None, None, HEAD_DIM, n16), lambda b, g, i: (b, 1, g, 0, 0)),
                  k_spec(2),
                  pl.BlockSpec((None, None, HEAD_DIM, seq), lambda b, g, i: (b, g, 0, 0)),
                  k_spec(4),
                  pl.BlockSpec((None, None, seq // tq, HEAD_DIM, tq), lambda b, g, i: (b, g, 0, 0, 0)),
                  pl.BlockSpec((None, n_gates, tq), lambda b, g, i: (b, 0, i))],
        out_specs=pl.BlockSpec((None, tq, gw), lambda b, g, i: (b, i, g)),
        out_shape=jax.ShapeDtypeStruct((batch, seq, NSA_WIDTH), BF16),
        scratch_shapes=[pltpu.VMEM((rows, tq), F32),
                        pltpu.VMEM((seq, lanes), F32), pltpu.VMEM((seq, lanes), BF16),
                        pltpu.VMEM((win_len, lanes), F32), pltpu.VMEM((win_len, lanes), BF16),
                        pltpu.VMEM((3, HEAD_DIM, lanes), F32)],
        compiler_params=_params("parallel", "parallel", "arbitrary"),
        name="nsa",
    )(proj, kv_c, kv_ct, proj, vs_t, proj, vw_t, ng_t)


GLA_SUB = 16


def _gla_kernel(q_ref, k_ref, v_ref, gg_ref, small_ref, wa_ref, ba_ref, ng_ref, o_ref,
                state_ref, attn_ref, *, chunk):
    @pl.when(pl.program_id(2) == 0)
    def _():
        state_ref[...] = jnp.zeros_like(state_ref)

    q = q_ref[...].astype(F32) * (GLA_DK ** -0.5)
    k = k_ref[...].astype(F32)
    v = v_ref[...]
    z = _dot(small_ref[...], wa_ref[...]) + ba_ref[...]
    log_a = (jnp.minimum(z, 0.0) - jnp.log(1.0 + jnp.exp(-jnp.abs(z)))) * (1.0 / GLA_TAU)
    tri = (lax.broadcasted_iota(I32, (chunk, chunk), 1)
           <= lax.broadcasted_iota(I32, (chunk, chunk), 0)).astype(BF16)
    b = _dot_split3(tri, log_a)

    attn_ref[...] = jnp.zeros_like(attn_ref)
    t_idx = lax.broadcasted_iota(I32, (GLA_SUB, 1), 0)
    s_lane = lax.broadcasted_iota(I32, (GLA_SUB, GLA_SUB), 1)
    for i in range(chunk // GLA_SUB):
        r0 = i * GLA_SUB
        bi, qi, ki = b[r0:r0 + GLA_SUB], q[r0:r0 + GLA_SUB], k[r0:r0 + GLA_SUB]
        diag = jnp.zeros((GLA_SUB, GLA_SUB), F32)
        for s in range(GLA_SUB):
            decay = jnp.exp(jnp.where(t_idx >= s, bi - bi[s:s + 1], NEG))
            col = jnp.sum(qi * ki[s:s + 1] * decay, axis=-1, keepdims=True)
            diag = jnp.where(s_lane == s, col, diag)
        attn_ref[r0:r0 + GLA_SUB, r0:r0 + GLA_SUB] = diag
        if i > 0:
            ref_b = b[r0:r0 + 1]
            q_dec = (qi * jnp.exp(bi - ref_b)).astype(BF16)
            k_dec = (k[:r0] * jnp.exp(ref_b - b[:r0])).astype(BF16)
            attn_ref[r0:r0 + GLA_SUB, 0:r0] = _dot_nt(q_dec, k_dec)

    state_t = state_ref[...]
    o = _dot(attn_ref[...].astype(BF16), v) + _dot_nt((q * jnp.exp(b)).astype(BF16), state_t.astype(BF16))
    b_last = b[chunk - 1:chunk]
    k_dec = (k * jnp.exp(b_last - b)).astype(BF16)
    state_ref[...] = state_t * jnp.exp(b_last) + _dot(v.astype(F32).T.astype(BF16), k_dec)

    gate = gg_ref[...].astype(F32)
    o_ref[...] = (_rms(o, ng_ref[...]) * (gate * jax.nn.sigmoid(gate))).astype(o_ref.dtype)


def gla_attention(proj, gla_wa, gla_ba, gla_norm_g, batch, seq, chunk=128):
    chunk = min(chunk, seq)
    assert seq % chunk == 0 and chunk % GLA_SUB == 0
    wa = jnp.zeros((LANES, GLA_KEY_WIDTH), BF16).at[SMALL_GA_LANE:SMALL_GA_LANE + GLA_RANK].set(gla_wa.astype(BF16))
    kern = functools.partial(_gla_kernel, chunk=chunk)
    return pl.pallas_call(
        kern,
        grid=(batch, GLA_HEADS, seq // chunk),
        in_specs=[pl.BlockSpec((None, chunk, GLA_DK), lambda b, h, c: (b, c, C_GQ // GLA_DK + h)),
                  pl.BlockSpec((None, chunk, GLA_DK), lambda b, h, c: (b, c, C_GK // GLA_DK + h)),
                  pl.BlockSpec((None, chunk, GLA_DV), lambda b, h, c: (b, c, C_GV // GLA_DV + h)),
                  pl.BlockSpec((None, chunk, GLA_DV), lambda b, h, c: (b, c, C_GG // GLA_DV + h)),
                  pl.BlockSpec((None, chunk, LANES), lambda b, h, c: (b, c, C_SMALL // LANES)),
                  pl.BlockSpec((LANES, GLA_DK), lambda b, h, c: (0, h)),
                  pl.BlockSpec((1, GLA_DK), lambda b, h, c: (0, h)),
                  pl.BlockSpec((1, GLA_DV), lambda b, h, c: (0, 0))],
        out_specs=pl.BlockSpec((None, chunk, GLA_DV), lambda b, h, c: (b, c, h)),
        out_shape=jax.ShapeDtypeStruct((batch, seq, GLA_WIDTH), BF16),
        scratch_shapes=[pltpu.VMEM((GLA_DV, GLA_DK), F32), pltpu.VMEM((chunk, chunk), F32)],
        compiler_params=_params("parallel", "parallel", "arbitrary"),
        name="gla",
    )(proj, proj, proj, proj, proj, wa, gla_ba.reshape(1, GLA_KEY_WIDTH).astype(F32),
      gla_norm_g.reshape(1, GLA_DV).astype(F32))


def _merge_kernel(om_ref, on_ref, og_ref, wm_ref, wn_ref, wg_ref, gm_ref, gn_ref, gl_ref, o_ref):
    def gated(gate_ref, a_ref, w_ref):
        return jax.nn.sigmoid(gate_ref[...].astype(F32)) * _dot(a_ref[...], w_ref[...])

    o_ref[...] = (gated(gm_ref, om_ref, wm_ref) + gated(gn_ref, on_ref, wn_ref)
                  + gated(gl_ref, og_ref, wg_ref)).astype(o_ref.dtype)


def merge_branches(o_m, o_n, o_g, w_m, w_n, w_g, proj2d, tm=1024, tn=512):
    m = o_m.shape[0]
    tm = min(tm, m)
    assert C_MG % tn == 0 and D_MODEL % tn == 0

    def gate_spec(c):
        return pl.BlockSpec((tm, tn), lambda i, j, c=c: (i, (C_MG + c * D_MODEL) // tn + j))

    def act_spec(width):
        return pl.BlockSpec((tm, width), lambda i, j: (i, 0))

    def w_spec(width):
        return pl.BlockSpec((width, tn), lambda i, j: (0, j))

    return pl.pallas_call(
        _merge_kernel,
        grid=(m // tm, D_MODEL // tn),
        in_specs=[act_spec(MOBA_WIDTH), act_spec(NSA_WIDTH), act_spec(GLA_WIDTH),
                  w_spec(MOBA_WIDTH), w_spec(NSA_WIDTH), w_spec(GLA_WIDTH),
                  gate_spec(0), gate_spec(1), gate_spec(2)],
        out_specs=pl.BlockSpec((tm, tn), lambda i, j: (i, j)),
        out_shape=jax.ShapeDtypeStruct((m, D_MODEL), BF16),
        compiler_params=_params("parallel", "parallel"),
        name="merge",
    )(o_m, o_n, o_g, w_m, w_n, w_g, proj2d, proj2d, proj2d)


def _router_kernel(x_ref, g_ref, w_ref, b_ref, h_ref, logit_ref):
    h = _rms(x_ref[...], g_ref[...])
    h_ref[...] = h.astype(h_ref.dtype)
    h1 = h.astype(BF16)
    r1 = h - h1.astype(F32)
    h2 = r1.astype(BF16)
    h3 = (r1 - h2.astype(F32)).astype(BF16)
    w1 = w_ref[0]
    w2 = w_ref[1]
    logit_ref[...] = (_dot(h1, w1) + (_dot(h1, w2) + _dot(h2, w1)) + (_dot(h2, w2) + _dot(h3, w1))) + b_ref[...]


def router(x2d, norm_g, rg_w, rg_b, re_w, re_b, tm=512):
    m, d = x2d.shape
    tm = min(tm, m)
    n_real = MOE_GROUPS + MOE_EXPERTS
    w = jnp.zeros((d, ROUTER_LANES), F32).at[:, :n_real].set(jnp.concatenate([rg_w, re_w], axis=1))
    w_hi = w.astype(BF16)
    w_lo = (w - w_hi.astype(F32)).astype(BF16)
    bias = jnp.zeros((1, ROUTER_LANES), F32).at[0, :n_real].set(jnp.concatenate([rg_b, re_b]))
    return pl.pallas_call(
        _router_kernel,
        grid=(m // tm,),
        in_specs=[pl.BlockSpec((tm, d), lambda i: (i, 0)), pl.BlockSpec((1, d), lambda i: (0, 0)),
                  pl.BlockSpec((2, d, ROUTER_LANES), lambda i: (0, 0, 0)),
                  pl.BlockSpec((1, ROUTER_LANES), lambda i: (0, 0))],
        out_specs=[pl.BlockSpec((tm, d), lambda i: (i, 0)), pl.BlockSpec((tm, ROUTER_LANES), lambda i: (i, 0))],
        out_shape=[jax.ShapeDtypeStruct((m, d), BF16), jax.ShapeDtypeStruct((m, ROUTER_LANES), F32)],
        compiler_params=_params("parallel"),
        name="router",
    )(x2d, norm_g.reshape(1, d).astype(F32), jnp.stack([w_hi, w_lo]), bias)


def _expert_kernel(blk_e_ref, n_used_ref, x_ref, wg_ref, wu_ref, wd_ref, o_ref, wg_bf, wu_bf, wd_bf):
    i = pl.program_id(0)
    new_expert = (i == 0) | (blk_e_ref[i] != blk_e_ref[jnp.maximum(i - 1, 0)])

    @pl.when(new_expert)
    def _():
        wg_bf[...] = wg_ref[...].astype(BF16)
        wu_bf[...] = wu_ref[...].astype(BF16)
        wd_bf[...] = wd_ref[...].astype(BF16)

    @pl.when(i < n_used_ref[0])
    def _():
        x = x_ref[...]
        gate = _dot(x, wg_bf[...])
        hid = gate * jax.nn.sigmoid(gate) * _dot(x, wu_bf[...])
        o_ref[...] = _dot(hid.astype(BF16), wd_bf[...]).astype(o_ref.dtype)

    @pl.when(i >= n_used_ref[0])
    def _():
        o_ref[...] = jnp.zeros_like(o_ref)


def expert_blocks(xs, blk_e, n_used, layer, w_gate, w_up, w_down):
    p, d = xs.shape
    ff = w_gate.shape[3]
    n_blk = p // MOE_ROWS
    grid_spec = pltpu.PrefetchScalarGridSpec(
        num_scalar_prefetch=2,
        grid=(n_blk,),
        in_specs=[pl.BlockSpec((MOE_ROWS, d), lambda i, e, n: (i, 0)),
                  pl.BlockSpec((None, None, d, ff), lambda i, e, n: (layer, e[i], 0, 0)),
                  pl.BlockSpec((None, None, d, ff), lambda i, e, n: (layer, e[i], 0, 0)),
                  pl.BlockSpec((None, None, ff, d), lambda i, e, n: (layer, e[i], 0, 0))],
        out_specs=pl.BlockSpec((MOE_ROWS, d), lambda i, e, n: (i, 0)),
        scratch_shapes=[pltpu.VMEM((d, ff), BF16), pltpu.VMEM((d, ff), BF16), pltpu.VMEM((ff, d), BF16)],
    )
    return pl.pallas_call(
        _expert_kernel,
        grid_spec=grid_spec,
        out_shape=jax.ShapeDtypeStruct((p, d), BF16),
        compiler_params=_params("arbitrary"),
        name="experts",
    )(blk_e, n_used, xs, w_gate, w_up, w_down)


def _combine_kernel(x_ref, y0_ref, y1_ref, w_ref, g_ref, o_ref, *, final_norm):
    w = w_ref[...]
    x = x_ref[...] + (w[:, 0:1] * y0_ref[...].astype(F32) + w[:, 1:2] * y1_ref[...].astype(F32))
    o_ref[...] = _rms(x, g_ref[...]) if final_norm else x


def combine(x2d, y0, y1, w, norm_g, final_norm, tm=512):
    m, d = x2d.shape
    tm = min(tm, m)
    row = pl.BlockSpec((tm, d), lambda i: (i, 0))
    return pl.pallas_call(
        functools.partial(_combine_kernel, final_norm=final_norm),
        grid=(m // tm,),
        in_specs=[row, row, row, pl.BlockSpec((tm, MOE_TOPK), lambda i: (i, 0)),
                  pl.BlockSpec((1, d), lambda i: (0, 0))],
        out_specs=row,
        out_shape=jax.ShapeDtypeStruct((m, d), F32),
        compiler_params=_params("parallel"),
        name="combine",
    )(x2d, y0, y1, w, norm_g.reshape(1, d).astype(F32))


def hier_moe(x2d, norm_g, rg_w, rg_b, re_w, re_b, layer, w_gate, w_up, w_down, out_norm_g, final_norm):
    t = x2d.shape[0]
    h, logits = router(x2d, norm_g, rg_w, rg_b, re_w, re_b)
    g_logits = logits[:, :MOE_GROUPS]
    grp = jnp.argmax(g_logits, axis=-1)
    p_grp = jnp.take_along_axis(jax.nn.softmax(g_logits, axis=-1), grp[:, None], axis=1)[:, 0]
    e_logits = logits[:, MOE_GROUPS:MOE_GROUPS + MOE_EXPERTS].reshape(t, MOE_GROUPS, MOE_EXPERTS_PER_GROUP)
    e_in = jnp.take_along_axis(e_logits, grp[:, None, None], axis=1)[:, 0]
    top_v, top_i = lax.top_k(e_in, MOE_TOPK)
    w = jax.nn.softmax(top_v, axis=-1) * p_grp[:, None]
    expert = (grp[:, None] * MOE_EXPERTS_PER_GROUP + top_i).astype(I32)

    a = t * MOE_TOPK
    e_flat = expert.reshape(a)
    onehot = (e_flat[:, None] == jnp.arange(MOE_EXPERTS, dtype=I32)[None, :]).astype(I32)
    running = jnp.cumsum(onehot, axis=0)
    counts = running[-1]
    rank = jnp.take_along_axis(running, e_flat[:, None], axis=1)[:, 0] - 1
    padded = (counts + MOE_ROWS - 1) // MOE_ROWS * MOE_ROWS
    pends = jnp.cumsum(padded)
    dest = (pends - padded)[e_flat] + rank
    p_rows = (a + MOE_EXPERTS * (MOE_ROWS - 1)) // MOE_ROWS * MOE_ROWS
    n_blk = p_rows // MOE_ROWS
    row_tok = jnp.zeros((p_rows,), I32).at[dest].set(jnp.arange(a, dtype=I32) // MOE_TOPK)
    blk_start = jnp.arange(n_blk, dtype=I32) * MOE_ROWS
    blk_e = jnp.minimum(jnp.sum((pends[None, :] <= blk_start[:, None]).astype(I32), axis=1), MOE_EXPERTS - 1)
    n_used = (pends[-1] // MOE_ROWS).astype(I32).reshape(1)

    xs = jnp.take(h, row_tok, axis=0)
    y_rows = expert_blocks(xs, blk_e, n_used, layer, w_gate, w_up, w_down)
    dest2 = dest.reshape(t, MOE_TOPK)
    y0 = jnp.take(y_rows, dest2[:, 0], axis=0)
    y1 = jnp.take(y_rows, dest2[:, 1], axis=0)
    return combine(x2d, y0, y1, w, out_norm_g, final_norm)


def _permute_w_in(w_in):
    d = w_in.shape[0]
    pad = jnp.zeros((d, PROJ_WIDTH - _SRC_END), w_in.dtype)
    return jnp.concatenate([w_in[:, :_SRC_NG], w_in[:, _SRC_GQ:_SRC_GA], w_in[:, _SRC_GG:_SRC_END],
                            w_in[:, _SRC_GA:_SRC_GG], w_in[:, _SRC_NG:_SRC_GQ], pad], axis=1).astype(BF16)


def hybrid_layer(x, norm1_g, w_in, nsa_cmp_pe, nsa_cmp_w1, nsa_cmp_w2, gla_wa, gla_ba, gla_norm_g,
                 w_br_moba, w_br_nsa, w_br_gla, w_out, norm2_g, router_group_w, router_group_b,
                 router_expert_w, router_expert_b, layer, expert_w_gate, expert_w_up, expert_w_down,
                 out_norm_g, final_norm):
    batch, seq, d = x.shape
    t = batch * seq
    x2d = x.reshape(t, d)
    h = rmsnorm(x2d, norm1_g, BF16)
    proj2d = matmul(h, _permute_w_in(w_in), BF16)
    proj = proj2d.reshape(batch, seq, PROJ_WIDTH)
    o_m = moba_attention(proj, batch, seq)
    kv_c, kv_ct = nsa_compress(proj, nsa_cmp_pe, nsa_cmp_w1, nsa_cmp_w2, batch, seq)
    o_n = nsa_attention(proj, kv_c, kv_ct, batch, seq)
    o_g = gla_attention(proj, gla_wa, gla_ba, gla_norm_g, batch, seq)
    merged = merge_branches(o_m.reshape(t, -1), o_n.reshape(t, -1), o_g.reshape(t, -1),
                            w_br_moba.astype(BF16), w_br_nsa.astype(BF16), w_br_gla.astype(BF16), proj2d)
    x2d = matmul_residual(merged, w_out.astype(BF16), x2d)
    x2d = hier_moe(x2d, norm2_g, router_group_w, router_group_b, router_expert_w, router_expert_b,
                   layer, expert_w_gate, expert_w_up, expert_w_down, out_norm_g, final_norm)
    return x2d.reshape(batch, seq, d)


def kernel(x, norm1_g, w_in, nsa_cmp_pe, nsa_cmp_w1, nsa_cmp_w2, gla_wa, gla_ba, gla_norm_g, w_br_moba,
           w_br_nsa, w_br_gla, w_out, norm2_g, router_group_w, router_group_b, router_expert_w,
           router_expert_b, expert_w_gate, expert_w_up, expert_w_down, final_norm_g):
    for l in range(DEPTH):
        x = hybrid_layer(x, norm1_g[l], w_in[l], nsa_cmp_pe[l], nsa_cmp_w1[l], nsa_cmp_w2[l], gla_wa[l],
                         gla_ba[l], gla_norm_g[l], w_br_moba[l], w_br_nsa[l], w_br_gla[l], w_out[l],
                         norm2_g[l], router_group_w[l], router_group_b[l], router_expert_w[l],
                         router_expert_b[l], l, expert_w_gate, expert_w_up, expert_w_down,
                         final_norm_g, l == DEPTH - 1)
    return x
```

```python
import functools

import jax
import jax.numpy as jnp
import numpy as np
from jax import lax
from jax.experimental import pallas as pl
from jax.experimental.pallas import tpu as pltpu

F32 = jnp.float32
BF16 = jnp.bfloat16
I32 = jnp.int32

D_MODEL = 2048
DEPTH = 2
HEAD_DIM = 128
NEG = -1e30
FORCE = 1e9
EPS = 1e-6
LOG2E = 1.4426950408889634

MOBA_HEADS = 8
MOBA_BLOCK = 256
MOBA_TOPK = 3
MOBA_WIDTH = MOBA_HEADS * HEAD_DIM

NSA_HEADS = 8
NSA_KV_HEADS = 2
NSA_GROUP = NSA_HEADS // NSA_KV_HEADS
NSA_CMP_LEN = 32
NSA_CMP_STRIDE = 16
NSA_SLC_BLOCK = 64
NSA_TOPN = 8
NSA_WINDOW = 512
NSA_WIDTH = NSA_HEADS * HEAD_DIM
NSA_KV_WIDTH = NSA_KV_HEADS * HEAD_DIM

GLA_HEADS = 4
GLA_DK = 128
GLA_DV = 256
GLA_RANK = 16
GLA_TAU = 16.0
GLA_KEY_WIDTH = GLA_HEADS * GLA_DK
GLA_WIDTH = GLA_HEADS * GLA_DV

N_BRANCH = 3
MOE_GROUPS = 4
MOE_EXPERTS_PER_GROUP = 8
MOE_EXPERTS = MOE_GROUPS * MOE_EXPERTS_PER_GROUP
MOE_TOPK = 2
MOE_FF = D_MODEL // 4

LANES = 128
SUBLANES = 8
BF16_SUBLANES = 16
VMEM_LIMIT_BYTES = 56 * 1024 * 1024

_SRC_NG = MOBA_WIDTH * 3 + NSA_WIDTH + 6 * NSA_KV_WIDTH
_SRC_GQ = _SRC_NG + 3 * NSA_HEADS
_SRC_GA = _SRC_GQ + 2 * GLA_KEY_WIDTH + GLA_WIDTH
_SRC_GG = _SRC_GA + GLA_RANK
_SRC_END = _SRC_GG + GLA_WIDTH + N_BRANCH * D_MODEL

C_MQ = 0
C_MK = C_MQ + MOBA_WIDTH
C_MV = C_MK + MOBA_WIDTH
C_NQ = C_MV + MOBA_WIDTH
C_NKV = C_NQ + NSA_WIDTH
C_GQ = C_NKV + 6 * NSA_KV_WIDTH
C_GK = C_GQ + GLA_KEY_WIDTH
C_GV = C_GK + GLA_KEY_WIDTH
C_GG = C_GV + GLA_WIDTH
C_MG = C_GG + GLA_WIDTH
C_SMALL = C_MG + N_BRANCH * D_MODEL
PROJ_WIDTH = C_SMALL + LANES
SMALL_GA_LANE = 0
SMALL_NG_LANE = GLA_RANK

ROUTER_LANES = LANES
MOE_ROWS = 256


def _params(*semantics):
    return pltpu.CompilerParams(dimension_semantics=semantics, vmem_limit_bytes=VMEM_LIMIT_BYTES)


def _dot(a, b):
    return jnp.dot(a, b, preferred_element_type=F32)


def _dot_nt(a, b):
    return lax.dot_general(a, b, (((1,), (1,)), ((), ())), preferred_element_type=F32)


def _dot_split3(a01, x):
    x1 = x.astype(BF16)
    r1 = x - x1.astype(F32)
    x2 = r1.astype(BF16)
    x3 = (r1 - x2.astype(F32)).astype(BF16)
    return _dot(a01, x1) + _dot(a01, x2) + _dot(a01, x3)


def _rms(x, g):
    return x * lax.rsqrt(jnp.mean(x * x, axis=-1, keepdims=True) + EPS) * g


def _rmsnorm_kernel(x_ref, g_ref, o_ref):
    o_ref[...] = _rms(x_ref[...].astype(F32), g_ref[...]).astype(o_ref.dtype)


def rmsnorm(x2d, g, out_dtype, tm=512):
    m, d = x2d.shape
    tm = min(tm, m)
    return pl.pallas_call(
        _rmsnorm_kernel,
        grid=(m // tm,),
        in_specs=[pl.BlockSpec((tm, d), lambda i: (i, 0)), pl.BlockSpec((1, d), lambda i: (0, 0))],
        out_specs=pl.BlockSpec((tm, d), lambda i: (i, 0)),
        out_shape=jax.ShapeDtypeStruct((m, d), out_dtype),
        compiler_params=_params("parallel"),
        name="rmsnorm",
    )(x2d, g.reshape(1, d).astype(F32))


def _matmul_kernel(a_ref, b_ref, o_ref):
    o_ref[...] = _dot(a_ref[...], b_ref[...]).astype(o_ref.dtype)


def matmul(a, b, out_dtype, tm=1024, tn=1152):
    m, k = a.shape
    n = b.shape[1]
    tm, tn = min(tm, m), min(tn, n)
    return pl.pallas_call(
        _matmul_kernel,
        grid=(m // tm, n // tn),
        in_specs=[pl.BlockSpec((tm, k), lambda i, j: (i, 0)), pl.BlockSpec((k, tn), lambda i, j: (0, j))],
        out_specs=pl.BlockSpec((tm, tn), lambda i, j: (i, j)),
        out_shape=jax.ShapeDtypeStruct((m, n), out_dtype),
        compiler_params=_params("parallel", "parallel"),
        name="matmul",
    )(a, b)


def _matmul_residual_kernel(a_ref, b_ref, r_ref, o_ref):
    o_ref[...] = r_ref[...] + _dot(a_ref[...], b_ref[...])


def matmul_residual(a, b, res, tm=1024, tn=1024):
    m, k = a.shape
    n = b.shape[1]
    tm, tn = min(tm, m), min(tn, n)
    return pl.pallas_call(
        _matmul_residual_kernel,
        grid=(m // tm, n // tn),
        in_specs=[pl.BlockSpec((tm, k), lambda i, j: (i, 0)), pl.BlockSpec((k, tn), lambda i, j: (0, j)),
                  pl.BlockSpec((tm, tn), lambda i, j: (i, j))],
        out_specs=pl.BlockSpec((tm, tn), lambda i, j: (i, j)),
        out_shape=jax.ShapeDtypeStruct((m, n), F32),
        compiler_params=_params("parallel", "parallel"),
        name="matmul_residual",
    )(a, b, res)


def _softmax_step(carry, q, k, v, mask, scale):
    m_i, l_i, acc = carry
    s = jnp.where(mask, _dot_nt(q, k) * scale, NEG)
    m_new = jnp.maximum(m_i, jnp.max(s, axis=-1, keepdims=True))
    alpha = jnp.exp(m_i - m_new)
    p = jnp.where(mask, jnp.exp(s - m_new), 0.0)
    l_new = alpha * l_i + jnp.sum(p, axis=-1, keepdims=True)
    acc_new = alpha * acc + _dot(p.astype(v.dtype), v)
    return m_new, l_new, acc_new


def _softmax_init(m, e):
    return jnp.full((m, 1), NEG, F32), jnp.zeros((m, 1), F32), jnp.zeros((m, e), F32)


def _rank_desc(vals, n_candidates):
    lane = lax.broadcasted_iota(I32, vals.shape, 1)
    rank = jnp.zeros(vals.shape, F32)
    for m in range(n_candidates):
        vm = vals[:, m:m + 1]
        beats = (vm > vals) | ((vm == vals) & (lane > m))
        rank = rank + beats.astype(F32)
    return rank


def _softmax_step_t(carry, q, k, v_t, bias, scale):
    m_i, l_i, acc = carry
    s = _dot_nt(k, q) * scale + bias
    m_new = jnp.maximum(m_i, jnp.max(s, axis=0, keepdims=True))
    alpha = jnp.exp(m_i - m_new)
    p = jnp.exp(s - m_new)
    l_new = alpha * l_i + jnp.sum(p, axis=0, keepdims=True)
    acc_new = alpha * acc + _dot(v_t, p.astype(v_t.dtype))
    return m_new, l_new, acc_new


def _softmax_init_t(m, e):
    return jnp.full((1, m), NEG, F32), jnp.zeros((1, m), F32), jnp.zeros((e, m), F32)


def _rank_desc_rows(vals, n_candidates):
    row = lax.broadcasted_iota(I32, vals.shape, 0)
    rank = jnp.zeros(vals.shape, F32)
    for m in range(n_candidates):
        vm = vals[m:m + 1, :]
        beats = (vm > vals) | ((vm == vals) & (row > m))
        rank = rank + beats.astype(F32)
    return rank


def _lane_column(x, idx):
    lane = lax.broadcasted_iota(I32, x.shape, 1)
    return jnp.sum(jnp.where(lane == idx, x, 0.0), axis=-1, keepdims=True)


def _moba_kernel(q_ref, k_ref, vt_ref, o_ref, kmean_ref, bias_ref, s_ref, p_ref, acc_ref, *, n_blocks, topk,
                 heads):
    blk = MOBA_BLOCK
    cur = pl.program_id(2)
    scale = HEAD_DIM ** -0.5
    cols = [slice(h * HEAD_DIM, (h + 1) * HEAD_DIM) for h in range(heads)]

    @pl.when(cur == 0)
    def _():
        kmean_ref[...] = jnp.zeros_like(kmean_ref)
        for h in range(heads):
            for n in range(n_blocks):
                k_blk = k_ref[n * blk:(n + 1) * blk, cols[h]].astype(F32)
                kmean_ref[h, n:n + 1, :] = jnp.sum(k_blk, axis=0, keepdims=True) * (1.0 / blk)

    for h in range(heads):
        gate = _dot_nt(kmean_ref[h].astype(BF16), q_ref[:, cols[h]])
        row = lax.broadcasted_iota(I32, gate.shape, 0)
        valid = row < cur
        gate = jnp.where(valid, gate, NEG)
        chosen = valid & (_rank_desc_rows(gate, n_blocks) < topk)
        bias_ref[h] = jnp.where(chosen, 0.0, NEG)

    seq = k_ref.shape[0]
    lo_blocks = (n_blocks + 1) // 2
    lo = lo_blocks * blk
    need_hi = cur >= lo_blocks

    for h in range(heads):
        s_ref[h, 0:lo, :] = _dot_nt(k_ref[0:lo, cols[h]], q_ref[:, cols[h]])

    if lo < seq:
        @pl.when(need_hi)
        def _():
            for h in range(heads):
                s_ref[h, lo:seq, :] = _dot_nt(k_ref[lo:seq, cols[h]], q_ref[:, cols[h]])

    sub = LANES
    base = pl.multiple_of(cur * blk, blk)
    qry_i = lax.broadcasted_iota(I32, (sub, blk), 1)
    key_i = [j * sub + lax.broadcasted_iota(I32, (sub, blk), 0) for j in range(blk // sub)]
    exp_scale = scale * LOG2E

    def fold_max(x):
        return jnp.max(x.reshape(sub // SUBLANES, SUBLANES, blk), axis=0)

    def fold_sum(x):
        return jnp.sum(x.reshape(sub // SUBLANES, SUBLANES, blk), axis=0)

    def tiles(h, start):
        return [s_ref[h, pl.ds(pl.multiple_of(start + j * sub, sub), sub), :] for j in range(blk // sub)]

    def max_body(n, m8):
        out = []
        for h in range(heads):
            blk_max = functools.reduce(jnp.maximum, [fold_max(t) for t in tiles(h, n * blk)])
            out.append(jnp.maximum(m8[h], blk_max * scale + bias_ref[h, pl.ds(n, 1), :]))
        return tuple(out)

    m8 = []
    for h in range(heads):
        own = [fold_max(jnp.where(key_i[j] <= qry_i, t, NEG)) for j, t in enumerate(tiles(h, base))]
        m8.append(functools.reduce(jnp.maximum, own) * scale)
    m8 = lax.fori_loop(0, cur, max_body, tuple(m8))
    m_log2 = [jnp.max(m, axis=0, keepdims=True) * LOG2E for m in m8]

    def store_p(h, start, j, p):
        p_ref[h, pl.ds(pl.multiple_of(start + j * sub, sub), sub), :] = p.astype(p_ref.dtype)

    def prob_body(n, l8):
        out = []
        for h in range(heads):
            shift = bias_ref[h, pl.ds(n, 1), :] * LOG2E - m_log2[h]
            acc = l8[h]
            for j, t in enumerate(tiles(h, n * blk)):
                p = jnp.exp2(t * exp_scale + shift)
                store_p(h, n * blk, j, p)
                acc = acc + fold_sum(p)
            out.append(acc)
        return tuple(out)

    l8 = []
    for h in range(heads):
        acc = jnp.zeros((SUBLANES, blk), F32)
        for j, t in enumerate(tiles(h, base)):
            p = jnp.where(key_i[j] <= qry_i, jnp.exp2(t * exp_scale - m_log2[h]), 0.0)
            store_p(h, base, j, p)
            acc = acc + fold_sum(p)
        l8.append(acc)
    l8 = lax.fori_loop(0, cur, prob_body, tuple(l8))

    def zero_body(n, carry):
        for h in range(heads):
            for j in range(blk // sub):
                store_p(h, n * blk, j, jnp.zeros((sub, blk), F32))
        return carry

    lax.fori_loop(cur + 1, jnp.where(need_hi, n_blocks, lo_blocks), zero_body, 0)
    for h in range(heads):
        acc_ref[h] = _dot(vt_ref[h, :, 0:lo], p_ref[h, 0:lo, :])

    if lo < seq:
        @pl.when(need_hi)
        def _():
            for h in range(heads):
                acc_ref[h] += _dot(vt_ref[h, :, lo:seq], p_ref[h, lo:seq, :])

    for h in range(heads):
        l_sum = jnp.sum(l8[h], axis=0, keepdims=True)
        o_ref[:, cols[h]] = (acc_ref[h] / l_sum).T.astype(o_ref.dtype)


def moba_attention(proj, batch, seq, heads=2):
    assert seq % MOBA_BLOCK == 0 and MOBA_HEADS % heads == 0
    n_blocks = seq // MOBA_BLOCK
    rows = -(-n_blocks // BF16_SUBLANES) * BF16_SUBLANES
    width = heads * HEAD_DIM
    qb, kb = C_MQ // width, C_MK // width
    v_t = proj[:, :, C_MV:C_MV + MOBA_WIDTH].reshape(batch, seq, MOBA_HEADS, HEAD_DIM).transpose(0, 2, 3, 1)
    kern = functools.partial(_moba_kernel, n_blocks=n_blocks, topk=min(MOBA_TOPK, n_blocks), heads=heads)
    return pl.pallas_call(
        kern,
        grid=(batch, MOBA_HEADS // heads, n_blocks),
        in_specs=[pl.BlockSpec((None, MOBA_BLOCK, width), lambda b, h, i: (b, i, qb + h)),
                  pl.BlockSpec((None, seq, width), lambda b, h, i: (b, 0, kb + h)),
                  pl.BlockSpec((None, heads, HEAD_DIM, seq), lambda b, h, i: (b, h, 0, 0))],
        out_specs=pl.BlockSpec((None, MOBA_BLOCK, width), lambda b, h, i: (b, i, h)),
        out_shape=jax.ShapeDtypeStruct((batch, seq, MOBA_WIDTH), BF16),
        scratch_shapes=[pltpu.VMEM((heads, rows, HEAD_DIM), F32), pltpu.VMEM((heads, rows, MOBA_BLOCK), F32),
                        pltpu.VMEM((heads, seq, MOBA_BLOCK), F32), pltpu.VMEM((heads, seq, MOBA_BLOCK), BF16),
                        pltpu.VMEM((heads, HEAD_DIM, MOBA_BLOCK), F32)],
        compiler_params=_params("parallel", "parallel", "arbitrary"),
        name="moba",
    )(proj, proj, v_t)


def _nsa_compress_kernel(x_ref, pe_ref, w1_ref, w2_ref, o_ref, ot_ref):
    x = x_ref[...].astype(F32)
    half = x.shape[1]
    lo = _dot((x + pe_ref[:, :half]).astype(BF16), w1_ref[:half, :])
    hi = _dot((x + pe_ref[:, half:]).astype(BF16), w1_ref[half:, :])
    pre = lo + pltpu.roll(hi, hi.shape[0] - 1, 0)
    hid = pre * jax.nn.sigmoid(pre)
    out = _dot(hid.astype(BF16), w2_ref[...])
    o_ref[...] = out.astype(o_ref.dtype)
    ot_ref[...] = out.T.astype(ot_ref.dtype)


def nsa_compress(proj, cmp_pe, cmp_w1, cmp_w2, batch, seq):
    n16 = seq // NSA_CMP_STRIDE
    width = NSA_CMP_STRIDE * HEAD_DIM
    x = proj[:, :, C_NKV:C_NKV + 2 * NSA_KV_WIDTH].reshape(batch, seq, 2, NSA_KV_HEADS, HEAD_DIM)
    x = x.transpose(0, 2, 3, 1, 4).reshape(batch, 2, NSA_KV_HEADS, n16, width)
    pe = cmp_pe.reshape(2, 1, NSA_CMP_LEN * HEAD_DIM).astype(F32)
    return pl.pallas_call(
        _nsa_compress_kernel,
        grid=(batch, 2, NSA_KV_HEADS),
        in_specs=[pl.BlockSpec((None, None, None, n16, width), lambda b, c, g: (b, c, g, 0, 0)),
                  pl.BlockSpec((None, 1, 2 * width), lambda b, c, g: (c, 0, 0)),
                  pl.BlockSpec((None, 2 * width, HEAD_DIM), lambda b, c, g: (c, 0, 0)),
                  pl.BlockSpec((None, HEAD_DIM, HEAD_DIM), lambda b, c, g: (c, 0, 0))],
        out_specs=[pl.BlockSpec((None, None, None, n16, HEAD_DIM), lambda b, c, g: (b, c, g, 0, 0)),
                   pl.BlockSpec((None, None, None, HEAD_DIM, n16), lambda b, c, g: (b, c, g, 0, 0))],
        out_shape=[jax.ShapeDtypeStruct((batch, 2, NSA_KV_HEADS, n16, HEAD_DIM), BF16),
                   jax.ShapeDtypeStruct((batch, 2, NSA_KV_HEADS, HEAD_DIM, n16), BF16)],
        compiler_params=_params("parallel", "parallel", "parallel"),
        name="nsa_compress",
    )(x, pe, cmp_w1.astype(BF16), cmp_w2.astype(BF16))


def _nsa_kernel(q_ref, kc_ref, vct_ref, ks_ref, vst_ref, kw_ref, vwt_ref, ngt_ref, o_ref,
                bias_ref, s_ref, p_ref, sw_ref, pw_ref, out_ref, *, tq, span, n_cmp, n_slc, topn, win_len):
    g = pl.program_id(1)
    qi = pl.program_id(2)
    seq = ks_ref.shape[0]
    n16 = kc_ref.shape[0]
    lanes = NSA_GROUP * tq
    blk = NSA_SLC_BLOCK
    scale = HEAD_DIM ** -0.5
    exp_scale = scale * LOG2E
    start = qi * tq
    heads = [slice(r * tq, (r + 1) * tq) for r in range(NSA_GROUP)]
    q = [q_ref[:, r * HEAD_DIM:(r + 1) * HEAD_DIM] for r in range(NSA_GROUP)]

    def per_group(row):
        return jnp.concatenate([row] * NSA_GROUP, axis=1)

    def scores(keys):
        return jnp.concatenate([_dot_nt(keys, q[r]) for r in range(NSA_GROUP)], axis=1)

    def fold(op, x):
        return op(x.reshape(x.shape[0] // SUBLANES, SUBLANES, lanes), axis=0)

    pos1 = start + lax.broadcasted_iota(I32, (1, tq), 1)
    pos = per_group(pos1)

    n_idx = lax.broadcasted_iota(I32, (n16, lanes), 0)
    in_range = n_idx < n_cmp
    cmask = (n_idx * NSA_CMP_STRIDE + (NSA_CMP_LEN - 1) <= pos) & in_range
    s_c = jnp.where(cmask, scores(kc_ref[...]) * scale, NEG)
    e_c = jnp.where(in_range, jnp.exp(s_c - jnp.max(s_c, axis=0, keepdims=True)), 0.0)
    p_c = jnp.where(cmask, e_c / jnp.sum(e_c, axis=0, keepdims=True), 0.0)
    out_ref[0] = _dot(vct_ref[...], p_c.astype(BF16))

    p_sum = functools.reduce(lambda a, b: a + b, [p_c[:, h] for h in heads])
    rows = bias_ref.shape[0]
    oj = lax.broadcasted_iota(I32, (rows, n16), 0)
    on = lax.broadcasted_iota(I32, (rows, n16), 1)
    overlap_t = ((on * NSA_CMP_STRIDE < (oj + 1) * blk) & (on * NSA_CMP_STRIDE + (NSA_CMP_LEN - 1) >= oj * blk)
                 & (on < n_cmp) & (oj < n_slc)).astype(BF16)
    p_hi = p_sum.astype(BF16)
    p_lo = (p_sum - p_hi.astype(F32)).astype(BF16)
    imp = _dot(overlap_t, p_hi) + _dot(overlap_t, p_lo)
    j_idx = lax.broadcasted_iota(I32, (rows, tq), 0)
    cur_blk = pos1 // blk
    forced = (j_idx == 0) | (j_idx == cur_blk) | (j_idx == cur_blk - 1)
    imp = jnp.where(forced, FORCE, imp)
    imp = jnp.where(j_idx > cur_blk, NEG, imp)
    chosen = (_rank_desc_rows(imp, n_slc) < topn) & (j_idx <= cur_blk)
    bias_ref[...] = jnp.where(chosen, 0.0, NEG)

    n_spans = seq // span
    for k in range(n_spans):
        def span_scores(k=k):
            s_ref[k * span:(k + 1) * span, :] = scores(ks_ref[k * span:(k + 1) * span, :])
        if k == 0:
            span_scores()
        else:
            pl.when(start >= k * span)(span_scores)

    per_tile = tq // blk
    first_own = qi * per_tile

    def block_bias(j):
        return per_group(bias_ref[pl.ds(j, 1), :])

    def block_rows(j):
        return pl.ds(pl.multiple_of(j * blk, blk), blk)

    def causal_mask(j):
        key = j * blk + lax.broadcasted_iota(I32, (blk, lanes), 0)
        return key <= pos

    def max_body(i, m8):
        for d in range(per_tile):
            j = i * per_tile + d
            m8 = jnp.maximum(m8, fold(jnp.max, s_ref[block_rows(j), :]) * scale + block_bias(j))
        return m8

    m8 = lax.fori_loop(0, qi, max_body, jnp.full((SUBLANES, lanes), NEG, F32))
    for d in range(per_tile):
        j = first_own + d
        own = jnp.where(causal_mask(j), s_ref[block_rows(j), :], NEG)
        m8 = jnp.maximum(m8, fold(jnp.max, own) * scale + block_bias(j))
    m_log2 = jnp.max(m8, axis=0, keepdims=True) * LOG2E

    def prob_body(i, l8):
        for d in range(per_tile):
            j = i * per_tile + d
            p = jnp.exp2(s_ref[block_rows(j), :] * exp_scale + (block_bias(j) * LOG2E - m_log2))
            p_ref[block_rows(j), :] = p.astype(p_ref.dtype)
            l8 = l8 + fold(jnp.sum, p)
        return l8

    l8 = lax.fori_loop(0, qi, prob_body, jnp.zeros((SUBLANES, lanes), F32))
    for d in range(per_tile):
        j = first_own + d
        p = jnp.exp2(s_ref[block_rows(j), :] * exp_scale + (block_bias(j) * LOG2E - m_log2))
        p = jnp.where(causal_mask(j), p, 0.0)
        p_ref[block_rows(j), :] = p.astype(p_ref.dtype)
        l8 = l8 + fold(jnp.sum, p)

    def zero_body(i, carry):
        p_ref[pl.ds(pl.multiple_of(i * tq, tq), tq), :] = jnp.zeros((tq, lanes), p_ref.dtype)
        return carry

    visible_tiles = (start // span + 1) * (span // tq)
    lax.fori_loop(qi + 1, visible_tiles, zero_body, 0)

    for k in range(n_spans):
        def span_pv(k=k):
            pv = _dot(vst_ref[:, k * span:(k + 1) * span], p_ref[k * span:(k + 1) * span, :])
            if k == 0:
                out_ref[1] = pv
            else:
                out_ref[1] += pv
        if k == 0:
            span_pv()
        else:
            pl.when(start >= k * span)(span_pv)
    out_ref[1] = out_ref[1] / jnp.sum(l8, axis=0, keepdims=True)

    w0 = pl.multiple_of(jnp.maximum(start + tq - win_len, 0), tq)
    sw_ref[...] = scores(kw_ref[pl.ds(w0, win_len), :])
    n_tiles = win_len // blk

    def win_tile(t):
        key = w0 + t * blk + lax.broadcasted_iota(I32, (blk, lanes), 0)
        return sw_ref[t * blk:(t + 1) * blk, :], (key <= pos) & (key > pos - NSA_WINDOW)

    m8 = jnp.full((SUBLANES, lanes), NEG, F32)
    for t in range(n_tiles):
        tile, mask = win_tile(t)
        m8 = jnp.maximum(m8, fold(jnp.max, jnp.where(mask, tile, NEG)))
    m_log2 = jnp.max(m8, axis=0, keepdims=True) * exp_scale
    l8 = jnp.zeros((SUBLANES, lanes), F32)
    for t in range(n_tiles):
        tile, mask = win_tile(t)
        p = jnp.where(mask, jnp.exp2(tile * exp_scale - m_log2), 0.0)
        pw_ref[t * blk:(t + 1) * blk, :] = p.astype(pw_ref.dtype)
        l8 = l8 + fold(jnp.sum, p)
    w_tile0 = w0 // tq
    pv = _dot(vwt_ref[w_tile0], pw_ref[0:tq, :])
    for c in range(1, win_len // tq):
        pv = pv + _dot(vwt_ref[w_tile0 + c], pw_ref[c * tq:(c + 1) * tq, :])
    out_ref[2] = pv / jnp.sum(l8, axis=0, keepdims=True)

    for r in range(NSA_GROUP):
        gate_row = (g * NSA_GROUP + r) * 3
        mix = jnp.zeros((HEAD_DIM, tq), F32)
        for c in range(3):
            mix = mix + jax.nn.sigmoid(ngt_ref[pl.ds(gate_row + c, 1), :]) * out_ref[c, :, heads[r]]
        o_ref[:, r * HEAD_DIM:(r + 1) * HEAD_DIM] = mix.T.astype(o_ref.dtype)


def nsa_attention(proj, kv_c, kv_ct, batch, seq, tq=128):
    assert tq == LANES and seq % tq == 0 and tq % NSA_SLC_BLOCK == 0
    n_cmp = (seq - NSA_CMP_LEN) // NSA_CMP_STRIDE + 1
    n_slc = seq // NSA_SLC_BLOCK
    rows = -(-n_slc // BF16_SUBLANES) * BF16_SUBLANES
    span = max(tq, seq // 4)
    win_len = min(NSA_WINDOW + tq, seq)
    n16 = seq // NSA_CMP_STRIDE
    gw = NSA_GROUP * HEAD_DIM
    lanes = NSA_GROUP * tq
    nkv = C_NKV // HEAD_DIM

    def kv_cols(slot):
        c0 = C_NKV + slot * NSA_KV_WIDTH
        return proj[:, :, c0:c0 + NSA_KV_WIDTH].reshape(batch, seq, NSA_KV_HEADS, HEAD_DIM)

    vs_t = kv_cols(3).transpose(0, 2, 3, 1)
    vw_t = kv_cols(5).reshape(batch, seq // tq, tq, NSA_KV_HEADS, HEAD_DIM).transpose(0, 3, 1, 4, 2)
    n_gates = 3 * NSA_HEADS
    ng_t = proj[:, :, C_SMALL + SMALL_NG_LANE:C_SMALL + SMALL_NG_LANE + n_gates].astype(F32).transpose(0, 2, 1)

    def k_spec(slot):
        return pl.BlockSpec((None, seq, HEAD_DIM),
                            lambda b, g, i, slot=slot: (b, 0, nkv + slot * NSA_KV_HEADS + g))

    kern = functools.partial(_nsa_kernel, tq=tq, span=span, n_cmp=n_cmp, n_slc=n_slc,
                             topn=min(NSA_TOPN, n_slc), win_len=win_len)
    return pl.pallas_call(
        kern,
        grid=(batch, NSA_KV_HEADS, seq // tq),
        in_specs=[pl.BlockSpec((None, tq, gw), lambda b, g, i: (b, i, C_NQ // gw + g)),
                  pl.BlockSpec((None, None, None, n16, HEAD_DIM), lambda b, g, i: (b, 0, g, 0, 0)),
                  pl.BlockSpec((None, None, None, HEAD_DIM, n16), lambda b, g, i: (b, 1, g, 0, 0)),
                  k_spec(2),
                  pl.BlockSpec((None, None, HEAD_DIM, seq), lambda b, g, i: (b, g, 0, 0)),
                  k_spec(4),
                  pl.BlockSpec((None, None, seq // tq, HEAD_DIM, tq), lambda b, g, i: (b, g, 0, 0, 0)),
                  pl.BlockSpec((None, n_gates, tq), lambda b, g, i: (b, 0, i))],
        out_specs=pl.BlockSpec((None, tq, gw), lambda b, g, i: (b, i, g)),
        out_shape=jax.ShapeDtypeStruct((batch, seq, NSA_WIDTH), BF16),
        scratch_shapes=[pltpu.VMEM((rows, tq), F32),
                        pltpu.VMEM((seq, lanes), F32), pltpu.VMEM((seq, lanes), BF16),
                        pltpu.VMEM((win_len, lanes), F32), pltpu.VMEM((win_len, lanes), BF16),
                        pltpu.VMEM((3, HEAD_DIM, lanes), F32)],
        compiler_params=_params("parallel", "parallel", "arbitrary"),
        name="nsa",
    )(proj, kv_c, kv_ct, proj, vs_t, proj, vw_t, ng_t)


GLA_SUB = 16


def _gla_head(q, k, v, gate, z, norm_g, state_ref, attn_ref, chunk):
    log_a = (jnp.minimum(z, 0.0) - jnp.log(1.0 + jnp.exp(-jnp.abs(z)))) * (1.0 / GLA_TAU)
    tri = (lax.broadcasted_iota(I32, (chunk, chunk), 1)
           <= lax.broadcasted_iota(I32, (chunk, chunk), 0)).astype(BF16)
    b = _dot_split3(tri, log_a)

    attn_ref[...] = jnp.zeros_like(attn_ref)
    t_idx = lax.broadcasted_iota(I32, (GLA_SUB, 1), 0)
    s_lane = lax.broadcasted_iota(I32, (GLA_SUB, GLA_SUB), 1)
    for i in range(chunk // GLA_SUB):
        r0 = i * GLA_SUB
        bi, qi, ki = b[r0:r0 + GLA_SUB], q[r0:r0 + GLA_SUB], k[r0:r0 + GLA_SUB]
        diag = jnp.zeros((GLA_SUB, GLA_SUB), F32)
        for s in range(GLA_SUB):
            decay = jnp.exp(jnp.where(t_idx >= s, bi - bi[s:s + 1], NEG))
            col = jnp.sum(qi * ki[s:s + 1] * decay, axis=-1, keepdims=True)
            diag = jnp.where(s_lane == s, col, diag)
        attn_ref[r0:r0 + GLA_SUB, r0:r0 + GLA_SUB] = diag
        if i > 0:
            ref_b = b[r0:r0 + 1]
            q_dec = (qi * jnp.exp(bi - ref_b)).astype(BF16)
            k_dec = (k[:r0] * jnp.exp(ref_b - b[:r0])).astype(BF16)
            attn_ref[r0:r0 + GLA_SUB, 0:r0] = _dot_nt(q_dec, k_dec)

    state_t = state_ref[...]
    o = _dot(attn_ref[...].astype(BF16), v) + _dot_nt((q * jnp.exp(b)).astype(BF16), state_t.astype(BF16))
    b_last = b[chunk - 1:chunk]
    k_dec = (k * jnp.exp(b_last - b)).astype(BF16)
    state_ref[...] = state_t * jnp.exp(b_last) + _dot(v.astype(F32).T.astype(BF16), k_dec)
    return _rms(o, norm_g) * (gate * jax.nn.sigmoid(gate))


def _gla_kernel(*refs, chunk):
    q_ref, k_ref = refs[0:2]
    v_refs = refs[2:2 + GLA_HEADS]
    gg_refs = refs[2 + GLA_HEADS:2 + 2 * GLA_HEADS]
    small_ref, wa_ref, ba_ref, ng_ref, o_ref, state_ref, attn_ref = refs[2 + 2 * GLA_HEADS:]

    @pl.when(pl.program_id(1) == 0)
    def _():
        state_ref[...] = jnp.zeros_like(state_ref)

    z_all = _dot(small_ref[...], wa_ref[...]) + ba_ref[...]
    for h in range(GLA_HEADS):
        keys = slice(h * GLA_DK, (h + 1) * GLA_DK)
        out = _gla_head(q_ref[:, keys].astype(F32) * (GLA_DK ** -0.5), k_ref[:, keys].astype(F32),
                        v_refs[h][...], gg_refs[h][...].astype(F32), z_all[:, keys], ng_ref[...],
                        state_ref.at[h], attn_ref.at[h], chunk)
        o_ref[:, h * GLA_DV:(h + 1) * GLA_DV] = out.astype(o_ref.dtype)


def gla_attention(proj, gla_wa, gla_ba, gla_norm_g, batch, seq, chunk=128):
    chunk = min(chunk, seq)
    assert seq % chunk == 0 and chunk % GLA_SUB == 0
    wa = jnp.zeros((LANES, GLA_KEY_WIDTH), BF16).at[SMALL_GA_LANE:SMALL_GA_LANE + GLA_RANK].set(gla_wa.astype(BF16))

    def head_spec(c0, h):
        return pl.BlockSpec((None, chunk, GLA_DV), lambda b, c, h=h: (b, c, c0 // GLA_DV + h))

    const = lambda b, c: (0, 0)
    return pl.pallas_call(
        functools.partial(_gla_kernel, chunk=chunk),
        grid=(batch, seq // chunk),
        in_specs=[pl.BlockSpec((None, chunk, GLA_KEY_WIDTH), lambda b, c: (b, c, C_GQ // GLA_KEY_WIDTH)),
                  pl.BlockSpec((None, chunk, GLA_KEY_WIDTH), lambda b, c: (b, c, C_GK // GLA_KEY_WIDTH)),
                  *[head_spec(C_GV, h) for h in range(GLA_HEADS)],
                  *[head_spec(C_GG, h) for h in range(GLA_HEADS)],
                  pl.BlockSpec((None, chunk, LANES), lambda b, c: (b, c, C_SMALL // LANES)),
                  pl.BlockSpec((LANES, GLA_KEY_WIDTH), const),
                  pl.BlockSpec((1, GLA_KEY_WIDTH), const),
                  pl.BlockSpec((1, GLA_DV), const)],
        out_specs=pl.BlockSpec((None, chunk, GLA_WIDTH), lambda b, c: (b, c, 0)),
        out_shape=jax.ShapeDtypeStruct((batch, seq, GLA_WIDTH), BF16),
        scratch_shapes=[pltpu.VMEM((GLA_HEADS, GLA_DV, GLA_DK), F32), pltpu.VMEM((GLA_HEADS, chunk, chunk), F32)],
        compiler_params=_params("parallel", "arbitrary"),
        name="gla",
    )(proj, proj, *([proj] * (2 * GLA_HEADS)), proj, wa, gla_ba.reshape(1, GLA_KEY_WIDTH).astype(F32),
      gla_norm_g.reshape(1, GLA_DV).astype(F32))


def _merge_kernel(om_ref, on_ref, og_ref, wm_ref, wn_ref, wg_ref, gm_ref, gn_ref, gl_ref, o_ref):
    def gated(gate_ref, a_ref, w_ref):
        return jax.nn.sigmoid(gate_ref[...].astype(F32)) * _dot(a_ref[...], w_ref[...])

    o_ref[...] = (gated(gm_ref, om_ref, wm_ref) + gated(gn_ref, on_ref, wn_ref)
                  + gated(gl_ref, og_ref, wg_ref)).astype(o_ref.dtype)


def merge_branches(o_m, o_n, o_g, w_m, w_n, w_g, proj2d, tm=1024, tn=512):
    m = o_m.shape[0]
    tm = min(tm, m)
    assert C_MG % tn == 0 and D_MODEL % tn == 0

    def gate_spec(c):
        return pl.BlockSpec((tm, tn), lambda i, j, c=c: (i, (C_MG + c * D_MODEL) // tn + j))

    def act_spec(width):
        return pl.BlockSpec((tm, width), lambda i, j: (i, 0))

    def w_spec(width):
        return pl.BlockSpec((width, tn), lambda i, j: (0, j))

    return pl.pallas_call(
        _merge_kernel,
        grid=(m // tm, D_MODEL // tn),
        in_specs=[act_spec(MOBA_WIDTH), act_spec(NSA_WIDTH), act_spec(GLA_WIDTH),
                  w_spec(MOBA_WIDTH), w_spec(NSA_WIDTH), w_spec(GLA_WIDTH),
                  gate_spec(0), gate_spec(1), gate_spec(2)],
        out_specs=pl.BlockSpec((tm, tn), lambda i, j: (i, j)),
        out_shape=jax.ShapeDtypeStruct((m, D_MODEL), BF16),
        compiler_params=_params("parallel", "parallel"),
        name="merge",
    )(o_m, o_n, o_g, w_m, w_n, w_g, proj2d, proj2d, proj2d)


def _router_kernel(x_ref, g_ref, w_ref, b_ref, h_ref, logit_ref):
    h = _rms(x_ref[...], g_ref[...])
    h_ref[...] = h.astype(h_ref.dtype)
    h1 = h.astype(BF16)
    r1 = h - h1.astype(F32)
    h2 = r1.astype(BF16)
    h3 = (r1 - h2.astype(F32)).astype(BF16)
    w1 = w_ref[0]
    w2 = w_ref[1]
    logit_ref[...] = (_dot(h1, w1) + (_dot(h1, w2) + _dot(h2, w1)) + (_dot(h2, w2) + _dot(h3, w1))) + b_ref[...]


def router(x2d, norm_g, rg_w, rg_b, re_w, re_b, tm=512):
    m, d = x2d.shape
    tm = min(tm, m)
    n_real = MOE_GROUPS + MOE_EXPERTS
    w = jnp.zeros((d, ROUTER_LANES), F32).at[:, :n_real].set(jnp.concatenate([rg_w, re_w], axis=1))
    w_hi = w.astype(BF16)
    w_lo = (w - w_hi.astype(F32)).astype(BF16)
    bias = jnp.zeros((1, ROUTER_LANES), F32).at[0, :n_real].set(jnp.concatenate([rg_b, re_b]))
    return pl.pallas_call(
        _router_kernel,
        grid=(m // tm,),
        in_specs=[pl.BlockSpec((tm, d), lambda i: (i, 0)), pl.BlockSpec((1, d), lambda i: (0, 0)),
                  pl.BlockSpec((2, d, ROUTER_LANES), lambda i: (0, 0, 0)),
                  pl.BlockSpec((1, ROUTER_LANES), lambda i: (0, 0))],
        out_specs=[pl.BlockSpec((tm, d), lambda i: (i, 0)), pl.BlockSpec((tm, ROUTER_LANES), lambda i: (i, 0))],
        out_shape=[jax.ShapeDtypeStruct((m, d), BF16), jax.ShapeDtypeStruct((m, ROUTER_LANES), F32)],
        compiler_params=_params("parallel"),
        name="router",
    )(x2d, norm_g.reshape(1, d).astype(F32), jnp.stack([w_hi, w_lo]), bias)


def _expert_kernel(blk_e_ref, n_used_ref, x_ref, wg_ref, wu_ref, wd_ref, o_ref, wg_bf, wu_bf, wd_bf):
    i = pl.program_id(0)
    new_expert = (i == 0) | (blk_e_ref[i] != blk_e_ref[jnp.maximum(i - 1, 0)])

    @pl.when(new_expert)
    def _():
        wg_bf[...] = wg_ref[...].astype(BF16)
        wu_bf[...] = wu_ref[...].astype(BF16)
        wd_bf[...] = wd_ref[...].astype(BF16)

    @pl.when(i < n_used_ref[0])
    def _():
        x = x_ref[...]
        gate = _dot(x, wg_bf[...])
        hid = gate * jax.nn.sigmoid(gate) * _dot(x, wu_bf[...])
        o_ref[...] = _dot(hid.astype(BF16), wd_bf[...]).astype(o_ref.dtype)

    @pl.when(i >= n_used_ref[0])
    def _():
        o_ref[...] = jnp.zeros_like(o_ref)


def expert_blocks(xs, blk_e, n_used, layer, w_gate, w_up, w_down):
    p, d = xs.shape
    ff = w_gate.shape[3]
    n_blk = p // MOE_ROWS
    grid_spec = pltpu.PrefetchScalarGridSpec(
        num_scalar_prefetch=2,
        grid=(n_blk,),
        in_specs=[pl.BlockSpec((MOE_ROWS, d), lambda i, e, n: (i, 0)),
                  pl.BlockSpec((None, None, d, ff), lambda i, e, n: (layer, e[i], 0, 0)),
                  pl.BlockSpec((None, None, d, ff), lambda i, e, n: (layer, e[i], 0, 0)),
                  pl.BlockSpec((None, None, ff, d), lambda i, e, n: (layer, e[i], 0, 0))],
        out_specs=pl.BlockSpec((MOE_ROWS, d), lambda i, e, n: (i, 0)),
        scratch_shapes=[pltpu.VMEM((d, ff), BF16), pltpu.VMEM((d, ff), BF16), pltpu.VMEM((ff, d), BF16)],
    )
    return pl.pallas_call(
        _expert_kernel,
        grid_spec=grid_spec,
        out_shape=jax.ShapeDtypeStruct((p, d), BF16),
        compiler_params=_params("arbitrary"),
        name="experts",
    )(blk_e, n_used, xs, w_gate, w_up, w_down)


def _combine_kernel(x_ref, y0_ref, y1_ref, w_ref, g_ref, o_ref, *, final_norm):
    w = w_ref[...]
    x = x_ref[...] + (w[:, 0:1] * y0_ref[...].astype(F32) + w[:, 1:2] * y1_ref[...].astype(F32))
    o_ref[...] = _rms(x, g_ref[...]) if final_norm else x


def combine(x2d, y0, y1, w, norm_g, final_norm, tm=512):
    m, d = x2d.shape
    tm = min(tm, m)
    row = pl.BlockSpec((tm, d), lambda i: (i, 0))
    return pl.pallas_call(
        functools.partial(_combine_kernel, final_norm=final_norm),
        grid=(m // tm,),
        in_specs=[row, row, row, pl.BlockSpec((tm, MOE_TOPK), lambda i: (i, 0)),
                  pl.BlockSpec((1, d), lambda i: (0, 0))],
        out_specs=row,
        out_shape=jax.ShapeDtypeStruct((m, d), F32),
        compiler_params=_params("parallel"),
        name="combine",
    )(x2d, y0, y1, w, norm_g.reshape(1, d).astype(F32))


def hier_moe(x2d, norm_g, rg_w, rg_b, re_w, re_b, layer, w_gate, w_up, w_down, out_norm_g, final_norm):
    t = x2d.shape[0]
    h, logits = router(x2d, norm_g, rg_w, rg_b, re_w, re_b)
    g_logits = logits[:, :MOE_GROUPS]
    grp = jnp.argmax(g_logits, axis=-1)
    p_grp = jnp.take_along_axis(jax.nn.softmax(g_logits, axis=-1), grp[:, None], axis=1)[:, 0]
    e_logits = logits[:, MOE_GROUPS:MOE_GROUPS + MOE_EXPERTS].reshape(t, MOE_GROUPS, MOE_EXPERTS_PER_GROUP)
    e_in = jnp.take_along_axis(e_logits, grp[:, None, None], axis=1)[:, 0]
    top_v, top_i = lax.top_k(e_in, MOE_TOPK)
    w = jax.nn.softmax(top_v, axis=-1) * p_grp[:, None]
    expert = (grp[:, None] * MOE_EXPERTS_PER_GROUP + top_i).astype(I32)

    a = t * MOE_TOPK
    e_flat = expert.reshape(a)
    onehot = (e_flat[:, None] == jnp.arange(MOE_EXPERTS, dtype=I32)[None, :]).astype(I32)
    running = jnp.cumsum(onehot, axis=0)
    counts = running[-1]
    rank = jnp.take_along_axis(running, e_flat[:, None], axis=1)[:, 0] - 1
    padded = (counts + MOE_ROWS - 1) // MOE_ROWS * MOE_ROWS
    pends = jnp.cumsum(padded)
    dest = (pends - padded)[e_flat] + rank
    p_rows = (a + MOE_EXPERTS * (MOE_ROWS - 1)) // MOE_ROWS * MOE_ROWS
    n_blk = p_rows // MOE_ROWS
    row_tok = (jnp.arange(p_rows, dtype=I32) % t).at[dest].set(jnp.arange(a, dtype=I32) // MOE_TOPK)
    blk_start = jnp.arange(n_blk, dtype=I32) * MOE_ROWS
    blk_e = jnp.minimum(jnp.sum((pends[None, :] <= blk_start[:, None]).astype(I32), axis=1), MOE_EXPERTS - 1)
    n_used = (pends[-1] // MOE_ROWS).astype(I32).reshape(1)

    xs = jnp.take(h, row_tok, axis=0)
    y_rows = expert_blocks(xs, blk_e, n_used, layer, w_gate, w_up, w_down)
    dest2 = dest.reshape(t, MOE_TOPK)
    y0 = jnp.take(y_rows, dest2[:, 0], axis=0)
    y1 = jnp.take(y_rows, dest2[:, 1], axis=0)
    return combine(x2d, y0, y1, w, out_norm_g, final_norm)


def _permute_w_in(w_in):
    d = w_in.shape[0]
    pad = jnp.zeros((d, PROJ_WIDTH - _SRC_END), w_in.dtype)
    return jnp.concatenate([w_in[:, :_SRC_NG], w_in[:, _SRC_GQ:_SRC_GA], w_in[:, _SRC_GG:_SRC_END],
                            w_in[:, _SRC_GA:_SRC_GG], w_in[:, _SRC_NG:_SRC_GQ], pad], axis=1).astype(BF16)


def hybrid_layer(x, norm1_g, w_in, nsa_cmp_pe, nsa_cmp_w1, nsa_cmp_w2, gla_wa, gla_ba, gla_norm_g,
                 w_br_moba, w_br_nsa, w_br_gla, w_out, norm2_g, router_group_w, router_group_b,
                 router_expert_w, router_expert_b, layer, expert_w_gate, expert_w_up, expert_w_down,
                 out_norm_g, final_norm):
    batch, seq, d = x.shape
    t = batch * seq
    x2d = x.reshape(t, d)
    h = rmsnorm(x2d, norm1_g, BF16)
    proj2d = matmul(h, _permute_w_in(w_in), BF16)
    proj = proj2d.reshape(batch, seq, PROJ_WIDTH)
    o_m = moba_attention(proj, batch, seq)
    kv_c, kv_ct = nsa_compress(proj, nsa_cmp_pe, nsa_cmp_w1, nsa_cmp_w2, batch, seq)
    o_n = nsa_attention(proj, kv_c, kv_ct, batch, seq)
    o_g = gla_attention(proj, gla_wa, gla_ba, gla_norm_g, batch, seq)
    merged = merge_branches(o_m.reshape(t, -1), o_n.reshape(t, -1), o_g.reshape(t, -1),
                            w_br_moba.astype(BF16), w_br_nsa.astype(BF16), w_br_gla.astype(BF16), proj2d)
    x2d = matmul_residual(merged, w_out.astype(BF16), x2d)
    x2d = hier_moe(x2d, norm2_g, router_group_w, router_group_b, router_expert_w, router_expert_b,
                   layer, expert_w_gate, expert_w_up, expert_w_down, out_norm_g, final_norm)
    return x2d.reshape(batch, seq, d)


def kernel(x, norm1_g, w_in, nsa_cmp_pe, nsa_cmp_w1, nsa_cmp_w2, gla_wa, gla_ba, gla_norm_g, w_br_moba,
           w_br_nsa, w_br_gla, w_out, norm2_g, router_group_w, router_group_b, router_expert_w,
           router_expert_b, expert_w_gate, expert_w_up, expert_w_down, final_norm_g):
    for l in range(DEPTH):
        x = hybrid_layer(x, norm1_g[l], w_in[l], nsa_cmp_pe[l], nsa_cmp_w1[l], nsa_cmp_w2[l], gla_wa[l],
                         gla_ba[l], gla_norm_g[l], w_br_moba[l], w_br_nsa[l], w_br_gla[l], w_out[l],
                         norm2_g[l], router_group_w[l], router_group_b[l], router_expert_w[l],
                         router_expert_b[l], l, expert_w_gate, expert_w_up, expert_w_down,
                         final_norm_g, l == DEPTH - 1)
    return x
```

```python
import functools

import jax
import jax.numpy as jnp
import numpy as np
from jax import lax
from jax.experimental import pallas as pl
from jax.experimental.pallas import tpu as pltpu

F32 = jnp.float32
BF16 = jnp.bfloat16
I32 = jnp.int32

D_MODEL = 2048
DEPTH = 2
HEAD_DIM = 128
NEG = -1e30
FORCE = 1e9
EPS = 1e-6
LOG2E = 1.4426950408889634

MOBA_HEADS = 8
MOBA_BLOCK = 256
MOBA_TOPK = 3
MOBA_WIDTH = MOBA_HEADS * HEAD_DIM

NSA_HEADS = 8
NSA_KV_HEADS = 2
NSA_GROUP = NSA_HEADS // NSA_KV_HEADS
NSA_CMP_LEN = 32
NSA_CMP_STRIDE = 16
NSA_SLC_BLOCK = 64
NSA_TOPN = 8
NSA_WINDOW = 512
NSA_WIDTH = NSA_HEADS * HEAD_DIM
NSA_KV_WIDTH = NSA_KV_HEADS * HEAD_DIM

GLA_HEADS = 4
GLA_DK = 128
GLA_DV = 256
GLA_RANK = 16
GLA_TAU = 16.0
GLA_KEY_WIDTH = GLA_HEADS * GLA_DK
GLA_WIDTH = GLA_HEADS * GLA_DV

N_BRANCH = 3
MOE_GROUPS = 4
MOE_EXPERTS_PER_GROUP = 8
MOE_EXPERTS = MOE_GROUPS * MOE_EXPERTS_PER_GROUP
MOE_TOPK = 2
MOE_FF = D_MODEL // 4

LANES = 128
SUBLANES = 8
BF16_SUBLANES = 16
VMEM_LIMIT_BYTES = 56 * 1024 * 1024

_SRC_NG = MOBA_WIDTH * 3 + NSA_WIDTH + 6 * NSA_KV_WIDTH
_SRC_GQ = _SRC_NG + 3 * NSA_HEADS
_SRC_GA = _SRC_GQ + 2 * GLA_KEY_WIDTH + GLA_WIDTH
_SRC_GG = _SRC_GA + GLA_RANK
_SRC_END = _SRC_GG + GLA_WIDTH + N_BRANCH * D_MODEL

C_MQ = 0
C_MK = C_MQ + MOBA_WIDTH
C_MV = C_MK + MOBA_WIDTH
C_NQ = C_MV + MOBA_WIDTH
C_NKV = C_NQ + NSA_WIDTH
C_GQ = C_NKV + 6 * NSA_KV_WIDTH
C_GK = C_GQ + GLA_KEY_WIDTH
C_GV = C_GK + GLA_KEY_WIDTH
C_GG = C_GV + GLA_WIDTH
C_MG = C_GG + GLA_WIDTH
C_SMALL = C_MG + N_BRANCH * D_MODEL
PROJ_WIDTH = C_SMALL + LANES
SMALL_GA_LANE = 0
SMALL_NG_LANE = GLA_RANK

ROUTER_LANES = LANES
MOE_ROWS = 256


def _params(*semantics):
    return pltpu.CompilerParams(dimension_semantics=semantics, vmem_limit_bytes=VMEM_LIMIT_BYTES)


def _dot(a, b):
    return jnp.dot(a, b, preferred_element_type=F32)


def _dot_nt(a, b):
    return lax.dot_general(a, b, (((1,), (1,)), ((), ())), preferred_element_type=F32)


def _dot_split3(a01, x):
    x1 = x.astype(BF16)
    r1 = x - x1.astype(F32)
    x2 = r1.astype(BF16)
    x3 = (r1 - x2.astype(F32)).astype(BF16)
    return _dot(a01, x1) + _dot(a01, x2) + _dot(a01, x3)


def _rms(x, g):
    return x * lax.rsqrt(jnp.mean(x * x, axis=-1, keepdims=True) + EPS) * g


def _rmsnorm_kernel(x_ref, g_ref, o_ref):
    o_ref[...] = _rms(x_ref[...].astype(F32), g_ref[...]).astype(o_ref.dtype)


def rmsnorm(x2d, g, out_dtype, tm=512):
    m, d = x2d.shape
    tm = min(tm, m)
    return pl.pallas_call(
        _rmsnorm_kernel,
        grid=(m // tm,),
        in_specs=[pl.BlockSpec((tm, d), lambda i: (i, 0)), pl.BlockSpec((1, d), lambda i: (0, 0))],
        out_specs=pl.BlockSpec((tm, d), lambda i: (i, 0)),
        out_shape=jax.ShapeDtypeStruct((m, d), out_dtype),
        compiler_params=_params("parallel"),
        name="rmsnorm",
    )(x2d, g.reshape(1, d).astype(F32))


def _matmul_kernel(a_ref, b_ref, o_ref):
    o_ref[...] = _dot(a_ref[...], b_ref[...]).astype(o_ref.dtype)


def matmul(a, b, out_dtype, tm=1024, tn=1152):
    m, k = a.shape
    n = b.shape[1]
    tm, tn = min(tm, m), min(tn, n)
    return pl.pallas_call(
        _matmul_kernel,
        grid=(m // tm, n // tn),
        in_specs=[pl.BlockSpec((tm, k), lambda i, j: (i, 0)), pl.BlockSpec((k, tn), lambda i, j: (0, j))],
        out_specs=pl.BlockSpec((tm, tn), lambda i, j: (i, j)),
        out_shape=jax.ShapeDtypeStruct((m, n), out_dtype),
        compiler_params=_params("parallel", "parallel"),
        name="matmul",
    )(a, b)


def _matmul_residual_kernel(a_ref, b_ref, r_ref, o_ref):
    o_ref[...] = r_ref[...] + _dot(a_ref[...], b_ref[...])


def matmul_residual(a, b, res, tm=1024, tn=1024):
    m, k = a.shape
    n = b.shape[1]
    tm, tn = min(tm, m), min(tn, n)
    return pl.pallas_call(
        _matmul_residual_kernel,
        grid=(m // tm, n // tn),
        in_specs=[pl.BlockSpec((tm, k), lambda i, j: (i, 0)), pl.BlockSpec((k, tn), lambda i, j: (0, j)),
                  pl.BlockSpec((tm, tn), lambda i, j: (i, j))],
        out_specs=pl.BlockSpec((tm, tn), lambda i, j: (i, j)),
        out_shape=jax.ShapeDtypeStruct((m, n), F32),
        compiler_params=_params("parallel", "parallel"),
        name="matmul_residual",
    )(a, b, res)


def _softmax_step(carry, q, k, v, mask, scale):
    m_i, l_i, acc = carry
    s = jnp.where(mask, _dot_nt(q, k) * scale, NEG)
    m_new = jnp.maximum(m_i, jnp.max(s, axis=-1, keepdims=True))
    alpha = jnp.exp(m_i - m_new)
    p = jnp.where(mask, jnp.exp(s - m_new), 0.0)
    l_new = alpha * l_i + jnp.sum(p, axis=-1, keepdims=True)
    acc_new = alpha * acc + _dot(p.astype(v.dtype), v)
    return m_new, l_new, acc_new


def _softmax_init(m, e):
    return jnp.full((m, 1), NEG, F32), jnp.zeros((m, 1), F32), jnp.zeros((m, e), F32)


def _rank_desc(vals, n_candidates):
    lane = lax.broadcasted_iota(I32, vals.shape, 1)
    rank = jnp.zeros(vals.shape, F32)
    for m in range(n_candidates):
        vm = vals[:, m:m + 1]
        beats = (vm > vals) | ((vm == vals) & (lane > m))
        rank = rank + beats.astype(F32)
    return rank


def _softmax_step_t(carry, q, k, v_t, bias, scale):
    m_i, l_i, acc = carry
    s = _dot_nt(k, q) * scale + bias
    m_new = jnp.maximum(m_i, jnp.max(s, axis=0, keepdims=True))
    alpha = jnp.exp(m_i - m_new)
    p = jnp.exp(s - m_new)
    l_new = alpha * l_i + jnp.sum(p, axis=0, keepdims=True)
    acc_new = alpha * acc + _dot(v_t, p.astype(v_t.dtype))
    return m_new, l_new, acc_new


def _softmax_init_t(m, e):
    return jnp.full((1, m), NEG, F32), jnp.zeros((1, m), F32), jnp.zeros((e, m), F32)


def _rank_desc_rows(vals, n_candidates):
    row = lax.broadcasted_iota(I32, vals.shape, 0)
    rank = jnp.zeros(vals.shape, F32)
    for m in range(n_candidates):
        vm = vals[m:m + 1, :]
        beats = (vm > vals) | ((vm == vals) & (row > m))
        rank = rank + beats.astype(F32)
    return rank


def _lane_column(x, idx):
    lane = lax.broadcasted_iota(I32, x.shape, 1)
    return jnp.sum(jnp.where(lane == idx, x, 0.0), axis=-1, keepdims=True)


def _moba_kernel(q_ref, k_ref, vt_ref, o_ref, kmean_ref, bias_ref, s_ref, p_ref, acc_ref, *, n_blocks, topk,
                 heads):
    blk = MOBA_BLOCK
    cur = pl.program_id(2)
    scale = HEAD_DIM ** -0.5
    cols = [slice(h * HEAD_DIM, (h + 1) * HEAD_DIM) for h in range(heads)]

    @pl.when(cur == 0)
    def _():
        kmean_ref[...] = jnp.zeros_like(kmean_ref)
        for h in range(heads):
            for n in range(n_blocks):
                k_blk = k_ref[n * blk:(n + 1) * blk, cols[h]].astype(F32)
                kmean_ref[h, n:n + 1, :] = jnp.sum(k_blk, axis=0, keepdims=True) * (1.0 / blk)

    for h in range(heads):
        gate = _dot_nt(kmean_ref[h].astype(BF16), q_ref[:, cols[h]])
        row = lax.broadcasted_iota(I32, gate.shape, 0)
        valid = row < cur
        gate = jnp.where(valid, gate, NEG)
        chosen = valid & (_rank_desc_rows(gate, n_blocks) < topk)
        bias_ref[h] = jnp.where(chosen, 0.0, NEG)

    seq = k_ref.shape[0]
    lo_blocks = (n_blocks + 1) // 2
    lo = lo_blocks * blk
    need_hi = cur >= lo_blocks

    for h in range(heads):
        s_ref[h, 0:lo, :] = _dot_nt(k_ref[0:lo, cols[h]], q_ref[:, cols[h]])

    if lo < seq:
        @pl.when(need_hi)
        def _():
            for h in range(heads):
                s_ref[h, lo:seq, :] = _dot_nt(k_ref[lo:seq, cols[h]], q_ref[:, cols[h]])

    sub = LANES
    base = pl.multiple_of(cur * blk, blk)
    qry_i = lax.broadcasted_iota(I32, (sub, blk), 1)
    key_i = [j * sub + lax.broadcasted_iota(I32, (sub, blk), 0) for j in range(blk // sub)]
    exp_scale = scale * LOG2E

    def fold_max(x):
        return jnp.max(x.reshape(sub // SUBLANES, SUBLANES, blk), axis=0)

    def fold_sum(x):
        return jnp.sum(x.reshape(sub // SUBLANES, SUBLANES, blk), axis=0)

    def tiles(h, start):
        return [s_ref[h, pl.ds(pl.multiple_of(start + j * sub, sub), sub), :] for j in range(blk // sub)]

    def max_body(n, m8):
        out = []
        for h in range(heads):
            blk_max = functools.reduce(jnp.maximum, [fold_max(t) for t in tiles(h, n * blk)])
            out.append(jnp.maximum(m8[h], blk_max * scale + bias_ref[h, pl.ds(n, 1), :]))
        return tuple(out)

    m8 = []
    for h in range(heads):
        own = [fold_max(jnp.where(key_i[j] <= qry_i, t, NEG)) for j, t in enumerate(tiles(h, base))]
        m8.append(functools.reduce(jnp.maximum, own) * scale)
    m8 = lax.fori_loop(0, cur, max_body, tuple(m8))
    m_log2 = [jnp.max(m, axis=0, keepdims=True) * LOG2E for m in m8]

    def store_p(h, start, j, p):
        p_ref[h, pl.ds(pl.multiple_of(start + j * sub, sub), sub), :] = p.astype(p_ref.dtype)

    def prob_body(n, l8):
        out = []
        for h in range(heads):
            shift = bias_ref[h, pl.ds(n, 1), :] * LOG2E - m_log2[h]
            acc = l8[h]
            for j, t in enumerate(tiles(h, n * blk)):
                p = jnp.exp2(t * exp_scale + shift)
                store_p(h, n * blk, j, p)
                acc = acc + fold_sum(p)
            out.append(acc)
        return tuple(out)

    l8 = []
    for h in range(heads):
        acc = jnp.zeros((SUBLANES, blk), F32)
        for j, t in enumerate(tiles(h, base)):
            p = jnp.where(key_i[j] <= qry_i, jnp.exp2(t * exp_scale - m_log2[h]), 0.0)
            store_p(h, base, j, p)
            acc = acc + fold_sum(p)
        l8.append(acc)
    l8 = lax.fori_loop(0, cur, prob_body, tuple(l8))

    def zero_body(n, carry):
        for h in range(heads):
            for j in range(blk // sub):
                store_p(h, n * blk, j, jnp.zeros((sub, blk), F32))
        return carry

    lax.fori_loop(cur + 1, jnp.where(need_hi, n_blocks, lo_blocks), zero_body, 0)
    for h in range(heads):
        acc_ref[h] = _dot(vt_ref[h, :, 0:lo], p_ref[h, 0:lo, :])

    if lo < seq:
        @pl.when(need_hi)
        def _():
            for h in range(heads):
                acc_ref[h] += _dot(vt_ref[h, :, lo:seq], p_ref[h, lo:seq, :])

    for h in range(heads):
        l_sum = jnp.sum(l8[h], axis=0, keepdims=True)
        o_ref[:, cols[h]] = (acc_ref[h] / l_sum).T.astype(o_ref.dtype)


def moba_attention(proj, batch, seq, heads=2):
    assert seq % MOBA_BLOCK == 0 and MOBA_HEADS % heads == 0
    n_blocks = seq // MOBA_BLOCK
    rows = -(-n_blocks // BF16_SUBLANES) * BF16_SUBLANES
    width = heads * HEAD_DIM
    qb, kb = C_MQ // width, C_MK // width
    v_t = proj[:, :, C_MV:C_MV + MOBA_WIDTH].reshape(batch, seq, MOBA_HEADS, HEAD_DIM).transpose(0, 2, 3, 1)
    kern = functools.partial(_moba_kernel, n_blocks=n_blocks, topk=min(MOBA_TOPK, n_blocks), heads=heads)
    return pl.pallas_call(
        kern,
        grid=(batch, MOBA_HEADS // heads, n_blocks),
        in_specs=[pl.BlockSpec((None, MOBA_BLOCK, width), lambda b, h, i: (b, i, qb + h)),
                  pl.BlockSpec((None, seq, width), lambda b, h, i: (b, 0, kb + h)),
                  pl.BlockSpec((None, heads, HEAD_DIM, seq), lambda b, h, i: (b, h, 0, 0))],
        out_specs=pl.BlockSpec((None, MOBA_BLOCK, width), lambda b, h, i: (b, i, h)),
        out_shape=jax.ShapeDtypeStruct((batch, seq, MOBA_WIDTH), BF16),
        scratch_shapes=[pltpu.VMEM((heads, rows, HEAD_DIM), F32), pltpu.VMEM((heads, rows, MOBA_BLOCK), F32),
                        pltpu.VMEM((heads, seq, MOBA_BLOCK), F32), pltpu.VMEM((heads, seq, MOBA_BLOCK), BF16),
                        pltpu.VMEM((heads, HEAD_DIM, MOBA_BLOCK), F32)],
        compiler_params=_params("parallel", "parallel", "arbitrary"),
        name="moba",
    )(proj, proj, v_t)


def _nsa_compress_kernel(x_ref, pe_ref, w1_ref, w2_ref, o_ref, ot_ref):
    x = x_ref[...].astype(F32)
    half = x.shape[1]
    lo = _dot((x + pe_ref[:, :half]).astype(BF16), w1_ref[:half, :])
    hi = _dot((x + pe_ref[:, half:]).astype(BF16), w1_ref[half:, :])
    pre = lo + pltpu.roll(hi, hi.shape[0] - 1, 0)
    hid = pre * jax.nn.sigmoid(pre)
    out = _dot(hid.astype(BF16), w2_ref[...])
    o_ref[...] = out.astype(o_ref.dtype)
    ot_ref[...] = out.T.astype(ot_ref.dtype)


def nsa_compress(proj, cmp_pe, cmp_w1, cmp_w2, batch, seq):
    n16 = seq // NSA_CMP_STRIDE
    width = NSA_CMP_STRIDE * HEAD_DIM
    x = proj[:, :, C_NKV:C_NKV + 2 * NSA_KV_WIDTH].reshape(batch, seq, 2, NSA_KV_HEADS, HEAD_DIM)
    x = x.transpose(0, 2, 3, 1, 4).reshape(batch, 2, NSA_KV_HEADS, n16, width)
    pe = cmp_pe.reshape(2, 1, NSA_CMP_LEN * HEAD_DIM).astype(F32)
    return pl.pallas_call(
        _nsa_compress_kernel,
        grid=(batch, 2, NSA_KV_HEADS),
        in_specs=[pl.BlockSpec((None, None, None, n16, width), lambda b, c, g: (b, c, g, 0, 0)),
                  pl.BlockSpec((None, 1, 2 * width), lambda b, c, g: (c, 0, 0)),
                  pl.BlockSpec((None, 2 * width, HEAD_DIM), lambda b, c, g: (c, 0, 0)),
                  pl.BlockSpec((None, HEAD_DIM, HEAD_DIM), lambda b, c, g: (c, 0, 0))],
        out_specs=[pl.BlockSpec((None, None, None, n16, HEAD_DIM), lambda b, c, g: (b, c, g, 0, 0)),
                   pl.BlockSpec((None, None, None, HEAD_DIM, n16), lambda b, c, g: (b, c, g, 0, 0))],
        out_shape=[jax.ShapeDtypeStruct((batch, 2, NSA_KV_HEADS, n16, HEAD_DIM), BF16),
                   jax.ShapeDtypeStruct((batch, 2, NSA_KV_HEADS, HEAD_DIM, n16), BF16)],
        compiler_params=_params("parallel", "parallel", "parallel"),
        name="nsa_compress",
    )(x, pe, cmp_w1.astype(BF16), cmp_w2.astype(BF16))


def _nsa_kernel(q_ref, kc_ref, vct_ref, ks_ref, vst_ref, kw_ref, vwt_ref, ngt_ref, o_ref,
                bias_ref, s_ref, p_ref, sw_ref, pw_ref, out_ref, *, tq, span, n_cmp, n_slc, topn, win_len):
    g = pl.program_id(1)
    qi = pl.program_id(2)
    seq = ks_ref.shape[0]
    n16 = kc_ref.shape[0]
    lanes = NSA_GROUP * tq
    blk = NSA_SLC_BLOCK
    scale = HEAD_DIM ** -0.5
    exp_scale = scale * LOG2E
    start = qi * tq
    heads = [slice(r * tq, (r + 1) * tq) for r in range(NSA_GROUP)]
    q = [q_ref[:, r * HEAD_DIM:(r + 1) * HEAD_DIM] for r in range(NSA_GROUP)]

    def per_group(row):
        return jnp.concatenate([row] * NSA_GROUP, axis=1)

    def scores(keys):
        return jnp.concatenate([_dot_nt(keys, q[r]) for r in range(NSA_GROUP)], axis=1)

    def fold(op, x):
        return op(x.reshape(x.shape[0] // SUBLANES, SUBLANES, lanes), axis=0)

    pos1 = start + lax.broadcasted_iota(I32, (1, tq), 1)
    pos = per_group(pos1)

    n_idx = lax.broadcasted_iota(I32, (n16, lanes), 0)
    in_range = n_idx < n_cmp
    cmask = (n_idx * NSA_CMP_STRIDE + (NSA_CMP_LEN - 1) <= pos) & in_range
    s_c = jnp.where(cmask, scores(kc_ref[...]) * scale, NEG)
    e_c = jnp.where(in_range, jnp.exp(s_c - jnp.max(s_c, axis=0, keepdims=True)), 0.0)
    p_c = jnp.where(cmask, e_c / jnp.sum(e_c, axis=0, keepdims=True), 0.0)
    out_ref[0] = _dot(vct_ref[...], p_c.astype(BF16))

    p_sum = functools.reduce(lambda a, b: a + b, [p_c[:, h] for h in heads])
    rows = bias_ref.shape[0]
    oj = lax.broadcasted_iota(I32, (rows, n16), 0)
    on = lax.broadcasted_iota(I32, (rows, n16), 1)
    overlap_t = ((on * NSA_CMP_STRIDE < (oj + 1) * blk) & (on * NSA_CMP_STRIDE + (NSA_CMP_LEN - 1) >= oj * blk)
                 & (on < n_cmp) & (oj < n_slc)).astype(BF16)
    p_hi = p_sum.astype(BF16)
    p_lo = (p_sum - p_hi.astype(F32)).astype(BF16)
    imp = _dot(overlap_t, p_hi) + _dot(overlap_t, p_lo)
    j_idx = lax.broadcasted_iota(I32, (rows, tq), 0)
    cur_blk = pos1 // blk
    forced = (j_idx == 0) | (j_idx == cur_blk) | (j_idx == cur_blk - 1)
    imp = jnp.where(forced, FORCE, imp)
    imp = jnp.where(j_idx > cur_blk, NEG, imp)
    chosen = (_rank_desc_rows(imp, n_slc) < topn) & (j_idx <= cur_blk)
    bias_ref[...] = jnp.where(chosen, 0.0, NEG)

    n_spans = seq // span
    for k in range(n_spans):
        def span_scores(k=k):
            s_ref[k * span:(k + 1) * span, :] = scores(ks_ref[k * span:(k + 1) * span, :])
        if k == 0:
            span_scores()
        else:
            pl.when(start >= k * span)(span_scores)

    per_tile = tq // blk
    first_own = qi * per_tile

    def block_bias(j):
        return per_group(bias_ref[pl.ds(j, 1), :])

    def block_rows(j):
        return pl.ds(pl.multiple_of(j * blk, blk), blk)

    def causal_mask(j):
        key = j * blk + lax.broadcasted_iota(I32, (blk, lanes), 0)
        return key <= pos

    def max_body(i, m8):
        for d in range(per_tile):
            j = i * per_tile + d
            m8 = jnp.maximum(m8, fold(jnp.max, s_ref[block_rows(j), :]) * scale + block_bias(j))
        return m8

    m8 = lax.fori_loop(0, qi, max_body, jnp.full((SUBLANES, lanes), NEG, F32))
    for d in range(per_tile):
        j = first_own + d
        own = jnp.where(causal_mask(j), s_ref[block_rows(j), :], NEG)
        m8 = jnp.maximum(m8, fold(jnp.max, own) * scale + block_bias(j))
    m_log2 = jnp.max(m8, axis=0, keepdims=True) * LOG2E

    def prob_body(i, l8):
        for d in range(per_tile):
            j = i * per_tile + d
            p = jnp.exp2(s_ref[block_rows(j), :] * exp_scale + (block_bias(j) * LOG2E - m_log2))
            p_ref[block_rows(j), :] = p.astype(p_ref.dtype)
            l8 = l8 + fold(jnp.sum, p)
        return l8

    l8 = lax.fori_loop(0, qi, prob_body, jnp.zeros((SUBLANES, lanes), F32))
    for d in range(per_tile):
        j = first_own + d
        p = jnp.exp2(s_ref[block_rows(j), :] * exp_scale + (block_bias(j) * LOG2E - m_log2))
        p = jnp.where(causal_mask(j), p, 0.0)
        p_ref[block_rows(j), :] = p.astype(p_ref.dtype)
        l8 = l8 + fold(jnp.sum, p)

    def zero_body(i, carry):
        p_ref[pl.ds(pl.multiple_of(i * tq, tq), tq), :] = jnp.zeros((tq, lanes), p_ref.dtype)
        return carry

    visible_tiles = (start // span + 1) * (span // tq)
    lax.fori_loop(qi + 1, visible_tiles, zero_body, 0)

    for k in range(n_spans):
        def span_pv(k=k):
            pv = _dot(vst_ref[:, k * span:(k + 1) * span], p_ref[k * span:(k + 1) * span, :])
            if k == 0:
                out_ref[1] = pv
            else:
                out_ref[1] += pv
        if k == 0:
            span_pv()
        else:
            pl.when(start >= k * span)(span_pv)
    out_ref[1] = out_ref[1] / jnp.sum(l8, axis=0, keepdims=True)

    w0 = pl.multiple_of(jnp.maximum(start + tq - win_len, 0), tq)
    sw_ref[...] = scores(kw_ref[pl.ds(w0, win_len), :])
    n_tiles = win_len // blk

    def win_tile(t):
        key = w0 + t * blk + lax.broadcasted_iota(I32, (blk, lanes), 0)
        return sw_ref[t * blk:(t + 1) * blk, :], (key <= pos) & (key > pos - NSA_WINDOW)

    m8 = jnp.full((SUBLANES, lanes), NEG, F32)
    for t in range(n_tiles):
        tile, mask = win_tile(t)
        m8 = jnp.maximum(m8, fold(jnp.max, jnp.where(mask, tile, NEG)))
    m_log2 = jnp.max(m8, axis=0, keepdims=True) * exp_scale
    l8 = jnp.zeros((SUBLANES, lanes), F32)
    for t in range(n_tiles):
        tile, mask = win_tile(t)
        p = jnp.where(mask, jnp.exp2(tile * exp_scale - m_log2), 0.0)
        pw_ref[t * blk:(t + 1) * blk, :] = p.astype(pw_ref.dtype)
        l8 = l8 + fold(jnp.sum, p)
    w_tile0 = w0 // tq
    pv = _dot(vwt_ref[w_tile0], pw_ref[0:tq, :])
    for c in range(1, win_len // tq):
        pv = pv + _dot(vwt_ref[w_tile0 + c], pw_ref[c * tq:(c + 1) * tq, :])
    out_ref[2] = pv / jnp.sum(l8, axis=0, keepdims=True)

    for r in range(NSA_GROUP):
        gate_row = (g * NSA_GROUP + r) * 3
        mix = jnp.zeros((HEAD_DIM, tq), F32)
        for c in range(3):
            mix = mix + jax.nn.sigmoid(ngt_ref[pl.ds(gate_row + c, 1), :]) * out_ref[c, :, heads[r]]
        o_ref[:, r * HEAD_DIM:(r + 1) * HEAD_DIM] = mix.T.astype(o_ref.dtype)


def nsa_attention(proj, kv_c, kv_ct, batch, seq, tq=128):
    assert tq == LANES and seq % tq == 0 and tq % NSA_SLC_BLOCK == 0
    n_cmp = (seq - NSA_CMP_LEN) // NSA_CMP_STRIDE + 1
    n_slc = seq // NSA_SLC_BLOCK
    rows = -(-n_slc // BF16_SUBLANES) * BF16_SUBLANES
    span = max(tq, seq // 4)
    win_len = min(NSA_WINDOW + tq, seq)
    n16 = seq // NSA_CMP_STRIDE
    gw = NSA_GROUP * HEAD_DIM
    lanes = NSA_GROUP * tq
    nkv = C_NKV // HEAD_DIM

    def kv_cols(slot):
        c0 = C_NKV + slot * NSA_KV_WIDTH
        return proj[:, :, c0:c0 + NSA_KV_WIDTH].reshape(batch, seq, NSA_KV_HEADS, HEAD_DIM)

    vs_t = kv_cols(3).transpose(0, 2, 3, 1)
    vw_t = kv_cols(5).reshape(batch, seq // tq, tq, NSA_KV_HEADS, HEAD_DIM).transpose(0, 3, 1, 4, 2)
    n_gates = 3 * NSA_HEADS
    ng_t = proj[:, :, C_SMALL + SMALL_NG_LANE:C_SMALL + SMALL_NG_LANE + n_gates].astype(F32).transpose(0, 2, 1)

    def k_spec(slot):
        return pl.BlockSpec((None, seq, HEAD_DIM),
                            lambda b, g, i, slot=slot: (b, 0, nkv + slot * NSA_KV_HEADS + g))

    kern = functools.partial(_nsa_kernel, tq=tq, span=span, n_cmp=n_cmp, n_slc=n_slc,
                             topn=min(NSA_TOPN, n_slc), win_len=win_len)
    return pl.pallas_call(
        kern,
        grid=(batch, NSA_KV_HEADS, seq // tq),
        in_specs=[pl.BlockSpec((None, tq, gw), lambda b, g, i: (b, i, C_NQ // gw + g)),
                  pl.BlockSpec((None, None, None, n16, HEAD_DIM), lambda b, g, i: (b, 0, g, 0, 0)),
                  pl.BlockSpec((None, None, None, HEAD_DIM, n16), lambda b, g, i: (b, 1, g, 0, 0)),
                  k_spec(2),
                  pl.BlockSpec((None, None, HEAD_DIM, seq), lambda b, g, i: (b, g, 0, 0)),
                  k_spec(4),
                  pl.BlockSpec((None, None, seq // tq, HEAD_DIM, tq), lambda b, g, i: (b, g, 0, 0, 0)),
                  pl.BlockSpec((None, n_gates, tq), lambda b, g, i: (b, 0, i))],
        out_specs=pl.BlockSpec((None, tq, gw), lambda b, g, i: (b, i, g)),
        out_shape=jax.ShapeDtypeStruct((batch, seq, NSA_WIDTH), BF16),
        scratch_shapes=[pltpu.VMEM((rows, tq), F32),
                        pltpu.VMEM((seq, lanes), F32), pltpu.VMEM((seq, lanes), BF16),
                        pltpu.VMEM((win_len, lanes), F32), pltpu.VMEM((win_len, lanes), BF16),
                        pltpu.VMEM((3, HEAD_DIM, lanes), F32)],
        compiler_params=_params("parallel", "parallel", "arbitrary"),
        name="nsa",
    )(proj, kv_c, kv_ct, proj, vs_t, proj, vw_t, ng_t)


GLA_SUB = 8


def _gla_kernel(*refs, chunk):
    q_ref, k_ref = refs[0:2]
    v_refs = refs[2:2 + GLA_HEADS]
    gg_refs = refs[2 + GLA_HEADS:2 + 2 * GLA_HEADS]
    small_ref, wa_ref, ba_ref, ng_ref, o_ref, state_ref, attn_ref = refs[2 + 2 * GLA_HEADS:]
    hs = range(GLA_HEADS)
    keys = [slice(h * GLA_DK, (h + 1) * GLA_DK) for h in hs]

    @pl.when(pl.program_id(1) == 0)
    def _():
        state_ref[...] = jnp.zeros_like(state_ref)

    q = q_ref[...].astype(F32) * (GLA_DK ** -0.5)
    k = k_ref[...].astype(F32)
    z = _dot(small_ref[...], wa_ref[...]) + ba_ref[...]
    log_a = (jnp.minimum(z, 0.0) - jnp.log(1.0 + jnp.exp(-jnp.abs(z)))) * (1.0 / GLA_TAU)
    tri = (lax.broadcasted_iota(I32, (chunk, chunk), 1)
           <= lax.broadcasted_iota(I32, (chunk, chunk), 0)).astype(BF16)
    b = _dot_split3(tri, log_a)

    attn_ref[...] = jnp.zeros_like(attn_ref)
    t_idx = lax.broadcasted_iota(I32, (GLA_SUB, 1), 0)
    s_lane = lax.broadcasted_iota(I32, (GLA_SUB, GLA_SUB), 1)
    for i in range(chunk // GLA_SUB):
        r0 = i * GLA_SUB
        bi, qi, ki = b[r0:r0 + GLA_SUB], q[r0:r0 + GLA_SUB], k[r0:r0 + GLA_SUB]
        diag = [jnp.zeros((GLA_SUB, GLA_SUB), F32) for _ in hs]
        for s in range(GLA_SUB):
            decay = jnp.exp(jnp.where(t_idx >= s, bi - bi[s:s + 1], NEG))
            prod = qi * ki[s:s + 1] * decay
            for h in hs:
                col = jnp.sum(prod[:, keys[h]], axis=-1, keepdims=True)
                diag[h] = jnp.where(s_lane == s, col, diag[h])
        for h in hs:
            attn_ref[h, r0:r0 + GLA_SUB, r0:r0 + GLA_SUB] = diag[h]
        if i > 0:
            ref_b = b[r0:r0 + 1]
            q_dec = (qi * jnp.exp(bi - ref_b)).astype(BF16)
            k_dec = (k[:r0] * jnp.exp(ref_b - b[:r0])).astype(BF16)
            for h in hs:
                attn_ref[h, r0:r0 + GLA_SUB, 0:r0] = _dot_nt(q_dec[:, keys[h]], k_dec[:, keys[h]])

    q_in = (q * jnp.exp(b)).astype(BF16)
    b_last = b[chunk - 1:chunk]
    k_out = (k * jnp.exp(b_last - b)).astype(BF16)
    carry = jnp.exp(b_last)
    for h in hs:
        v = v_refs[h][...]
        state_t = state_ref[h]
        o = _dot(attn_ref[h].astype(BF16), v) + _dot_nt(q_in[:, keys[h]], state_t.astype(BF16))
        state_ref[h] = state_t * carry[:, keys[h]] + _dot(v.astype(F32).T.astype(BF16), k_out[:, keys[h]])
        gate = gg_refs[h][...].astype(F32)
        out = _rms(o, ng_ref[...]) * (gate * jax.nn.sigmoid(gate))
        o_ref[:, h * GLA_DV:(h + 1) * GLA_DV] = out.astype(o_ref.dtype)


def gla_attention(proj, gla_wa, gla_ba, gla_norm_g, batch, seq, chunk=128):
    chunk = min(chunk, seq)
    assert seq % chunk == 0 and chunk % GLA_SUB == 0
    wa = jnp.zeros((LANES, GLA_KEY_WIDTH), BF16).at[SMALL_GA_LANE:SMALL_GA_LANE + GLA_RANK].set(gla_wa.astype(BF16))

    def head_spec(c0, h):
        return pl.BlockSpec((None, chunk, GLA_DV), lambda b, c, h=h: (b, c, c0 // GLA_DV + h))

    const = lambda b, c: (0, 0)
    return pl.pallas_call(
        functools.partial(_gla_kernel, chunk=chunk),
        grid=(batch, seq // chunk),
        in_specs=[pl.BlockSpec((None, chunk, GLA_KEY_WIDTH), lambda b, c: (b, c, C_GQ // GLA_KEY_WIDTH)),
                  pl.BlockSpec((None, chunk, GLA_KEY_WIDTH), lambda b, c: (b, c, C_GK // GLA_KEY_WIDTH)),
                  *[head_spec(C_GV, h) for h in range(GLA_HEADS)],
                  *[head_spec(C_GG, h) for h in range(GLA_HEADS)],
                  pl.BlockSpec((None, chunk, LANES), lambda b, c: (b, c, C_SMALL // LANES)),
                  pl.BlockSpec((LANES, GLA_KEY_WIDTH), const),
                  pl.BlockSpec((1, GLA_KEY_WIDTH), const),
                  pl.BlockSpec((1, GLA_DV), const)],
        out_specs=pl.BlockSpec((None, chunk, GLA_WIDTH), lambda b, c: (b, c, 0)),
        out_shape=jax.ShapeDtypeStruct((batch, seq, GLA_WIDTH), BF16),
        scratch_shapes=[pltpu.VMEM((GLA_HEADS, GLA_DV, GLA_DK), F32), pltpu.VMEM((GLA_HEADS, chunk, chunk), F32)],
        compiler_params=_params("parallel", "arbitrary"),
        name="gla",
    )(proj, proj, *([proj] * (2 * GLA_HEADS)), proj, wa, gla_ba.reshape(1, GLA_KEY_WIDTH).astype(F32),
      gla_norm_g.reshape(1, GLA_DV).astype(F32))


def _merge_kernel(om_ref, on_ref, og_ref, wm_ref, wn_ref, wg_ref, gm_ref, gn_ref, gl_ref, o_ref):
    def gated(gate_ref, a_ref, w_ref):
        return jax.nn.sigmoid(gate_ref[...].astype(F32)) * _dot(a_ref[...], w_ref[...])

    o_ref[...] = (gated(gm_ref, om_ref, wm_ref) + gated(gn_ref, on_ref, wn_ref)
                  + gated(gl_ref, og_ref, wg_ref)).astype(o_ref.dtype)


def merge_branches(o_m, o_n, o_g, w_m, w_n, w_g, proj2d, tm=1024, tn=512):
    m = o_m.shape[0]
    tm = min(tm, m)
    assert C_MG % tn == 0 and D_MODEL % tn == 0

    def gate_spec(c):
        return pl.BlockSpec((tm, tn), lambda i, j, c=c: (i, (C_MG + c * D_MODEL) // tn + j))

    def act_spec(width):
        return pl.BlockSpec((tm, width), lambda i, j: (i, 0))

    def w_spec(width):
        return pl.BlockSpec((width, tn), lambda i, j: (0, j))

    return pl.pallas_call(
        _merge_kernel,
        grid=(m // tm, D_MODEL // tn),
        in_specs=[act_spec(MOBA_WIDTH), act_spec(NSA_WIDTH), act_spec(GLA_WIDTH),
                  w_spec(MOBA_WIDTH), w_spec(NSA_WIDTH), w_spec(GLA_WIDTH),
                  gate_spec(0), gate_spec(1), gate_spec(2)],
        out_specs=pl.BlockSpec((tm, tn), lambda i, j: (i, j)),
        out_shape=jax.ShapeDtypeStruct((m, D_MODEL), BF16),
        compiler_params=_params("parallel", "parallel"),
        name="merge",
    )(o_m, o_n, o_g, w_m, w_n, w_g, proj2d, proj2d, proj2d)


def _first_max(vals, lane):
    top = jnp.max(vals, axis=-1, keepdims=True)
    idx = jnp.min(jnp.where(vals == top, lane, float(ROUTER_LANES)), axis=-1, keepdims=True)
    return top, idx


def _router_kernel(x_ref, g_ref, w_ref, b_ref, h_ref, route_ref):
    h = _rms(x_ref[...], g_ref[...])
    h_ref[...] = h.astype(h_ref.dtype)
    h1 = h.astype(BF16)
    r1 = h - h1.astype(F32)
    h2 = r1.astype(BF16)
    h3 = (r1 - h2.astype(F32)).astype(BF16)
    w1 = w_ref[0]
    w2 = w_ref[1]
    logits = (_dot(h1, w1) + (_dot(h1, w2) + _dot(h2, w1)) + (_dot(h2, w2) + _dot(h3, w1))) + b_ref[...]

    lane = lax.broadcasted_iota(I32, logits.shape, 1).astype(F32)
    g_logits = jnp.where(lane < MOE_GROUPS, logits, -jnp.inf)
    g_top, grp = _first_max(g_logits, lane)
    p_grp = 1.0 / jnp.sum(jnp.exp(g_logits - g_top), axis=-1, keepdims=True)
    first = MOE_GROUPS + grp * MOE_EXPERTS_PER_GROUP
    e_logits = jnp.where((lane >= first) & (lane < first + MOE_EXPERTS_PER_GROUP), logits, -jnp.inf)
    top1, lane1 = _first_max(e_logits, lane)
    top2, lane2 = _first_max(jnp.where(lane == lane1, -jnp.inf, e_logits), lane)
    ratio = jnp.exp(top2 - top1)
    w_first = p_grp / (1.0 + ratio)
    route = jnp.where(lane == 0, lane1 - MOE_GROUPS, jnp.where(lane == 1, lane2 - MOE_GROUPS,
                      jnp.where(lane == 2, w_first, jnp.where(lane == 3, w_first * ratio, 0.0))))
    route_ref[...] = route


def router(x2d, norm_g, rg_w, rg_b, re_w, re_b, tm=512):
    m, d = x2d.shape
    tm = min(tm, m)
    n_real = MOE_GROUPS + MOE_EXPERTS
    w = jnp.zeros((d, ROUTER_LANES), F32).at[:, :n_real].set(jnp.concatenate([rg_w, re_w], axis=1))
    w_hi = w.astype(BF16)
    w_lo = (w - w_hi.astype(F32)).astype(BF16)
    bias = jnp.zeros((1, ROUTER_LANES), F32).at[0, :n_real].set(jnp.concatenate([rg_b, re_b]))
    return pl.pallas_call(
        _router_kernel,
        grid=(m // tm,),
        in_specs=[pl.BlockSpec((tm, d), lambda i: (i, 0)), pl.BlockSpec((1, d), lambda i: (0, 0)),
                  pl.BlockSpec((2, d, ROUTER_LANES), lambda i: (0, 0, 0)),
                  pl.BlockSpec((1, ROUTER_LANES), lambda i: (0, 0))],
        out_specs=[pl.BlockSpec((tm, d), lambda i: (i, 0)), pl.BlockSpec((tm, ROUTER_LANES), lambda i: (i, 0))],
        out_shape=[jax.ShapeDtypeStruct((m, d), BF16), jax.ShapeDtypeStruct((m, ROUTER_LANES), F32)],
        compiler_params=_params("parallel"),
        name="router",
    )(x2d, norm_g.reshape(1, d).astype(F32), jnp.stack([w_hi, w_lo]), bias)


def _expert_kernel(blk_e_ref, n_used_ref, x_ref, wg_ref, wu_ref, wd_ref, o_ref, wg_bf, wu_bf, wd_bf):
    i = pl.program_id(0)
    new_expert = (i == 0) | (blk_e_ref[i] != blk_e_ref[jnp.maximum(i - 1, 0)])

    @pl.when(new_expert)
    def _():
        wg_bf[...] = wg_ref[...].astype(BF16)
        wu_bf[...] = wu_ref[...].astype(BF16)
        wd_bf[...] = wd_ref[...].astype(BF16)

    @pl.when(i < n_used_ref[0])
    def _():
        x = x_ref[...]
        gate = _dot(x, wg_bf[...])
        hid = gate * jax.nn.sigmoid(gate) * _dot(x, wu_bf[...])
        o_ref[...] = _dot(hid.astype(BF16), wd_bf[...]).astype(o_ref.dtype)

    @pl.when(i >= n_used_ref[0])
    def _():
        o_ref[...] = jnp.zeros_like(o_ref)


def expert_blocks(xs, blk_e, n_used, layer, w_gate, w_up, w_down):
    p, d = xs.shape
    ff = w_gate.shape[3]
    n_blk = p // MOE_ROWS
    grid_spec = pltpu.PrefetchScalarGridSpec(
        num_scalar_prefetch=2,
        grid=(n_blk,),
        in_specs=[pl.BlockSpec((MOE_ROWS, d), lambda i, e, n: (i, 0)),
                  pl.BlockSpec((None, None, d, ff), lambda i, e, n: (layer, e[i], 0, 0)),
                  pl.BlockSpec((None, None, d, ff), lambda i, e, n: (layer, e[i], 0, 0)),
                  pl.BlockSpec((None, None, ff, d), lambda i, e, n: (layer, e[i], 0, 0))],
        out_specs=pl.BlockSpec((MOE_ROWS, d), lambda i, e, n: (i, 0)),
        scratch_shapes=[pltpu.VMEM((d, ff), BF16), pltpu.VMEM((d, ff), BF16), pltpu.VMEM((ff, d), BF16)],
    )
    return pl.pallas_call(
        _expert_kernel,
        grid_spec=grid_spec,
        out_shape=jax.ShapeDtypeStruct((p, d), BF16),
        compiler_params=_params("arbitrary"),
        name="experts",
    )(blk_e, n_used, xs, w_gate, w_up, w_down)


def _combine_kernel(x_ref, y0_ref, y1_ref, w_ref, g_ref, o_ref, *, final_norm):
    w = w_ref[...]
    x = x_ref[...] + (w[:, 0:1] * y0_ref[...].astype(F32) + w[:, 1:2] * y1_ref[...].astype(F32))
    o_ref[...] = _rms(x, g_ref[...]) if final_norm else x


def combine(x2d, y0, y1, w, norm_g, final_norm, tm=512):
    m, d = x2d.shape
    tm = min(tm, m)
    row = pl.BlockSpec((tm, d), lambda i: (i, 0))
    return pl.pallas_call(
        functools.partial(_combine_kernel, final_norm=final_norm),
        grid=(m // tm,),
        in_specs=[row, row, row, pl.BlockSpec((tm, MOE_TOPK), lambda i: (i, 0)),
                  pl.BlockSpec((1, d), lambda i: (0, 0))],
        out_specs=row,
        out_shape=jax.ShapeDtypeStruct((m, d), F32),
        compiler_params=_params("parallel"),
        name="combine",
    )(x2d, y0, y1, w, norm_g.reshape(1, d).astype(F32))


def hier_moe(x2d, norm_g, rg_w, rg_b, re_w, re_b, layer, w_gate, w_up, w_down, out_norm_g, final_norm):
    t = x2d.shape[0]
    h, route = router(x2d, norm_g, rg_w, rg_b, re_w, re_b)
    expert = route[:, 0:MOE_TOPK].astype(I32)
    w = route[:, MOE_TOPK:2 * MOE_TOPK]

    a = t * MOE_TOPK
    e_flat = expert.reshape(a)
    onehot = (e_flat[:, None] == jnp.arange(MOE_EXPERTS, dtype=I32)[None, :]).astype(I32)
    running = jnp.cumsum(onehot, axis=0)
    counts = running[-1]
    rank = jnp.take_along_axis(running, e_flat[:, None], axis=1)[:, 0] - 1
    padded = (counts + MOE_ROWS - 1) // MOE_ROWS * MOE_ROWS
    pends = jnp.cumsum(padded)
    dest = (pends - padded)[e_flat] + rank
    p_rows = (a + MOE_EXPERTS * (MOE_ROWS - 1)) // MOE_ROWS * MOE_ROWS
    n_blk = p_rows // MOE_ROWS
    row_tok = (jnp.arange(p_rows, dtype=I32) % t).at[dest].set(jnp.arange(a, dtype=I32) // MOE_TOPK)
    blk_start = jnp.arange(n_blk, dtype=I32) * MOE_ROWS
    blk_e = jnp.minimum(jnp.sum((pends[None, :] <= blk_start[:, None]).astype(I32), axis=1), MOE_EXPERTS - 1)
    n_used = (pends[-1] // MOE_ROWS).astype(I32).reshape(1)

    xs = jnp.take(h, row_tok, axis=0)
    y_rows = expert_blocks(xs, blk_e, n_used, layer, w_gate, w_up, w_down)
    dest2 = dest.reshape(t, MOE_TOPK)
    y0 = jnp.take(y_rows, dest2[:, 0], axis=0)
    y1 = jnp.take(y_rows, dest2[:, 1], axis=0)
    return combine(x2d, y0, y1, w, out_norm_g, final_norm)


PERM_TILE = 512
PERM_REGIONS = ((0, 0, C_GQ // PERM_TILE),
                (_SRC_GQ - C_GQ, C_GQ // PERM_TILE, C_GG // PERM_TILE),
                (_SRC_GG - C_GG, C_GG // PERM_TILE, C_SMALL // PERM_TILE))


def _permute_kernel(a_ref, b_ref, small_ref, o_ref):
    j = pl.program_id(1)
    for shift, j0, j1 in PERM_REGIONS:
        def copy(shift=shift):
            if shift == 0:
                o_ref[...] = a_ref[...].astype(o_ref.dtype)
            else:
                src = jnp.concatenate([a_ref[...], b_ref[...]], axis=1)
                o_ref[...] = src[:, shift:shift + PERM_TILE].astype(o_ref.dtype)
        pl.when((j >= j0) & (j < j1))(copy)

    @pl.when(j == C_SMALL // PERM_TILE)
    def _():
        o_ref[:, 0:LANES] = small_ref[...].astype(o_ref.dtype)


def permute_w_in(w_in, layer, tk=512):
    assert C_GQ % PERM_TILE == 0 and C_GG % PERM_TILE == 0 and C_SMALL % PERM_TILE == 0
    assert all(0 <= shift < LANES for shift, _, _ in PERM_REGIONS)
    d = w_in.shape[1]
    last_lane_tile = (_SRC_END - 1) // LANES
    per_tile = PERM_TILE // LANES
    small = jnp.concatenate([w_in[layer, :, _SRC_GA:_SRC_GG], w_in[layer, :, _SRC_NG:_SRC_GQ],
                             jnp.zeros((d, LANES - GLA_RANK - 3 * NSA_HEADS), w_in.dtype)], axis=1)
    return pl.pallas_call(
        _permute_kernel,
        grid=(d // tk, pl.cdiv(PROJ_WIDTH, PERM_TILE)),
        in_specs=[pl.BlockSpec((None, tk, PERM_TILE), lambda k, j: (layer, k, j)),
                  pl.BlockSpec((None, tk, LANES),
                               lambda k, j: (layer, k, jnp.minimum((j + 1) * per_tile, last_lane_tile))),
                  pl.BlockSpec((tk, LANES), lambda k, j: (k, 0))],
        out_specs=pl.BlockSpec((tk, PERM_TILE), lambda k, j: (k, j)),
        out_shape=jax.ShapeDtypeStruct((d, PROJ_WIDTH), BF16),
        compiler_params=_params("parallel", "parallel"),
        name="permute_w_in",
    )(w_in, w_in, small)


def hybrid_layer(x, norm1_g, w_in, nsa_cmp_pe, nsa_cmp_w1, nsa_cmp_w2, gla_wa, gla_ba, gla_norm_g,
                 w_br_moba, w_br_nsa, w_br_gla, w_out, norm2_g, router_group_w, router_group_b,
                 router_expert_w, router_expert_b, layer, expert_w_gate, expert_w_up, expert_w_down,
                 out_norm_g, final_norm):
    batch, seq, d = x.shape
    t = batch * seq
    x2d = x.reshape(t, d)
    h = rmsnorm(x2d, norm1_g, BF16)
    proj2d = matmul(h, permute_w_in(w_in, layer), BF16)
    proj = proj2d.reshape(batch, seq, PROJ_WIDTH)
    o_m = moba_attention(proj, batch, seq)
    kv_c, kv_ct = nsa_compress(proj, nsa_cmp_pe, nsa_cmp_w1, nsa_cmp_w2, batch, seq)
    o_n = nsa_attention(proj, kv_c, kv_ct, batch, seq)
    o_g = gla_attention(proj, gla_wa, gla_ba, gla_norm_g, batch, seq)
    merged = merge_branches(o_m.reshape(t, -1), o_n.reshape(t, -1), o_g.reshape(t, -1),
                            w_br_moba.astype(BF16), w_br_nsa.astype(BF16), w_br_gla.astype(BF16), proj2d)
    x2d = matmul_residual(merged, w_out.astype(BF16), x2d)
    x2d = hier_moe(x2d, norm2_g, router_group_w, router_group_b, router_expert_w, router_expert_b,
                   layer, expert_w_gate, expert_w_up, expert_w_down, out_norm_g, final_norm)
    return x2d.reshape(batch, seq, d)


def kernel(x, norm1_g, w_in, nsa_cmp_pe, nsa_cmp_w1, nsa_cmp_w2, gla_wa, gla_ba, gla_norm_g, w_br_moba,
           w_br_nsa, w_br_gla, w_out, norm2_g, router_group_w, router_group_b, router_expert_w,
           router_expert_b, expert_w_gate, expert_w_up, expert_w_down, final_norm_g):
    for l in range(DEPTH):
        x = hybrid_layer(x, norm1_g[l], w_in, nsa_cmp_pe[l], nsa_cmp_w1[l], nsa_cmp_w2[l], gla_wa[l],
                         gla_ba[l], gla_norm_g[l], w_br_moba[l], w_br_nsa[l], w_br_gla[l], w_out[l],
                         norm2_g[l], router_group_w[l], router_group_b[l], router_expert_w[l],
                         router_expert_b[l], l, expert_w_gate, expert_w_up, expert_w_down,
                         final_norm_g, l == DEPTH - 1)
    return x
```

```python
import functools

import jax
import jax.numpy as jnp
import numpy as np
from jax import lax
from jax.experimental import pallas as pl
from jax.experimental.pallas import tpu as pltpu

F32 = jnp.float32
BF16 = jnp.bfloat16
I32 = jnp.int32

D_MODEL = 2048
DEPTH = 2
HEAD_DIM = 128
NEG = -1e30
FORCE = 1e9
EPS = 1e-6
LOG2E = 1.4426950408889634

MOBA_HEADS = 8
MOBA_BLOCK = 256
MOBA_TOPK = 3
MOBA_WIDTH = MOBA_HEADS * HEAD_DIM

NSA_HEADS = 8
NSA_KV_HEADS = 2
NSA_GROUP = NSA_HEADS // NSA_KV_HEADS
NSA_CMP_LEN = 32
NSA_CMP_STRIDE = 16
NSA_SLC_BLOCK = 64
NSA_TOPN = 8
NSA_WINDOW = 512
NSA_WIDTH = NSA_HEADS * HEAD_DIM
NSA_KV_WIDTH = NSA_KV_HEADS * HEAD_DIM

GLA_HEADS = 4
GLA_DK = 128
GLA_DV = 256
GLA_RANK = 16
GLA_TAU = 16.0
GLA_KEY_WIDTH = GLA_HEADS * GLA_DK
GLA_WIDTH = GLA_HEADS * GLA_DV

N_BRANCH = 3
MOE_GROUPS = 4
MOE_EXPERTS_PER_GROUP = 8
MOE_EXPERTS = MOE_GROUPS * MOE_EXPERTS_PER_GROUP
MOE_TOPK = 2
MOE_FF = D_MODEL // 4

LANES = 128
SUBLANES = 8
BF16_SUBLANES = 16
VMEM_LIMIT_BYTES = 56 * 1024 * 1024

_SRC_NG = MOBA_WIDTH * 3 + NSA_WIDTH + 6 * NSA_KV_WIDTH
_SRC_GQ = _SRC_NG + 3 * NSA_HEADS
_SRC_GA = _SRC_GQ + 2 * GLA_KEY_WIDTH + GLA_WIDTH
_SRC_GG = _SRC_GA + GLA_RANK
_SRC_END = _SRC_GG + GLA_WIDTH + N_BRANCH * D_MODEL

C_MQ = 0
C_MK = C_MQ + MOBA_WIDTH
C_MV = C_MK + MOBA_WIDTH
C_NQ = C_MV + MOBA_WIDTH
C_NKV = C_NQ + NSA_WIDTH
C_GQ = C_NKV + 6 * NSA_KV_WIDTH
C_GK = C_GQ + GLA_KEY_WIDTH
C_GV = C_GK + GLA_KEY_WIDTH
C_GG = C_GV + GLA_WIDTH
C_MG = C_GG + GLA_WIDTH
C_SMALL = C_MG + N_BRANCH * D_MODEL
PROJ_WIDTH = C_SMALL + LANES
SMALL_GA_LANE = 0
SMALL_NG_LANE = GLA_RANK

ROUTER_LANES = LANES
MOE_ROWS = 512


def _params(*semantics):
    return pltpu.CompilerParams(dimension_semantics=semantics, vmem_limit_bytes=VMEM_LIMIT_BYTES)


def _dot(a, b):
    return jnp.dot(a, b, preferred_element_type=F32)


def _dot_nt(a, b):
    return lax.dot_general(a, b, (((1,), (1,)), ((), ())), preferred_element_type=F32)


def _dot_split3(a01, x):
    x1 = x.astype(BF16)
    r1 = x - x1.astype(F32)
    x2 = r1.astype(BF16)
    x3 = (r1 - x2.astype(F32)).astype(BF16)
    return _dot(a01, x1) + _dot(a01, x2) + _dot(a01, x3)


def _rms(x, g):
    return x * lax.rsqrt(jnp.mean(x * x, axis=-1, keepdims=True) + EPS) * g


def _rmsnorm_kernel(x_ref, g_ref, o_ref):
    o_ref[...] = _rms(x_ref[...].astype(F32), g_ref[...]).astype(o_ref.dtype)


def rmsnorm(x2d, g, out_dtype, tm=512):
    m, d = x2d.shape
    tm = min(tm, m)
    return pl.pallas_call(
        _rmsnorm_kernel,
        grid=(m // tm,),
        in_specs=[pl.BlockSpec((tm, d), lambda i: (i, 0)), pl.BlockSpec((1, d), lambda i: (0, 0))],
        out_specs=pl.BlockSpec((tm, d), lambda i: (i, 0)),
        out_shape=jax.ShapeDtypeStruct((m, d), out_dtype),
        compiler_params=_params("parallel"),
        name="rmsnorm",
    )(x2d, g.reshape(1, d).astype(F32))


def _matmul_kernel(a_ref, b_ref, o_ref):
    o_ref[...] = _dot(a_ref[...], b_ref[...]).astype(o_ref.dtype)


def matmul(a, b, out_dtype, tm=1024, tn=1152):
    m, k = a.shape
    n = b.shape[1]
    tm, tn = min(tm, m), min(tn, n)
    return pl.pallas_call(
        _matmul_kernel,
        grid=(m // tm, n // tn),
        in_specs=[pl.BlockSpec((tm, k), lambda i, j: (i, 0)), pl.BlockSpec((k, tn), lambda i, j: (0, j))],
        out_specs=pl.BlockSpec((tm, tn), lambda i, j: (i, j)),
        out_shape=jax.ShapeDtypeStruct((m, n), out_dtype),
        compiler_params=_params("parallel", "parallel"),
        name="matmul",
    )(a, b)


def _matmul_residual_kernel(a_ref, b_ref, r_ref, o_ref):
    o_ref[...] = r_ref[...] + _dot(a_ref[...], b_ref[...])


def matmul_residual(a, b, res, tm=1024, tn=1024):
    m, k = a.shape
    n = b.shape[1]
    tm, tn = min(tm, m), min(tn, n)
    return pl.pallas_call(
        _matmul_residual_kernel,
        grid=(m // tm, n // tn),
        in_specs=[pl.BlockSpec((tm, k), lambda i, j: (i, 0)), pl.BlockSpec((k, tn), lambda i, j: (0, j)),
                  pl.BlockSpec((tm, tn), lambda i, j: (i, j))],
        out_specs=pl.BlockSpec((tm, tn), lambda i, j: (i, j)),
        out_shape=jax.ShapeDtypeStruct((m, n), F32),
        compiler_params=_params("parallel", "parallel"),
        name="matmul_residual",
    )(a, b, res)


def _softmax_step(carry, q, k, v, mask, scale):
    m_i, l_i, acc = carry
    s = jnp.where(mask, _dot_nt(q, k) * scale, NEG)
    m_new = jnp.maximum(m_i, jnp.max(s, axis=-1, keepdims=True))
    alpha = jnp.exp(m_i - m_new)
    p = jnp.where(mask, jnp.exp(s - m_new), 0.0)
    l_new = alpha * l_i + jnp.sum(p, axis=-1, keepdims=True)
    acc_new = alpha * acc + _dot(p.astype(v.dtype), v)
    return m_new, l_new, acc_new


def _softmax_init(m, e):
    return jnp.full((m, 1), NEG, F32), jnp.zeros((m, 1), F32), jnp.zeros((m, e), F32)


def _rank_desc(vals, n_candidates):
    lane = lax.broadcasted_iota(I32, vals.shape, 1)
    rank = jnp.zeros(vals.shape, F32)
    for m in range(n_candidates):
        vm = vals[:, m:m + 1]
        beats = (vm > vals) | ((vm == vals) & (lane > m))
        rank = rank + beats.astype(F32)
    return rank


def _softmax_step_t(carry, q, k, v_t, bias, scale):
    m_i, l_i, acc = carry
    s = _dot_nt(k, q) * scale + bias
    m_new = jnp.maximum(m_i, jnp.max(s, axis=0, keepdims=True))
    alpha = jnp.exp(m_i - m_new)
    p = jnp.exp(s - m_new)
    l_new = alpha * l_i + jnp.sum(p, axis=0, keepdims=True)
    acc_new = alpha * acc + _dot(v_t, p.astype(v_t.dtype))
    return m_new, l_new, acc_new


def _softmax_init_t(m, e):
    return jnp.full((1, m), NEG, F32), jnp.zeros((1, m), F32), jnp.zeros((e, m), F32)


def _rank_desc_rows(vals, n_candidates):
    row = lax.broadcasted_iota(I32, vals.shape, 0)
    rank = jnp.zeros(vals.shape, F32)
    for m in range(n_candidates):
        vm = vals[m:m + 1, :]
        beats = (vm > vals) | ((vm == vals) & (row > m))
        rank = rank + beats.astype(F32)
    return rank


def _lane_column(x, idx):
    lane = lax.broadcasted_iota(I32, x.shape, 1)
    return jnp.sum(jnp.where(lane == idx, x, 0.0), axis=-1, keepdims=True)


def _moba_kernel(q_ref, k_ref, vt_ref, o_ref, kmean_ref, bias_ref, s_ref, p_ref, acc_ref, *, n_blocks, topk,
                 heads):
    blk = MOBA_BLOCK
    cur = pl.program_id(2)
    scale = HEAD_DIM ** -0.5
    cols = [slice(h * HEAD_DIM, (h + 1) * HEAD_DIM) for h in range(heads)]

    @pl.when(cur == 0)
    def _():
        kmean_ref[...] = jnp.zeros_like(kmean_ref)
        for h in range(heads):
            for n in range(n_blocks):
                k_blk = k_ref[n * blk:(n + 1) * blk, cols[h]].astype(F32)
                kmean_ref[h, n:n + 1, :] = jnp.sum(k_blk, axis=0, keepdims=True) * (1.0 / blk)

    for h in range(heads):
        gate = _dot_nt(kmean_ref[h].astype(BF16), q_ref[:, cols[h]])
        row = lax.broadcasted_iota(I32, gate.shape, 0)
        valid = row < cur
        gate = jnp.where(valid, gate, NEG)
        chosen = valid & (_rank_desc_rows(gate, n_blocks) < topk)
        bias_ref[h] = jnp.where(chosen, 0.0, NEG)

    seq = k_ref.shape[0]
    lo_blocks = (n_blocks + 1) // 2
    lo = lo_blocks * blk
    need_hi = cur >= lo_blocks

    for h in range(heads):
        s_ref[h, 0:lo, :] = _dot_nt(k_ref[0:lo, cols[h]], q_ref[:, cols[h]])

    if lo < seq:
        @pl.when(need_hi)
        def _():
            for h in range(heads):
                s_ref[h, lo:seq, :] = _dot_nt(k_ref[lo:seq, cols[h]], q_ref[:, cols[h]])

    sub = LANES
    base = pl.multiple_of(cur * blk, blk)
    qry_i = lax.broadcasted_iota(I32, (sub, blk), 1)
    key_i = [j * sub + lax.broadcasted_iota(I32, (sub, blk), 0) for j in range(blk // sub)]
    exp_scale = scale * LOG2E

    def fold_max(x):
        return jnp.max(x.reshape(sub // SUBLANES, SUBLANES, blk), axis=0)

    def fold_sum(x):
        return jnp.sum(x.reshape(sub // SUBLANES, SUBLANES, blk), axis=0)

    def tiles(h, start):
        return [s_ref[h, pl.ds(pl.multiple_of(start + j * sub, sub), sub), :] for j in range(blk // sub)]

    def max_body(n, m8):
        out = []
        for h in range(heads):
            blk_max = functools.reduce(jnp.maximum, [fold_max(t) for t in tiles(h, n * blk)])
            out.append(jnp.maximum(m8[h], blk_max * scale + bias_ref[h, pl.ds(n, 1), :]))
        return tuple(out)

    m8 = []
    for h in range(heads):
        own = [fold_max(jnp.where(key_i[j] <= qry_i, t, NEG)) for j, t in enumerate(tiles(h, base))]
        m8.append(functools.reduce(jnp.maximum, own) * scale)
    m8 = lax.fori_loop(0, cur, max_body, tuple(m8))
    m_log2 = [jnp.max(m, axis=0, keepdims=True) * LOG2E for m in m8]

    def store_p(h, start, j, p):
        p_ref[h, pl.ds(pl.multiple_of(start + j * sub, sub), sub), :] = p.astype(p_ref.dtype)

    def prob_body(n, l8):
        out = []
        for h in range(heads):
            shift = bias_ref[h, pl.ds(n, 1), :] * LOG2E - m_log2[h]
            acc = l8[h]
            for j, t in enumerate(tiles(h, n * blk)):
                p = jnp.exp2(t * exp_scale + shift)
                store_p(h, n * blk, j, p)
                acc = acc + fold_sum(p)
            out.append(acc)
        return tuple(out)

    l8 = []
    for h in range(heads):
        acc = jnp.zeros((SUBLANES, blk), F32)
        for j, t in enumerate(tiles(h, base)):
            p = jnp.where(key_i[j] <= qry_i, jnp.exp2(t * exp_scale - m_log2[h]), 0.0)
            store_p(h, base, j, p)
            acc = acc + fold_sum(p)
        l8.append(acc)
    l8 = lax.fori_loop(0, cur, prob_body, tuple(l8))

    def zero_body(n, carry):
        for h in range(heads):
            for j in range(blk // sub):
                store_p(h, n * blk, j, jnp.zeros((sub, blk), F32))
        return carry

    lax.fori_loop(cur + 1, jnp.where(need_hi, n_blocks, lo_blocks), zero_body, 0)
    for h in range(heads):
        acc_ref[h] = _dot(vt_ref[h, :, 0:lo], p_ref[h, 0:lo, :])

    if lo < seq:
        @pl.when(need_hi)
        def _():
            for h in range(heads):
                acc_ref[h] += _dot(vt_ref[h, :, lo:seq], p_ref[h, lo:seq, :])

    for h in range(heads):
        l_sum = jnp.sum(l8[h], axis=0, keepdims=True)
        o_ref[:, cols[h]] = (acc_ref[h] / l_sum).T.astype(o_ref.dtype)


def moba_attention(proj, batch, seq, heads=2):
    assert seq % MOBA_BLOCK == 0 and MOBA_HEADS % heads == 0
    n_blocks = seq // MOBA_BLOCK
    rows = -(-n_blocks // BF16_SUBLANES) * BF16_SUBLANES
    width = heads * HEAD_DIM
    qb, kb = C_MQ // width, C_MK // width
    v_t = proj[:, :, C_MV:C_MV + MOBA_WIDTH].reshape(batch, seq, MOBA_HEADS, HEAD_DIM).transpose(0, 2, 3, 1)
    kern = functools.partial(_moba_kernel, n_blocks=n_blocks, topk=min(MOBA_TOPK, n_blocks), heads=heads)
    return pl.pallas_call(
        kern,
        grid=(batch, MOBA_HEADS // heads, n_blocks),
        in_specs=[pl.BlockSpec((None, MOBA_BLOCK, width), lambda b, h, i: (b, i, qb + h)),
                  pl.BlockSpec((None, seq, width), lambda b, h, i: (b, 0, kb + h)),
                  pl.BlockSpec((None, heads, HEAD_DIM, seq), lambda b, h, i: (b, h, 0, 0))],
        out_specs=pl.BlockSpec((None, MOBA_BLOCK, width), lambda b, h, i: (b, i, h)),
        out_shape=jax.ShapeDtypeStruct((batch, seq, MOBA_WIDTH), BF16),
        scratch_shapes=[pltpu.VMEM((heads, rows, HEAD_DIM), F32), pltpu.VMEM((heads, rows, MOBA_BLOCK), F32),
                        pltpu.VMEM((heads, seq, MOBA_BLOCK), F32), pltpu.VMEM((heads, seq, MOBA_BLOCK), BF16),
                        pltpu.VMEM((heads, HEAD_DIM, MOBA_BLOCK), F32)],
        compiler_params=_params("parallel", "parallel", "arbitrary"),
        name="moba",
    )(proj, proj, v_t)


def _nsa_compress_kernel(x_ref, pe_ref, w1_ref, w2_ref, o_ref, ot_ref):
    x = x_ref[...].astype(F32)
    half = x.shape[1]
    lo = _dot((x + pe_ref[:, :half]).astype(BF16), w1_ref[:half, :])
    hi = _dot((x + pe_ref[:, half:]).astype(BF16), w1_ref[half:, :])
    pre = lo + pltpu.roll(hi, hi.shape[0] - 1, 0)
    hid = pre * jax.nn.sigmoid(pre)
    out = _dot(hid.astype(BF16), w2_ref[...])
    o_ref[...] = out.astype(o_ref.dtype)
    ot_ref[...] = out.T.astype(ot_ref.dtype)


def nsa_compress(proj, cmp_pe, cmp_w1, cmp_w2, batch, seq):
    n16 = seq // NSA_CMP_STRIDE
    width = NSA_CMP_STRIDE * HEAD_DIM
    x = proj[:, :, C_NKV:C_NKV + 2 * NSA_KV_WIDTH].reshape(batch, seq, 2, NSA_KV_HEADS, HEAD_DIM)
    x = x.transpose(0, 2, 3, 1, 4).reshape(batch, 2, NSA_KV_HEADS, n16, width)
    pe = cmp_pe.reshape(2, 1, NSA_CMP_LEN * HEAD_DIM).astype(F32)
    return pl.pallas_call(
        _nsa_compress_kernel,
        grid=(batch, 2, NSA_KV_HEADS),
        in_specs=[pl.BlockSpec((None, None, None, n16, width), lambda b, c, g: (b, c, g, 0, 0)),
                  pl.BlockSpec((None, 1, 2 * width), lambda b, c, g: (c, 0, 0)),
                  pl.BlockSpec((None, 2 * width, HEAD_DIM), lambda b, c, g: (c, 0, 0)),
                  pl.BlockSpec((None, HEAD_DIM, HEAD_DIM), lambda b, c, g: (c, 0, 0))],
        out_specs=[pl.BlockSpec((None, None, None, n16, HEAD_DIM), lambda b, c, g: (b, c, g, 0, 0)),
                   pl.BlockSpec((None, None, None, HEAD_DIM, n16), lambda b, c, g: (b, c, g, 0, 0))],
        out_shape=[jax.ShapeDtypeStruct((batch, 2, NSA_KV_HEADS, n16, HEAD_DIM), BF16),
                   jax.ShapeDtypeStruct((batch, 2, NSA_KV_HEADS, HEAD_DIM, n16), BF16)],
        compiler_params=_params("parallel", "parallel", "parallel"),
        name="nsa_compress",
    )(x, pe, cmp_w1.astype(BF16), cmp_w2.astype(BF16))


def _nsa_kernel(q_ref, kc_ref, vct_ref, ks_ref, vst_ref, kw_ref, vwt_ref, ngt_ref, o_ref,
                bias_ref, s_ref, p_ref, sw_ref, pw_ref, out_ref, *, tq, span, n_cmp, n_slc, topn, win_len):
    g = pl.program_id(1)
    qi = pl.program_id(2)
    seq = ks_ref.shape[0]
    n16 = kc_ref.shape[0]
    lanes = NSA_GROUP * tq
    blk = NSA_SLC_BLOCK
    scale = HEAD_DIM ** -0.5
    exp_scale = scale * LOG2E
    start = qi * tq
    heads = [slice(r * tq, (r + 1) * tq) for r in range(NSA_GROUP)]
    q = [q_ref[:, r * HEAD_DIM:(r + 1) * HEAD_DIM] for r in range(NSA_GROUP)]

    def per_group(row):
        return jnp.concatenate([row] * NSA_GROUP, axis=1)

    def scores(keys):
        return jnp.concatenate([_dot_nt(keys, q[r]) for r in range(NSA_GROUP)], axis=1)

    def fold(op, x):
        return op(x.reshape(x.shape[0] // SUBLANES, SUBLANES, lanes), axis=0)

    pos1 = start + lax.broadcasted_iota(I32, (1, tq), 1)
    pos = per_group(pos1)

    n_idx = lax.broadcasted_iota(I32, (n16, lanes), 0)
    in_range = n_idx < n_cmp
    cmask = (n_idx * NSA_CMP_STRIDE + (NSA_CMP_LEN - 1) <= pos) & in_range
    s_c = jnp.where(cmask, scores(kc_ref[...]) * scale, NEG)
    e_c = jnp.where(in_range, jnp.exp(s_c - jnp.max(s_c, axis=0, keepdims=True)), 0.0)
    p_c = jnp.where(cmask, e_c / jnp.sum(e_c, axis=0, keepdims=True), 0.0)
    out_ref[0] = _dot(vct_ref[...], p_c.astype(BF16))

    p_sum = functools.reduce(lambda a, b: a + b, [p_c[:, h] for h in heads])
    rows = bias_ref.shape[0]
    oj = lax.broadcasted_iota(I32, (rows, n16), 0)
    on = lax.broadcasted_iota(I32, (rows, n16), 1)
    overlap_t = ((on * NSA_CMP_STRIDE < (oj + 1) * blk) & (on * NSA_CMP_STRIDE + (NSA_CMP_LEN - 1) >= oj * blk)
                 & (on < n_cmp) & (oj < n_slc)).astype(BF16)
    p_hi = p_sum.astype(BF16)
    p_lo = (p_sum - p_hi.astype(F32)).astype(BF16)
    imp = _dot(overlap_t, p_hi) + _dot(overlap_t, p_lo)
    j_idx = lax.broadcasted_iota(I32, (rows, tq), 0)
    cur_blk = pos1 // blk
    forced = (j_idx == 0) | (j_idx == cur_blk) | (j_idx == cur_blk - 1)
    imp = jnp.where(forced, FORCE, imp)
    imp = jnp.where(j_idx > cur_blk, NEG, imp)
    chosen = (_rank_desc_rows(imp, n_slc) < topn) & (j_idx <= cur_blk)
    bias_ref[...] = jnp.where(chosen, 0.0, NEG)

    n_spans = seq // span
    for k in range(n_spans):
        def span_scores(k=k):
            s_ref[k * span:(k + 1) * span, :] = scores(ks_ref[k * span:(k + 1) * span, :])
        if k == 0:
            span_scores()
        else:
            pl.when(start >= k * span)(span_scores)

    per_tile = tq // blk
    first_own = qi * per_tile

    def block_bias(j):
        return per_group(bias_ref[pl.ds(j, 1), :])

    def block_rows(j):
        return pl.ds(pl.multiple_of(j * blk, blk), blk)

    def causal_mask(j):
        key = j * blk + lax.broadcasted_iota(I32, (blk, lanes), 0)
        return key <= pos

    def max_body(i, m8):
        for d in range(per_tile):
            j = i * per_tile + d
            m8 = jnp.maximum(m8, fold(jnp.max, s_ref[block_rows(j), :]) * scale + block_bias(j))
        return m8

    m8 = lax.fori_loop(0, qi, max_body, jnp.full((SUBLANES, lanes), NEG, F32))
    for d in range(per_tile):
        j = first_own + d
        own = jnp.where(causal_mask(j), s_ref[block_rows(j), :], NEG)
        m8 = jnp.maximum(m8, fold(jnp.max, own) * scale + block_bias(j))
    m_log2 = jnp.max(m8, axis=0, keepdims=True) * LOG2E

    def prob_body(i, l8):
        for d in range(per_tile):
            j = i * per_tile + d
            p = jnp.exp2(s_ref[block_rows(j), :] * exp_scale + (block_bias(j) * LOG2E - m_log2))
            p_ref[block_rows(j), :] = p.astype(p_ref.dtype)
            l8 = l8 + fold(jnp.sum, p)
        return l8

    l8 = lax.fori_loop(0, qi, prob_body, jnp.zeros((SUBLANES, lanes), F32))
    for d in range(per_tile):
        j = first_own + d
        p = jnp.exp2(s_ref[block_rows(j), :] * exp_scale + (block_bias(j) * LOG2E - m_log2))
        p = jnp.where(causal_mask(j), p, 0.0)
        p_ref[block_rows(j), :] = p.astype(p_ref.dtype)
        l8 = l8 + fold(jnp.sum, p)

    def zero_body(i, carry):
        p_ref[pl.ds(pl.multiple_of(i * tq, tq), tq), :] = jnp.zeros((tq, lanes), p_ref.dtype)
        return carry

    visible_tiles = (start // span + 1) * (span // tq)
    lax.fori_loop(qi + 1, visible_tiles, zero_body, 0)

    for k in range(n_spans):
        def span_pv(k=k):
            pv = _dot(vst_ref[:, k * span:(k + 1) * span], p_ref[k * span:(k + 1) * span, :])
            if k == 0:
                out_ref[1] = pv
            else:
                out_ref[1] += pv
        if k == 0:
            span_pv()
        else:
            pl.when(start >= k * span)(span_pv)
    out_ref[1] = out_ref[1] / jnp.sum(l8, axis=0, keepdims=True)

    w0 = pl.multiple_of(jnp.maximum(start + tq - win_len, 0), tq)
    sw_ref[...] = scores(kw_ref[pl.ds(w0, win_len), :])
    n_tiles = win_len // blk

    def win_tile(t):
        key = w0 + t * blk + lax.broadcasted_iota(I32, (blk, lanes), 0)
        return sw_ref[t * blk:(t + 1) * blk, :], (key <= pos) & (key > pos - NSA_WINDOW)

    m8 = jnp.full((SUBLANES, lanes), NEG, F32)
    for t in range(n_tiles):
        tile, mask = win_tile(t)
        m8 = jnp.maximum(m8, fold(jnp.max, jnp.where(mask, tile, NEG)))
    m_log2 = jnp.max(m8, axis=0, keepdims=True) * exp_scale
    l8 = jnp.zeros((SUBLANES, lanes), F32)
    for t in range(n_tiles):
        tile, mask = win_tile(t)
        p = jnp.where(mask, jnp.exp2(tile * exp_scale - m_log2), 0.0)
        pw_ref[t * blk:(t + 1) * blk, :] = p.astype(pw_ref.dtype)
        l8 = l8 + fold(jnp.sum, p)
    w_tile0 = w0 // tq
    pv = _dot(vwt_ref[w_tile0], pw_ref[0:tq, :])
    for c in range(1, win_len // tq):
        pv = pv + _dot(vwt_ref[w_tile0 + c], pw_ref[c * tq:(c + 1) * tq, :])
    out_ref[2] = pv / jnp.sum(l8, axis=0, keepdims=True)

    for r in range(NSA_GROUP):
        gate_row = (g * NSA_GROUP + r) * 3
        mix = jnp.zeros((HEAD_DIM, tq), F32)
        for c in range(3):
            mix = mix + jax.nn.sigmoid(ngt_ref[pl.ds(gate_row + c, 1), :]) * out_ref[c, :, heads[r]]
        o_ref[:, r * HEAD_DIM:(r + 1) * HEAD_DIM] = mix.T.astype(o_ref.dtype)


def nsa_attention(proj, kv_c, kv_ct, batch, seq, tq=128):
    assert tq == LANES and seq % tq == 0 and tq % NSA_SLC_BLOCK == 0
    n_cmp = (seq - NSA_CMP_LEN) // NSA_CMP_STRIDE + 1
    n_slc = seq // NSA_SLC_BLOCK
    rows = -(-n_slc // BF16_SUBLANES) * BF16_SUBLANES
    span = max(tq, seq // 4)
    win_len = min(NSA_WINDOW + tq, seq)
    n16 = seq // NSA_CMP_STRIDE
    gw = NSA_GROUP * HEAD_DIM
    lanes = NSA_GROUP * tq
    nkv = C_NKV // HEAD_DIM

    def kv_cols(slot):
        c0 = C_NKV + slot * NSA_KV_WIDTH
        return proj[:, :, c0:c0 + NSA_KV_WIDTH].reshape(batch, seq, NSA_KV_HEADS, HEAD_DIM)

    vs_t = kv_cols(3).transpose(0, 2, 3, 1)
    vw_t = kv_cols(5).reshape(batch, seq // tq, tq, NSA_KV_HEADS, HEAD_DIM).transpose(0, 3, 1, 4, 2)
    n_gates = 3 * NSA_HEADS
    ng_t = proj[:, :, C_SMALL + SMALL_NG_LANE:C_SMALL + SMALL_NG_LANE + n_gates].astype(F32).transpose(0, 2, 1)

    def k_spec(slot):
        return pl.BlockSpec((None, seq, HEAD_DIM),
                            lambda b, g, i, slot=slot: (b, 0, nkv + slot * NSA_KV_HEADS + g))

    kern = functools.partial(_nsa_kernel, tq=tq, span=span, n_cmp=n_cmp, n_slc=n_slc,
                             topn=min(NSA_TOPN, n_slc), win_len=win_len)
    return pl.pallas_call(
        kern,
        grid=(batch, NSA_KV_HEADS, seq // tq),
        in_specs=[pl.BlockSpec((None, tq, gw), lambda b, g, i: (b, i, C_NQ // gw + g)),
                  pl.BlockSpec((None, None, None, n16, HEAD_DIM), lambda b, g, i: (b, 0, g, 0, 0)),
                  pl.BlockSpec((None, None, None, HEAD_DIM, n16), lambda b, g, i: (b, 1, g, 0, 0)),
                  k_spec(2),
                  pl.BlockSpec((None, None, HEAD_DIM, seq), lambda b, g, i: (b, g, 0, 0)),
                  k_spec(4),
                  pl.BlockSpec((None, None, seq // tq, HEAD_DIM, tq), lambda b, g, i: (b, g, 0, 0, 0)),
                  pl.BlockSpec((None, n_gates, tq), lambda b, g, i: (b, 0, i))],
        out_specs=pl.BlockSpec((None, tq, gw), lambda b, g, i: (b, i, g)),
        out_shape=jax.ShapeDtypeStruct((batch, seq, NSA_WIDTH), BF16),
        scratch_shapes=[pltpu.VMEM((rows, tq), F32),
                        pltpu.VMEM((seq, lanes), F32), pltpu.VMEM((seq, lanes), BF16),
                        pltpu.VMEM((win_len, lanes), F32), pltpu.VMEM((win_len, lanes), BF16),
                        pltpu.VMEM((3, HEAD_DIM, lanes), F32)],
        compiler_params=_params("parallel", "parallel", "arbitrary"),
        name="nsa",
    )(proj, kv_c, kv_ct, proj, vs_t, proj, vw_t, ng_t)


GLA_SUB = 8


def _gla_kernel(*refs, chunk):
    q_ref, k_ref = refs[0:2]
    v_refs = refs[2:2 + GLA_HEADS]
    gg_refs = refs[2 + GLA_HEADS:2 + 2 * GLA_HEADS]
    small_ref, wa_ref, ba_ref, ng_ref, o_ref, state_ref, attn_ref = refs[2 + 2 * GLA_HEADS:]
    hs = range(GLA_HEADS)
    keys = [slice(h * GLA_DK, (h + 1) * GLA_DK) for h in hs]

    @pl.when(pl.program_id(1) == 0)
    def _():
        state_ref[...] = jnp.zeros_like(state_ref)

    q = q_ref[...].astype(F32) * (GLA_DK ** -0.5)
    k = k_ref[...].astype(F32)
    z = _dot(small_ref[...], wa_ref[...]) + ba_ref[...]
    log_a = (jnp.minimum(z, 0.0) - jnp.log(1.0 + jnp.exp(-jnp.abs(z)))) * (1.0 / GLA_TAU)
    tri = (lax.broadcasted_iota(I32, (chunk, chunk), 1)
           <= lax.broadcasted_iota(I32, (chunk, chunk), 0)).astype(BF16)
    b = _dot_split3(tri, log_a)

    attn_ref[...] = jnp.zeros_like(attn_ref)
    t_idx = lax.broadcasted_iota(I32, (GLA_SUB, 1), 0)
    s_lane = lax.broadcasted_iota(I32, (GLA_SUB, GLA_SUB), 1)
    for i in range(chunk // GLA_SUB):
        r0 = i * GLA_SUB
        bi, qi, ki = b[r0:r0 + GLA_SUB], q[r0:r0 + GLA_SUB], k[r0:r0 + GLA_SUB]
        diag = [jnp.zeros((GLA_SUB, GLA_SUB), F32) for _ in hs]
        for s in range(GLA_SUB):
            decay = jnp.exp(jnp.where(t_idx >= s, bi - bi[s:s + 1], NEG))
            prod = qi * ki[s:s + 1] * decay
            for h in hs:
                col = jnp.sum(prod[:, keys[h]], axis=-1, keepdims=True)
                diag[h] = jnp.where(s_lane == s, col, diag[h])
        for h in hs:
            attn_ref[h, r0:r0 + GLA_SUB, r0:r0 + GLA_SUB] = diag[h]
        if i > 0:
            ref_b = b[r0:r0 + 1]
            q_dec = (qi * jnp.exp(bi - ref_b)).astype(BF16)
            k_dec = (k[:r0] * jnp.exp(ref_b - b[:r0])).astype(BF16)
            for h in hs:
                attn_ref[h, r0:r0 + GLA_SUB, 0:r0] = _dot_nt(q_dec[:, keys[h]], k_dec[:, keys[h]])

    q_in = (q * jnp.exp(b)).astype(BF16)
    b_last = b[chunk - 1:chunk]
    k_out = (k * jnp.exp(b_last - b)).astype(BF16)
    carry = jnp.exp(b_last)
    for h in hs:
        v = v_refs[h][...]
        state_t = state_ref[h]
        o = _dot(attn_ref[h].astype(BF16), v) + _dot_nt(q_in[:, keys[h]], state_t.astype(BF16))
        state_ref[h] = state_t * carry[:, keys[h]] + _dot(v.astype(F32).T.astype(BF16), k_out[:, keys[h]])
        gate = gg_refs[h][...].astype(F32)
        out = _rms(o, ng_ref[...]) * (gate * jax.nn.sigmoid(gate))
        o_ref[:, h * GLA_DV:(h + 1) * GLA_DV] = out.astype(o_ref.dtype)


def gla_attention(proj, gla_wa, gla_ba, gla_norm_g, batch, seq, chunk=128):
    chunk = min(chunk, seq)
    assert seq % chunk == 0 and chunk % GLA_SUB == 0
    wa = jnp.zeros((LANES, GLA_KEY_WIDTH), BF16).at[SMALL_GA_LANE:SMALL_GA_LANE + GLA_RANK].set(gla_wa.astype(BF16))

    def head_spec(c0, h):
        return pl.BlockSpec((None, chunk, GLA_DV), lambda b, c, h=h: (b, c, c0 // GLA_DV + h))

    const = lambda b, c: (0, 0)
    return pl.pallas_call(
        functools.partial(_gla_kernel, chunk=chunk),
        grid=(batch, seq // chunk),
        in_specs=[pl.BlockSpec((None, chunk, GLA_KEY_WIDTH), lambda b, c: (b, c, C_GQ // GLA_KEY_WIDTH)),
                  pl.BlockSpec((None, chunk, GLA_KEY_WIDTH), lambda b, c: (b, c, C_GK // GLA_KEY_WIDTH)),
                  *[head_spec(C_GV, h) for h in range(GLA_HEADS)],
                  *[head_spec(C_GG, h) for h in range(GLA_HEADS)],
                  pl.BlockSpec((None, chunk, LANES), lambda b, c: (b, c, C_SMALL // LANES)),
                  pl.BlockSpec((LANES, GLA_KEY_WIDTH), const),
                  pl.BlockSpec((1, GLA_KEY_WIDTH), const),
                  pl.BlockSpec((1, GLA_DV), const)],
        out_specs=pl.BlockSpec((None, chunk, GLA_WIDTH), lambda b, c: (b, c, 0)),
        out_shape=jax.ShapeDtypeStruct((batch, seq, GLA_WIDTH), BF16),
        scratch_shapes=[pltpu.VMEM((GLA_HEADS, GLA_DV, GLA_DK), F32), pltpu.VMEM((GLA_HEADS, chunk, chunk), F32)],
        compiler_params=_params("parallel", "arbitrary"),
        name="gla",
    )(proj, proj, *([proj] * (2 * GLA_HEADS)), proj, wa, gla_ba.reshape(1, GLA_KEY_WIDTH).astype(F32),
      gla_norm_g.reshape(1, GLA_DV).astype(F32))


def _merge_kernel(om_ref, on_ref, og_ref, wm_ref, wn_ref, wg_ref, gm_ref, gn_ref, gl_ref, o_ref):
    def gated(gate_ref, a_ref, w_ref):
        return jax.nn.sigmoid(gate_ref[...].astype(F32)) * _dot(a_ref[...], w_ref[...])

    o_ref[...] = (gated(gm_ref, om_ref, wm_ref) + gated(gn_ref, on_ref, wn_ref)
                  + gated(gl_ref, og_ref, wg_ref)).astype(o_ref.dtype)


def merge_branches(o_m, o_n, o_g, w_m, w_n, w_g, proj2d, tm=1024, tn=512):
    m = o_m.shape[0]
    tm = min(tm, m)
    assert C_MG % tn == 0 and D_MODEL % tn == 0

    def gate_spec(c):
        return pl.BlockSpec((tm, tn), lambda i, j, c=c: (i, (C_MG + c * D_MODEL) // tn + j))

    def act_spec(width):
        return pl.BlockSpec((tm, width), lambda i, j: (i, 0))

    def w_spec(width):
        return pl.BlockSpec((width, tn), lambda i, j: (0, j))

    return pl.pallas_call(
        _merge_kernel,
        grid=(m // tm, D_MODEL // tn),
        in_specs=[act_spec(MOBA_WIDTH), act_spec(NSA_WIDTH), act_spec(GLA_WIDTH),
                  w_spec(MOBA_WIDTH), w_spec(NSA_WIDTH), w_spec(GLA_WIDTH),
                  gate_spec(0), gate_spec(1), gate_spec(2)],
        out_specs=pl.BlockSpec((tm, tn), lambda i, j: (i, j)),
        out_shape=jax.ShapeDtypeStruct((m, D_MODEL), BF16),
        compiler_params=_params("parallel", "parallel"),
        name="merge",
    )(o_m, o_n, o_g, w_m, w_n, w_g, proj2d, proj2d, proj2d)


def _first_max(vals, lane):
    top = jnp.max(vals, axis=-1, keepdims=True)
    idx = jnp.min(jnp.where(vals == top, lane, float(ROUTER_LANES)), axis=-1, keepdims=True)
    return top, idx


def _router_kernel(x_ref, g_ref, w_ref, b_ref, h_ref, route_ref):
    h = _rms(x_ref[...], g_ref[...])
    h_ref[...] = h.astype(h_ref.dtype)
    h1 = h.astype(BF16)
    r1 = h - h1.astype(F32)
    h2 = r1.astype(BF16)
    h3 = (r1 - h2.astype(F32)).astype(BF16)
    w1 = w_ref[0]
    w2 = w_ref[1]
    logits = (_dot(h1, w1) + (_dot(h1, w2) + _dot(h2, w1)) + (_dot(h2, w2) + _dot(h3, w1))) + b_ref[...]

    lane = lax.broadcasted_iota(I32, logits.shape, 1).astype(F32)
    g_logits = jnp.where(lane < MOE_GROUPS, logits, -jnp.inf)
    g_top, grp = _first_max(g_logits, lane)
    p_grp = 1.0 / jnp.sum(jnp.exp(g_logits - g_top), axis=-1, keepdims=True)
    first = MOE_GROUPS + grp * MOE_EXPERTS_PER_GROUP
    e_logits = jnp.where((lane >= first) & (lane < first + MOE_EXPERTS_PER_GROUP), logits, -jnp.inf)
    top1, lane1 = _first_max(e_logits, lane)
    top2, lane2 = _first_max(jnp.where(lane == lane1, -jnp.inf, e_logits), lane)
    ratio = jnp.exp(top2 - top1)
    w_first = p_grp / (1.0 + ratio)
    route = jnp.where(lane == 0, lane1 - MOE_GROUPS, jnp.where(lane == 1, lane2 - MOE_GROUPS,
                      jnp.where(lane == 2, w_first, jnp.where(lane == 3, w_first * ratio, 0.0))))
    route_ref[...] = route


def router(x2d, norm_g, rg_w, rg_b, re_w, re_b, tm=512):
    m, d = x2d.shape
    tm = min(tm, m)
    n_real = MOE_GROUPS + MOE_EXPERTS
    w = jnp.zeros((d, ROUTER_LANES), F32).at[:, :n_real].set(jnp.concatenate([rg_w, re_w], axis=1))
    w_hi = w.astype(BF16)
    w_lo = (w - w_hi.astype(F32)).astype(BF16)
    bias = jnp.zeros((1, ROUTER_LANES), F32).at[0, :n_real].set(jnp.concatenate([rg_b, re_b]))
    return pl.pallas_call(
        _router_kernel,
        grid=(m // tm,),
        in_specs=[pl.BlockSpec((tm, d), lambda i: (i, 0)), pl.BlockSpec((1, d), lambda i: (0, 0)),
                  pl.BlockSpec((2, d, ROUTER_LANES), lambda i: (0, 0, 0)),
                  pl.BlockSpec((1, ROUTER_LANES), lambda i: (0, 0))],
        out_specs=[pl.BlockSpec((tm, d), lambda i: (i, 0)), pl.BlockSpec((tm, ROUTER_LANES), lambda i: (i, 0))],
        out_shape=[jax.ShapeDtypeStruct((m, d), BF16), jax.ShapeDtypeStruct((m, ROUTER_LANES), F32)],
        compiler_params=_params("parallel"),
        name="router",
    )(x2d, norm_g.reshape(1, d).astype(F32), jnp.stack([w_hi, w_lo]), bias)


def _expert_kernel(blk_e_ref, n_used_ref, x_ref, wg_ref, wu_ref, wd_ref, o_ref, wg_bf, wu_bf, wd_bf):
    i = pl.program_id(0)
    new_expert = (i == 0) | (blk_e_ref[i] != blk_e_ref[jnp.maximum(i - 1, 0)])

    @pl.when(new_expert)
    def _():
        wg_bf[...] = wg_ref[...].astype(BF16)
        wu_bf[...] = wu_ref[...].astype(BF16)
        wd_bf[...] = wd_ref[...].astype(BF16)

    @pl.when(i < n_used_ref[0])
    def _():
        x = x_ref[...]
        gate = _dot(x, wg_bf[...])
        hid = gate * jax.nn.sigmoid(gate) * _dot(x, wu_bf[...])
        o_ref[...] = _dot(hid.astype(BF16), wd_bf[...]).astype(o_ref.dtype)

    @pl.when(i >= n_used_ref[0])
    def _():
        o_ref[...] = jnp.zeros_like(o_ref)


def expert_blocks(xs, blk_e, n_used, layer, w_gate, w_up, w_down):
    p, d = xs.shape
    ff = w_gate.shape[3]
    n_blk = p // MOE_ROWS
    grid_spec = pltpu.PrefetchScalarGridSpec(
        num_scalar_prefetch=2,
        grid=(n_blk,),
        in_specs=[pl.BlockSpec((MOE_ROWS, d), lambda i, e, n: (i, 0)),
                  pl.BlockSpec((None, None, d, ff), lambda i, e, n: (layer, e[i], 0, 0)),
                  pl.BlockSpec((None, None, d, ff), lambda i, e, n: (layer, e[i], 0, 0)),
                  pl.BlockSpec((None, None, ff, d), lambda i, e, n: (layer, e[i], 0, 0))],
        out_specs=pl.BlockSpec((MOE_ROWS, d), lambda i, e, n: (i, 0)),
        scratch_shapes=[pltpu.VMEM((d, ff), BF16), pltpu.VMEM((d, ff), BF16), pltpu.VMEM((ff, d), BF16)],
    )
    return pl.pallas_call(
        _expert_kernel,
        grid_spec=grid_spec,
        out_shape=jax.ShapeDtypeStruct((p, d), BF16),
        compiler_params=_params("arbitrary"),
        name="experts",
    )(blk_e, n_used, xs, w_gate, w_up, w_down)


def _combine_kernel(x_ref, y0_ref, y1_ref, w_ref, g_ref, o_ref, *, final_norm):
    w = w_ref[...]
    x = x_ref[...] + (w[:, 0:1] * y0_ref[...].astype(F32) + w[:, 1:2] * y1_ref[...].astype(F32))
    o_ref[...] = _rms(x, g_ref[...]) if final_norm else x


def combine(x2d, y0, y1, w, norm_g, final_norm, tm=512):
    m, d = x2d.shape
    tm = min(tm, m)
    row = pl.BlockSpec((tm, d), lambda i: (i, 0))
    return pl.pallas_call(
        functools.partial(_combine_kernel, final_norm=final_norm),
        grid=(m // tm,),
        in_specs=[row, row, row, pl.BlockSpec((tm, MOE_TOPK), lambda i: (i, 0)),
                  pl.BlockSpec((1, d), lambda i: (0, 0))],
        out_specs=row,
        out_shape=jax.ShapeDtypeStruct((m, d), F32),
        compiler_params=_params("parallel"),
        name="combine",
    )(x2d, y0, y1, w, norm_g.reshape(1, d).astype(F32))


def hier_moe(x2d, norm_g, rg_w, rg_b, re_w, re_b, layer, w_gate, w_up, w_down, out_norm_g, final_norm):
    t = x2d.shape[0]
    h, route = router(x2d, norm_g, rg_w, rg_b, re_w, re_b)
    expert = route[:, 0:MOE_TOPK].astype(I32)
    w = route[:, MOE_TOPK:2 * MOE_TOPK]

    a = t * MOE_TOPK
    e_flat = expert.reshape(a)
    onehot = (e_flat[:, None] == jnp.arange(MOE_EXPERTS, dtype=I32)[None, :]).astype(I32)
    running = jnp.cumsum(onehot, axis=0)
    counts = running[-1]
    rank = jnp.take_along_axis(running, e_flat[:, None], axis=1)[:, 0] - 1
    padded = (counts + MOE_ROWS - 1) // MOE_ROWS * MOE_ROWS
    pends = jnp.cumsum(padded)
    dest = (pends - padded)[e_flat] + rank
    p_rows = (a + MOE_EXPERTS * (MOE_ROWS - 1)) // MOE_ROWS * MOE_ROWS
    n_blk = p_rows // MOE_ROWS
    row_tok = (jnp.arange(p_rows, dtype=I32) % t).at[dest].set(jnp.arange(a, dtype=I32) // MOE_TOPK)
    blk_start = jnp.arange(n_blk, dtype=I32) * MOE_ROWS
    blk_e = jnp.minimum(jnp.sum((pends[None, :] <= blk_start[:, None]).astype(I32), axis=1), MOE_EXPERTS - 1)
    n_used = (pends[-1] // MOE_ROWS).astype(I32).reshape(1)

    xs = jnp.take(h, row_tok, axis=0)
    y_rows = expert_blocks(xs, blk_e, n_used, layer, w_gate, w_up, w_down)
    dest2 = dest.reshape(t, MOE_TOPK)
    y0 = jnp.take(y_rows, dest2[:, 0], axis=0)
    y1 = jnp.take(y_rows, dest2[:, 1], axis=0)
    return combine(x2d, y0, y1, w, out_norm_g, final_norm)


PERM_TILE = 512
PERM_REGIONS = ((0, 0, C_GQ // PERM_TILE),
                (_SRC_GQ - C_GQ, C_GQ // PERM_TILE, C_GG // PERM_TILE),
                (_SRC_GG - C_GG, C_GG // PERM_TILE, C_SMALL // PERM_TILE))


def _permute_kernel(src_ref, small_ref, o_ref):
    @pl.when(pl.program_id(1) < C_SMALL // PERM_TILE)
    def _():
        o_ref[...] = src_ref[0].T.astype(o_ref.dtype)

    @pl.when(pl.program_id(1) == C_SMALL // PERM_TILE)
    def _():
        o_ref[:, 0:LANES] = small_ref[...].astype(o_ref.dtype)


def permute_w_in(w_in, layer, tk=512):
    assert C_GQ % PERM_TILE == 0 and C_GG % PERM_TILE == 0 and C_SMALL % PERM_TILE == 0
    assert all(shift % SUBLANES == 0 for shift, _, _ in PERM_REGIONS)
    d = w_in.shape[1]
    w_t = jnp.swapaxes(w_in, 1, 2)
    small = jnp.concatenate([w_in[layer, :, _SRC_GA:_SRC_GG], w_in[layer, :, _SRC_NG:_SRC_GQ],
                             jnp.zeros((d, LANES - GLA_RANK - 3 * NSA_HEADS), w_in.dtype)], axis=1)

    def src_row(j):
        shift = sum(jnp.where((j >= j0) & (j < j1), s, 0) for s, j0, j1 in PERM_REGIONS)
        return pl.multiple_of(jnp.minimum(j * PERM_TILE + shift, _SRC_END - PERM_TILE), SUBLANES)

    return pl.pallas_call(
        _permute_kernel,
        grid=(d // tk, pl.cdiv(PROJ_WIDTH, PERM_TILE)),
        in_specs=[pl.BlockSpec((pl.Element(1), pl.Element(PERM_TILE), pl.Element(tk)),
                               lambda k, j: (layer, src_row(j), k * tk)),
                  pl.BlockSpec((tk, LANES), lambda k, j: (k, 0))],
        out_specs=pl.BlockSpec((tk, PERM_TILE), lambda k, j: (k, j)),
        out_shape=jax.ShapeDtypeStruct((d, PROJ_WIDTH), BF16),
        compiler_params=_params("parallel", "parallel"),
        name="permute_w_in",
    )(w_t, small)


def hybrid_layer(x, norm1_g, w_in, nsa_cmp_pe, nsa_cmp_w1, nsa_cmp_w2, gla_wa, gla_ba, gla_norm_g,
                 w_br_moba, w_br_nsa, w_br_gla, w_out, norm2_g, router_group_w, router_group_b,
                 router_expert_w, router_expert_b, layer, expert_w_gate, expert_w_up, expert_w_down,
                 out_norm_g, final_norm):
    batch, seq, d = x.shape
    t = batch * seq
    x2d = x.reshape(t, d)
    h = rmsnorm(x2d, norm1_g, BF16)
    proj2d = matmul(h, permute_w_in(w_in, layer), BF16)
    proj = proj2d.reshape(batch, seq, PROJ_WIDTH)
    o_m = moba_attention(proj, batch, seq)
    kv_c, kv_ct = nsa_compress(proj, nsa_cmp_pe, nsa_cmp_w1, nsa_cmp_w2, batch, seq)
    o_n = nsa_attention(proj, kv_c, kv_ct, batch, seq)
    o_g = gla_attention(proj, gla_wa, gla_ba, gla_norm_g, batch, seq)
    merged = merge_branches(o_m.reshape(t, -1), o_n.reshape(t, -1), o_g.reshape(t, -1),
                            w_br_moba.astype(BF16), w_br_nsa.astype(BF16), w_br_gla.astype(BF16), proj2d)
    x2d = matmul_residual(merged, w_out.astype(BF16), x2d)
    x2d = hier_moe(x2d, norm2_g, router_group_w, router_group_b, router_expert_w, router_expert_b,
                   layer, expert_w_gate, expert_w_up, expert_w_down, out_norm_g, final_norm)
    return x2d.reshape(batch, seq, d)


def kernel(x, norm1_g, w_in, nsa_cmp_pe, nsa_cmp_w1, nsa_cmp_w2, gla_wa, gla_ba, gla_norm_g, w_br_moba,
           w_br_nsa, w_br_gla, w_out, norm2_g, router_group_w, router_group_b, router_expert_w,
           router_expert_b, expert_w_gate, expert_w_up, expert_w_down, final_norm_g):
    for l in range(DEPTH):
        x = hybrid_layer(x, norm1_g[l], w_in, nsa_cmp_pe[l], nsa_cmp_w1[l], nsa_cmp_w2[l], gla_wa[l],
                         gla_ba[l], gla_norm_g[l], w_br_moba[l], w_br_nsa[l], w_br_gla[l], w_out[l],
                         norm2_g[l], router_group_w[l], router_group_b[l], router_expert_w[l],
                         router_expert_b[l], l, expert_w_gate, expert_w_up, expert_w_down,
                         final_norm_g, l == DEPTH - 1)
    return x
```

```python
import functools

import jax
import jax.numpy as jnp
import numpy as np
from jax import lax
from jax.experimental import pallas as pl
from jax.experimental.pallas import tpu as pltpu

F32 = jnp.float32
BF16 = jnp.bfloat16
I32 = jnp.int32

D_MODEL = 2048
DEPTH = 2
HEAD_DIM = 128
NEG = -1e30
FORCE = 1e9
EPS = 1e-6
LOG2E = 1.4426950408889634

MOBA_HEADS = 8
MOBA_BLOCK = 256
MOBA_TOPK = 3
MOBA_WIDTH = MOBA_HEADS * HEAD_DIM

NSA_HEADS = 8
NSA_KV_HEADS = 2
NSA_GROUP = NSA_HEADS // NSA_KV_HEADS
NSA_CMP_LEN = 32
NSA_CMP_STRIDE = 16
NSA_SLC_BLOCK = 64
NSA_TOPN = 8
NSA_WINDOW = 512
NSA_WIDTH = NSA_HEADS * HEAD_DIM
NSA_KV_WIDTH = NSA_KV_HEADS * HEAD_DIM

GLA_HEADS = 4
GLA_DK = 128
GLA_DV = 256
GLA_RANK = 16
GLA_TAU = 16.0
GLA_KEY_WIDTH = GLA_HEADS * GLA_DK
GLA_WIDTH = GLA_HEADS * GLA_DV

N_BRANCH = 3
MOE_GROUPS = 4
MOE_EXPERTS_PER_GROUP = 8
MOE_EXPERTS = MOE_GROUPS * MOE_EXPERTS_PER_GROUP
MOE_TOPK = 2
MOE_FF = D_MODEL // 4

LANES = 128
SUBLANES = 8
BF16_SUBLANES = 16
VMEM_LIMIT_BYTES = 56 * 1024 * 1024

_SRC_NG = MOBA_WIDTH * 3 + NSA_WIDTH + 6 * NSA_KV_WIDTH
_SRC_GQ = _SRC_NG + 3 * NSA_HEADS
_SRC_GA = _SRC_GQ + 2 * GLA_KEY_WIDTH + GLA_WIDTH
_SRC_GG = _SRC_GA + GLA_RANK
_SRC_END = _SRC_GG + GLA_WIDTH + N_BRANCH * D_MODEL

C_MQ = 0
C_MK = C_MQ + MOBA_WIDTH
C_MV = C_MK + MOBA_WIDTH
C_NQ = C_MV + MOBA_WIDTH
C_NKV = C_NQ + NSA_WIDTH
C_GQ = C_NKV + 6 * NSA_KV_WIDTH
C_GK = C_GQ + GLA_KEY_WIDTH
C_GV = C_GK + GLA_KEY_WIDTH
C_GG = C_GV + GLA_WIDTH
C_MG = C_GG + GLA_WIDTH
C_SMALL = C_MG + N_BRANCH * D_MODEL
PROJ_WIDTH = C_SMALL + LANES
SMALL_GA_LANE = 0
SMALL_NG_LANE = GLA_RANK

ROUTER_LANES = LANES
MOE_ROWS = 256


def _params(*semantics):
    return pltpu.CompilerParams(dimension_semantics=semantics, vmem_limit_bytes=VMEM_LIMIT_BYTES)


def _dot(a, b):
    return jnp.dot(a, b, preferred_element_type=F32)


def _dot_nt(a, b):
    return lax.dot_general(a, b, (((1,), (1,)), ((), ())), preferred_element_type=F32)


def _dot_split3(a01, x):
    x1 = x.astype(BF16)
    r1 = x - x1.astype(F32)
    x2 = r1.astype(BF16)
    x3 = (r1 - x2.astype(F32)).astype(BF16)
    return _dot(a01, x1) + _dot(a01, x2) + _dot(a01, x3)


def _rms(x, g):
    return x * lax.rsqrt(jnp.mean(x * x, axis=-1, keepdims=True) + EPS) * g


def _norm_matmul_kernel(x_ref, g_ref, b_ref, o_ref, h_ref):
    @pl.when(pl.program_id(1) == 0)
    def _():
        h_ref[...] = _rms(x_ref[...], g_ref[...]).astype(h_ref.dtype)

    o_ref[...] = _dot(h_ref[...], b_ref[...]).astype(o_ref.dtype)


def norm_matmul(x, g, b, out_dtype, tm=1024, tn=1152):
    m, k = x.shape
    n = b.shape[1]
    tm, tn = min(tm, m), min(tn, n)
    return pl.pallas_call(
        _norm_matmul_kernel,
        grid=(m // tm, n // tn),
        in_specs=[pl.BlockSpec((tm, k), lambda i, j: (i, 0)), pl.BlockSpec((1, k), lambda i, j: (0, 0)),
                  pl.BlockSpec((k, tn), lambda i, j: (0, j))],
        out_specs=pl.BlockSpec((tm, tn), lambda i, j: (i, j)),
        out_shape=jax.ShapeDtypeStruct((m, n), out_dtype),
        scratch_shapes=[pltpu.VMEM((tm, k), BF16)],
        compiler_params=_params("parallel", "arbitrary"),
        name="norm_matmul",
    )(x, g.reshape(1, k).astype(F32), b)


def _matmul_residual_kernel(a_ref, b_ref, r_ref, o_ref):
    o_ref[...] = r_ref[...] + _dot(a_ref[...], b_ref[...])


def matmul_residual(a, b, res, tm=1024, tn=1024):
    m, k = a.shape
    n = b.shape[1]
    tm, tn = min(tm, m), min(tn, n)
    return pl.pallas_call(
        _matmul_residual_kernel,
        grid=(m // tm, n // tn),
        in_specs=[pl.BlockSpec((tm, k), lambda i, j: (i, 0)), pl.BlockSpec((k, tn), lambda i, j: (0, j)),
                  pl.BlockSpec((tm, tn), lambda i, j: (i, j))],
        out_specs=pl.BlockSpec((tm, tn), lambda i, j: (i, j)),
        out_shape=jax.ShapeDtypeStruct((m, n), F32),
        compiler_params=_params("parallel", "parallel"),
        name="matmul_residual",
    )(a, b, res)


def _softmax_step(carry, q, k, v, mask, scale):
    m_i, l_i, acc = carry
    s = jnp.where(mask, _dot_nt(q, k) * scale, NEG)
    m_new = jnp.maximum(m_i, jnp.max(s, axis=-1, keepdims=True))
    alpha = jnp.exp(m_i - m_new)
    p = jnp.where(mask, jnp.exp(s - m_new), 0.0)
    l_new = alpha * l_i + jnp.sum(p, axis=-1, keepdims=True)
    acc_new = alpha * acc + _dot(p.astype(v.dtype), v)
    return m_new, l_new, acc_new


def _softmax_init(m, e):
    return jnp.full((m, 1), NEG, F32), jnp.zeros((m, 1), F32), jnp.zeros((m, e), F32)


def _rank_desc(vals, n_candidates):
    lane = lax.broadcasted_iota(I32, vals.shape, 1)
    rank = jnp.zeros(vals.shape, F32)
    for m in range(n_candidates):
        vm = vals[:, m:m + 1]
        beats = (vm > vals) | ((vm == vals) & (lane > m))
        rank = rank + beats.astype(F32)
    return rank


def _softmax_step_t(carry, q, k, v_t, bias, scale):
    m_i, l_i, acc = carry
    s = _dot_nt(k, q) * scale + bias
    m_new = jnp.maximum(m_i, jnp.max(s, axis=0, keepdims=True))
    alpha = jnp.exp(m_i - m_new)
    p = jnp.exp(s - m_new)
    l_new = alpha * l_i + jnp.sum(p, axis=0, keepdims=True)
    acc_new = alpha * acc + _dot(v_t, p.astype(v_t.dtype))
    return m_new, l_new, acc_new


def _softmax_init_t(m, e):
    return jnp.full((1, m), NEG, F32), jnp.zeros((1, m), F32), jnp.zeros((e, m), F32)


def _rank_desc_rows(vals, n_candidates):
    row = lax.broadcasted_iota(I32, vals.shape, 0)
    rank = jnp.zeros(vals.shape, F32)
    for m in range(n_candidates):
        vm = vals[m:m + 1, :]
        beats = (vm > vals) | ((vm == vals) & (row > m))
        rank = rank + beats.astype(F32)
    return rank


def _lane_column(x, idx):
    lane = lax.broadcasted_iota(I32, x.shape, 1)
    return jnp.sum(jnp.where(lane == idx, x, 0.0), axis=-1, keepdims=True)


def _moba_kernel(q_ref, k_ref, vt_ref, o_ref, kmean_ref, bias_ref, s_ref, p_ref, acc_ref, bmax_ref, m_ref, *,
                 n_blocks, topk, heads):
    blk = MOBA_BLOCK
    cur = pl.program_id(2)
    scale = HEAD_DIM ** -0.5
    cols = [slice(h * HEAD_DIM, (h + 1) * HEAD_DIM) for h in range(heads)]

    @pl.when(cur == 0)
    def _():
        kmean_ref[...] = jnp.zeros_like(kmean_ref)
        for h in range(heads):
            for n in range(n_blocks):
                k_blk = k_ref[n * blk:(n + 1) * blk, cols[h]].astype(F32)
                kmean_ref[h, n:n + 1, :] = jnp.sum(k_blk, axis=0, keepdims=True) * (1.0 / blk)

    for h in range(heads):
        gate = _dot_nt(kmean_ref[h].astype(BF16), q_ref[:, cols[h]])
        row = lax.broadcasted_iota(I32, gate.shape, 0)
        valid = row < cur
        gate = jnp.where(valid, gate, NEG)
        chosen = valid & (_rank_desc_rows(gate, n_blocks) < topk)
        bias_ref[h] = jnp.where(chosen, 0.0, NEG)

    seq = k_ref.shape[0]
    lo_blocks = (n_blocks + 1) // 2
    lo = lo_blocks * blk
    need_hi = cur >= lo_blocks

    def score_blocks(b0, b1):
        for h in range(heads):
            s = _dot_nt(k_ref[b0 * blk:b1 * blk, cols[h]], q_ref[:, cols[h]])
            s_ref[h, b0 * blk:b1 * blk, :] = s
            for n in range(b0, b1):
                s_blk = s[(n - b0) * blk:(n - b0 + 1) * blk]
                bmax_ref[h, n] = jnp.max(s_blk.reshape(blk // SUBLANES, SUBLANES, blk), axis=0)

    def past_max(h, b0, b1):
        return functools.reduce(jnp.maximum, [bmax_ref[h, n] * scale + bias_ref[h, n:n + 1, :]
                                              for n in range(b0, b1)])

    score_blocks(0, lo_blocks)
    if lo < seq:
        pl.when(need_hi)(lambda: score_blocks(lo_blocks, n_blocks))

    sub = LANES
    base = pl.multiple_of(cur * blk, blk)
    qry_i = lax.broadcasted_iota(I32, (sub, blk), 1)
    key_i = [j * sub + lax.broadcasted_iota(I32, (sub, blk), 0) for j in range(blk // sub)]
    exp_scale = scale * LOG2E

    def fold_max(x):
        return jnp.max(x.reshape(sub // SUBLANES, SUBLANES, blk), axis=0)

    def fold_sum(x):
        return jnp.sum(x.reshape(sub // SUBLANES, SUBLANES, blk), axis=0)

    def tiles(h, start):
        return [s_ref[h, pl.ds(pl.multiple_of(start + j * sub, sub), sub), :] for j in range(blk // sub)]

    for h in range(heads):
        own = [fold_max(jnp.where(key_i[j] <= qry_i, t, NEG)) for j, t in enumerate(tiles(h, base))]
        m_ref[h] = jnp.maximum(functools.reduce(jnp.maximum, own) * scale, past_max(h, 0, lo_blocks))

    if lo < seq:
        @pl.when(need_hi)
        def _():
            for h in range(heads):
                m_ref[h] = jnp.maximum(m_ref[h], past_max(h, lo_blocks, n_blocks))

    m_log2 = [jnp.max(m_ref[h], axis=0, keepdims=True) * LOG2E for h in range(heads)]

    def store_p(h, start, j, p):
        p_ref[h, pl.ds(pl.multiple_of(start + j * sub, sub), sub), :] = p.astype(p_ref.dtype)

    def prob_body(n, l8):
        out = []
        for h in range(heads):
            shift = bias_ref[h, pl.ds(n, 1), :] * LOG2E - m_log2[h]
            acc = l8[h]
            for j, t in enumerate(tiles(h, n * blk)):
                p = jnp.exp2(t * exp_scale + shift)
                store_p(h, n * blk, j, p)
                acc = acc + fold_sum(p)
            out.append(acc)
        return tuple(out)

    l8 = []
    for h in range(heads):
        acc = jnp.zeros((SUBLANES, blk), F32)
        for j, t in enumerate(tiles(h, base)):
            p = jnp.where(key_i[j] <= qry_i, jnp.exp2(t * exp_scale - m_log2[h]), 0.0)
            store_p(h, base, j, p)
            acc = acc + fold_sum(p)
        l8.append(acc)
    l8 = lax.fori_loop(0, cur, prob_body, tuple(l8))

    def zero_body(n, carry):
        for h in range(heads):
            for j in range(blk // sub):
                store_p(h, n * blk, j, jnp.zeros((sub, blk), F32))
        return carry

    lax.fori_loop(cur + 1, jnp.where(need_hi, n_blocks, lo_blocks), zero_body, 0)
    for h in range(heads):
        acc_ref[h] = _dot(vt_ref[h, :, 0:lo], p_ref[h, 0:lo, :])

    if lo < seq:
        @pl.when(need_hi)
        def _():
            for h in range(heads):
                acc_ref[h] += _dot(vt_ref[h, :, lo:seq], p_ref[h, lo:seq, :])

    for h in range(heads):
        l_sum = jnp.sum(l8[h], axis=0, keepdims=True)
        o_ref[:, cols[h]] = (acc_ref[h] / l_sum).T.astype(o_ref.dtype)


def moba_attention(proj, batch, seq, heads=2):
    assert seq % MOBA_BLOCK == 0 and MOBA_HEADS % heads == 0
    n_blocks = seq // MOBA_BLOCK
    rows = -(-n_blocks // BF16_SUBLANES) * BF16_SUBLANES
    width = heads * HEAD_DIM
    qb, kb = C_MQ // width, C_MK // width
    v_t = proj[:, :, C_MV:C_MV + MOBA_WIDTH].reshape(batch, seq, MOBA_HEADS, HEAD_DIM).transpose(0, 2, 3, 1)
    kern = functools.partial(_moba_kernel, n_blocks=n_blocks, topk=min(MOBA_TOPK, n_blocks), heads=heads)
    return pl.pallas_call(
        kern,
        grid=(batch, MOBA_HEADS // heads, n_blocks),
        in_specs=[pl.BlockSpec((None, MOBA_BLOCK, width), lambda b, h, i: (b, i, qb + h)),
                  pl.BlockSpec((None, seq, width), lambda b, h, i: (b, 0, kb + h)),
                  pl.BlockSpec((None, heads, HEAD_DIM, seq), lambda b, h, i: (b, h, 0, 0))],
        out_specs=pl.BlockSpec((None, MOBA_BLOCK, width), lambda b, h, i: (b, i, h)),
        out_shape=jax.ShapeDtypeStruct((batch, seq, MOBA_WIDTH), BF16),
        scratch_shapes=[pltpu.VMEM((heads, rows, HEAD_DIM), F32), pltpu.VMEM((heads, rows, MOBA_BLOCK), F32),
                        pltpu.VMEM((heads, seq, MOBA_BLOCK), F32), pltpu.VMEM((heads, seq, MOBA_BLOCK), BF16),
                        pltpu.VMEM((heads, HEAD_DIM, MOBA_BLOCK), F32),
                        pltpu.VMEM((heads, n_blocks, SUBLANES, MOBA_BLOCK), F32),
                        pltpu.VMEM((heads, SUBLANES, MOBA_BLOCK), F32)],
        compiler_params=_params("parallel", "parallel", "arbitrary"),
        name="moba",
    )(proj, proj, v_t)


def _nsa_compress_kernel(x_ref, pe_ref, w1_ref, w2_ref, o_ref, ot_ref):
    x = x_ref[...].astype(F32)
    half = x.shape[1]
    lo = _dot((x + pe_ref[:, :half]).astype(BF16), w1_ref[:half, :])
    hi = _dot((x + pe_ref[:, half:]).astype(BF16), w1_ref[half:, :])
    pre = lo + pltpu.roll(hi, hi.shape[0] - 1, 0)
    hid = pre * jax.nn.sigmoid(pre)
    out = _dot(hid.astype(BF16), w2_ref[...])
    o_ref[...] = out.astype(o_ref.dtype)
    ot_ref[...] = out.T.astype(ot_ref.dtype)


def nsa_compress(proj, cmp_pe, cmp_w1, cmp_w2, batch, seq):
    n16 = seq // NSA_CMP_STRIDE
    width = NSA_CMP_STRIDE * HEAD_DIM
    x = proj[:, :, C_NKV:C_NKV + 2 * NSA_KV_WIDTH].reshape(batch, seq, 2, NSA_KV_HEADS, HEAD_DIM)
    x = x.transpose(0, 2, 3, 1, 4).reshape(batch, 2, NSA_KV_HEADS, n16, width)
    pe = cmp_pe.reshape(2, 1, NSA_CMP_LEN * HEAD_DIM).astype(F32)
    return pl.pallas_call(
        _nsa_compress_kernel,
        grid=(batch, 2, NSA_KV_HEADS),
        in_specs=[pl.BlockSpec((None, None, None, n16, width), lambda b, c, g: (b, c, g, 0, 0)),
                  pl.BlockSpec((None, 1, 2 * width), lambda b, c, g: (c, 0, 0)),
                  pl.BlockSpec((None, 2 * width, HEAD_DIM), lambda b, c, g: (c, 0, 0)),
                  pl.BlockSpec((None, HEAD_DIM, HEAD_DIM), lambda b, c, g: (c, 0, 0))],
        out_specs=[pl.BlockSpec((None, None, None, n16, HEAD_DIM), lambda b, c, g: (b, c, g, 0, 0)),
                   pl.BlockSpec((None, None, None, HEAD_DIM, n16), lambda b, c, g: (b, c, g, 0, 0))],
        out_shape=[jax.ShapeDtypeStruct((batch, 2, NSA_KV_HEADS, n16, HEAD_DIM), BF16),
                   jax.ShapeDtypeStruct((batch, 2, NSA_KV_HEADS, HEAD_DIM, n16), BF16)],
        compiler_params=_params("parallel", "parallel", "parallel"),
        name="nsa_compress",
    )(x, pe, cmp_w1.astype(BF16), cmp_w2.astype(BF16))


def _nsa_kernel(q_ref, kc_ref, vct_ref, ks_ref, vst_ref, kw_ref, vwt_ref, ngt_ref, o_ref,
                bias_ref, s_ref, p_ref, sw_ref, pw_ref, out_ref, *, tq, span, n_cmp, n_slc, topn, win_len):
    g = pl.program_id(1)
    qi = pl.program_id(2)
    seq = ks_ref.shape[0]
    n16 = kc_ref.shape[0]
    lanes = NSA_GROUP * tq
    blk = NSA_SLC_BLOCK
    scale = HEAD_DIM ** -0.5
    exp_scale = scale * LOG2E
    start = qi * tq
    heads = [slice(r * tq, (r + 1) * tq) for r in range(NSA_GROUP)]
    q = [q_ref[:, r * HEAD_DIM:(r + 1) * HEAD_DIM] for r in range(NSA_GROUP)]

    def per_group(row):
        return jnp.concatenate([row] * NSA_GROUP, axis=1)

    def scores(keys):
        return jnp.concatenate([_dot_nt(keys, q[r]) for r in range(NSA_GROUP)], axis=1)

    def fold(op, x):
        return op(x.reshape(x.shape[0] // SUBLANES, SUBLANES, lanes), axis=0)

    pos1 = start + lax.broadcasted_iota(I32, (1, tq), 1)
    pos = per_group(pos1)

    n_idx = lax.broadcasted_iota(I32, (n16, lanes), 0)
    in_range = n_idx < n_cmp
    cmask = (n_idx * NSA_CMP_STRIDE + (NSA_CMP_LEN - 1) <= pos) & in_range
    s_c = jnp.where(cmask, scores(kc_ref[...]) * scale, NEG)
    e_c = jnp.where(in_range, jnp.exp(s_c - jnp.max(s_c, axis=0, keepdims=True)), 0.0)
    p_c = jnp.where(cmask, e_c / jnp.sum(e_c, axis=0, keepdims=True), 0.0)
    out_ref[0] = _dot(vct_ref[...], p_c.astype(BF16))

    p_sum = functools.reduce(lambda a, b: a + b, [p_c[:, h] for h in heads])
    rows = bias_ref.shape[0]
    oj = lax.broadcasted_iota(I32, (rows, n16), 0)
    on = lax.broadcasted_iota(I32, (rows, n16), 1)
    overlap_t = ((on * NSA_CMP_STRIDE < (oj + 1) * blk) & (on * NSA_CMP_STRIDE + (NSA_CMP_LEN - 1) >= oj * blk)
                 & (on < n_cmp) & (oj < n_slc)).astype(BF16)
    p_hi = p_sum.astype(BF16)
    p_lo = (p_sum - p_hi.astype(F32)).astype(BF16)
    imp = _dot(overlap_t, p_hi) + _dot(overlap_t, p_lo)
    j_idx = lax.broadcasted_iota(I32, (rows, tq), 0)
    cur_blk = pos1 // blk
    forced = (j_idx == 0) | (j_idx == cur_blk) | (j_idx == cur_blk - 1)
    imp = jnp.where(forced, FORCE, imp)
    imp = jnp.where(j_idx > cur_blk, NEG, imp)
    chosen = (_rank_desc_rows(imp, n_slc) < topn) & (j_idx <= cur_blk)
    bias_ref[...] = jnp.where(chosen, 0.0, NEG)

    n_spans = seq // span
    s_ref[0:span, :] = scores(ks_ref[0:span, :])
    w0 = pl.multiple_of(jnp.maximum(start + tq - win_len, 0), tq)
    sw_ref[...] = scores(kw_ref[pl.ds(w0, win_len), :])
    for k in range(1, n_spans):
        def span_scores(k=k):
            s_ref[k * span:(k + 1) * span, :] = scores(ks_ref[k * span:(k + 1) * span, :])
        pl.when(start >= k * span)(span_scores)

    per_tile = tq // blk
    first_own = qi * per_tile

    def block_bias(j):
        return per_group(bias_ref[pl.ds(j, 1), :])

    def block_rows(j):
        return pl.ds(pl.multiple_of(j * blk, blk), blk)

    def causal_mask(j):
        key = j * blk + lax.broadcasted_iota(I32, (blk, lanes), 0)
        return key <= pos

    def max_body(i, m8):
        for d in range(per_tile):
            j = i * per_tile + d
            m8 = jnp.maximum(m8, fold(jnp.max, s_ref[block_rows(j), :]) * scale + block_bias(j))
        return m8

    m8 = lax.fori_loop(0, qi, max_body, jnp.full((SUBLANES, lanes), NEG, F32))
    for d in range(per_tile):
        j = first_own + d
        own = jnp.where(causal_mask(j), s_ref[block_rows(j), :], NEG)
        m8 = jnp.maximum(m8, fold(jnp.max, own) * scale + block_bias(j))
    m_log2 = jnp.max(m8, axis=0, keepdims=True) * LOG2E

    def prob_body(i, l8):
        for d in range(per_tile):
            j = i * per_tile + d
            p = jnp.exp2(s_ref[block_rows(j), :] * exp_scale + (block_bias(j) * LOG2E - m_log2))
            p_ref[block_rows(j), :] = p.astype(p_ref.dtype)
            l8 = l8 + fold(jnp.sum, p)
        return l8

    l8 = lax.fori_loop(0, qi, prob_body, jnp.zeros((SUBLANES, lanes), F32))
    for d in range(per_tile):
        j = first_own + d
        p = jnp.exp2(s_ref[block_rows(j), :] * exp_scale + (block_bias(j) * LOG2E - m_log2))
        p = jnp.where(causal_mask(j), p, 0.0)
        p_ref[block_rows(j), :] = p.astype(p_ref.dtype)
        l8 = l8 + fold(jnp.sum, p)

    def zero_body(i, carry):
        p_ref[pl.ds(pl.multiple_of(i * tq, tq), tq), :] = jnp.zeros((tq, lanes), p_ref.dtype)
        return carry

    visible_tiles = (start // span + 1) * (span // tq)
    lax.fori_loop(qi + 1, visible_tiles, zero_body, 0)

    out_ref[1] = _dot(vst_ref[:, 0:span], p_ref[0:span, :])

    n_tiles = win_len // blk

    def win_tile(t):
        key = w0 + t * blk + lax.broadcasted_iota(I32, (blk, lanes), 0)
        return sw_ref[t * blk:(t + 1) * blk, :], (key <= pos) & (key > pos - NSA_WINDOW)

    wm8 = jnp.full((SUBLANES, lanes), NEG, F32)
    for t in range(n_tiles):
        tile, mask = win_tile(t)
        wm8 = jnp.maximum(wm8, fold(jnp.max, jnp.where(mask, tile, NEG)))
    wm_log2 = jnp.max(wm8, axis=0, keepdims=True) * exp_scale
    wl8 = jnp.zeros((SUBLANES, lanes), F32)
    for t in range(n_tiles):
        tile, mask = win_tile(t)
        p = jnp.where(mask, jnp.exp2(tile * exp_scale - wm_log2), 0.0)
        pw_ref[t * blk:(t + 1) * blk, :] = p.astype(pw_ref.dtype)
        wl8 = wl8 + fold(jnp.sum, p)
    w_tile0 = w0 // tq
    pv = _dot(vwt_ref[w_tile0], pw_ref[0:tq, :])
    for c in range(1, win_len // tq):
        pv = pv + _dot(vwt_ref[w_tile0 + c], pw_ref[c * tq:(c + 1) * tq, :])
    out_ref[2] = pv / jnp.sum(wl8, axis=0, keepdims=True)

    for k in range(1, n_spans):
        def span_pv(k=k):
            out_ref[1] += _dot(vst_ref[:, k * span:(k + 1) * span], p_ref[k * span:(k + 1) * span, :])
        pl.when(start >= k * span)(span_pv)
    out_ref[1] = out_ref[1] / jnp.sum(l8, axis=0, keepdims=True)

    for r in range(NSA_GROUP):
        gate_row = (g * NSA_GROUP + r) * 3
        mix = jnp.zeros((HEAD_DIM, tq), F32)
        for c in range(3):
            mix = mix + jax.nn.sigmoid(ngt_ref[pl.ds(gate_row + c, 1), :]) * out_ref[c, :, heads[r]]
        o_ref[:, r * HEAD_DIM:(r + 1) * HEAD_DIM] = mix.T.astype(o_ref.dtype)


def nsa_attention(proj, kv_c, kv_ct, batch, seq, tq=128):
    assert tq == LANES and seq % tq == 0 and tq % NSA_SLC_BLOCK == 0
    n_cmp = (seq - NSA_CMP_LEN) // NSA_CMP_STRIDE + 1
    n_slc = seq // NSA_SLC_BLOCK
    rows = -(-n_slc // BF16_SUBLANES) * BF16_SUBLANES
    span = max(tq, seq // 4)
    win_len = min(NSA_WINDOW + tq, seq)
    n16 = seq // NSA_CMP_STRIDE
    gw = NSA_GROUP * HEAD_DIM
    lanes = NSA_GROUP * tq
    nkv = C_NKV // HEAD_DIM

    def kv_cols(slot):
        c0 = C_NKV + slot * NSA_KV_WIDTH
        return proj[:, :, c0:c0 + NSA_KV_WIDTH].reshape(batch, seq, NSA_KV_HEADS, HEAD_DIM)

    vs_t = kv_cols(3).transpose(0, 2, 3, 1)
    vw_t = kv_cols(5).reshape(batch, seq // tq, tq, NSA_KV_HEADS, HEAD_DIM).transpose(0, 3, 1, 4, 2)
    n_gates = 3 * NSA_HEADS
    ng_t = proj[:, :, C_SMALL + SMALL_NG_LANE:C_SMALL + SMALL_NG_LANE + n_gates].astype(F32).transpose(0, 2, 1)

    def k_spec(slot):
        return pl.BlockSpec((None, seq, HEAD_DIM),
                            lambda b, g, i, slot=slot: (b, 0, nkv + slot * NSA_KV_HEADS + g))

    kern = functools.partial(_nsa_kernel, tq=tq, span=span, n_cmp=n_cmp, n_slc=n_slc,
                             topn=min(NSA_TOPN, n_slc), win_len=win_len)
    return pl.pallas_call(
        kern,
        grid=(batch, NSA_KV_HEADS, seq // tq),
        in_specs=[pl.BlockSpec((None, tq, gw), lambda b, g, i: (b, i, C_NQ // gw + g)),
                  pl.BlockSpec((None, None, None, n16, HEAD_DIM), lambda b, g, i: (b, 0, g, 0, 0)),
                  pl.BlockSpec((None, None, None, HEAD_DIM, n16), lambda b, g, i: (b, 1, g, 0, 0)),
                  k_spec(2),
                  pl.BlockSpec((None, None, HEAD_DIM, seq), lambda b, g, i: (b, g, 0, 0)),
                  k_spec(4),
                  pl.BlockSpec((None, None, seq // tq, HEAD_DIM, tq), lambda b, g, i: (b, g, 0, 0, 0)),
                  pl.BlockSpec((None, n_gates, tq), lambda b, g, i: (b, 0, i))],
        out_specs=pl.BlockSpec((None, tq, gw), lambda b, g, i: (b, i, g)),
        out_shape=jax.ShapeDtypeStruct((batch, seq, NSA_WIDTH), BF16),
        scratch_shapes=[pltpu.VMEM((rows, tq), F32),
                        pltpu.VMEM((seq, lanes), F32), pltpu.VMEM((seq, lanes), BF16),
                        pltpu.VMEM((win_len, lanes), F32), pltpu.VMEM((win_len, lanes), BF16),
                        pltpu.VMEM((3, HEAD_DIM, lanes), F32)],
        compiler_params=_params("parallel", "parallel", "arbitrary"),
        name="nsa",
    )(proj, kv_c, kv_ct, proj, vs_t, proj, vw_t, ng_t)


GLA_SUB = 8


def _gla_kernel(*refs, chunk):
    q_ref, k_ref = refs[0:2]
    v_refs = refs[2:2 + GLA_HEADS]
    gg_refs = refs[2 + GLA_HEADS:2 + 2 * GLA_HEADS]
    small_ref, wa_ref, ba_ref, ng_ref, o_ref, state_ref, attn_ref = refs[2 + 2 * GLA_HEADS:]
    hs = range(GLA_HEADS)
    keys = [slice(h * GLA_DK, (h + 1) * GLA_DK) for h in hs]

    @pl.when(pl.program_id(1) == 0)
    def _():
        state_ref[...] = jnp.zeros_like(state_ref)

    q = q_ref[...].astype(F32) * (GLA_DK ** -0.5)
    k = k_ref[...].astype(F32)
    z = _dot(small_ref[...], wa_ref[...]) + ba_ref[...]
    log_a = (jnp.minimum(z, 0.0) - jnp.log(1.0 + jnp.exp(-jnp.abs(z)))) * (1.0 / GLA_TAU)
    tri = (lax.broadcasted_iota(I32, (chunk, chunk), 1)
           <= lax.broadcasted_iota(I32, (chunk, chunk), 0)).astype(BF16)
    b = _dot_split3(tri, log_a)

    attn_ref[...] = jnp.zeros_like(attn_ref)
    t_idx = lax.broadcasted_iota(I32, (GLA_SUB, 1), 0)
    s_lane = lax.broadcasted_iota(I32, (GLA_SUB, GLA_SUB), 1)
    for i in range(chunk // GLA_SUB):
        r0 = i * GLA_SUB
        bi, qi, ki = b[r0:r0 + GLA_SUB], q[r0:r0 + GLA_SUB], k[r0:r0 + GLA_SUB]
        diag = [jnp.zeros((GLA_SUB, GLA_SUB), F32) for _ in hs]
        for s in range(GLA_SUB):
            decay = jnp.exp(jnp.where(t_idx >= s, bi - bi[s:s + 1], NEG))
            prod = qi * ki[s:s + 1] * decay
            for h in hs:
                col = jnp.sum(prod[:, keys[h]], axis=-1, keepdims=True)
                diag[h] = jnp.where(s_lane == s, col, diag[h])
        for h in hs:
            attn_ref[h, r0:r0 + GLA_SUB, r0:r0 + GLA_SUB] = diag[h]
        if i > 0:
            ref_b = b[r0:r0 + 1]
            q_dec = (qi * jnp.exp(bi - ref_b)).astype(BF16)
            k_dec = (k[:r0] * jnp.exp(ref_b - b[:r0])).astype(BF16)
            for h in hs:
                attn_ref[h, r0:r0 + GLA_SUB, 0:r0] = _dot_nt(q_dec[:, keys[h]], k_dec[:, keys[h]])

    q_in = (q * jnp.exp(b)).astype(BF16)
    b_last = b[chunk - 1:chunk]
    k_out = (k * jnp.exp(b_last - b)).astype(BF16)
    carry = jnp.exp(b_last)
    for h in hs:
        v = v_refs[h][...]
        state_t = state_ref[h]
        o = _dot(attn_ref[h].astype(BF16), v) + _dot_nt(q_in[:, keys[h]], state_t.astype(BF16))
        state_ref[h] = state_t * carry[:, keys[h]] + _dot(v.astype(F32).T.astype(BF16), k_out[:, keys[h]])
        gate = gg_refs[h][...].astype(F32)
        out = _rms(o, ng_ref[...]) * (gate * jax.nn.sigmoid(gate))
        o_ref[:, h * GLA_DV:(h + 1) * GLA_DV] = out.astype(o_ref.dtype)


def gla_attention(proj, gla_wa, gla_ba, gla_norm_g, batch, seq, chunk=128):
    chunk = min(chunk, seq)
    assert seq % chunk == 0 and chunk % GLA_SUB == 0
    wa = jnp.zeros((LANES, GLA_KEY_WIDTH), BF16).at[SMALL_GA_LANE:SMALL_GA_LANE + GLA_RANK].set(gla_wa.astype(BF16))

    def head_spec(c0, h):
        return pl.BlockSpec((None, chunk, GLA_DV), lambda b, c, h=h: (b, c, c0 // GLA_DV + h))

    const = lambda b, c: (0, 0)
    return pl.pallas_call(
        functools.partial(_gla_kernel, chunk=chunk),
        grid=(batch, seq // chunk),
        in_specs=[pl.BlockSpec((None, chunk, GLA_KEY_WIDTH), lambda b, c: (b, c, C_GQ // GLA_KEY_WIDTH)),
                  pl.BlockSpec((None, chunk, GLA_KEY_WIDTH), lambda b, c: (b, c, C_GK // GLA_KEY_WIDTH)),
                  *[head_spec(C_GV, h) for h in range(GLA_HEADS)],
                  *[head_spec(C_GG, h) for h in range(GLA_HEADS)],
                  pl.BlockSpec((None, chunk, LANES), lambda b, c: (b, c, C_SMALL // LANES)),
                  pl.BlockSpec((LANES, GLA_KEY_WIDTH), const),
                  pl.BlockSpec((1, GLA_KEY_WIDTH), const),
                  pl.BlockSpec((1, GLA_DV), const)],
        out_specs=pl.BlockSpec((None, chunk, GLA_WIDTH), lambda b, c: (b, c, 0)),
        out_shape=jax.ShapeDtypeStruct((batch, seq, GLA_WIDTH), BF16),
        scratch_shapes=[pltpu.VMEM((GLA_HEADS, GLA_DV, GLA_DK), F32), pltpu.VMEM((GLA_HEADS, chunk, chunk), F32)],
        compiler_params=_params("parallel", "arbitrary"),
        name="gla",
    )(proj, proj, *([proj] * (2 * GLA_HEADS)), proj, wa, gla_ba.reshape(1, GLA_KEY_WIDTH).astype(F32),
      gla_norm_g.reshape(1, GLA_DV).astype(F32))


def _merge_kernel(om_ref, on_ref, og_ref, wm_ref, wn_ref, wg_ref, gm_ref, gn_ref, gl_ref, o_ref):
    def gated(gate_ref, a_ref, w_ref):
        return jax.nn.sigmoid(gate_ref[...].astype(F32)) * _dot(a_ref[...], w_ref[...])

    o_ref[...] = (gated(gm_ref, om_ref, wm_ref) + gated(gn_ref, on_ref, wn_ref)
                  + gated(gl_ref, og_ref, wg_ref)).astype(o_ref.dtype)


def merge_branches(o_m, o_n, o_g, w_m, w_n, w_g, proj2d, tm=1024, tn=512):
    m = o_m.shape[0]
    tm = min(tm, m)
    assert C_MG % tn == 0 and D_MODEL % tn == 0

    def gate_spec(c):
        return pl.BlockSpec((tm, tn), lambda i, j, c=c: (i, (C_MG + c * D_MODEL) // tn + j))

    def act_spec(width):
        return pl.BlockSpec((tm, width), lambda i, j: (i, 0))

    def w_spec(width):
        return pl.BlockSpec((width, tn), lambda i, j: (0, j))

    return pl.pallas_call(
        _merge_kernel,
        grid=(m // tm, D_MODEL // tn),
        in_specs=[act_spec(MOBA_WIDTH), act_spec(NSA_WIDTH), act_spec(GLA_WIDTH),
                  w_spec(MOBA_WIDTH), w_spec(NSA_WIDTH), w_spec(GLA_WIDTH),
                  gate_spec(0), gate_spec(1), gate_spec(2)],
        out_specs=pl.BlockSpec((tm, tn), lambda i, j: (i, j)),
        out_shape=jax.ShapeDtypeStruct((m, D_MODEL), BF16),
        compiler_params=_params("parallel", "parallel"),
        name="merge",
    )(o_m, o_n, o_g, w_m, w_n, w_g, proj2d, proj2d, proj2d)


def _first_max(vals, lane):
    top = jnp.max(vals, axis=-1, keepdims=True)
    idx = jnp.min(jnp.where(vals == top, lane, float(ROUTER_LANES)), axis=-1, keepdims=True)
    return top, idx


def _router_kernel(x_ref, g_ref, w_ref, b_ref, h_ref, route_ref):
    h = _rms(x_ref[...], g_ref[...])
    h_ref[...] = h.astype(h_ref.dtype)
    h1 = h.astype(BF16)
    r1 = h - h1.astype(F32)
    h2 = r1.astype(BF16)
    h3 = (r1 - h2.astype(F32)).astype(BF16)
    w1 = w_ref[0]
    w2 = w_ref[1]
    logits = (_dot(h1, w1) + (_dot(h1, w2) + _dot(h2, w1)) + (_dot(h2, w2) + _dot(h3, w1))) + b_ref[...]

    lane = lax.broadcasted_iota(I32, logits.shape, 1).astype(F32)
    g_logits = jnp.where(lane < MOE_GROUPS, logits, -jnp.inf)
    g_top, grp = _first_max(g_logits, lane)
    p_grp = 1.0 / jnp.sum(jnp.exp(g_logits - g_top), axis=-1, keepdims=True)
    first = MOE_GROUPS + grp * MOE_EXPERTS_PER_GROUP
    e_logits = jnp.where((lane >= first) & (lane < first + MOE_EXPERTS_PER_GROUP), logits, -jnp.inf)
    top1, lane1 = _first_max(e_logits, lane)
    top2, lane2 = _first_max(jnp.where(lane == lane1, -jnp.inf, e_logits), lane)
    ratio = jnp.exp(top2 - top1)
    w_first = p_grp / (1.0 + ratio)
    route = jnp.where(lane == 0, lane1 - MOE_GROUPS, jnp.where(lane == 1, lane2 - MOE_GROUPS,
                      jnp.where(lane == 2, w_first, jnp.where(lane == 3, w_first * ratio, 0.0))))
    route_ref[...] = route


def router(x2d, norm_g, rg_w, rg_b, re_w, re_b, tm=512):
    m, d = x2d.shape
    tm = min(tm, m)
    n_real = MOE_GROUPS + MOE_EXPERTS
    w = jnp.zeros((d, ROUTER_LANES), F32).at[:, :n_real].set(jnp.concatenate([rg_w, re_w], axis=1))
    w_hi = w.astype(BF16)
    w_lo = (w - w_hi.astype(F32)).astype(BF16)
    bias = jnp.zeros((1, ROUTER_LANES), F32).at[0, :n_real].set(jnp.concatenate([rg_b, re_b]))
    return pl.pallas_call(
        _router_kernel,
        grid=(m // tm,),
        in_specs=[pl.BlockSpec((tm, d), lambda i: (i, 0)), pl.BlockSpec((1, d), lambda i: (0, 0)),
                  pl.BlockSpec((2, d, ROUTER_LANES), lambda i: (0, 0, 0)),
                  pl.BlockSpec((1, ROUTER_LANES), lambda i: (0, 0))],
        out_specs=[pl.BlockSpec((tm, d), lambda i: (i, 0)), pl.BlockSpec((tm, ROUTER_LANES), lambda i: (i, 0))],
        out_shape=[jax.ShapeDtypeStruct((m, d), BF16), jax.ShapeDtypeStruct((m, ROUTER_LANES), F32)],
        compiler_params=_params("parallel"),
        name="router",
    )(x2d, norm_g.reshape(1, d).astype(F32), jnp.stack([w_hi, w_lo]), bias)


def _expert_kernel(blk_e_ref, n_used_ref, x_ref, wg_ref, wu_ref, wd_ref, o_ref, wg_bf, wu_bf, wd_bf):
    i = pl.program_id(0)
    new_expert = (i == 0) | (blk_e_ref[i] != blk_e_ref[jnp.maximum(i - 1, 0)])

    @pl.when(new_expert)
    def _():
        wg_bf[...] = wg_ref[...].astype(BF16)
        wu_bf[...] = wu_ref[...].astype(BF16)
        wd_bf[...] = wd_ref[...].astype(BF16)

    @pl.when(i < n_used_ref[0])
    def _():
        x = x_ref[...]
        gate = _dot(x, wg_bf[...])
        hid = gate * jax.nn.sigmoid(gate) * _dot(x, wu_bf[...])
        o_ref[...] = _dot(hid.astype(BF16), wd_bf[...]).astype(o_ref.dtype)

    @pl.when(i >= n_used_ref[0])
    def _():
        o_ref[...] = jnp.zeros_like(o_ref)


def expert_blocks(xs, blk_e, n_used, layer, w_gate, w_up, w_down):
    p, d = xs.shape
    ff = w_gate.shape[3]
    n_blk = p // MOE_ROWS
    grid_spec = pltpu.PrefetchScalarGridSpec(
        num_scalar_prefetch=2,
        grid=(n_blk,),
        in_specs=[pl.BlockSpec((MOE_ROWS, d), lambda i, e, n: (i, 0)),
                  pl.BlockSpec((None, None, d, ff), lambda i, e, n: (layer, e[i], 0, 0)),
                  pl.BlockSpec((None, None, d, ff), lambda i, e, n: (layer, e[i], 0, 0)),
                  pl.BlockSpec((None, None, ff, d), lambda i, e, n: (layer, e[i], 0, 0))],
        out_specs=pl.BlockSpec((MOE_ROWS, d), lambda i, e, n: (i, 0)),
        scratch_shapes=[pltpu.VMEM((d, ff), BF16), pltpu.VMEM((d, ff), BF16), pltpu.VMEM((ff, d), BF16)],
    )
    return pl.pallas_call(
        _expert_kernel,
        grid_spec=grid_spec,
        out_shape=jax.ShapeDtypeStruct((p, d), BF16),
        compiler_params=_params("arbitrary"),
        name="experts",
    )(blk_e, n_used, xs, w_gate, w_up, w_down)


def _combine_kernel(x_ref, y0_ref, y1_ref, w_ref, g_ref, o_ref, *, final_norm):
    w = w_ref[...]
    x = x_ref[...] + (w[:, 0:1] * y0_ref[...].astype(F32) + w[:, 1:2] * y1_ref[...].astype(F32))
    o_ref[...] = _rms(x, g_ref[...]) if final_norm else x


def combine(x2d, y0, y1, w, norm_g, final_norm, tm=512):
    m, d = x2d.shape
    tm = min(tm, m)
    row = pl.BlockSpec((tm, d), lambda i: (i, 0))
    return pl.pallas_call(
        functools.partial(_combine_kernel, final_norm=final_norm),
        grid=(m // tm,),
        in_specs=[row, row, row, pl.BlockSpec((tm, MOE_TOPK), lambda i: (i, 0)),
                  pl.BlockSpec((1, d), lambda i: (0, 0))],
        out_specs=row,
        out_shape=jax.ShapeDtypeStruct((m, d), F32),
        compiler_params=_params("parallel"),
        name="combine",
    )(x2d, y0, y1, w, norm_g.reshape(1, d).astype(F32))


def hier_moe(x2d, norm_g, rg_w, rg_b, re_w, re_b, layer, w_gate, w_up, w_down, out_norm_g, final_norm):
    t = x2d.shape[0]
    h, route = router(x2d, norm_g, rg_w, rg_b, re_w, re_b)
    expert = route[:, 0:MOE_TOPK].astype(I32)
    w = route[:, MOE_TOPK:2 * MOE_TOPK]

    a = t * MOE_TOPK
    e_flat = expert.reshape(a)
    onehot = (e_flat[:, None] == jnp.arange(MOE_EXPERTS, dtype=I32)[None, :]).astype(I32)
    running = jnp.cumsum(onehot, axis=0)
    counts = running[-1]
    rank = jnp.take_along_axis(running, e_flat[:, None], axis=1)[:, 0] - 1
    padded = (counts + MOE_ROWS - 1) // MOE_ROWS * MOE_ROWS
    pends = jnp.cumsum(padded)
    dest = (pends - padded)[e_flat] + rank
    p_rows = (a + MOE_EXPERTS * (MOE_ROWS - 1)) // MOE_ROWS * MOE_ROWS
    n_blk = p_rows // MOE_ROWS
    row_tok = (jnp.arange(p_rows, dtype=I32) % t).at[dest].set(jnp.arange(a, dtype=I32) // MOE_TOPK)
    blk_start = jnp.arange(n_blk, dtype=I32) * MOE_ROWS
    blk_e = jnp.minimum(jnp.sum((pends[None, :] <= blk_start[:, None]).astype(I32), axis=1), MOE_EXPERTS - 1)
    n_used = (pends[-1] // MOE_ROWS).astype(I32).reshape(1)

    xs = jnp.take(h, row_tok, axis=0)
    y_rows = expert_blocks(xs, blk_e, n_used, layer, w_gate, w_up, w_down)
    dest2 = dest.reshape(t, MOE_TOPK)
    y0 = jnp.take(y_rows, dest2[:, 0], axis=0)
    y1 = jnp.take(y_rows, dest2[:, 1], axis=0)
    return combine(x2d, y0, y1, w, out_norm_g, final_norm)


PERM_TILE = 512
PERM_REGIONS = ((0, 0, C_GQ // PERM_TILE),
                (_SRC_GQ - C_GQ, C_GQ // PERM_TILE, C_GG // PERM_TILE),
                (_SRC_GG - C_GG, C_GG // PERM_TILE, C_SMALL // PERM_TILE))


def _permute_kernel(src_ref, small_ref, o_ref):
    @pl.when(pl.program_id(1) < C_SMALL // PERM_TILE)
    def _():
        o_ref[...] = src_ref[0].T.astype(o_ref.dtype)

    @pl.when(pl.program_id(1) == C_SMALL // PERM_TILE)
    def _():
        o_ref[:, 0:LANES] = small_ref[...].astype(o_ref.dtype)


def permute_w_in(w_in, layer, tk=512):
    assert C_GQ % PERM_TILE == 0 and C_GG % PERM_TILE == 0 and C_SMALL % PERM_TILE == 0
    assert all(shift % SUBLANES == 0 for shift, _, _ in PERM_REGIONS)
    d = w_in.shape[1]
    w_t = jnp.swapaxes(w_in, 1, 2)
    small = jnp.concatenate([w_in[layer, :, _SRC_GA:_SRC_GG], w_in[layer, :, _SRC_NG:_SRC_GQ],
                             jnp.zeros((d, LANES - GLA_RANK - 3 * NSA_HEADS), w_in.dtype)], axis=1)

    def src_row(j):
        shift = sum(jnp.where((j >= j0) & (j < j1), s, 0) for s, j0, j1 in PERM_REGIONS)
        return pl.multiple_of(jnp.minimum(j * PERM_TILE + shift, _SRC_END - PERM_TILE), SUBLANES)

    return pl.pallas_call(
        _permute_kernel,
        grid=(d // tk, pl.cdiv(PROJ_WIDTH, PERM_TILE)),
        in_specs=[pl.BlockSpec((pl.Element(1), pl.Element(PERM_TILE), pl.Element(tk)),
                               lambda k, j: (layer, src_row(j), k * tk)),
                  pl.BlockSpec((tk, LANES), lambda k, j: (k, 0))],
        out_specs=pl.BlockSpec((tk, PERM_TILE), lambda k, j: (k, j)),
        out_shape=jax.ShapeDtypeStruct((d, PROJ_WIDTH), BF16),
        compiler_params=_params("parallel", "parallel"),
        name="permute_w_in",
    )(w_t, small)


def hybrid_layer(x, norm1_g, w_in, nsa_cmp_pe, nsa_cmp_w1, nsa_cmp_w2, gla_wa, gla_ba, gla_norm_g,
                 w_br_moba, w_br_nsa, w_br_gla, w_out, norm2_g, router_group_w, router_group_b,
                 router_expert_w, router_expert_b, layer, expert_w_gate, expert_w_up, expert_w_down,
                 out_norm_g, final_norm):
    batch, seq, d = x.shape
    t = batch * seq
    x2d = x.reshape(t, d)
    proj2d = norm_matmul(x2d, norm1_g, permute_w_in(w_in, layer), BF16)
    proj = proj2d.reshape(batch, seq, PROJ_WIDTH)
    o_m = moba_attention(proj, batch, seq)
    kv_c, kv_ct = nsa_compress(proj, nsa_cmp_pe, nsa_cmp_w1, nsa_cmp_w2, batch, seq)
    o_n = nsa_attention(proj, kv_c, kv_ct, batch, seq)
    o_g = gla_attention(proj, gla_wa, gla_ba, gla_norm_g, batch, seq)
    merged = merge_branches(o_m.reshape(t, -1), o_n.reshape(t, -1), o_g.reshape(t, -1),
                            w_br_moba.astype(BF16), w_br_nsa.astype(BF16), w_br_gla.astype(BF16), proj2d)
    x2d = matmul_residual(merged, w_out.astype(BF16), x2d)
    x2d = hier_moe(x2d, norm2_g, router_group_w, router_group_b, router_expert_w, router_expert_b,
                   layer, expert_w_gate, expert_w_up, expert_w_down, out_norm_g, final_norm)
    return x2d.reshape(batch, seq, d)


def kernel(x, norm1_g, w_in, nsa_cmp_pe, nsa_cmp_w1, nsa_cmp_w2, gla_wa, gla_ba, gla_norm_g, w_br_moba,
           w_br_nsa, w_br_gla, w_out, norm2_g, router_group_w, router_group_b, router_expert_w,
           router_expert_b, expert_w_gate, expert_w_up, expert_w_down, final_norm_g):
    for l in range(DEPTH):
        x = hybrid_layer(x, norm1_g[l], w_in, nsa_cmp_pe[l], nsa_cmp_w1[l], nsa_cmp_w2[l], gla_wa[l],
                         gla_ba[l], gla_norm_g[l], w_br_moba[l], w_br_nsa[l], w_br_gla[l], w_out[l],
                         norm2_g[l], router_group_w[l], router_group_b[l], router_expert_w[l],
                         router_expert_b[l], l, expert_w_gate, expert_w_up, expert_w_down,
                         final_norm_g, l == DEPTH - 1)
    return x
```

```python
import functools

import jax
import jax.numpy as jnp
import numpy as np
from jax import lax
from jax.experimental import pallas as pl
from jax.experimental.pallas import tpu as pltpu

F32 = jnp.float32
BF16 = jnp.bfloat16
I32 = jnp.int32

D_MODEL = 2048
DEPTH = 2
HEAD_DIM = 128
NEG = -1e30
FORCE = 1e9
EPS = 1e-6
LOG2E = 1.4426950408889634

MOBA_HEADS = 8
MOBA_BLOCK = 256
MOBA_TOPK = 3
MOBA_WIDTH = MOBA_HEADS * HEAD_DIM

NSA_HEADS = 8
NSA_KV_HEADS = 2
NSA_GROUP = NSA_HEADS // NSA_KV_HEADS
NSA_CMP_LEN = 32
NSA_CMP_STRIDE = 16
NSA_SLC_BLOCK = 64
NSA_TOPN = 8
NSA_WINDOW = 512
NSA_WIDTH = NSA_HEADS * HEAD_DIM
NSA_KV_WIDTH = NSA_KV_HEADS * HEAD_DIM

GLA_HEADS = 4
GLA_DK = 128
GLA_DV = 256
GLA_RANK = 16
GLA_TAU = 16.0
GLA_KEY_WIDTH = GLA_HEADS * GLA_DK
GLA_WIDTH = GLA_HEADS * GLA_DV

N_BRANCH = 3
MOE_GROUPS = 4
MOE_EXPERTS_PER_GROUP = 8
MOE_EXPERTS = MOE_GROUPS * MOE_EXPERTS_PER_GROUP
MOE_TOPK = 2
MOE_FF = D_MODEL // 4

LANES = 128
SUBLANES = 8
BF16_SUBLANES = 16
VMEM_LIMIT_BYTES = 56 * 1024 * 1024

_SRC_NG = MOBA_WIDTH * 3 + NSA_WIDTH + 6 * NSA_KV_WIDTH
_SRC_GQ = _SRC_NG + 3 * NSA_HEADS
_SRC_GA = _SRC_GQ + 2 * GLA_KEY_WIDTH + GLA_WIDTH
_SRC_GG = _SRC_GA + GLA_RANK
_SRC_END = _SRC_GG + GLA_WIDTH + N_BRANCH * D_MODEL

C_MQ = 0
C_MK = C_MQ + MOBA_WIDTH
C_MV = C_MK + MOBA_WIDTH
C_NQ = C_MV + MOBA_WIDTH
C_NKV = C_NQ + NSA_WIDTH
C_GQ = C_NKV + 6 * NSA_KV_WIDTH
C_GK = C_GQ + GLA_KEY_WIDTH
C_GV = C_GK + GLA_KEY_WIDTH
C_GG = C_GV + GLA_WIDTH
C_MG = C_GG + GLA_WIDTH
C_SMALL = C_MG + N_BRANCH * D_MODEL
PROJ_WIDTH = C_SMALL + LANES
SMALL_GA_LANE = 0
SMALL_NG_LANE = GLA_RANK

ROUTER_LANES = LANES
MOE_ROWS = 256


def _params(*semantics):
    return pltpu.CompilerParams(dimension_semantics=semantics, vmem_limit_bytes=VMEM_LIMIT_BYTES)


def _dot(a, b):
    return jnp.dot(a, b, preferred_element_type=F32)


def _dot_nt(a, b):
    return lax.dot_general(a, b, (((1,), (1,)), ((), ())), preferred_element_type=F32)


def _dot_split3(a01, x):
    x1 = x.astype(BF16)
    r1 = x - x1.astype(F32)
    x2 = r1.astype(BF16)
    x3 = (r1 - x2.astype(F32)).astype(BF16)
    return _dot(a01, x1) + _dot(a01, x2) + _dot(a01, x3)


def _rms(x, g):
    return x * lax.rsqrt(jnp.mean(x * x, axis=-1, keepdims=True) + EPS) * g


def _norm_matmul_kernel(x_ref, g_ref, b_ref, o_ref, h_ref):
    @pl.when(pl.program_id(1) == 0)
    def _():
        h_ref[...] = _rms(x_ref[...], g_ref[...]).astype(h_ref.dtype)

    o_ref[...] = _dot(h_ref[...], b_ref[...]).astype(o_ref.dtype)


def norm_matmul(x, g, b, out_dtype, tm=1024, tn=1152):
    m, k = x.shape
    n = b.shape[1]
    tm, tn = min(tm, m), min(tn, n)
    return pl.pallas_call(
        _norm_matmul_kernel,
        grid=(m // tm, n // tn),
        in_specs=[pl.BlockSpec((tm, k), lambda i, j: (i, 0)), pl.BlockSpec((1, k), lambda i, j: (0, 0)),
                  pl.BlockSpec((k, tn), lambda i, j: (0, j))],
        out_specs=pl.BlockSpec((tm, tn), lambda i, j: (i, j)),
        out_shape=jax.ShapeDtypeStruct((m, n), out_dtype),
        scratch_shapes=[pltpu.VMEM((tm, k), BF16)],
        compiler_params=_params("parallel", "arbitrary"),
        name="norm_matmul",
    )(x, g.reshape(1, k).astype(F32), b)


def _matmul_residual_kernel(a_ref, b_ref, r_ref, o_ref):
    o_ref[...] = r_ref[...] + _dot(a_ref[...], b_ref[...])


def matmul_residual(a, b, res, tm=1024, tn=1024):
    m, k = a.shape
    n = b.shape[1]
    tm, tn = min(tm, m), min(tn, n)
    return pl.pallas_call(
        _matmul_residual_kernel,
        grid=(m // tm, n // tn),
        in_specs=[pl.BlockSpec((tm, k), lambda i, j: (i, 0)), pl.BlockSpec((k, tn), lambda i, j: (0, j)),
                  pl.BlockSpec((tm, tn), lambda i, j: (i, j))],
        out_specs=pl.BlockSpec((tm, tn), lambda i, j: (i, j)),
        out_shape=jax.ShapeDtypeStruct((m, n), F32),
        compiler_params=_params("parallel", "parallel"),
        name="matmul_residual",
    )(a, b, res)


def _softmax_step(carry, q, k, v, mask, scale):
    m_i, l_i, acc = carry
    s = jnp.where(mask, _dot_nt(q, k) * scale, NEG)
    m_new = jnp.maximum(m_i, jnp.max(s, axis=-1, keepdims=True))
    alpha = jnp.exp(m_i - m_new)
    p = jnp.where(mask, jnp.exp(s - m_new), 0.0)
    l_new = alpha * l_i + jnp.sum(p, axis=-1, keepdims=True)
    acc_new = alpha * acc + _dot(p.astype(v.dtype), v)
    return m_new, l_new, acc_new


def _softmax_init(m, e):
    return jnp.full((m, 1), NEG, F32), jnp.zeros((m, 1), F32), jnp.zeros((m, e), F32)


def _rank_desc(vals, n_candidates):
    lane = lax.broadcasted_iota(I32, vals.shape, 1)
    rank = jnp.zeros(vals.shape, F32)
    for m in range(n_candidates):
        vm = vals[:, m:m + 1]
        beats = (vm > vals) | ((vm == vals) & (lane > m))
        rank = rank + beats.astype(F32)
    return rank


def _softmax_step_t(carry, q, k, v_t, bias, scale):
    m_i, l_i, acc = carry
    s = _dot_nt(k, q) * scale + bias
    m_new = jnp.maximum(m_i, jnp.max(s, axis=0, keepdims=True))
    alpha = jnp.exp(m_i - m_new)
    p = jnp.exp(s - m_new)
    l_new = alpha * l_i + jnp.sum(p, axis=0, keepdims=True)
    acc_new = alpha * acc + _dot(v_t, p.astype(v_t.dtype))
    return m_new, l_new, acc_new


def _softmax_init_t(m, e):
    return jnp.full((1, m), NEG, F32), jnp.zeros((1, m), F32), jnp.zeros((e, m), F32)


def _rank_desc_rows(vals, n_candidates):
    row = lax.broadcasted_iota(I32, vals.shape, 0)
    rank = jnp.zeros(vals.shape, F32)
    for m in range(n_candidates):
        vm = vals[m:m + 1, :]
        beats = (vm > vals) | ((vm == vals) & (row > m))
        rank = rank + beats.astype(F32)
    return rank


def _append_ones_rows(v_t):
    lead = v_t.shape[:-2]
    n = v_t.shape[-1]
    ones = jnp.ones(lead + (1, n), v_t.dtype)
    zeros = jnp.zeros(lead + (BF16_SUBLANES - 1, n), v_t.dtype)
    return jnp.concatenate([v_t, ones, zeros], axis=-2)


def _lane_column(x, idx):
    lane = lax.broadcasted_iota(I32, x.shape, 1)
    return jnp.sum(jnp.where(lane == idx, x, 0.0), axis=-1, keepdims=True)


def _moba_kernel(q_ref, k_ref, vt_ref, o_ref, kmean_ref, bias_ref, s_ref, p_ref, acc_ref, bmax_ref, m_ref, *,
                 n_blocks, topk, heads):
    blk = MOBA_BLOCK
    cur = pl.program_id(2)
    scale = HEAD_DIM ** -0.5
    cols = [slice(h * HEAD_DIM, (h + 1) * HEAD_DIM) for h in range(heads)]

    @pl.when(cur == 0)
    def _():
        kmean_ref[...] = jnp.zeros_like(kmean_ref)
        for h in range(heads):
            for n in range(n_blocks):
                k_blk = k_ref[n * blk:(n + 1) * blk, cols[h]].astype(F32)
                kmean_ref[h, n:n + 1, :] = jnp.sum(k_blk, axis=0, keepdims=True) * (1.0 / blk)

    for h in range(heads):
        gate = _dot_nt(kmean_ref[h].astype(BF16), q_ref[:, cols[h]])
        row = lax.broadcasted_iota(I32, gate.shape, 0)
        valid = row < cur
        gate = jnp.where(valid, gate, NEG)
        chosen = valid & (_rank_desc_rows(gate, n_blocks) < topk)
        bias_ref[h] = jnp.where(chosen, 0.0, NEG)

    seq = k_ref.shape[0]
    lo_blocks = (n_blocks + 1) // 2
    lo = lo_blocks * blk
    need_hi = cur >= lo_blocks

    def score_blocks(b0, b1):
        for h in range(heads):
            s = _dot_nt(k_ref[b0 * blk:b1 * blk, cols[h]], q_ref[:, cols[h]])
            s_ref[h, b0 * blk:b1 * blk, :] = s
            for n in range(b0, b1):
                s_blk = s[(n - b0) * blk:(n - b0 + 1) * blk]
                bmax_ref[h, n] = jnp.max(s_blk.reshape(blk // SUBLANES, SUBLANES, blk), axis=0)

    def past_max(h, b0, b1):
        return functools.reduce(jnp.maximum, [bmax_ref[h, n] * scale + bias_ref[h, n:n + 1, :]
                                              for n in range(b0, b1)])

    score_blocks(0, lo_blocks)
    if lo < seq:
        pl.when(need_hi)(lambda: score_blocks(lo_blocks, n_blocks))

    sub = LANES
    base = pl.multiple_of(cur * blk, blk)
    qry_i = lax.broadcasted_iota(I32, (sub, blk), 1)
    key_i = [j * sub + lax.broadcasted_iota(I32, (sub, blk), 0) for j in range(blk // sub)]
    exp_scale = scale * LOG2E

    def fold_max(x):
        return jnp.max(x.reshape(sub // SUBLANES, SUBLANES, blk), axis=0)

    def tiles(h, start):
        return [s_ref[h, pl.ds(pl.multiple_of(start + j * sub, sub), sub), :] for j in range(blk // sub)]

    for h in range(heads):
        own = [fold_max(jnp.where(key_i[j] <= qry_i, t, NEG)) for j, t in enumerate(tiles(h, base))]
        m_ref[h] = jnp.maximum(functools.reduce(jnp.maximum, own) * scale, past_max(h, 0, lo_blocks))

    if lo < seq:
        @pl.when(need_hi)
        def _():
            for h in range(heads):
                m_ref[h] = jnp.maximum(m_ref[h], past_max(h, lo_blocks, n_blocks))

    m_log2 = [jnp.max(m_ref[h], axis=0, keepdims=True) * LOG2E for h in range(heads)]

    def store_p(h, start, j, p):
        p_ref[h, pl.ds(pl.multiple_of(start + j * sub, sub), sub), :] = p

    def prob_body(n, carry):
        for h in range(heads):
            shift = bias_ref[h, pl.ds(n, 1), :] * LOG2E - m_log2[h]
            for j, t in enumerate(tiles(h, n * blk)):
                store_p(h, n * blk, j, jnp.exp2((t * exp_scale + shift).astype(BF16)))
        return carry

    for h in range(heads):
        for j, t in enumerate(tiles(h, base)):
            p = jnp.exp2((t * exp_scale - m_log2[h]).astype(BF16))
            store_p(h, base, j, jnp.where(key_i[j] <= qry_i, p, jnp.zeros_like(p)))
    lax.fori_loop(0, cur, prob_body, 0)

    def zero_body(n, carry):
        for h in range(heads):
            for j in range(blk // sub):
                store_p(h, n * blk, j, jnp.zeros((sub, blk), BF16))
        return carry

    lax.fori_loop(cur + 1, jnp.where(need_hi, n_blocks, lo_blocks), zero_body, 0)
    for h in range(heads):
        acc_ref[h] = _dot(vt_ref[h, :, 0:lo], p_ref[h, 0:lo, :])

    if lo < seq:
        @pl.when(need_hi)
        def _():
            for h in range(heads):
                acc_ref[h] += _dot(vt_ref[h, :, lo:seq], p_ref[h, lo:seq, :])

    for h in range(heads):
        acc = acc_ref[h]
        o_ref[:, cols[h]] = (acc[0:HEAD_DIM] / acc[HEAD_DIM:HEAD_DIM + 1]).T.astype(o_ref.dtype)


def moba_attention(proj, batch, seq, heads=2):
    assert seq % MOBA_BLOCK == 0 and MOBA_HEADS % heads == 0
    n_blocks = seq // MOBA_BLOCK
    rows = -(-n_blocks // BF16_SUBLANES) * BF16_SUBLANES
    width = heads * HEAD_DIM
    qb, kb = C_MQ // width, C_MK // width
    v_t = proj[:, :, C_MV:C_MV + MOBA_WIDTH].reshape(batch, seq, MOBA_HEADS, HEAD_DIM).transpose(0, 2, 3, 1)
    v_t = _append_ones_rows(v_t)
    v_rows = HEAD_DIM + BF16_SUBLANES
    kern = functools.partial(_moba_kernel, n_blocks=n_blocks, topk=min(MOBA_TOPK, n_blocks), heads=heads)
    return pl.pallas_call(
        kern,
        grid=(batch, MOBA_HEADS // heads, n_blocks),
        in_specs=[pl.BlockSpec((None, MOBA_BLOCK, width), lambda b, h, i: (b, i, qb + h)),
                  pl.BlockSpec((None, seq, width), lambda b, h, i: (b, 0, kb + h)),
                  pl.BlockSpec((None, heads, v_rows, seq), lambda b, h, i: (b, h, 0, 0))],
        out_specs=pl.BlockSpec((None, MOBA_BLOCK, width), lambda b, h, i: (b, i, h)),
        out_shape=jax.ShapeDtypeStruct((batch, seq, MOBA_WIDTH), BF16),
        scratch_shapes=[pltpu.VMEM((heads, rows, HEAD_DIM), F32), pltpu.VMEM((heads, rows, MOBA_BLOCK), F32),
                        pltpu.VMEM((heads, seq, MOBA_BLOCK), F32), pltpu.VMEM((heads, seq, MOBA_BLOCK), BF16),
                        pltpu.VMEM((heads, v_rows, MOBA_BLOCK), F32),
                        pltpu.VMEM((heads, n_blocks, SUBLANES, MOBA_BLOCK), F32),
                        pltpu.VMEM((heads, SUBLANES, MOBA_BLOCK), F32)],
        compiler_params=_params("parallel", "parallel", "arbitrary"),
        name="moba",
    )(proj, proj, v_t)


def _nsa_compress_kernel(x_ref, pe_ref, w1_ref, w2_ref, o_ref, ot_ref):
    x = x_ref[...].astype(F32)
    half = x.shape[1]
    lo = _dot((x + pe_ref[:, :half]).astype(BF16), w1_ref[:half, :])
    hi = _dot((x + pe_ref[:, half:]).astype(BF16), w1_ref[half:, :])
    pre = lo + pltpu.roll(hi, hi.shape[0] - 1, 0)
    hid = pre * jax.nn.sigmoid(pre)
    out = _dot(hid.astype(BF16), w2_ref[...])
    o_ref[...] = out.astype(o_ref.dtype)
    ot_ref[...] = out.T.astype(ot_ref.dtype)


def nsa_compress(proj, cmp_pe, cmp_w1, cmp_w2, batch, seq):
    n16 = seq // NSA_CMP_STRIDE
    width = NSA_CMP_STRIDE * HEAD_DIM
    x = proj[:, :, C_NKV:C_NKV + 2 * NSA_KV_WIDTH].reshape(batch, seq, 2, NSA_KV_HEADS, HEAD_DIM)
    x = x.transpose(0, 2, 3, 1, 4).reshape(batch, 2, NSA_KV_HEADS, n16, width)
    pe = cmp_pe.reshape(2, 1, NSA_CMP_LEN * HEAD_DIM).astype(F32)
    return pl.pallas_call(
        _nsa_compress_kernel,
        grid=(batch, 2, NSA_KV_HEADS),
        in_specs=[pl.BlockSpec((None, None, None, n16, width), lambda b, c, g: (b, c, g, 0, 0)),
                  pl.BlockSpec((None, 1, 2 * width), lambda b, c, g: (c, 0, 0)),
                  pl.BlockSpec((None, 2 * width, HEAD_DIM), lambda b, c, g: (c, 0, 0)),
                  pl.BlockSpec((None, HEAD_DIM, HEAD_DIM), lambda b, c, g: (c, 0, 0))],
        out_specs=[pl.BlockSpec((None, None, None, n16, HEAD_DIM), lambda b, c, g: (b, c, g, 0, 0)),
                   pl.BlockSpec((None, None, None, HEAD_DIM, n16), lambda b, c, g: (b, c, g, 0, 0))],
        out_shape=[jax.ShapeDtypeStruct((batch, 2, NSA_KV_HEADS, n16, HEAD_DIM), BF16),
                   jax.ShapeDtypeStruct((batch, 2, NSA_KV_HEADS, HEAD_DIM, n16), BF16)],
        compiler_params=_params("parallel", "parallel", "parallel"),
        name="nsa_compress",
    )(x, pe, cmp_w1.astype(BF16), cmp_w2.astype(BF16))


def _nsa_kernel(q_ref, kc_ref, vct_ref, ks_ref, vst_ref, kw_ref, vwt_ref, ngt_ref, o_ref,
                bias_ref, s_ref, p_ref, sw_ref, pw_ref, out_ref, *, tq, span, n_cmp, n_slc, topn, win_len):
    g = pl.program_id(1)
    qi = pl.program_id(2)
    seq = ks_ref.shape[0]
    n16 = kc_ref.shape[0]
    lanes = NSA_GROUP * tq
    blk = NSA_SLC_BLOCK
    scale = HEAD_DIM ** -0.5
    exp_scale = scale * LOG2E
    start = qi * tq
    heads = [slice(r * tq, (r + 1) * tq) for r in range(NSA_GROUP)]
    q = [q_ref[:, r * HEAD_DIM:(r + 1) * HEAD_DIM] for r in range(NSA_GROUP)]

    def per_group(row):
        return jnp.concatenate([row] * NSA_GROUP, axis=1)

    def scores(keys):
        return jnp.concatenate([_dot_nt(keys, q[r]) for r in range(NSA_GROUP)], axis=1)

    def fold(op, x):
        return op(x.reshape(x.shape[0] // SUBLANES, SUBLANES, lanes), axis=0)

    pos1 = start + lax.broadcasted_iota(I32, (1, tq), 1)
    pos = per_group(pos1)

    n_idx = lax.broadcasted_iota(I32, (n16, lanes), 0)
    in_range = n_idx < n_cmp
    cmask = (n_idx * NSA_CMP_STRIDE + (NSA_CMP_LEN - 1) <= pos) & in_range
    s_c = jnp.where(cmask, scores(kc_ref[...]) * scale, NEG)
    e_c = jnp.where(in_range, jnp.exp(s_c - jnp.max(s_c, axis=0, keepdims=True)), 0.0)
    p_c = jnp.where(cmask, e_c / jnp.sum(e_c, axis=0, keepdims=True), 0.0)
    out_ref[0] = _dot(vct_ref[...], p_c.astype(BF16))

    p_sum = functools.reduce(lambda a, b: a + b, [p_c[:, h] for h in heads])
    rows = bias_ref.shape[0]
    oj = lax.broadcasted_iota(I32, (rows, n16), 0)
    on = lax.broadcasted_iota(I32, (rows, n16), 1)
    overlap_t = ((on * NSA_CMP_STRIDE < (oj + 1) * blk) & (on * NSA_CMP_STRIDE + (NSA_CMP_LEN - 1) >= oj * blk)
                 & (on < n_cmp) & (oj < n_slc)).astype(BF16)
    p_hi = p_sum.astype(BF16)
    p_lo = (p_sum - p_hi.astype(F32)).astype(BF16)
    imp = _dot(overlap_t, p_hi) + _dot(overlap_t, p_lo)
    j_idx = lax.broadcasted_iota(I32, (rows, tq), 0)
    cur_blk = pos1 // blk
    forced = (j_idx == 0) | (j_idx == cur_blk) | (j_idx == cur_blk - 1)
    imp = jnp.where(forced, FORCE, imp)
    imp = jnp.where(j_idx > cur_blk, NEG, imp)
    chosen = (_rank_desc_rows(imp, n_slc) < topn) & (j_idx <= cur_blk)
    bias_ref[...] = jnp.where(chosen, 0.0, NEG)

    n_spans = seq // span
    s_ref[0:span, :] = scores(ks_ref[0:span, :])
    w0 = pl.multiple_of(jnp.maximum(start + tq - win_len, 0), tq)
    sw_ref[...] = scores(kw_ref[pl.ds(w0, win_len), :])
    for k in range(1, n_spans):
        def span_scores(k=k):
            s_ref[k * span:(k + 1) * span, :] = scores(ks_ref[k * span:(k + 1) * span, :])
        pl.when(start >= k * span)(span_scores)

    per_tile = tq // blk
    first_own = qi * per_tile

    def block_bias(j):
        return per_group(bias_ref[pl.ds(j, 1), :])

    def block_rows(j):
        return pl.ds(pl.multiple_of(j * blk, blk), blk)

    def causal_mask(j):
        key = j * blk + lax.broadcasted_iota(I32, (blk, lanes), 0)
        return key <= pos

    def max_body(i, m8):
        for d in range(per_tile):
            j = i * per_tile + d
            m8 = jnp.maximum(m8, fold(jnp.max, s_ref[block_rows(j), :]) * scale + block_bias(j))
        return m8

    m8 = lax.fori_loop(0, qi, max_body, jnp.full((SUBLANES, lanes), NEG, F32))
    for d in range(per_tile):
        j = first_own + d
        own = jnp.where(causal_mask(j), s_ref[block_rows(j), :], NEG)
        m8 = jnp.maximum(m8, fold(jnp.max, own) * scale + block_bias(j))
    m_log2 = jnp.max(m8, axis=0, keepdims=True) * LOG2E

    def prob_body(i, l8):
        for d in range(per_tile):
            j = i * per_tile + d
            p = jnp.exp2(s_ref[block_rows(j), :] * exp_scale + (block_bias(j) * LOG2E - m_log2))
            p_ref[block_rows(j), :] = p.astype(p_ref.dtype)
            l8 = l8 + fold(jnp.sum, p)
        return l8

    l8 = lax.fori_loop(0, qi, prob_body, jnp.zeros((SUBLANES, lanes), F32))
    for d in range(per_tile):
        j = first_own + d
        p = jnp.exp2(s_ref[block_rows(j), :] * exp_scale + (block_bias(j) * LOG2E - m_log2))
        p = jnp.where(causal_mask(j), p, 0.0)
        p_ref[block_rows(j), :] = p.astype(p_ref.dtype)
        l8 = l8 + fold(jnp.sum, p)

    def zero_body(i, carry):
        p_ref[pl.ds(pl.multiple_of(i * tq, tq), tq), :] = jnp.zeros((tq, lanes), p_ref.dtype)
        return carry

    visible_tiles = (start // span + 1) * (span // tq)
    lax.fori_loop(qi + 1, visible_tiles, zero_body, 0)

    out_ref[1] = _dot(vst_ref[:, 0:span], p_ref[0:span, :])

    n_tiles = win_len // blk

    def win_tile(t):
        key = w0 + t * blk + lax.broadcasted_iota(I32, (blk, lanes), 0)
        return sw_ref[t * blk:(t + 1) * blk, :], (key <= pos) & (key > pos - NSA_WINDOW)

    wm8 = jnp.full((SUBLANES, lanes), NEG, F32)
    for t in range(n_tiles):
        tile, mask = win_tile(t)
        wm8 = jnp.maximum(wm8, fold(jnp.max, jnp.where(mask, tile, NEG)))
    wm_log2 = jnp.max(wm8, axis=0, keepdims=True) * exp_scale
    wl8 = jnp.zeros((SUBLANES, lanes), F32)
    for t in range(n_tiles):
        tile, mask = win_tile(t)
        p = jnp.where(mask, jnp.exp2(tile * exp_scale - wm_log2), 0.0)
        pw_ref[t * blk:(t + 1) * blk, :] = p.astype(pw_ref.dtype)
        wl8 = wl8 + fold(jnp.sum, p)
    w_tile0 = w0 // tq
    pv = _dot(vwt_ref[w_tile0], pw_ref[0:tq, :])
    for c in range(1, win_len // tq):
        pv = pv + _dot(vwt_ref[w_tile0 + c], pw_ref[c * tq:(c + 1) * tq, :])
    out_ref[2] = pv / jnp.sum(wl8, axis=0, keepdims=True)

    for k in range(1, n_spans):
        def span_pv(k=k):
            out_ref[1] += _dot(vst_ref[:, k * span:(k + 1) * span], p_ref[k * span:(k + 1) * span, :])
        pl.when(start >= k * span)(span_pv)
    out_ref[1] = out_ref[1] / jnp.sum(l8, axis=0, keepdims=True)

    for r in range(NSA_GROUP):
        gate_row = (g * NSA_GROUP + r) * 3
        mix = jnp.zeros((HEAD_DIM, tq), F32)
        for c in range(3):
            mix = mix + jax.nn.sigmoid(ngt_ref[pl.ds(gate_row + c, 1), :]) * out_ref[c, :, heads[r]]
        o_ref[:, r * HEAD_DIM:(r + 1) * HEAD_DIM] = mix.T.astype(o_ref.dtype)


def nsa_attention(proj, kv_c, kv_ct, batch, seq, tq=128):
    assert tq == LANES and seq % tq == 0 and tq % NSA_SLC_BLOCK == 0
    n_cmp = (seq - NSA_CMP_LEN) // NSA_CMP_STRIDE + 1
    n_slc = seq // NSA_SLC_BLOCK
    rows = -(-n_slc // BF16_SUBLANES) * BF16_SUBLANES
    span = max(tq, seq // 4)
    win_len = min(NSA_WINDOW + tq, seq)
    n16 = seq // NSA_CMP_STRIDE
    gw = NSA_GROUP * HEAD_DIM
    lanes = NSA_GROUP * tq
    nkv = C_NKV // HEAD_DIM

    def kv_cols(slot):
        c0 = C_NKV + slot * NSA_KV_WIDTH
        return proj[:, :, c0:c0 + NSA_KV_WIDTH].reshape(batch, seq, NSA_KV_HEADS, HEAD_DIM)

    vs_t = kv_cols(3).transpose(0, 2, 3, 1)
    vw_t = kv_cols(5).reshape(batch, seq // tq, tq, NSA_KV_HEADS, HEAD_DIM).transpose(0, 3, 1, 4, 2)
    n_gates = 3 * NSA_HEADS
    ng_t = proj[:, :, C_SMALL + SMALL_NG_LANE:C_SMALL + SMALL_NG_LANE + n_gates].astype(F32).transpose(0, 2, 1)

    def k_spec(slot):
        return pl.BlockSpec((None, seq, HEAD_DIM),
                            lambda b, g, i, slot=slot: (b, 0, nkv + slot * NSA_KV_HEADS + g))

    kern = functools.partial(_nsa_kernel, tq=tq, span=span, n_cmp=n_cmp, n_slc=n_slc,
                             topn=min(NSA_TOPN, n_slc), win_len=win_len)
    return pl.pallas_call(
        kern,
        grid=(batch, NSA_KV_HEADS, seq // tq),
        in_specs=[pl.BlockSpec((None, tq, gw), lambda b, g, i: (b, i, C_NQ // gw + g)),
                  pl.BlockSpec((None, None, None, n16, HEAD_DIM), lambda b, g, i: (b, 0, g, 0, 0)),
                  pl.BlockSpec((None, None, None, HEAD_DIM, n16), lambda b, g, i: (b, 1, g, 0, 0)),
                  k_spec(2),
                  pl.BlockSpec((None, None, HEAD_DIM, seq), lambda b, g, i: (b, g, 0, 0)),
                  k_spec(4),
                  pl.BlockSpec((None, None, seq // tq, HEAD_DIM, tq), lambda b, g, i: (b, g, 0, 0, 0)),
                  pl.BlockSpec((None, n_gates, tq), lambda b, g, i: (b, 0, i))],
        out_specs=pl.BlockSpec((None, tq, gw), lambda b, g, i: (b, i, g)),
        out_shape=jax.ShapeDtypeStruct((batch, seq, NSA_WIDTH), BF16),
        scratch_shapes=[pltpu.VMEM((rows, tq), F32),
                        pltpu.VMEM((seq, lanes), F32), pltpu.VMEM((seq, lanes), BF16),
                        pltpu.VMEM((win_len, lanes), F32), pltpu.VMEM((win_len, lanes), BF16),
                        pltpu.VMEM((3, HEAD_DIM, lanes), F32)],
        compiler_params=_params("parallel", "parallel", "arbitrary"),
        name="nsa",
    )(proj, kv_c, kv_ct, proj, vs_t, proj, vw_t, ng_t)


GLA_SUB = 8


def _gla_kernel(*refs, chunk):
    q_ref, k_ref = refs[0:2]
    v_refs = refs[2:2 + GLA_HEADS]
    gg_refs = refs[2 + GLA_HEADS:2 + 2 * GLA_HEADS]
    small_ref, wa_ref, ba_ref, ng_ref, o_ref, state_ref, attn_ref = refs[2 + 2 * GLA_HEADS:]
    hs = range(GLA_HEADS)
    keys = [slice(h * GLA_DK, (h + 1) * GLA_DK) for h in hs]

    @pl.when(pl.program_id(1) == 0)
    def _():
        state_ref[...] = jnp.zeros_like(state_ref)

    q = q_ref[...].astype(F32) * (GLA_DK ** -0.5)
    k = k_ref[...].astype(F32)
    z = _dot(small_ref[...], wa_ref[...]) + ba_ref[...]
    log_a = (jnp.minimum(z, 0.0) - jnp.log(1.0 + jnp.exp(-jnp.abs(z)))) * (1.0 / GLA_TAU)
    tri = (lax.broadcasted_iota(I32, (chunk, chunk), 1)
           <= lax.broadcasted_iota(I32, (chunk, chunk), 0)).astype(BF16)
    b = _dot_split3(tri, log_a)

    attn_ref[...] = jnp.zeros_like(attn_ref)
    t_idx = lax.broadcasted_iota(I32, (GLA_SUB, 1), 0)
    s_lane = lax.broadcasted_iota(I32, (GLA_SUB, GLA_SUB), 1)
    for i in range(chunk // GLA_SUB):
        r0 = i * GLA_SUB
        bi, qi, ki = b[r0:r0 + GLA_SUB], q[r0:r0 + GLA_SUB], k[r0:r0 + GLA_SUB]
        diag = [jnp.zeros((GLA_SUB, GLA_SUB), F32) for _ in hs]
        for s in range(GLA_SUB):
            decay = jnp.exp(jnp.where(t_idx >= s, bi - bi[s:s + 1], NEG))
            prod = qi * ki[s:s + 1] * decay
            for h in hs:
                col = jnp.sum(prod[:, keys[h]], axis=-1, keepdims=True)
                diag[h] = jnp.where(s_lane == s, col, diag[h])
        for h in hs:
            attn_ref[h, r0:r0 + GLA_SUB, r0:r0 + GLA_SUB] = diag[h]
        if i > 0:
            ref_b = b[r0:r0 + 1]
            q_dec = (qi * jnp.exp(bi - ref_b)).astype(BF16)
            k_dec = (k[:r0] * jnp.exp(ref_b - b[:r0])).astype(BF16)
            for h in hs:
                attn_ref[h, r0:r0 + GLA_SUB, 0:r0] = _dot_nt(q_dec[:, keys[h]], k_dec[:, keys[h]])

    q_in = (q * jnp.exp(b)).astype(BF16)
    b_last = b[chunk - 1:chunk]
    k_out = (k * jnp.exp(b_last - b)).astype(BF16)
    carry = jnp.exp(b_last)
    for h in hs:
        v = v_refs[h][...]
        state_t = state_ref[h]
        o = _dot(attn_ref[h].astype(BF16), v) + _dot_nt(q_in[:, keys[h]], state_t.astype(BF16))
        state_ref[h] = state_t * carry[:, keys[h]] + _dot(v.astype(F32).T.astype(BF16), k_out[:, keys[h]])
        gate = gg_refs[h][...].astype(F32)
        out = _rms(o, ng_ref[...]) * (gate * jax.nn.sigmoid(gate))
        o_ref[:, h * GLA_DV:(h + 1) * GLA_DV] = out.astype(o_ref.dtype)


def gla_attention(proj, gla_wa, gla_ba, gla_norm_g, batch, seq, chunk=128):
    chunk = min(chunk, seq)
    assert seq % chunk == 0 and chunk % GLA_SUB == 0
    wa = jnp.zeros((LANES, GLA_KEY_WIDTH), BF16).at[SMALL_GA_LANE:SMALL_GA_LANE + GLA_RANK].set(gla_wa.astype(BF16))

    def head_spec(c0, h):
        return pl.BlockSpec((None, chunk, GLA_DV), lambda b, c, h=h: (b, c, c0 // GLA_DV + h))

    const = lambda b, c: (0, 0)
    return pl.pallas_call(
        functools.partial(_gla_kernel, chunk=chunk),
        grid=(batch, seq // chunk),
        in_specs=[pl.BlockSpec((None, chunk, GLA_KEY_WIDTH), lambda b, c: (b, c, C_GQ // GLA_KEY_WIDTH)),
                  pl.BlockSpec((None, chunk, GLA_KEY_WIDTH), lambda b, c: (b, c, C_GK // GLA_KEY_WIDTH)),
                  *[head_spec(C_GV, h) for h in range(GLA_HEADS)],
                  *[head_spec(C_GG, h) for h in range(GLA_HEADS)],
                  pl.BlockSpec((None, chunk, LANES), lambda b, c: (b, c, C_SMALL // LANES)),
                  pl.BlockSpec((LANES, GLA_KEY_WIDTH), const),
                  pl.BlockSpec((1, GLA_KEY_WIDTH), const),
                  pl.BlockSpec((1, GLA_DV), const)],
        out_specs=pl.BlockSpec((None, chunk, GLA_WIDTH), lambda b, c: (b, c, 0)),
        out_shape=jax.ShapeDtypeStruct((batch, seq, GLA_WIDTH), BF16),
        scratch_shapes=[pltpu.VMEM((GLA_HEADS, GLA_DV, GLA_DK), F32), pltpu.VMEM((GLA_HEADS, chunk, chunk), F32)],
        compiler_params=_params("parallel", "arbitrary"),
        name="gla",
    )(proj, proj, *([proj] * (2 * GLA_HEADS)), proj, wa, gla_ba.reshape(1, GLA_KEY_WIDTH).astype(F32),
      gla_norm_g.reshape(1, GLA_DV).astype(F32))


def _merge_kernel(om_ref, on_ref, og_ref, wm_ref, wn_ref, wg_ref, gm_ref, gn_ref, gl_ref, o_ref):
    def gated(gate_ref, a_ref, w_ref):
        return jax.nn.sigmoid(gate_ref[...].astype(F32)) * _dot(a_ref[...], w_ref[...])

    o_ref[...] = (gated(gm_ref, om_ref, wm_ref) + gated(gn_ref, on_ref, wn_ref)
                  + gated(gl_ref, og_ref, wg_ref)).astype(o_ref.dtype)


def merge_branches(o_m, o_n, o_g, w_m, w_n, w_g, proj2d, tm=1024, tn=512):
    m = o_m.shape[0]
    tm = min(tm, m)
    assert C_MG % tn == 0 and D_MODEL % tn == 0

    def gate_spec(c):
        return pl.BlockSpec((tm, tn), lambda i, j, c=c: (i, (C_MG + c * D_MODEL) // tn + j))

    def act_spec(width):
        return pl.BlockSpec((tm, width), lambda i, j: (i, 0))

    def w_spec(width):
        return pl.BlockSpec((width, tn), lambda i, j: (0, j))

    return pl.pallas_call(
        _merge_kernel,
        grid=(m // tm, D_MODEL // tn),
        in_specs=[act_spec(MOBA_WIDTH), act_spec(NSA_WIDTH), act_spec(GLA_WIDTH),
                  w_spec(MOBA_WIDTH), w_spec(NSA_WIDTH), w_spec(GLA_WIDTH),
                  gate_spec(0), gate_spec(1), gate_spec(2)],
        out_specs=pl.BlockSpec((tm, tn), lambda i, j: (i, j)),
        out_shape=jax.ShapeDtypeStruct((m, D_MODEL), BF16),
        compiler_params=_params("parallel", "parallel"),
        name="merge",
    )(o_m, o_n, o_g, w_m, w_n, w_g, proj2d, proj2d, proj2d)


def _first_max(vals, lane):
    top = jnp.max(vals, axis=-1, keepdims=True)
    idx = jnp.min(jnp.where(vals == top, lane, float(ROUTER_LANES)), axis=-1, keepdims=True)
    return top, idx


def _router_kernel(x_ref, g_ref, w_ref, b_ref, h_ref, route_ref):
    h = _rms(x_ref[...], g_ref[...])
    h_ref[...] = h.astype(h_ref.dtype)
    h1 = h.astype(BF16)
    r1 = h - h1.astype(F32)
    h2 = r1.astype(BF16)
    h3 = (r1 - h2.astype(F32)).astype(BF16)
    hi, lo = slice(0, ROUTER_LANES), slice(ROUTER_LANES, 2 * ROUTER_LANES)
    a = _dot(h1, w_ref[...])
    b = _dot(h2, w_ref[...])
    logits = (a[:, hi] + (a[:, lo] + b[:, hi]) + (b[:, lo] + _dot(h3, w_ref[:, hi]))) + b_ref[...]

    lane = lax.broadcasted_iota(I32, logits.shape, 1).astype(F32)
    g_logits = jnp.where(lane < MOE_GROUPS, logits, -jnp.inf)
    g_top, grp = _first_max(g_logits, lane)
    p_grp = 1.0 / jnp.sum(jnp.exp(g_logits - g_top), axis=-1, keepdims=True)
    first = MOE_GROUPS + grp * MOE_EXPERTS_PER_GROUP
    e_logits = jnp.where((lane >= first) & (lane < first + MOE_EXPERTS_PER_GROUP), logits, -jnp.inf)
    top1, lane1 = _first_max(e_logits, lane)
    top2, lane2 = _first_max(jnp.where(lane == lane1, -jnp.inf, e_logits), lane)
    ratio = jnp.exp(top2 - top1)
    w_first = p_grp / (1.0 + ratio)
    route = jnp.where(lane == 0, lane1 - MOE_GROUPS, jnp.where(lane == 1, lane2 - MOE_GROUPS,
                      jnp.where(lane == 2, w_first, jnp.where(lane == 3, w_first * ratio, 0.0))))
    route_ref[...] = route


def router(x2d, norm_g, rg_w, rg_b, re_w, re_b, tm=512):
    m, d = x2d.shape
    tm = min(tm, m)
    n_real = MOE_GROUPS + MOE_EXPERTS
    w = jnp.zeros((d, ROUTER_LANES), F32).at[:, :n_real].set(jnp.concatenate([rg_w, re_w], axis=1))
    w_hi = w.astype(BF16)
    w_lo = (w - w_hi.astype(F32)).astype(BF16)
    bias = jnp.zeros((1, ROUTER_LANES), F32).at[0, :n_real].set(jnp.concatenate([rg_b, re_b]))
    return pl.pallas_call(
        _router_kernel,
        grid=(m // tm,),
        in_specs=[pl.BlockSpec((tm, d), lambda i: (i, 0)), pl.BlockSpec((1, d), lambda i: (0, 0)),
                  pl.BlockSpec((d, 2 * ROUTER_LANES), lambda i: (0, 0)),
                  pl.BlockSpec((1, ROUTER_LANES), lambda i: (0, 0))],
        out_specs=[pl.BlockSpec((tm, d), lambda i: (i, 0)), pl.BlockSpec((tm, ROUTER_LANES), lambda i: (i, 0))],
        out_shape=[jax.ShapeDtypeStruct((m, d), BF16), jax.ShapeDtypeStruct((m, ROUTER_LANES), F32)],
        compiler_params=_params("parallel"),
        name="router",
    )(x2d, norm_g.reshape(1, d).astype(F32), jnp.concatenate([w_hi, w_lo], axis=1), bias)


def _expert_kernel(blk_e_ref, n_used_ref, x_ref, wg_ref, wu_ref, wd_ref, o_ref, wg_bf, wu_bf, wd_bf):
    i = pl.program_id(0)
    new_expert = (i == 0) | (blk_e_ref[i] != blk_e_ref[jnp.maximum(i - 1, 0)])

    @pl.when(new_expert)
    def _():
        wg_bf[...] = wg_ref[...].astype(BF16)
        wu_bf[...] = wu_ref[...].astype(BF16)
        wd_bf[...] = wd_ref[...].astype(BF16)

    @pl.when(i < n_used_ref[0])
    def _():
        x = x_ref[...]
        gate = _dot(x, wg_bf[...])
        hid = gate * jax.nn.sigmoid(gate) * _dot(x, wu_bf[...])
        o_ref[...] = _dot(hid.astype(BF16), wd_bf[...]).astype(o_ref.dtype)

    @pl.when(i >= n_used_ref[0])
    def _():
        o_ref[...] = jnp.zeros_like(o_ref)


def expert_blocks(xs, blk_e, n_used, layer, w_gate, w_up, w_down):
    p, d = xs.shape
    ff = w_gate.shape[3]
    n_blk = p // MOE_ROWS
    grid_spec = pltpu.PrefetchScalarGridSpec(
        num_scalar_prefetch=2,
        grid=(n_blk,),
        in_specs=[pl.BlockSpec((MOE_ROWS, d), lambda i, e, n: (i, 0)),
                  pl.BlockSpec((None, None, d, ff), lambda i, e, n: (layer, e[i], 0, 0)),
                  pl.BlockSpec((None, None, d, ff), lambda i, e, n: (layer, e[i], 0, 0)),
                  pl.BlockSpec((None, None, ff, d), lambda i, e, n: (layer, e[i], 0, 0))],
        out_specs=pl.BlockSpec((MOE_ROWS, d), lambda i, e, n: (i, 0)),
        scratch_shapes=[pltpu.VMEM((d, ff), BF16), pltpu.VMEM((d, ff), BF16), pltpu.VMEM((ff, d), BF16)],
    )
    return pl.pallas_call(
        _expert_kernel,
        grid_spec=grid_spec,
        out_shape=jax.ShapeDtypeStruct((p, d), BF16),
        compiler_params=_params("arbitrary"),
        name="experts",
    )(blk_e, n_used, xs, w_gate, w_up, w_down)


def _combine_kernel(x_ref, y_ref, w_ref, g_ref, o_ref, *, final_norm):
    d = x_ref.shape[1]
    w = w_ref[...]
    x = x_ref[...]
    for k in range(MOE_TOPK):
        x = x + w[:, k:k + 1] * y_ref[:, k * d:(k + 1) * d].astype(F32)
    o_ref[...] = _rms(x, g_ref[...]) if final_norm else x


def combine(x2d, y, w, norm_g, final_norm, tm=512):
    m, d = x2d.shape
    tm = min(tm, m)
    row = pl.BlockSpec((tm, d), lambda i: (i, 0))
    return pl.pallas_call(
        functools.partial(_combine_kernel, final_norm=final_norm),
        grid=(m // tm,),
        in_specs=[row, pl.BlockSpec((tm, MOE_TOPK * d), lambda i: (i, 0)),
                  pl.BlockSpec((tm, MOE_TOPK), lambda i: (i, 0)), pl.BlockSpec((1, d), lambda i: (0, 0))],
        out_specs=row,
        out_shape=jax.ShapeDtypeStruct((m, d), F32),
        compiler_params=_params("parallel"),
        name="combine",
    )(x2d, y, w, norm_g.reshape(1, d).astype(F32))


def hier_moe(x2d, norm_g, rg_w, rg_b, re_w, re_b, layer, w_gate, w_up, w_down, out_norm_g, final_norm):
    t = x2d.shape[0]
    h, route = router(x2d, norm_g, rg_w, rg_b, re_w, re_b)
    expert = route[:, 0:MOE_TOPK].astype(I32)
    w = route[:, MOE_TOPK:2 * MOE_TOPK]

    a = t * MOE_TOPK
    e_flat = expert.reshape(a)
    onehot = (e_flat[:, None] == jnp.arange(MOE_EXPERTS, dtype=I32)[None, :]).astype(I32)
    running = jnp.cumsum(onehot, axis=0)
    counts = running[-1]
    rank = jnp.take_along_axis(running, e_flat[:, None], axis=1)[:, 0] - 1
    padded = (counts + MOE_ROWS - 1) // MOE_ROWS * MOE_ROWS
    pends = jnp.cumsum(padded)
    dest = (pends - padded)[e_flat] + rank
    p_rows = (a + MOE_EXPERTS * (MOE_ROWS - 1)) // MOE_ROWS * MOE_ROWS
    n_blk = p_rows // MOE_ROWS
    row_tok = (jnp.arange(p_rows, dtype=I32) % t).at[dest].set(jnp.arange(a, dtype=I32) // MOE_TOPK)
    blk_start = jnp.arange(n_blk, dtype=I32) * MOE_ROWS
    blk_e = jnp.minimum(jnp.sum((pends[None, :] <= blk_start[:, None]).astype(I32), axis=1), MOE_EXPERTS - 1)
    n_used = (pends[-1] // MOE_ROWS).astype(I32).reshape(1)

    xs = jnp.take(h, row_tok, axis=0)
    y_rows = expert_blocks(xs, blk_e, n_used, layer, w_gate, w_up, w_down)
    y = jnp.take(y_rows, dest, axis=0).reshape(t, MOE_TOPK * y_rows.shape[1])
    return combine(x2d, y, w, out_norm_g, final_norm)


PERM_TILE = 512
PERM_REGIONS = ((0, 0, C_GQ // PERM_TILE),
                (_SRC_GQ - C_GQ, C_GQ // PERM_TILE, C_GG // PERM_TILE),
                (_SRC_GG - C_GG, C_GG // PERM_TILE, C_SMALL // PERM_TILE))


def _permute_kernel(src_ref, small_ref, o_ref):
    @pl.when(pl.program_id(1) < C_SMALL // PERM_TILE)
    def _():
        o_ref[...] = src_ref[0].T.astype(o_ref.dtype)

    @pl.when(pl.program_id(1) == C_SMALL // PERM_TILE)
    def _():
        o_ref[:, 0:LANES] = small_ref[...].astype(o_ref.dtype)


def permute_w_in(w_in, layer, tk=D_MODEL):
    assert C_GQ % PERM_TILE == 0 and C_GG % PERM_TILE == 0 and C_SMALL % PERM_TILE == 0
    assert all(shift % SUBLANES == 0 for shift, _, _ in PERM_REGIONS)
    d = w_in.shape[1]
    w_t = jnp.swapaxes(w_in, 1, 2)
    small = jnp.concatenate([w_in[layer, :, _SRC_GA:_SRC_GG], w_in[layer, :, _SRC_NG:_SRC_GQ],
                             jnp.zeros((d, LANES - GLA_RANK - 3 * NSA_HEADS), w_in.dtype)], axis=1)

    def src_row(j):
        shift = sum(jnp.where((j >= j0) & (j < j1), s, 0) for s, j0, j1 in PERM_REGIONS)
        return pl.multiple_of(jnp.minimum(j * PERM_TILE + shift, _SRC_END - PERM_TILE), SUBLANES)

    return pl.pallas_call(
        _permute_kernel,
        grid=(d // tk, pl.cdiv(PROJ_WIDTH, PERM_TILE)),
        in_specs=[pl.BlockSpec((pl.Element(1), pl.Element(PERM_TILE), pl.Element(tk)),
                               lambda k, j: (layer, src_row(j), k * tk)),
                  pl.BlockSpec((tk, LANES), lambda k, j: (k, 0))],
        out_specs=pl.BlockSpec((tk, PERM_TILE), lambda k, j: (k, j)),
        out_shape=jax.ShapeDtypeStruct((d, PROJ_WIDTH), BF16),
        compiler_params=_params("parallel", "parallel"),
        name="permute_w_in",
    )(w_t, small)


def hybrid_layer(x, norm1_g, w_in, nsa_cmp_pe, nsa_cmp_w1, nsa_cmp_w2, gla_wa, gla_ba, gla_norm_g,
                 w_br_moba, w_br_nsa, w_br_gla, w_out, norm2_g, router_group_w, router_group_b,
                 router_expert_w, router_expert_b, layer, expert_w_gate, expert_w_up, expert_w_down,
                 out_norm_g, final_norm):
    batch, seq, d = x.shape
    t = batch * seq
    x2d = x.reshape(t, d)
    proj2d = norm_matmul(x2d, norm1_g, permute_w_in(w_in, layer), BF16)
    proj = proj2d.reshape(batch, seq, PROJ_WIDTH)
    o_m = moba_attention(proj, batch, seq)
    kv_c, kv_ct = nsa_compress(proj, nsa_cmp_pe, nsa_cmp_w1, nsa_cmp_w2, batch, seq)
    o_n = nsa_attention(proj, kv_c, kv_ct, batch, seq)
    o_g = gla_attention(proj, gla_wa, gla_ba, gla_norm_g, batch, seq)
    merged = merge_branches(o_m.reshape(t, -1), o_n.reshape(t, -1), o_g.reshape(t, -1),
                            w_br_moba.astype(BF16), w_br_nsa.astype(BF16), w_br_gla.astype(BF16), proj2d)
    x2d = matmul_residual(merged, w_out.astype(BF16), x2d)
    x2d = hier_moe(x2d, norm2_g, router_group_w, router_group_b, router_expert_w, router_expert_b,
                   layer, expert_w_gate, expert_w_up, expert_w_down, out_norm_g, final_norm)
    return x2d.reshape(batch, seq, d)


def kernel(x, norm1_g, w_in, nsa_cmp_pe, nsa_cmp_w1, nsa_cmp_w2, gla_wa, gla_ba, gla_norm_g, w_br_moba,
           w_br_nsa, w_br_gla, w_out, norm2_g, router_group_w, router_group_b, router_expert_w,
           router_expert_b, expert_w_gate, expert_w_up, expert_w_down, final_norm_g):
    for l in range(DEPTH):
        x = hybrid_layer(x, norm1_g[l], w_in, nsa_cmp_pe[l], nsa_cmp_w1[l], nsa_cmp_w2[l], gla_wa[l],
                         gla_ba[l], gla_norm_g[l], w_br_moba[l], w_br_nsa[l], w_br_gla[l], w_out[l],
                         norm2_g[l], router_group_w[l], router_group_b[l], router_expert_w[l],
                         router_expert_b[l], l, expert_w_gate, expert_w_up, expert_w_down,
                         final_norm_g, l == DEPTH - 1)
    return x
```

```python
import functools

import jax
import jax.numpy as jnp
import numpy as np
from jax import lax
from jax.experimental import pallas as pl
from jax.experimental.pallas import tpu as pltpu

F32 = jnp.float32
BF16 = jnp.bfloat16
I32 = jnp.int32

D_MODEL = 2048
DEPTH = 2
HEAD_DIM = 128
NEG = -1e30
FORCE = 1e9
EPS = 1e-6
LOG2E = 1.4426950408889634

MOBA_HEADS = 8
MOBA_BLOCK = 256
MOBA_TOPK = 3
MOBA_WIDTH = MOBA_HEADS * HEAD_DIM

NSA_HEADS = 8
NSA_KV_HEADS = 2
NSA_GROUP = NSA_HEADS // NSA_KV_HEADS
NSA_CMP_LEN = 32
NSA_CMP_STRIDE = 16
NSA_SLC_BLOCK = 64
NSA_TOPN = 8
NSA_WINDOW = 512
NSA_WIDTH = NSA_HEADS * HEAD_DIM
NSA_KV_WIDTH = NSA_KV_HEADS * HEAD_DIM

GLA_HEADS = 4
GLA_DK = 128
GLA_DV = 256
GLA_RANK = 16
GLA_TAU = 16.0
GLA_KEY_WIDTH = GLA_HEADS * GLA_DK
GLA_WIDTH = GLA_HEADS * GLA_DV

N_BRANCH = 3
MOE_GROUPS = 4
MOE_EXPERTS_PER_GROUP = 8
MOE_EXPERTS = MOE_GROUPS * MOE_EXPERTS_PER_GROUP
MOE_TOPK = 2
MOE_FF = D_MODEL // 4

LANES = 128
SUBLANES = 8
BF16_SUBLANES = 16
VMEM_LIMIT_BYTES = 56 * 1024 * 1024

_SRC_NG = MOBA_WIDTH * 3 + NSA_WIDTH + 6 * NSA_KV_WIDTH
_SRC_GQ = _SRC_NG + 3 * NSA_HEADS
_SRC_GA = _SRC_GQ + 2 * GLA_KEY_WIDTH + GLA_WIDTH
_SRC_GG = _SRC_GA + GLA_RANK
_SRC_END = _SRC_GG + GLA_WIDTH + N_BRANCH * D_MODEL

C_MQ = 0
C_MK = C_MQ + MOBA_WIDTH
C_MV = C_MK + MOBA_WIDTH
C_NQ = C_MV + MOBA_WIDTH
C_NKV = C_NQ + NSA_WIDTH
C_GQ = C_NKV + 6 * NSA_KV_WIDTH
C_GK = C_GQ + GLA_KEY_WIDTH
C_GV = C_GK + GLA_KEY_WIDTH
C_GG = C_GV + GLA_WIDTH
C_MG = C_GG + GLA_WIDTH
C_SMALL = C_MG + N_BRANCH * D_MODEL
PROJ_WIDTH = C_SMALL + LANES
SMALL_GA_LANE = 0
SMALL_NG_LANE = GLA_RANK

ROUTER_LANES = LANES
MOE_ROWS = 256


def _params(*semantics):
    return pltpu.CompilerParams(dimension_semantics=semantics, vmem_limit_bytes=VMEM_LIMIT_BYTES)


def _dot(a, b):
    return jnp.dot(a, b, preferred_element_type=F32)


def _dot_nt(a, b):
    return lax.dot_general(a, b, (((1,), (1,)), ((), ())), preferred_element_type=F32)


def _dot_split3(a01, x):
    x1 = x.astype(BF16)
    r1 = x - x1.astype(F32)
    x2 = r1.astype(BF16)
    x3 = (r1 - x2.astype(F32)).astype(BF16)
    return _dot(a01, x1) + _dot(a01, x2) + _dot(a01, x3)


def _rms(x, g):
    return x * lax.rsqrt(jnp.mean(x * x, axis=-1, keepdims=True) + EPS) * g


def _norm_matmul_kernel(x_ref, g_ref, b_ref, o_ref, h_ref):
    @pl.when(pl.program_id(1) == 0)
    def _():
        h_ref[...] = _rms(x_ref[...], g_ref[...]).astype(h_ref.dtype)

    o_ref[...] = _dot(h_ref[...], b_ref[...]).astype(o_ref.dtype)


def norm_matmul(x, g, b, out_dtype, tm=1024, tn=1152):
    m, k = x.shape
    n = b.shape[1]
    tm, tn = min(tm, m), min(tn, n)
    return pl.pallas_call(
        _norm_matmul_kernel,
        grid=(m // tm, n // tn),
        in_specs=[pl.BlockSpec((tm, k), lambda i, j: (i, 0)), pl.BlockSpec((1, k), lambda i, j: (0, 0)),
                  pl.BlockSpec((k, tn), lambda i, j: (0, j))],
        out_specs=pl.BlockSpec((tm, tn), lambda i, j: (i, j)),
        out_shape=jax.ShapeDtypeStruct((m, n), out_dtype),
        scratch_shapes=[pltpu.VMEM((tm, k), BF16)],
        compiler_params=_params("parallel", "arbitrary"),
        name="norm_matmul",
    )(x, g.reshape(1, k).astype(F32), b)


def _matmul_residual_kernel(a_ref, b_ref, r_ref, o_ref):
    o_ref[...] = r_ref[...] + _dot(a_ref[...], b_ref[...])


def matmul_residual(a, b, res, tm=1024, tn=1024):
    m, k = a.shape
    n = b.shape[1]
    tm, tn = min(tm, m), min(tn, n)
    return pl.pallas_call(
        _matmul_residual_kernel,
        grid=(m // tm, n // tn),
        in_specs=[pl.BlockSpec((tm, k), lambda i, j: (i, 0)), pl.BlockSpec((k, tn), lambda i, j: (0, j)),
                  pl.BlockSpec((tm, tn), lambda i, j: (i, j))],
        out_specs=pl.BlockSpec((tm, tn), lambda i, j: (i, j)),
        out_shape=jax.ShapeDtypeStruct((m, n), F32),
        compiler_params=_params("parallel", "parallel"),
        name="matmul_residual",
    )(a, b, res)


def _softmax_step(carry, q, k, v, mask, scale):
    m_i, l_i, acc = carry
    s = jnp.where(mask, _dot_nt(q, k) * scale, NEG)
    m_new = jnp.maximum(m_i, jnp.max(s, axis=-1, keepdims=True))
    alpha = jnp.exp(m_i - m_new)
    p = jnp.where(mask, jnp.exp(s - m_new), 0.0)
    l_new = alpha * l_i + jnp.sum(p, axis=-1, keepdims=True)
    acc_new = alpha * acc + _dot(p.astype(v.dtype), v)
    return m_new, l_new, acc_new


def _softmax_init(m, e):
    return jnp.full((m, 1), NEG, F32), jnp.zeros((m, 1), F32), jnp.zeros((m, e), F32)


def _rank_desc(vals, n_candidates):
    lane = lax.broadcasted_iota(I32, vals.shape, 1)
    rank = jnp.zeros(vals.shape, F32)
    for m in range(n_candidates):
        vm = vals[:, m:m + 1]
        beats = (vm > vals) | ((vm == vals) & (lane > m))
        rank = rank + beats.astype(F32)
    return rank


def _softmax_step_t(carry, q, k, v_t, bias, scale):
    m_i, l_i, acc = carry
    s = _dot_nt(k, q) * scale + bias
    m_new = jnp.maximum(m_i, jnp.max(s, axis=0, keepdims=True))
    alpha = jnp.exp(m_i - m_new)
    p = jnp.exp(s - m_new)
    l_new = alpha * l_i + jnp.sum(p, axis=0, keepdims=True)
    acc_new = alpha * acc + _dot(v_t, p.astype(v_t.dtype))
    return m_new, l_new, acc_new


def _softmax_init_t(m, e):
    return jnp.full((1, m), NEG, F32), jnp.zeros((1, m), F32), jnp.zeros((e, m), F32)


def _rank_desc_rows(vals, n_candidates):
    row = lax.broadcasted_iota(I32, vals.shape, 0)
    rank = jnp.zeros(vals.shape, F32)
    for m in range(n_candidates):
        vm = vals[m:m + 1, :]
        beats = (vm > vals) | ((vm == vals) & (row > m))
        rank = rank + beats.astype(F32)
    return rank


def _lane_column(x, idx):
    lane = lax.broadcasted_iota(I32, x.shape, 1)
    return jnp.sum(jnp.where(lane == idx, x, 0.0), axis=-1, keepdims=True)


def _moba_kernel(q_ref, k_ref, vt_ref, o_ref, kmean_ref, bias_ref, s_ref, p_ref, acc_ref, bmax_ref, m_ref, *,
                 n_blocks, topk, heads):
    blk = MOBA_BLOCK
    cur = pl.program_id(2)
    scale = HEAD_DIM ** -0.5
    cols = [slice(h * HEAD_DIM, (h + 1) * HEAD_DIM) for h in range(heads)]

    @pl.when(cur == 0)
    def _():
        kmean_ref[...] = jnp.zeros_like(kmean_ref)
        for h in range(heads):
            for n in range(n_blocks):
                k_blk = k_ref[n * blk:(n + 1) * blk, cols[h]].astype(F32)
                kmean_ref[h, n:n + 1, :] = jnp.sum(k_blk, axis=0, keepdims=True) * (1.0 / blk)

    for h in range(heads):
        gate = _dot_nt(kmean_ref[h].astype(BF16), q_ref[:, cols[h]])
        row = lax.broadcasted_iota(I32, gate.shape, 0)
        valid = row < cur
        gate = jnp.where(valid, gate, NEG)
        chosen = valid & (_rank_desc_rows(gate, n_blocks) < topk)
        bias_ref[h] = jnp.where(chosen, 0.0, NEG)

    seq = k_ref.shape[0]
    lo_blocks = (n_blocks + 1) // 2
    lo = lo_blocks * blk
    need_hi = cur >= lo_blocks

    def score_blocks(b0, b1):
        for h in range(heads):
            s = _dot_nt(k_ref[b0 * blk:b1 * blk, cols[h]], q_ref[:, cols[h]])
            s_ref[h, b0 * blk:b1 * blk, :] = s
            for n in range(b0, b1):
                s_blk = s[(n - b0) * blk:(n - b0 + 1) * blk]
                bmax_ref[h, n] = jnp.max(s_blk.reshape(blk // SUBLANES, SUBLANES, blk), axis=0)

    def past_max(h, b0, b1):
        return functools.reduce(jnp.maximum, [bmax_ref[h, n] * scale + bias_ref[h, n:n + 1, :]
                                              for n in range(b0, b1)])

    score_blocks(0, lo_blocks)
    if lo < seq:
        pl.when(need_hi)(lambda: score_blocks(lo_blocks, n_blocks))

    sub = LANES
    base = pl.multiple_of(cur * blk, blk)
    qry_i = lax.broadcasted_iota(I32, (sub, blk), 1)
    key_i = [j * sub + lax.broadcasted_iota(I32, (sub, blk), 0) for j in range(blk // sub)]
    exp_scale = scale * LOG2E

    def fold_max(x):
        return jnp.max(x.reshape(sub // SUBLANES, SUBLANES, blk), axis=0)

    def tiles(h, start):
        return [s_ref[h, pl.ds(pl.multiple_of(start + j * sub, sub), sub), :] for j in range(blk // sub)]

    for h in range(heads):
        own = [fold_max(jnp.where(key_i[j] <= qry_i, t, NEG)) for j, t in enumerate(tiles(h, base))]
        m_ref[h] = jnp.maximum(functools.reduce(jnp.maximum, own) * scale, past_max(h, 0, lo_blocks))

    if lo < seq:
        @pl.when(need_hi)
        def _():
            for h in range(heads):
                m_ref[h] = jnp.maximum(m_ref[h], past_max(h, lo_blocks, n_blocks))

    m_log2 = [jnp.max(m_ref[h], axis=0, keepdims=True) * LOG2E for h in range(heads)]

    def fold_sum(x):
        return jnp.sum(x.reshape(sub // SUBLANES, SUBLANES, blk), axis=0)

    def store_p(h, start, j, p):
        p_ref[h, pl.ds(pl.multiple_of(start + j * sub, sub), sub), :] = p.astype(p_ref.dtype)

    def prob_body(n, l8):
        out = []
        for h in range(heads):
            shift = bias_ref[h, pl.ds(n, 1), :] * LOG2E - m_log2[h]
            acc = l8[h]
            for j, t in enumerate(tiles(h, n * blk)):
                p = jnp.exp2(t * exp_scale + shift)
                store_p(h, n * blk, j, p)
                acc = acc + fold_sum(p)
            out.append(acc)
        return tuple(out)

    l8 = []
    for h in range(heads):
        acc = jnp.zeros((SUBLANES, blk), F32)
        for j, t in enumerate(tiles(h, base)):
            p = jnp.where(key_i[j] <= qry_i, jnp.exp2(t * exp_scale - m_log2[h]), 0.0)
            store_p(h, base, j, p)
            acc = acc + fold_sum(p)
        l8.append(acc)
    l8 = lax.fori_loop(0, cur, prob_body, tuple(l8))

    def zero_body(n, carry):
        for h in range(heads):
            for j in range(blk // sub):
                store_p(h, n * blk, j, jnp.zeros((sub, blk), F32))
        return carry

    lax.fori_loop(cur + 1, jnp.where(need_hi, n_blocks, lo_blocks), zero_body, 0)
    for h in range(heads):
        acc_ref[h] = _dot(vt_ref[h, :, 0:lo], p_ref[h, 0:lo, :])

    if lo < seq:
        @pl.when(need_hi)
        def _():
            for h in range(heads):
                acc_ref[h] += _dot(vt_ref[h, :, lo:seq], p_ref[h, lo:seq, :])

    for h in range(heads):
        l_sum = jnp.sum(l8[h], axis=0, keepdims=True)
        o_ref[:, cols[h]] = (acc_ref[h] / l_sum).T.astype(o_ref.dtype)


def moba_attention(proj, batch, seq, heads=4):
    assert seq % MOBA_BLOCK == 0 and MOBA_HEADS % heads == 0
    n_blocks = seq // MOBA_BLOCK
    rows = -(-n_blocks // BF16_SUBLANES) * BF16_SUBLANES
    width = heads * HEAD_DIM
    qb, kb = C_MQ // width, C_MK // width
    v_t = proj[:, :, C_MV:C_MV + MOBA_WIDTH].reshape(batch, seq, MOBA_HEADS, HEAD_DIM).transpose(0, 2, 3, 1)
    kern = functools.partial(_moba_kernel, n_blocks=n_blocks, topk=min(MOBA_TOPK, n_blocks), heads=heads)
    return pl.pallas_call(
        kern,
        grid=(batch, MOBA_HEADS // heads, n_blocks),
        in_specs=[pl.BlockSpec((None, MOBA_BLOCK, width), lambda b, h, i: (b, i, qb + h)),
                  pl.BlockSpec((None, seq, width), lambda b, h, i: (b, 0, kb + h)),
                  pl.BlockSpec((None, heads, HEAD_DIM, seq), lambda b, h, i: (b, h, 0, 0))],
        out_specs=pl.BlockSpec((None, MOBA_BLOCK, width), lambda b, h, i: (b, i, h)),
        out_shape=jax.ShapeDtypeStruct((batch, seq, MOBA_WIDTH), BF16),
        scratch_shapes=[pltpu.VMEM((heads, rows, HEAD_DIM), F32), pltpu.VMEM((heads, rows, MOBA_BLOCK), F32),
                        pltpu.VMEM((heads, seq, MOBA_BLOCK), F32), pltpu.VMEM((heads, seq, MOBA_BLOCK), BF16),
                        pltpu.VMEM((heads, HEAD_DIM, MOBA_BLOCK), F32),
                        pltpu.VMEM((heads, n_blocks, SUBLANES, MOBA_BLOCK), F32),
                        pltpu.VMEM((heads, SUBLANES, MOBA_BLOCK), F32)],
        compiler_params=_params("parallel", "parallel", "arbitrary"),
        name="moba",
    )(proj, proj, v_t)


def _nsa_compress_kernel(x_ref, pe_ref, w1_ref, w2_ref, o_ref, ot_ref):
    x = x_ref[...].astype(F32)
    half = x.shape[1]
    lo = _dot((x + pe_ref[:, :half]).astype(BF16), w1_ref[:half, :])
    hi = _dot((x + pe_ref[:, half:]).astype(BF16), w1_ref[half:, :])
    pre = lo + pltpu.roll(hi, hi.shape[0] - 1, 0)
    hid = pre * jax.nn.sigmoid(pre)
    out = _dot(hid.astype(BF16), w2_ref[...])
    o_ref[...] = out.astype(o_ref.dtype)
    ot_ref[...] = out.T.astype(ot_ref.dtype)


def nsa_compress(proj, cmp_pe, cmp_w1, cmp_w2, batch, seq):
    n16 = seq // NSA_CMP_STRIDE
    width = NSA_CMP_STRIDE * HEAD_DIM
    x = proj[:, :, C_NKV:C_NKV + 2 * NSA_KV_WIDTH].reshape(batch, seq, 2, NSA_KV_HEADS, HEAD_DIM)
    x = x.transpose(0, 2, 3, 1, 4).reshape(batch, 2, NSA_KV_HEADS, n16, width)
    pe = cmp_pe.reshape(2, 1, NSA_CMP_LEN * HEAD_DIM).astype(F32)
    return pl.pallas_call(
        _nsa_compress_kernel,
        grid=(batch, 2, NSA_KV_HEADS),
        in_specs=[pl.BlockSpec((None, None, None, n16, width), lambda b, c, g: (b, c, g, 0, 0)),
                  pl.BlockSpec((None, 1, 2 * width), lambda b, c, g: (c, 0, 0)),
                  pl.BlockSpec((None, 2 * width, HEAD_DIM), lambda b, c, g: (c, 0, 0)),
                  pl.BlockSpec((None, HEAD_DIM, HEAD_DIM), lambda b, c, g: (c, 0, 0))],
        out_specs=[pl.BlockSpec((None, None, None, n16, HEAD_DIM), lambda b, c, g: (b, c, g, 0, 0)),
                   pl.BlockSpec((None, None, None, HEAD_DIM, n16), lambda b, c, g: (b, c, g, 0, 0))],
        out_shape=[jax.ShapeDtypeStruct((batch, 2, NSA_KV_HEADS, n16, HEAD_DIM), BF16),
                   jax.ShapeDtypeStruct((batch, 2, NSA_KV_HEADS, HEAD_DIM, n16), BF16)],
        compiler_params=_params("parallel", "parallel", "parallel"),
        name="nsa_compress",
    )(x, pe, cmp_w1.astype(BF16), cmp_w2.astype(BF16))


def _nsa_kernel(q_ref, kc_ref, vct_ref, ks_ref, vst_ref, kw_ref, vwt_ref, ngt_ref, o_ref,
                bias_ref, s_ref, p_ref, sw_ref, pw_ref, out_ref, bmax_ref, *, tq, span, n_cmp, n_slc, topn,
                win_len):
    g = pl.program_id(1)
    qi = pl.program_id(2)
    seq = ks_ref.shape[0]
    n16 = kc_ref.shape[0]
    lanes = NSA_GROUP * tq
    blk = NSA_SLC_BLOCK
    scale = HEAD_DIM ** -0.5
    exp_scale = scale * LOG2E
    start = qi * tq
    heads = [slice(r * tq, (r + 1) * tq) for r in range(NSA_GROUP)]
    q = [q_ref[:, r * HEAD_DIM:(r + 1) * HEAD_DIM] for r in range(NSA_GROUP)]

    def per_group(row):
        return jnp.concatenate([row] * NSA_GROUP, axis=1)

    def scores(keys):
        return jnp.concatenate([_dot_nt(keys, q[r]) for r in range(NSA_GROUP)], axis=1)

    def fold(op, x):
        return op(x.reshape(x.shape[0] // SUBLANES, SUBLANES, lanes), axis=0)

    pos1 = start + lax.broadcasted_iota(I32, (1, tq), 1)
    pos = per_group(pos1)

    n_idx = lax.broadcasted_iota(I32, (n16, lanes), 0)
    in_range = n_idx < n_cmp
    cmask = (n_idx * NSA_CMP_STRIDE + (NSA_CMP_LEN - 1) <= pos) & in_range
    s_c = jnp.where(cmask, scores(kc_ref[...]) * scale, NEG)
    e_c = jnp.where(in_range, jnp.exp(s_c - jnp.max(s_c, axis=0, keepdims=True)), 0.0)
    p_c = jnp.where(cmask, e_c / jnp.sum(e_c, axis=0, keepdims=True), 0.0)
    out_ref[0] = _dot(vct_ref[...], p_c.astype(BF16))

    p_sum = functools.reduce(lambda a, b: a + b, [p_c[:, h] for h in heads])
    rows = bias_ref.shape[0]
    oj = lax.broadcasted_iota(I32, (rows, n16), 0)
    on = lax.broadcasted_iota(I32, (rows, n16), 1)
    overlap_t = ((on * NSA_CMP_STRIDE < (oj + 1) * blk) & (on * NSA_CMP_STRIDE + (NSA_CMP_LEN - 1) >= oj * blk)
                 & (on < n_cmp) & (oj < n_slc)).astype(BF16)
    p_hi = p_sum.astype(BF16)
    p_lo = (p_sum - p_hi.astype(F32)).astype(BF16)
    imp = _dot(overlap_t, p_hi) + _dot(overlap_t, p_lo)
    j_idx = lax.broadcasted_iota(I32, (rows, tq), 0)
    cur_blk = pos1 // blk
    forced = (j_idx == 0) | (j_idx == cur_blk) | (j_idx == cur_blk - 1)
    imp = jnp.where(forced, FORCE, imp)
    imp = jnp.where(j_idx > cur_blk, NEG, imp)
    chosen = (_rank_desc_rows(imp, n_slc) < topn) & (j_idx <= cur_blk)
    bias_ref[...] = jnp.where(chosen, 0.0, NEG)

    n_spans = seq // span
    span_blocks = span // blk

    def span_scores(k):
        s = scores(ks_ref[k * span:(k + 1) * span, :])
        s_ref[k * span:(k + 1) * span, :] = s
        for j in range(span_blocks):
            bmax_ref[k * span_blocks + j] = fold(jnp.max, s[j * blk:(j + 1) * blk])

    span_scores(0)
    w0 = pl.multiple_of(jnp.maximum(start + tq - win_len, 0), tq)
    sw_ref[...] = scores(kw_ref[pl.ds(w0, win_len), :])
    for k in range(1, n_spans):
        pl.when(start >= k * span)(functools.partial(span_scores, k))

    per_tile = tq // blk
    first_own = qi * per_tile

    def block_bias(j):
        return per_group(bias_ref[pl.ds(j, 1), :])

    def block_rows(j):
        return pl.ds(pl.multiple_of(j * blk, blk), blk)

    def causal_mask(j):
        key = j * blk + lax.broadcasted_iota(I32, (blk, lanes), 0)
        return key <= pos

    def max_body(i, m8):
        for d in range(per_tile):
            j = i * per_tile + d
            m8 = jnp.maximum(m8, bmax_ref[j] * scale + block_bias(j))
        return m8

    m8 = lax.fori_loop(0, qi, max_body, jnp.full((SUBLANES, lanes), NEG, F32))
    for d in range(per_tile):
        j = first_own + d
        own = jnp.where(causal_mask(j), s_ref[block_rows(j), :], NEG)
        m8 = jnp.maximum(m8, fold(jnp.max, own) * scale + block_bias(j))
    m_log2 = jnp.max(m8, axis=0, keepdims=True) * LOG2E

    def prob_body(i, l8):
        for d in range(per_tile):
            j = i * per_tile + d
            p = jnp.exp2(s_ref[block_rows(j), :] * exp_scale + (block_bias(j) * LOG2E - m_log2))
            p_ref[block_rows(j), :] = p.astype(p_ref.dtype)
            l8 = l8 + fold(jnp.sum, p)
        return l8

    l8 = lax.fori_loop(0, qi, prob_body, jnp.zeros((SUBLANES, lanes), F32))
    for d in range(per_tile):
        j = first_own + d
        p = jnp.exp2(s_ref[block_rows(j), :] * exp_scale + (block_bias(j) * LOG2E - m_log2))
        p = jnp.where(causal_mask(j), p, 0.0)
        p_ref[block_rows(j), :] = p.astype(p_ref.dtype)
        l8 = l8 + fold(jnp.sum, p)

    def zero_body(i, carry):
        p_ref[pl.ds(pl.multiple_of(i * tq, tq), tq), :] = jnp.zeros((tq, lanes), p_ref.dtype)
        return carry

    visible_tiles = (start // span + 1) * (span // tq)
    lax.fori_loop(qi + 1, visible_tiles, zero_body, 0)

    out_ref[1] = _dot(vst_ref[:, 0:span], p_ref[0:span, :])

    n_tiles = win_len // blk

    def win_tile(t):
        key = w0 + t * blk + lax.broadcasted_iota(I32, (blk, lanes), 0)
        return sw_ref[t * blk:(t + 1) * blk, :], (key <= pos) & (key > pos - NSA_WINDOW)

    wm8 = jnp.full((SUBLANES, lanes), NEG, F32)
    for t in range(n_tiles):
        tile, mask = win_tile(t)
        wm8 = jnp.maximum(wm8, fold(jnp.max, jnp.where(mask, tile, NEG)))
    wm_log2 = jnp.max(wm8, axis=0, keepdims=True) * exp_scale
    wl8 = jnp.zeros((SUBLANES, lanes), F32)
    for t in range(n_tiles):
        tile, mask = win_tile(t)
        p = jnp.where(mask, jnp.exp2(tile * exp_scale - wm_log2), 0.0)
        pw_ref[t * blk:(t + 1) * blk, :] = p.astype(pw_ref.dtype)
        wl8 = wl8 + fold(jnp.sum, p)
    w_tile0 = w0 // tq
    pv = _dot(vwt_ref[w_tile0], pw_ref[0:tq, :])
    for c in range(1, win_len // tq):
        pv = pv + _dot(vwt_ref[w_tile0 + c], pw_ref[c * tq:(c + 1) * tq, :])
    out_ref[2] = pv / jnp.sum(wl8, axis=0, keepdims=True)

    for k in range(1, n_spans):
        def span_pv(k=k):
            out_ref[1] += _dot(vst_ref[:, k * span:(k + 1) * span], p_ref[k * span:(k + 1) * span, :])
        pl.when(start >= k * span)(span_pv)
    out_ref[1] = out_ref[1] / jnp.sum(l8, axis=0, keepdims=True)

    for r in range(NSA_GROUP):
        gate_row = (g * NSA_GROUP + r) * 3
        mix = jnp.zeros((HEAD_DIM, tq), F32)
        for c in range(3):
            mix = mix + jax.nn.sigmoid(ngt_ref[pl.ds(gate_row + c, 1), :]) * out_ref[c, :, heads[r]]
        o_ref[:, r * HEAD_DIM:(r + 1) * HEAD_DIM] = mix.T.astype(o_ref.dtype)


def nsa_attention(proj, kv_c, kv_ct, batch, seq, tq=128):
    assert tq == LANES and seq % tq == 0 and tq % NSA_SLC_BLOCK == 0
    n_cmp = (seq - NSA_CMP_LEN) // NSA_CMP_STRIDE + 1
    n_slc = seq // NSA_SLC_BLOCK
    rows = -(-n_slc // BF16_SUBLANES) * BF16_SUBLANES
    span = max(tq, seq // 4)
    win_len = min(NSA_WINDOW + tq, seq)
    n16 = seq // NSA_CMP_STRIDE
    gw = NSA_GROUP * HEAD_DIM
    lanes = NSA_GROUP * tq
    nkv = C_NKV // HEAD_DIM

    def kv_cols(slot):
        c0 = C_NKV + slot * NSA_KV_WIDTH
        return proj[:, :, c0:c0 + NSA_KV_WIDTH].reshape(batch, seq, NSA_KV_HEADS, HEAD_DIM)

    vs_t = kv_cols(3).transpose(0, 2, 3, 1)
    vw_t = kv_cols(5).reshape(batch, seq // tq, tq, NSA_KV_HEADS, HEAD_DIM).transpose(0, 3, 1, 4, 2)
    n_gates = 3 * NSA_HEADS
    ng_t = proj[:, :, C_SMALL + SMALL_NG_LANE:C_SMALL + SMALL_NG_LANE + n_gates].astype(F32).transpose(0, 2, 1)

    def k_spec(slot):
        return pl.BlockSpec((None, seq, HEAD_DIM),
                            lambda b, g, i, slot=slot: (b, 0, nkv + slot * NSA_KV_HEADS + g))

    kern = functools.partial(_nsa_kernel, tq=tq, span=span, n_cmp=n_cmp, n_slc=n_slc,
                             topn=min(NSA_TOPN, n_slc), win_len=win_len)
    return pl.pallas_call(
        kern,
        grid=(batch, NSA_KV_HEADS, seq // tq),
        in_specs=[pl.BlockSpec((None, tq, gw), lambda b, g, i: (b, i, C_NQ // gw + g)),
                  pl.BlockSpec((None, None, None, n16, HEAD_DIM), lambda b, g, i: (b, 0, g, 0, 0)),
                  pl.BlockSpec((None, None, None, HEAD_DIM, n16), lambda b, g, i: (b, 1, g, 0, 0)),
                  k_spec(2),
                  pl.BlockSpec((None, None, HEAD_DIM, seq), lambda b, g, i: (b, g, 0, 0)),
                  k_spec(4),
                  pl.BlockSpec((None, None, seq // tq, HEAD_DIM, tq), lambda b, g, i: (b, g, 0, 0, 0)),
                  pl.BlockSpec((None, n_gates, tq), lambda b, g, i: (b, 0, i))],
        out_specs=pl.BlockSpec((None, tq, gw), lambda b, g, i: (b, i, g)),
        out_shape=jax.ShapeDtypeStruct((batch, seq, NSA_WIDTH), BF16),
        scratch_shapes=[pltpu.VMEM((rows, tq), F32),
                        pltpu.VMEM((seq, lanes), F32), pltpu.VMEM((seq, lanes), BF16),
                        pltpu.VMEM((win_len, lanes), F32), pltpu.VMEM((win_len, lanes), BF16),
                        pltpu.VMEM((3, HEAD_DIM, lanes), F32), pltpu.VMEM((n_slc, SUBLANES, lanes), F32)],
        compiler_params=_params("parallel", "parallel", "arbitrary"),
        name="nsa",
    )(proj, kv_c, kv_ct, proj, vs_t, proj, vw_t, ng_t)


GLA_SUB = 8


def _gla_kernel(*refs, chunk):
    q_ref, k_ref = refs[0:2]
    v_refs = refs[2:2 + GLA_HEADS]
    gg_refs = refs[2 + GLA_HEADS:2 + 2 * GLA_HEADS]
    small_ref, wa_ref, ba_ref, ng_ref, o_ref, state_ref, attn_ref = refs[2 + 2 * GLA_HEADS:]
    hs = range(GLA_HEADS)
    keys = [slice(h * GLA_DK, (h + 1) * GLA_DK) for h in hs]

    @pl.when(pl.program_id(1) == 0)
    def _():
        state_ref[...] = jnp.zeros_like(state_ref)

    q = q_ref[...].astype(F32) * (GLA_DK ** -0.5)
    k = k_ref[...].astype(F32)
    z = _dot(small_ref[...], wa_ref[...]) + ba_ref[...]
    log_a = (jnp.minimum(z, 0.0) - jnp.log(1.0 + jnp.exp(-jnp.abs(z)))) * (1.0 / GLA_TAU)
    tri = (lax.broadcasted_iota(I32, (chunk, chunk), 1)
           <= lax.broadcasted_iota(I32, (chunk, chunk), 0)).astype(BF16)
    b = _dot_split3(tri, log_a)

    attn_ref[...] = jnp.zeros_like(attn_ref)
    t_idx = lax.broadcasted_iota(I32, (GLA_SUB, 1), 0)
    s_lane = lax.broadcasted_iota(I32, (GLA_SUB, GLA_SUB), 1)
    for i in range(chunk // GLA_SUB):
        r0 = i * GLA_SUB
        bi, qi, ki = b[r0:r0 + GLA_SUB], q[r0:r0 + GLA_SUB], k[r0:r0 + GLA_SUB]
        diag = [jnp.zeros((GLA_SUB, GLA_SUB), F32) for _ in hs]
        for s in range(GLA_SUB):
            decay = jnp.exp(jnp.where(t_idx >= s, bi - bi[s:s + 1], NEG))
            prod = qi * ki[s:s + 1] * decay
            for h in hs:
                col = jnp.sum(prod[:, keys[h]], axis=-1, keepdims=True)
                diag[h] = jnp.where(s_lane == s, col, diag[h])
        for h in hs:
            attn_ref[h, r0:r0 + GLA_SUB, r0:r0 + GLA_SUB] = diag[h]
        if i > 0:
            ref_b = b[r0:r0 + 1]
            q_dec = (qi * jnp.exp(bi - ref_b)).astype(BF16)
            k_dec = (k[:r0] * jnp.exp(ref_b - b[:r0])).astype(BF16)
            for h in hs:
                attn_ref[h, r0:r0 + GLA_SUB, 0:r0] = _dot_nt(q_dec[:, keys[h]], k_dec[:, keys[h]])

    q_in = (q * jnp.exp(b)).astype(BF16)
    b_last = b[chunk - 1:chunk]
    k_out = (k * jnp.exp(b_last - b)).astype(BF16)
    carry = jnp.exp(b_last)
    for h in hs:
        v = v_refs[h][...]
        state_t = state_ref[h]
        o = _dot(attn_ref[h].astype(BF16), v) + _dot_nt(q_in[:, keys[h]], state_t.astype(BF16))
        state_ref[h] = state_t * carry[:, keys[h]] + _dot(v.astype(F32).T.astype(BF16), k_out[:, keys[h]])
        gate = gg_refs[h][...].astype(F32)
        out = _rms(o, ng_ref[...]) * (gate * jax.nn.sigmoid(gate))
        o_ref[:, h * GLA_DV:(h + 1) * GLA_DV] = out.astype(o_ref.dtype)


def gla_attention(proj, gla_wa, gla_ba, gla_norm_g, batch, seq, chunk=128):
    chunk = min(chunk, seq)
    assert seq % chunk == 0 and chunk % GLA_SUB == 0
    wa = jnp.zeros((LANES, GLA_KEY_WIDTH), BF16).at[SMALL_GA_LANE:SMALL_GA_LANE + GLA_RANK].set(gla_wa.astype(BF16))

    def head_spec(c0, h):
        return pl.BlockSpec((None, chunk, GLA_DV), lambda b, c, h=h: (b, c, c0 // GLA_DV + h))

    const = lambda b, c: (0, 0)
    return pl.pallas_call(
        functools.partial(_gla_kernel, chunk=chunk),
        grid=(batch, seq // chunk),
        in_specs=[pl.BlockSpec((None, chunk, GLA_KEY_WIDTH), lambda b, c: (b, c, C_GQ // GLA_KEY_WIDTH)),
                  pl.BlockSpec((None, chunk, GLA_KEY_WIDTH), lambda b, c: (b, c, C_GK // GLA_KEY_WIDTH)),
                  *[head_spec(C_GV, h) for h in range(GLA_HEADS)],
                  *[head_spec(C_GG, h) for h in range(GLA_HEADS)],
                  pl.BlockSpec((None, chunk, LANES), lambda b, c: (b, c, C_SMALL // LANES)),
                  pl.BlockSpec((LANES, GLA_KEY_WIDTH), const),
                  pl.BlockSpec((1, GLA_KEY_WIDTH), const),
                  pl.BlockSpec((1, GLA_DV), const)],
        out_specs=pl.BlockSpec((None, chunk, GLA_WIDTH), lambda b, c: (b, c, 0)),
        out_shape=jax.ShapeDtypeStruct((batch, seq, GLA_WIDTH), BF16),
        scratch_shapes=[pltpu.VMEM((GLA_HEADS, GLA_DV, GLA_DK), F32), pltpu.VMEM((GLA_HEADS, chunk, chunk), F32)],
        compiler_params=_params("parallel", "arbitrary"),
        name="gla",
    )(proj, proj, *([proj] * (2 * GLA_HEADS)), proj, wa, gla_ba.reshape(1, GLA_KEY_WIDTH).astype(F32),
      gla_norm_g.reshape(1, GLA_DV).astype(F32))


def _merge_kernel(om_ref, on_ref, og_ref, wm_ref, wn_ref, wg_ref, gm_ref, gn_ref, gl_ref, o_ref):
    def gated(gate_ref, a_ref, w_ref):
        return jax.nn.sigmoid(gate_ref[...].astype(F32)) * _dot(a_ref[...], w_ref[...])

    o_ref[...] = (gated(gm_ref, om_ref, wm_ref) + gated(gn_ref, on_ref, wn_ref)
                  + gated(gl_ref, og_ref, wg_ref)).astype(o_ref.dtype)


def merge_branches(o_m, o_n, o_g, w_m, w_n, w_g, proj2d, tm=1024, tn=512):
    m = o_m.shape[0]
    tm = min(tm, m)
    assert C_MG % tn == 0 and D_MODEL % tn == 0

    def gate_spec(c):
        return pl.BlockSpec((tm, tn), lambda i, j, c=c: (i, (C_MG + c * D_MODEL) // tn + j))

    def act_spec(width):
        return pl.BlockSpec((tm, width), lambda i, j: (i, 0))

    def w_spec(width):
        return pl.BlockSpec((width, tn), lambda i, j: (0, j))

    return pl.pallas_call(
        _merge_kernel,
        grid=(m // tm, D_MODEL // tn),
        in_specs=[act_spec(MOBA_WIDTH), act_spec(NSA_WIDTH), act_spec(GLA_WIDTH),
                  w_spec(MOBA_WIDTH), w_spec(NSA_WIDTH), w_spec(GLA_WIDTH),
                  gate_spec(0), gate_spec(1), gate_spec(2)],
        out_specs=pl.BlockSpec((tm, tn), lambda i, j: (i, j)),
        out_shape=jax.ShapeDtypeStruct((m, D_MODEL), BF16),
        compiler_params=_params("parallel", "parallel"),
        name="merge",
    )(o_m, o_n, o_g, w_m, w_n, w_g, proj2d, proj2d, proj2d)


def _first_max(vals, lane):
    top = jnp.max(vals, axis=-1, keepdims=True)
    idx = jnp.min(jnp.where(vals == top, lane, float(ROUTER_LANES)), axis=-1, keepdims=True)
    return top, idx


def _router_kernel(x_ref, g_ref, w_ref, b_ref, h_ref, route_ref):
    h = _rms(x_ref[...], g_ref[...])
    h_ref[...] = h.astype(h_ref.dtype)
    h1 = h.astype(BF16)
    r1 = h - h1.astype(F32)
    h2 = r1.astype(BF16)
    h3 = (r1 - h2.astype(F32)).astype(BF16)
    hi, lo = slice(0, ROUTER_LANES), slice(ROUTER_LANES, 2 * ROUTER_LANES)
    a = _dot(h1, w_ref[...])
    b = _dot(h2, w_ref[...])
    logits = (a[:, hi] + (a[:, lo] + b[:, hi]) + (b[:, lo] + _dot(h3, w_ref[:, hi]))) + b_ref[...]

    lane = lax.broadcasted_iota(I32, logits.shape, 1).astype(F32)
    g_logits = jnp.where(lane < MOE_GROUPS, logits, -jnp.inf)
    g_top, grp = _first_max(g_logits, lane)
    p_grp = 1.0 / jnp.sum(jnp.exp(g_logits - g_top), axis=-1, keepdims=True)
    first = MOE_GROUPS + grp * MOE_EXPERTS_PER_GROUP
    e_logits = jnp.where((lane >= first) & (lane < first + MOE_EXPERTS_PER_GROUP), logits, -jnp.inf)
    top1, lane1 = _first_max(e_logits, lane)
    top2, lane2 = _first_max(jnp.where(lane == lane1, -jnp.inf, e_logits), lane)
    ratio = jnp.exp(top2 - top1)
    w_first = p_grp / (1.0 + ratio)
    route = jnp.where(lane == 0, lane1 - MOE_GROUPS, jnp.where(lane == 1, lane2 - MOE_GROUPS,
                      jnp.where(lane == 2, w_first, jnp.where(lane == 3, w_first * ratio, 0.0))))
    route_ref[...] = route


def router(x2d, norm_g, rg_w, rg_b, re_w, re_b, tm=512):
    m, d = x2d.shape
    tm = min(tm, m)
    n_real = MOE_GROUPS + MOE_EXPERTS
    w = jnp.zeros((d, ROUTER_LANES), F32).at[:, :n_real].set(jnp.concatenate([rg_w, re_w], axis=1))
    w_hi = w.astype(BF16)
    w_lo = (w - w_hi.astype(F32)).astype(BF16)
    bias = jnp.zeros((1, ROUTER_LANES), F32).at[0, :n_real].set(jnp.concatenate([rg_b, re_b]))
    return pl.pallas_call(
        _router_kernel,
        grid=(m // tm,),
        in_specs=[pl.BlockSpec((tm, d), lambda i: (i, 0)), pl.BlockSpec((1, d), lambda i: (0, 0)),
                  pl.BlockSpec((d, 2 * ROUTER_LANES), lambda i: (0, 0)),
                  pl.BlockSpec((1, ROUTER_LANES), lambda i: (0, 0))],
        out_specs=[pl.BlockSpec((tm, d), lambda i: (i, 0)), pl.BlockSpec((tm, ROUTER_LANES), lambda i: (i, 0))],
        out_shape=[jax.ShapeDtypeStruct((m, d), BF16), jax.ShapeDtypeStruct((m, ROUTER_LANES), F32)],
        compiler_params=_params("parallel"),
        name="router",
    )(x2d, norm_g.reshape(1, d).astype(F32), jnp.concatenate([w_hi, w_lo], axis=1), bias)


def _expert_kernel(blk_e_ref, n_used_ref, x_ref, wg_ref, wu_ref, wd_ref, o_ref, wg_bf, wu_bf, wd_bf):
    i = pl.program_id(0)
    new_expert = (i == 0) | (blk_e_ref[i] != blk_e_ref[jnp.maximum(i - 1, 0)])

    @pl.when(new_expert)
    def _():
        wg_bf[...] = wg_ref[...].astype(BF16)
        wu_bf[...] = wu_ref[...].astype(BF16)
        wd_bf[...] = wd_ref[...].astype(BF16)

    @pl.when(i < n_used_ref[0])
    def _():
        x = x_ref[...]
        gate = _dot(x, wg_bf[...])
        hid = gate * jax.nn.sigmoid(gate) * _dot(x, wu_bf[...])
        o_ref[...] = _dot(hid.astype(BF16), wd_bf[...]).astype(o_ref.dtype)

    @pl.when(i >= n_used_ref[0])
    def _():
        o_ref[...] = jnp.zeros_like(o_ref)


def expert_blocks(xs, blk_e, n_used, layer, w_gate, w_up, w_down):
    p, d = xs.shape
    ff = w_gate.shape[3]
    n_blk = p // MOE_ROWS
    grid_spec = pltpu.PrefetchScalarGridSpec(
        num_scalar_prefetch=2,
        grid=(n_blk,),
        in_specs=[pl.BlockSpec((MOE_ROWS, d), lambda i, e, n: (i, 0)),
                  pl.BlockSpec((None, None, d, ff), lambda i, e, n: (layer, e[i], 0, 0)),
                  pl.BlockSpec((None, None, d, ff), lambda i, e, n: (layer, e[i], 0, 0)),
                  pl.BlockSpec((None, None, ff, d), lambda i, e, n: (layer, e[i], 0, 0))],
        out_specs=pl.BlockSpec((MOE_ROWS, d), lambda i, e, n: (i, 0)),
        scratch_shapes=[pltpu.VMEM((d, ff), BF16), pltpu.VMEM((d, ff), BF16), pltpu.VMEM((ff, d), BF16)],
    )
    return pl.pallas_call(
        _expert_kernel,
        grid_spec=grid_spec,
        out_shape=jax.ShapeDtypeStruct((p, d), BF16),
        compiler_params=_params("arbitrary"),
        name="experts",
    )(blk_e, n_used, xs, w_gate, w_up, w_down)


def _combine_kernel(x_ref, y0_ref, y1_ref, w_ref, g_ref, o_ref, *, final_norm):
    w = w_ref[...]
    x = x_ref[...] + (w[:, 0:1] * y0_ref[...].astype(F32) + w[:, 1:2] * y1_ref[...].astype(F32))
    o_ref[...] = _rms(x, g_ref[...]) if final_norm else x


def combine(x2d, y0, y1, w, norm_g, final_norm, tm=512):
    m, d = x2d.shape
    tm = min(tm, m)
    row = pl.BlockSpec((tm, d), lambda i: (i, 0))
    return pl.pallas_call(
        functools.partial(_combine_kernel, final_norm=final_norm),
        grid=(m // tm,),
        in_specs=[row, row, row, pl.BlockSpec((tm, MOE_TOPK), lambda i: (i, 0)),
                  pl.BlockSpec((1, d), lambda i: (0, 0))],
        out_specs=row,
        out_shape=jax.ShapeDtypeStruct((m, d), F32),
        compiler_params=_params("parallel"),
        name="combine",
    )(x2d, y0, y1, w, norm_g.reshape(1, d).astype(F32))


def hier_moe(x2d, norm_g, rg_w, rg_b, re_w, re_b, layer, w_gate, w_up, w_down, out_norm_g, final_norm):
    t = x2d.shape[0]
    h, route = router(x2d, norm_g, rg_w, rg_b, re_w, re_b)
    expert = route[:, 0:MOE_TOPK].astype(I32)
    w = route[:, MOE_TOPK:2 * MOE_TOPK]

    a = t * MOE_TOPK
    e_flat = expert.reshape(a)
    onehot = (e_flat[:, None] == jnp.arange(MOE_EXPERTS, dtype=I32)[None, :]).astype(I32)
    running = jnp.cumsum(onehot, axis=0)
    counts = running[-1]
    rank = jnp.take_along_axis(running, e_flat[:, None], axis=1)[:, 0] - 1
    padded = (counts + MOE_ROWS - 1) // MOE_ROWS * MOE_ROWS
    pends = jnp.cumsum(padded)
    dest = (pends - padded)[e_flat] + rank
    p_rows = (a + MOE_EXPERTS * (MOE_ROWS - 1)) // MOE_ROWS * MOE_ROWS
    n_blk = p_rows // MOE_ROWS
    row_tok = (jnp.arange(p_rows, dtype=I32) % t).at[dest].set(jnp.arange(a, dtype=I32) // MOE_TOPK)
    blk_start = jnp.arange(n_blk, dtype=I32) * MOE_ROWS
    blk_e = jnp.minimum(jnp.sum((pends[None, :] <= blk_start[:, None]).astype(I32), axis=1), MOE_EXPERTS - 1)
    n_used = (pends[-1] // MOE_ROWS).astype(I32).reshape(1)

    xs = jnp.take(h, row_tok, axis=0)
    y_rows = expert_blocks(xs, blk_e, n_used, layer, w_gate, w_up, w_down)
    dest2 = dest.reshape(t, MOE_TOPK)
    y0 = jnp.take(y_rows, dest2[:, 0], axis=0)
    y1 = jnp.take(y_rows, dest2[:, 1], axis=0)
    return combine(x2d, y0, y1, w, out_norm_g, final_norm)


PERM_TILE = 512
PERM_REGIONS = ((0, 0, C_GQ // PERM_TILE),
                (_SRC_GQ - C_GQ, C_GQ // PERM_TILE, C_GG // PERM_TILE),
                (_SRC_GG - C_GG, C_GG // PERM_TILE, C_SMALL // PERM_TILE))


def _permute_kernel(src_ref, small_ref, o_ref):
    @pl.when(pl.program_id(1) < C_SMALL // PERM_TILE)
    def _():
        o_ref[...] = src_ref[0].T.astype(o_ref.dtype)

    @pl.when(pl.program_id(1) == C_SMALL // PERM_TILE)
    def _():
        o_ref[:, 0:LANES] = small_ref[...].astype(o_ref.dtype)


def permute_w_in(w_in, layer, tk=D_MODEL):
    assert C_GQ % PERM_TILE == 0 and C_GG % PERM_TILE == 0 and C_SMALL % PERM_TILE == 0
    assert all(shift % SUBLANES == 0 for shift, _, _ in PERM_REGIONS)
    d = w_in.shape[1]
    w_t = jnp.swapaxes(w_in, 1, 2)
    small = jnp.concatenate([w_in[layer, :, _SRC_GA:_SRC_GG], w_in[layer, :, _SRC_NG:_SRC_GQ],
                             jnp.zeros((d, LANES - GLA_RANK - 3 * NSA_HEADS), w_in.dtype)], axis=1)

    def src_row(j):
        shift = sum(jnp.where((j >= j0) & (j < j1), s, 0) for s, j0, j1 in PERM_REGIONS)
        return pl.multiple_of(jnp.minimum(j * PERM_TILE + shift, _SRC_END - PERM_TILE), SUBLANES)

    return pl.pallas_call(
        _permute_kernel,
        grid=(d // tk, pl.cdiv(PROJ_WIDTH, PERM_TILE)),
        in_specs=[pl.BlockSpec((pl.Element(1), pl.Element(PERM_TILE), pl.Element(tk)),
                               lambda k, j: (layer, src_row(j), k * tk)),
                  pl.BlockSpec((tk, LANES), lambda k, j: (k, 0))],
        out_specs=pl.BlockSpec((tk, PERM_TILE), lambda k, j: (k, j)),
        out_shape=jax.ShapeDtypeStruct((d, PROJ_WIDTH), BF16),
        compiler_params=_params("parallel", "parallel"),
        name="permute_w_in",
    )(w_t, small)


def hybrid_layer(x, norm1_g, w_in, nsa_cmp_pe, nsa_cmp_w1, nsa_cmp_w2, gla_wa, gla_ba, gla_norm_g,
                 w_br_moba, w_br_nsa, w_br_gla, w_out, norm2_g, router_group_w, router_group_b,
                 router_expert_w, router_expert_b, layer, expert_w_gate, expert_w_up, expert_w_down,
                 out_norm_g, final_norm):
    batch, seq, d = x.shape
    t = batch * seq
    x2d = x.reshape(t, d)
    proj2d = norm_matmul(x2d, norm1_g, permute_w_in(w_in, layer), BF16)
    proj = proj2d.reshape(batch, seq, PROJ_WIDTH)
    o_m = moba_attention(proj, batch, seq)
    kv_c, kv_ct = nsa_compress(proj, nsa_cmp_pe, nsa_cmp_w1, nsa_cmp_w2, batch, seq)
    o_n = nsa_attention(proj, kv_c, kv_ct, batch, seq)
    o_g = gla_attention(proj, gla_wa, gla_ba, gla_norm_g, batch, seq)
    merged = merge_branches(o_m.reshape(t, -1), o_n.reshape(t, -1), o_g.reshape(t, -1),
                            w_br_moba.astype(BF16), w_br_nsa.astype(BF16), w_br_gla.astype(BF16), proj2d)
    x2d = matmul_residual(merged, w_out.astype(BF16), x2d)
    x2d = hier_moe(x2d, norm2_g, router_group_w, router_group_b, router_expert_w, router_expert_b,
                   layer, expert_w_gate, expert_w_up, expert_w_down, out_norm_g, final_norm)
    return x2d.reshape(batch, seq, d)


def kernel(x, norm1_g, w_in, nsa_cmp_pe, nsa_cmp_w1, nsa_cmp_w2, gla_wa, gla_ba, gla_norm_g, w_br_moba,
           w_br_nsa, w_br_gla, w_out, norm2_g, router_group_w, router_group_b, router_expert_w,
           router_expert_b, expert_w_gate, expert_w_up, expert_w_down, final_norm_g):
    for l in range(DEPTH):
        x = hybrid_layer(x, norm1_g[l], w_in, nsa_cmp_pe[l], nsa_cmp_w1[l], nsa_cmp_w2[l], gla_wa[l],
                         gla_ba[l], gla_norm_g[l], w_br_moba[l], w_br_nsa[l], w_br_gla[l], w_out[l],
                         norm2_g[l], router_group_w[l], router_group_b[l], router_expert_w[l],
                         router_expert_b[l], l, expert_w_gate, expert_w_up, expert_w_down,
                         final_norm_g, l == DEPTH - 1)
    return x
```

```python
import functools

import jax
import jax.numpy as jnp
import numpy as np
from jax import lax
from jax.experimental import pallas as pl
from jax.experimental.pallas import tpu as pltpu

F32 = jnp.float32
BF16 = jnp.bfloat16
I32 = jnp.int32

D_MODEL = 2048
DEPTH = 2
HEAD_DIM = 128
NEG = -1e30
FORCE = 1e9
EPS = 1e-6
LOG2E = 1.4426950408889634

MOBA_HEADS = 8
MOBA_BLOCK = 256
MOBA_TOPK = 3
MOBA_WIDTH = MOBA_HEADS * HEAD_DIM

NSA_HEADS = 8
NSA_KV_HEADS = 2
NSA_GROUP = NSA_HEADS // NSA_KV_HEADS
NSA_CMP_LEN = 32
NSA_CMP_STRIDE = 16
NSA_SLC_BLOCK = 64
NSA_TOPN = 8
NSA_WINDOW = 512
NSA_WIDTH = NSA_HEADS * HEAD_DIM
NSA_KV_WIDTH = NSA_KV_HEADS * HEAD_DIM

GLA_HEADS = 4
GLA_DK = 128
GLA_DV = 256
GLA_RANK = 16
GLA_TAU = 16.0
GLA_KEY_WIDTH = GLA_HEADS * GLA_DK
GLA_WIDTH = GLA_HEADS * GLA_DV

N_BRANCH = 3
MOE_GROUPS = 4
MOE_EXPERTS_PER_GROUP = 8
MOE_EXPERTS = MOE_GROUPS * MOE_EXPERTS_PER_GROUP
MOE_TOPK = 2
MOE_FF = D_MODEL // 4

LANES = 128
SUBLANES = 8
BF16_SUBLANES = 16
VMEM_LIMIT_BYTES = 56 * 1024 * 1024

_SRC_NG = MOBA_WIDTH * 3 + NSA_WIDTH + 6 * NSA_KV_WIDTH
_SRC_GQ = _SRC_NG + 3 * NSA_HEADS
_SRC_GA = _SRC_GQ + 2 * GLA_KEY_WIDTH + GLA_WIDTH
_SRC_GG = _SRC_GA + GLA_RANK
_SRC_END = _SRC_GG + GLA_WIDTH + N_BRANCH * D_MODEL

C_MQ = 0
C_MK = C_MQ + MOBA_WIDTH
C_MV = C_MK + MOBA_WIDTH
C_NQ = C_MV + MOBA_WIDTH
C_NKV = C_NQ + NSA_WIDTH
C_GQ = C_NKV + 6 * NSA_KV_WIDTH
C_GK = C_GQ + GLA_KEY_WIDTH
C_GV = C_GK + GLA_KEY_WIDTH
C_GG = C_GV + GLA_WIDTH
C_MG = C_GG + GLA_WIDTH
C_SMALL = C_MG + N_BRANCH * D_MODEL
PROJ_WIDTH = C_SMALL + LANES
SMALL_GA_LANE = 0
SMALL_NG_LANE = GLA_RANK

ROUTER_LANES = LANES
MOE_ROWS = 256


def _params(*semantics):
    return pltpu.CompilerParams(dimension_semantics=semantics, vmem_limit_bytes=VMEM_LIMIT_BYTES)


def _dot(a, b):
    return jnp.dot(a, b, preferred_element_type=F32)


def _dot_nt(a, b):
    return lax.dot_general(a, b, (((1,), (1,)), ((), ())), preferred_element_type=F32)


def _dot_split3(a01, x):
    x1 = x.astype(BF16)
    r1 = x - x1.astype(F32)
    x2 = r1.astype(BF16)
    x3 = (r1 - x2.astype(F32)).astype(BF16)
    return _dot(a01, x1) + _dot(a01, x2) + _dot(a01, x3)


def _rms(x, g):
    return x * lax.rsqrt(jnp.mean(x * x, axis=-1, keepdims=True) + EPS) * g


def _norm_matmul_kernel(x_ref, g_ref, b_ref, o_ref, h_ref):
    @pl.when(pl.program_id(1) == 0)
    def _():
        h_ref[...] = _rms(x_ref[...], g_ref[...]).astype(h_ref.dtype)

    o_ref[...] = _dot(h_ref[...], b_ref[...]).astype(o_ref.dtype)


def norm_matmul(x, g, b, out_dtype, tm=1024, tn=1152):
    m, k = x.shape
    n = b.shape[1]
    tm, tn = min(tm, m), min(tn, n)
    return pl.pallas_call(
        _norm_matmul_kernel,
        grid=(m // tm, n // tn),
        in_specs=[pl.BlockSpec((tm, k), lambda i, j: (i, 0)), pl.BlockSpec((1, k), lambda i, j: (0, 0)),
                  pl.BlockSpec((k, tn), lambda i, j: (0, j))],
        out_specs=pl.BlockSpec((tm, tn), lambda i, j: (i, j)),
        out_shape=jax.ShapeDtypeStruct((m, n), out_dtype),
        scratch_shapes=[pltpu.VMEM((tm, k), BF16)],
        compiler_params=_params("parallel", "arbitrary"),
        name="norm_matmul",
    )(x, g.reshape(1, k).astype(F32), b)


def _matmul_residual_kernel(a_ref, b_ref, r_ref, o_ref):
    o_ref[...] = r_ref[...] + _dot(a_ref[...], b_ref[...])


def matmul_residual(a, b, res, tm=1024, tn=1024):
    m, k = a.shape
    n = b.shape[1]
    tm, tn = min(tm, m), min(tn, n)
    return pl.pallas_call(
        _matmul_residual_kernel,
        grid=(m // tm, n // tn),
        in_specs=[pl.BlockSpec((tm, k), lambda i, j: (i, 0)), pl.BlockSpec((k, tn), lambda i, j: (0, j)),
                  pl.BlockSpec((tm, tn), lambda i, j: (i, j))],
        out_specs=pl.BlockSpec((tm, tn), lambda i, j: (i, j)),
        out_shape=jax.ShapeDtypeStruct((m, n), F32),
        compiler_params=_params("parallel", "parallel"),
        name="matmul_residual",
    )(a, b, res)


def _softmax_step(carry, q, k, v, mask, scale):
    m_i, l_i, acc = carry
    s = jnp.where(mask, _dot_nt(q, k) * scale, NEG)
    m_new = jnp.maximum(m_i, jnp.max(s, axis=-1, keepdims=True))
    alpha = jnp.exp(m_i - m_new)
    p = jnp.where(mask, jnp.exp(s - m_new), 0.0)
    l_new = alpha * l_i + jnp.sum(p, axis=-1, keepdims=True)
    acc_new = alpha * acc + _dot(p.astype(v.dtype), v)
    return m_new, l_new, acc_new


def _softmax_init(m, e):
    return jnp.full((m, 1), NEG, F32), jnp.zeros((m, 1), F32), jnp.zeros((m, e), F32)


def _rank_desc(vals, n_candidates):
    lane = lax.broadcasted_iota(I32, vals.shape, 1)
    rank = jnp.zeros(vals.shape, F32)
    for m in range(n_candidates):
        vm = vals[:, m:m + 1]
        beats = (vm > vals) | ((vm == vals) & (lane > m))
        rank = rank + beats.astype(F32)
    return rank


def _softmax_step_t(carry, q, k, v_t, bias, scale):
    m_i, l_i, acc = carry
    s = _dot_nt(k, q) * scale + bias
    m_new = jnp.maximum(m_i, jnp.max(s, axis=0, keepdims=True))
    alpha = jnp.exp(m_i - m_new)
    p = jnp.exp(s - m_new)
    l_new = alpha * l_i + jnp.sum(p, axis=0, keepdims=True)
    acc_new = alpha * acc + _dot(v_t, p.astype(v_t.dtype))
    return m_new, l_new, acc_new


def _softmax_init_t(m, e):
    return jnp.full((1, m), NEG, F32), jnp.zeros((1, m), F32), jnp.zeros((e, m), F32)


def _rank_desc_rows(vals, n_candidates):
    row = lax.broadcasted_iota(I32, vals.shape, 0)
    rank = jnp.zeros(vals.shape, F32)
    for m in range(n_candidates):
        vm = vals[m:m + 1, :]
        beats = (vm > vals) | ((vm == vals) & (row > m))
        rank = rank + beats.astype(F32)
    return rank


def _lane_column(x, idx):
    lane = lax.broadcasted_iota(I32, x.shape, 1)
    return jnp.sum(jnp.where(lane == idx, x, 0.0), axis=-1, keepdims=True)


def _moba_kernel(q_ref, k_ref, vt_ref, o_ref, kmean_ref, bias_ref, s_ref, p_ref, acc_ref, bmax_ref, m_ref, *,
                 n_blocks, topk, heads):
    blk = MOBA_BLOCK
    cur = pl.program_id(2)
    scale = HEAD_DIM ** -0.5
    cols = [slice(h * HEAD_DIM, (h + 1) * HEAD_DIM) for h in range(heads)]

    @pl.when(cur == 0)
    def _():
        kmean_ref[...] = jnp.zeros_like(kmean_ref)
        for h in range(heads):
            for n in range(n_blocks):
                k_blk = k_ref[n * blk:(n + 1) * blk, cols[h]].astype(F32)
                kmean_ref[h, n:n + 1, :] = jnp.sum(k_blk, axis=0, keepdims=True) * (1.0 / blk)

    for h in range(heads):
        gate = _dot_nt(kmean_ref[h].astype(BF16), q_ref[:, cols[h]])
        row = lax.broadcasted_iota(I32, gate.shape, 0)
        valid = row < cur
        gate = jnp.where(valid, gate, NEG)
        chosen = valid & (_rank_desc_rows(gate, n_blocks) < topk)
        bias_ref[h] = jnp.where(chosen, 0.0, NEG)

    seq = k_ref.shape[0]
    lo_blocks = (n_blocks + 1) // 2
    lo = lo_blocks * blk
    need_hi = cur >= lo_blocks

    def score_blocks(b0, b1):
        for h in range(heads):
            s = _dot_nt(k_ref[b0 * blk:b1 * blk, cols[h]], q_ref[:, cols[h]])
            s_ref[h, b0 * blk:b1 * blk, :] = s
            for n in range(b0, b1):
                s_blk = s[(n - b0) * blk:(n - b0 + 1) * blk]
                bmax_ref[h, n] = jnp.max(s_blk.reshape(blk // SUBLANES, SUBLANES, blk), axis=0)

    def past_max(h, b0, b1):
        return functools.reduce(jnp.maximum, [bmax_ref[h, n] * scale + bias_ref[h, n:n + 1, :]
                                              for n in range(b0, b1)])

    score_blocks(0, lo_blocks)
    if lo < seq:
        pl.when(need_hi)(lambda: score_blocks(lo_blocks, n_blocks))

    sub = LANES
    base = pl.multiple_of(cur * blk, blk)
    qry_i = lax.broadcasted_iota(I32, (sub, blk), 1)
    key_i = [j * sub + lax.broadcasted_iota(I32, (sub, blk), 0) for j in range(blk // sub)]
    exp_scale = scale * LOG2E

    def fold_max(x):
        return jnp.max(x.reshape(sub // SUBLANES, SUBLANES, blk), axis=0)

    def tiles(h, start):
        return [s_ref[h, pl.ds(pl.multiple_of(start + j * sub, sub), sub), :] for j in range(blk // sub)]

    for h in range(heads):
        own = [fold_max(jnp.where(key_i[j] <= qry_i, t, NEG)) for j, t in enumerate(tiles(h, base))]
        m_ref[h] = jnp.maximum(functools.reduce(jnp.maximum, own) * scale, past_max(h, 0, lo_blocks))

    if lo < seq:
        @pl.when(need_hi)
        def _():
            for h in range(heads):
                m_ref[h] = jnp.maximum(m_ref[h], past_max(h, lo_blocks, n_blocks))

    m_log2 = [jnp.max(m_ref[h], axis=0, keepdims=True) * LOG2E for h in range(heads)]

    def fold_sum(x):
        return jnp.sum(x.reshape(sub // SUBLANES, SUBLANES, blk), axis=0)

    def store_p(h, start, j, p):
        p_ref[h, pl.ds(pl.multiple_of(start + j * sub, sub), sub), :] = p.astype(p_ref.dtype)

    def prob_body(n, l8):
        out = []
        for h in range(heads):
            shift = bias_ref[h, pl.ds(n, 1), :] * LOG2E - m_log2[h]
            acc = l8[h]
            for j, t in enumerate(tiles(h, n * blk)):
                p = jnp.exp2(t * exp_scale + shift)
                store_p(h, n * blk, j, p)
                acc = acc + fold_sum(p)
            out.append(acc)
        return tuple(out)

    l8 = []
    for h in range(heads):
        acc = jnp.zeros((SUBLANES, blk), F32)
        for j, t in enumerate(tiles(h, base)):
            p = jnp.where(key_i[j] <= qry_i, jnp.exp2(t * exp_scale - m_log2[h]), 0.0)
            store_p(h, base, j, p)
            acc = acc + fold_sum(p)
        l8.append(acc)
    l8 = lax.fori_loop(0, cur, prob_body, tuple(l8))

    def zero_body(n, carry):
        for h in range(heads):
            for j in range(blk // sub):
                store_p(h, n * blk, j, jnp.zeros((sub, blk), F32))
        return carry

    lax.fori_loop(cur + 1, jnp.where(need_hi, n_blocks, lo_blocks), zero_body, 0)
    for h in range(heads):
        acc_ref[h] = _dot(vt_ref[h, :, 0:lo], p_ref[h, 0:lo, :])

    if lo < seq:
        @pl.when(need_hi)
        def _():
            for h in range(heads):
                acc_ref[h] += _dot(vt_ref[h, :, lo:seq], p_ref[h, lo:seq, :])

    for h in range(heads):
        l_sum = jnp.sum(l8[h], axis=0, keepdims=True)
        o_ref[:, cols[h]] = (acc_ref[h] / l_sum).T.astype(o_ref.dtype)


def moba_attention(proj, batch, seq, heads=8):
    assert seq % MOBA_BLOCK == 0 and MOBA_HEADS % heads == 0
    n_blocks = seq // MOBA_BLOCK
    rows = -(-n_blocks // BF16_SUBLANES) * BF16_SUBLANES
    width = heads * HEAD_DIM
    qb, kb = C_MQ // width, C_MK // width
    v_t = proj[:, :, C_MV:C_MV + MOBA_WIDTH].reshape(batch, seq, MOBA_HEADS, HEAD_DIM).transpose(0, 2, 3, 1)
    kern = functools.partial(_moba_kernel, n_blocks=n_blocks, topk=min(MOBA_TOPK, n_blocks), heads=heads)
    return pl.pallas_call(
        kern,
        grid=(batch, MOBA_HEADS // heads, n_blocks),
        in_specs=[pl.BlockSpec((None, MOBA_BLOCK, width), lambda b, h, i: (b, i, qb + h)),
                  pl.BlockSpec((None, seq, width), lambda b, h, i: (b, 0, kb + h)),
                  pl.BlockSpec((None, heads, HEAD_DIM, seq), lambda b, h, i: (b, h, 0, 0))],
        out_specs=pl.BlockSpec((None, MOBA_BLOCK, width), lambda b, h, i: (b, i, h)),
        out_shape=jax.ShapeDtypeStruct((batch, seq, MOBA_WIDTH), BF16),
        scratch_shapes=[pltpu.VMEM((heads, rows, HEAD_DIM), F32), pltpu.VMEM((heads, rows, MOBA_BLOCK), F32),
                        pltpu.VMEM((heads, seq, MOBA_BLOCK), F32), pltpu.VMEM((heads, seq, MOBA_BLOCK), BF16),
                        pltpu.VMEM((heads, HEAD_DIM, MOBA_BLOCK), F32),
                        pltpu.VMEM((heads, n_blocks, SUBLANES, MOBA_BLOCK), F32),
                        pltpu.VMEM((heads, SUBLANES, MOBA_BLOCK), F32)],
        compiler_params=_params("parallel", "parallel", "arbitrary"),
        name="moba",
    )(proj, proj, v_t)


def _nsa_compress_kernel(x_ref, pe_ref, w1_ref, w2_ref, o_ref, ot_ref):
    x = x_ref[...].astype(F32)
    half = x.shape[1]
    lo = _dot((x + pe_ref[:, :half]).astype(BF16), w1_ref[:half, :])
    hi = _dot((x + pe_ref[:, half:]).astype(BF16), w1_ref[half:, :])
    pre = lo + pltpu.roll(hi, hi.shape[0] - 1, 0)
    hid = pre * jax.nn.sigmoid(pre)
    out = _dot(hid.astype(BF16), w2_ref[...])
    o_ref[...] = out.astype(o_ref.dtype)
    ot_ref[...] = out.T.astype(ot_ref.dtype)


def nsa_compress(proj, cmp_pe, cmp_w1, cmp_w2, batch, seq):
    n16 = seq // NSA_CMP_STRIDE
    width = NSA_CMP_STRIDE * HEAD_DIM
    x = proj[:, :, C_NKV:C_NKV + 2 * NSA_KV_WIDTH].reshape(batch, seq, 2, NSA_KV_HEADS, HEAD_DIM)
    x = x.transpose(0, 2, 3, 1, 4).reshape(batch, 2, NSA_KV_HEADS, n16, width)
    pe = cmp_pe.reshape(2, 1, NSA_CMP_LEN * HEAD_DIM).astype(F32)
    return pl.pallas_call(
        _nsa_compress_kernel,
        grid=(batch, 2, NSA_KV_HEADS),
        in_specs=[pl.BlockSpec((None, None, None, n16, width), lambda b, c, g: (b, c, g, 0, 0)),
                  pl.BlockSpec((None, 1, 2 * width), lambda b, c, g: (c, 0, 0)),
                  pl.BlockSpec((None, 2 * width, HEAD_DIM), lambda b, c, g: (c, 0, 0)),
                  pl.BlockSpec((None, HEAD_DIM, HEAD_DIM), lambda b, c, g: (c, 0, 0))],
        out_specs=[pl.BlockSpec((None, None, None, n16, HEAD_DIM), lambda b, c, g: (b, c, g, 0, 0)),
                   pl.BlockSpec((None, None, None, HEAD_DIM, n16), lambda b, c, g: (b, c, g, 0, 0))],
        out_shape=[jax.ShapeDtypeStruct((batch, 2, NSA_KV_HEADS, n16, HEAD_DIM), BF16),
                   jax.ShapeDtypeStruct((batch, 2, NSA_KV_HEADS, HEAD_DIM, n16), BF16)],
        compiler_params=_params("parallel", "parallel", "parallel"),
        name="nsa_compress",
    )(x, pe, cmp_w1.astype(BF16), cmp_w2.astype(BF16))


def _nsa_group(g, qi, q_ref, kc_ref, vct_ref, ks_ref, vst_ref, kw_ref, vwt_ref, ngt_ref, o_ref,
               bias_ref, s_ref, p_ref, sw_ref, pw_ref, out_ref, bmax_ref, *, tq, span, n_cmp, n_slc, topn,
               win_len):
    seq = ks_ref.shape[0]
    n16 = kc_ref.shape[0]
    lanes = NSA_GROUP * tq
    blk = NSA_SLC_BLOCK
    scale = HEAD_DIM ** -0.5
    exp_scale = scale * LOG2E
    start = qi * tq
    heads = [slice(r * tq, (r + 1) * tq) for r in range(NSA_GROUP)]
    q = [q_ref[:, r * HEAD_DIM:(r + 1) * HEAD_DIM] for r in range(NSA_GROUP)]

    def per_group(row):
        return jnp.concatenate([row] * NSA_GROUP, axis=1)

    def scores(keys):
        return jnp.concatenate([_dot_nt(keys, q[r]) for r in range(NSA_GROUP)], axis=1)

    def fold(op, x):
        return op(x.reshape(x.shape[0] // SUBLANES, SUBLANES, lanes), axis=0)

    pos1 = start + lax.broadcasted_iota(I32, (1, tq), 1)
    pos = per_group(pos1)

    n_idx = lax.broadcasted_iota(I32, (n16, lanes), 0)
    in_range = n_idx < n_cmp
    cmask = (n_idx * NSA_CMP_STRIDE + (NSA_CMP_LEN - 1) <= pos) & in_range
    s_c = jnp.where(cmask, scores(kc_ref[...]) * scale, NEG)
    e_c = jnp.where(in_range, jnp.exp(s_c - jnp.max(s_c, axis=0, keepdims=True)), 0.0)
    p_c = jnp.where(cmask, e_c / jnp.sum(e_c, axis=0, keepdims=True), 0.0)
    out_ref[0] = _dot(vct_ref[...], p_c.astype(BF16))

    p_sum = functools.reduce(lambda a, b: a + b, [p_c[:, h] for h in heads])
    rows = bias_ref.shape[0]
    oj = lax.broadcasted_iota(I32, (rows, n16), 0)
    on = lax.broadcasted_iota(I32, (rows, n16), 1)
    overlap_t = ((on * NSA_CMP_STRIDE < (oj + 1) * blk) & (on * NSA_CMP_STRIDE + (NSA_CMP_LEN - 1) >= oj * blk)
                 & (on < n_cmp) & (oj < n_slc)).astype(BF16)
    p_hi = p_sum.astype(BF16)
    p_lo = (p_sum - p_hi.astype(F32)).astype(BF16)
    imp = _dot(overlap_t, p_hi) + _dot(overlap_t, p_lo)
    j_idx = lax.broadcasted_iota(I32, (rows, tq), 0)
    cur_blk = pos1 // blk
    forced = (j_idx == 0) | (j_idx == cur_blk) | (j_idx == cur_blk - 1)
    imp = jnp.where(forced, FORCE, imp)
    imp = jnp.where(j_idx > cur_blk, NEG, imp)
    chosen = (_rank_desc_rows(imp, n_slc) < topn) & (j_idx <= cur_blk)
    bias_ref[...] = jnp.where(chosen, 0.0, NEG)

    n_spans = seq // span
    span_blocks = span // blk

    def span_scores(k):
        s = scores(ks_ref[k * span:(k + 1) * span, :])
        s_ref[k * span:(k + 1) * span, :] = s
        for j in range(span_blocks):
            bmax_ref[k * span_blocks + j] = fold(jnp.max, s[j * blk:(j + 1) * blk])

    span_scores(0)
    w0 = pl.multiple_of(jnp.maximum(start + tq - win_len, 0), tq)
    sw_ref[...] = scores(kw_ref[pl.ds(w0, win_len), :])
    for k in range(1, n_spans):
        pl.when(start >= k * span)(functools.partial(span_scores, k))

    per_tile = tq // blk
    first_own = qi * per_tile

    def block_bias(j):
        return per_group(bias_ref[pl.ds(j, 1), :])

    def block_rows(j):
        return pl.ds(pl.multiple_of(j * blk, blk), blk)

    def causal_mask(j):
        key = j * blk + lax.broadcasted_iota(I32, (blk, lanes), 0)
        return key <= pos

    def max_body(i, m8):
        for d in range(per_tile):
            j = i * per_tile + d
            m8 = jnp.maximum(m8, bmax_ref[j] * scale + block_bias(j))
        return m8

    m8 = lax.fori_loop(0, qi, max_body, jnp.full((SUBLANES, lanes), NEG, F32))
    for d in range(per_tile):
        j = first_own + d
        own = jnp.where(causal_mask(j), s_ref[block_rows(j), :], NEG)
        m8 = jnp.maximum(m8, fold(jnp.max, own) * scale + block_bias(j))
    m_log2 = jnp.max(m8, axis=0, keepdims=True) * LOG2E

    def prob_body(i, l8):
        for d in range(per_tile):
            j = i * per_tile + d
            p = jnp.exp2(s_ref[block_rows(j), :] * exp_scale + (block_bias(j) * LOG2E - m_log2))
            p_ref[block_rows(j), :] = p.astype(p_ref.dtype)
            l8 = l8 + fold(jnp.sum, p)
        return l8

    l8 = lax.fori_loop(0, qi, prob_body, jnp.zeros((SUBLANES, lanes), F32))
    for d in range(per_tile):
        j = first_own + d
        p = jnp.exp2(s_ref[block_rows(j), :] * exp_scale + (block_bias(j) * LOG2E - m_log2))
        p = jnp.where(causal_mask(j), p, 0.0)
        p_ref[block_rows(j), :] = p.astype(p_ref.dtype)
        l8 = l8 + fold(jnp.sum, p)

    def zero_body(i, carry):
        p_ref[pl.ds(pl.multiple_of(i * tq, tq), tq), :] = jnp.zeros((tq, lanes), p_ref.dtype)
        return carry

    visible_tiles = (start // span + 1) * (span // tq)
    lax.fori_loop(qi + 1, visible_tiles, zero_body, 0)

    out_ref[1] = _dot(vst_ref[:, 0:span], p_ref[0:span, :])

    n_tiles = win_len // blk

    def win_tile(t):
        key = w0 + t * blk + lax.broadcasted_iota(I32, (blk, lanes), 0)
        return sw_ref[t * blk:(t + 1) * blk, :], (key <= pos) & (key > pos - NSA_WINDOW)

    wm8 = jnp.full((SUBLANES, lanes), NEG, F32)
    for t in range(n_tiles):
        tile, mask = win_tile(t)
        wm8 = jnp.maximum(wm8, fold(jnp.max, jnp.where(mask, tile, NEG)))
    wm_log2 = jnp.max(wm8, axis=0, keepdims=True) * exp_scale
    wl8 = jnp.zeros((SUBLANES, lanes), F32)
    for t in range(n_tiles):
        tile, mask = win_tile(t)
        p = jnp.where(mask, jnp.exp2(tile * exp_scale - wm_log2), 0.0)
        pw_ref[t * blk:(t + 1) * blk, :] = p.astype(pw_ref.dtype)
        wl8 = wl8 + fold(jnp.sum, p)
    w_tile0 = w0 // tq
    pv = _dot(vwt_ref[w_tile0], pw_ref[0:tq, :])
    for c in range(1, win_len // tq):
        pv = pv + _dot(vwt_ref[w_tile0 + c], pw_ref[c * tq:(c + 1) * tq, :])
    out_ref[2] = pv / jnp.sum(wl8, axis=0, keepdims=True)

    for k in range(1, n_spans):
        def span_pv(k=k):
            out_ref[1] += _dot(vst_ref[:, k * span:(k + 1) * span], p_ref[k * span:(k + 1) * span, :])
        pl.when(start >= k * span)(span_pv)
    out_ref[1] = out_ref[1] / jnp.sum(l8, axis=0, keepdims=True)

    for r in range(NSA_GROUP):
        gate_row = (g * NSA_GROUP + r) * 3
        mix = jnp.zeros((HEAD_DIM, tq), F32)
        for c in range(3):
            mix = mix + jax.nn.sigmoid(ngt_ref[pl.ds(gate_row + c, 1), :]) * out_ref[c, :, heads[r]]
        head = g * NSA_GROUP + r
        o_ref[:, head * HEAD_DIM:(head + 1) * HEAD_DIM] = mix.T.astype(o_ref.dtype)


NSA_GROUP_OPERANDS = 7


def _nsa_kernel(*refs, **params):
    n_in = NSA_KV_HEADS * NSA_GROUP_OPERANDS
    ngt_ref, o_ref = refs[n_in:n_in + 2]
    for g in range(NSA_KV_HEADS):
        _nsa_group(g, pl.program_id(1), *refs[g * NSA_GROUP_OPERANDS:(g + 1) * NSA_GROUP_OPERANDS],
                   ngt_ref, o_ref, *refs[n_in + 2:], **params)


def nsa_attention(proj, kv_c, kv_ct, batch, seq, tq=128):
    assert tq == LANES and seq % tq == 0 and tq % NSA_SLC_BLOCK == 0
    n_cmp = (seq - NSA_CMP_LEN) // NSA_CMP_STRIDE + 1
    n_slc = seq // NSA_SLC_BLOCK
    rows = -(-n_slc // BF16_SUBLANES) * BF16_SUBLANES
    span = max(tq, seq // 4)
    win_len = min(NSA_WINDOW + tq, seq)
    n16 = seq // NSA_CMP_STRIDE
    gw = NSA_GROUP * HEAD_DIM
    lanes = NSA_GROUP * tq
    nkv = C_NKV // HEAD_DIM

    def kv_cols(slot):
        c0 = C_NKV + slot * NSA_KV_WIDTH
        return proj[:, :, c0:c0 + NSA_KV_WIDTH].reshape(batch, seq, NSA_KV_HEADS, HEAD_DIM)

    vs_t = kv_cols(3).transpose(0, 2, 3, 1)
    vw_t = kv_cols(5).reshape(batch, seq // tq, tq, NSA_KV_HEADS, HEAD_DIM).transpose(0, 3, 1, 4, 2)
    n_gates = 3 * NSA_HEADS
    ng_t = proj[:, :, C_SMALL + SMALL_NG_LANE:C_SMALL + SMALL_NG_LANE + n_gates].astype(F32).transpose(0, 2, 1)

    def group_specs(g):
        def k_spec(slot):
            return pl.BlockSpec((None, seq, HEAD_DIM), lambda b, i: (b, 0, nkv + slot * NSA_KV_HEADS + g))

        return [pl.BlockSpec((None, tq, gw), lambda b, i: (b, i, C_NQ // gw + g)),
                pl.BlockSpec((None, None, None, n16, HEAD_DIM), lambda b, i: (b, 0, g, 0, 0)),
                pl.BlockSpec((None, None, None, HEAD_DIM, n16), lambda b, i: (b, 1, g, 0, 0)),
                k_spec(2),
                pl.BlockSpec((None, None, HEAD_DIM, seq), lambda b, i: (b, g, 0, 0)),
                k_spec(4),
                pl.BlockSpec((None, None, seq // tq, HEAD_DIM, tq), lambda b, i: (b, g, 0, 0, 0))]

    kern = functools.partial(_nsa_kernel, tq=tq, span=span, n_cmp=n_cmp, n_slc=n_slc,
                             topn=min(NSA_TOPN, n_slc), win_len=win_len)
    return pl.pallas_call(
        kern,
        grid=(batch, seq // tq),
        in_specs=[spec for g in range(NSA_KV_HEADS) for spec in group_specs(g)]
                 + [pl.BlockSpec((None, n_gates, tq), lambda b, i: (b, 0, i))],
        out_specs=pl.BlockSpec((None, tq, NSA_WIDTH), lambda b, i: (b, i, 0)),
        out_shape=jax.ShapeDtypeStruct((batch, seq, NSA_WIDTH), BF16),
        scratch_shapes=[pltpu.VMEM((rows, tq), F32),
                        pltpu.VMEM((seq, lanes), F32), pltpu.VMEM((seq, lanes), BF16),
                        pltpu.VMEM((win_len, lanes), F32), pltpu.VMEM((win_len, lanes), BF16),
                        pltpu.VMEM((3, HEAD_DIM, lanes), F32), pltpu.VMEM((n_slc, SUBLANES, lanes), F32)],
        compiler_params=_params("parallel", "arbitrary"),
        name="nsa",
    )(*([proj, kv_c, kv_ct, proj, vs_t, proj, vw_t] * NSA_KV_HEADS), ng_t)


GLA_SUB = 8


def _gla_kernel(*refs, chunk):
    q_ref, k_ref = refs[0:2]
    v_refs = refs[2:2 + GLA_HEADS]
    gg_refs = refs[2 + GLA_HEADS:2 + 2 * GLA_HEADS]
    small_ref, wa_ref, ba_ref, ng_ref, o_ref, state_ref, attn_ref = refs[2 + 2 * GLA_HEADS:]
    hs = range(GLA_HEADS)
    keys = [slice(h * GLA_DK, (h + 1) * GLA_DK) for h in hs]

    @pl.when(pl.program_id(1) == 0)
    def _():
        state_ref[...] = jnp.zeros_like(state_ref)

    q = q_ref[...].astype(F32) * (GLA_DK ** -0.5)
    k = k_ref[...].astype(F32)
    z = _dot(small_ref[...], wa_ref[...]) + ba_ref[...]
    log_a = (jnp.minimum(z, 0.0) - jnp.log(1.0 + jnp.exp(-jnp.abs(z)))) * (1.0 / GLA_TAU)
    tri = (lax.broadcasted_iota(I32, (chunk, chunk), 1)
           <= lax.broadcasted_iota(I32, (chunk, chunk), 0)).astype(BF16)
    b = _dot_split3(tri, log_a)

    attn_ref[...] = jnp.zeros_like(attn_ref)
    t_idx = lax.broadcasted_iota(I32, (GLA_SUB, 1), 0)
    s_lane = lax.broadcasted_iota(I32, (GLA_SUB, GLA_SUB), 1)
    for i in range(chunk // GLA_SUB):
        r0 = i * GLA_SUB
        bi, qi, ki = b[r0:r0 + GLA_SUB], q[r0:r0 + GLA_SUB], k[r0:r0 + GLA_SUB]
        diag = [jnp.zeros((GLA_SUB, GLA_SUB), F32) for _ in hs]
        for s in range(GLA_SUB):
            decay = jnp.exp(jnp.where(t_idx >= s, bi - bi[s:s + 1], NEG))
            prod = qi * ki[s:s + 1] * decay
            for h in hs:
                col = jnp.sum(prod[:, keys[h]], axis=-1, keepdims=True)
                diag[h] = jnp.where(s_lane == s, col, diag[h])
        for h in hs:
            attn_ref[h, r0:r0 + GLA_SUB, r0:r0 + GLA_SUB] = diag[h]
        if i > 0:
            ref_b = b[r0:r0 + 1]
            q_dec = (qi * jnp.exp(bi - ref_b)).astype(BF16)
            k_dec = (k[:r0] * jnp.exp(ref_b - b[:r0])).astype(BF16)
            for h in hs:
                attn_ref[h, r0:r0 + GLA_SUB, 0:r0] = _dot_nt(q_dec[:, keys[h]], k_dec[:, keys[h]])

    q_in = (q * jnp.exp(b)).astype(BF16)
    b_last = b[chunk - 1:chunk]
    k_out = (k * jnp.exp(b_last - b)).astype(BF16)
    carry = jnp.exp(b_last)
    for h in hs:
        v = v_refs[h][...]
        state_t = state_ref[h]
        o = _dot(attn_ref[h].astype(BF16), v) + _dot_nt(q_in[:, keys[h]], state_t.astype(BF16))
        state_ref[h] = state_t * carry[:, keys[h]] + _dot(v.astype(F32).T.astype(BF16), k_out[:, keys[h]])
        gate = gg_refs[h][...].astype(F32)
        out = _rms(o, ng_ref[...]) * (gate * jax.nn.sigmoid(gate))
        o_ref[:, h * GLA_DV:(h + 1) * GLA_DV] = out.astype(o_ref.dtype)


def gla_attention(proj, gla_wa, gla_ba, gla_norm_g, batch, seq, chunk=128):
    chunk = min(chunk, seq)
    assert seq % chunk == 0 and chunk % GLA_SUB == 0
    wa = jnp.zeros((LANES, GLA_KEY_WIDTH), BF16).at[SMALL_GA_LANE:SMALL_GA_LANE + GLA_RANK].set(gla_wa.astype(BF16))

    def head_spec(c0, h):
        return pl.BlockSpec((None, chunk, GLA_DV), lambda b, c, h=h: (b, c, c0 // GLA_DV + h))

    const = lambda b, c: (0, 0)
    return pl.pallas_call(
        functools.partial(_gla_kernel, chunk=chunk),
        grid=(batch, seq // chunk),
        in_specs=[pl.BlockSpec((None, chunk, GLA_KEY_WIDTH), lambda b, c: (b, c, C_GQ // GLA_KEY_WIDTH)),
                  pl.BlockSpec((None, chunk, GLA_KEY_WIDTH), lambda b, c: (b, c, C_GK // GLA_KEY_WIDTH)),
                  *[head_spec(C_GV, h) for h in range(GLA_HEADS)],
                  *[head_spec(C_GG, h) for h in range(GLA_HEADS)],
                  pl.BlockSpec((None, chunk, LANES), lambda b, c: (b, c, C_SMALL // LANES)),
                  pl.BlockSpec((LANES, GLA_KEY_WIDTH), const),
                  pl.BlockSpec((1, GLA_KEY_WIDTH), const),
                  pl.BlockSpec((1, GLA_DV), const)],
        out_specs=pl.BlockSpec((None, chunk, GLA_WIDTH), lambda b, c: (b, c, 0)),
        out_shape=jax.ShapeDtypeStruct((batch, seq, GLA_WIDTH), BF16),
        scratch_shapes=[pltpu.VMEM((GLA_HEADS, GLA_DV, GLA_DK), F32), pltpu.VMEM((GLA_HEADS, chunk, chunk), F32)],
        compiler_params=_params("parallel", "arbitrary"),
        name="gla",
    )(proj, proj, *([proj] * (2 * GLA_HEADS)), proj, wa, gla_ba.reshape(1, GLA_KEY_WIDTH).astype(F32),
      gla_norm_g.reshape(1, GLA_DV).astype(F32))


def _merge_kernel(om_ref, on_ref, og_ref, wm_ref, wn_ref, wg_ref, gm_ref, gn_ref, gl_ref, o_ref):
    def gated(gate_ref, a_ref, w_ref):
        return jax.nn.sigmoid(gate_ref[...].astype(F32)) * _dot(a_ref[...], w_ref[...])

    o_ref[...] = (gated(gm_ref, om_ref, wm_ref) + gated(gn_ref, on_ref, wn_ref)
                  + gated(gl_ref, og_ref, wg_ref)).astype(o_ref.dtype)


def merge_branches(o_m, o_n, o_g, w_m, w_n, w_g, proj2d, tm=1024, tn=512):
    m = o_m.shape[0]
    tm = min(tm, m)
    assert C_MG % tn == 0 and D_MODEL % tn == 0

    def gate_spec(c):
        return pl.BlockSpec((tm, tn), lambda i, j, c=c: (i, (C_MG + c * D_MODEL) // tn + j))

    def act_spec(width):
        return pl.BlockSpec((tm, width), lambda i, j: (i, 0))

    def w_spec(width):
        return pl.BlockSpec((width, tn), lambda i, j: (0, j))

    return pl.pallas_call(
        _merge_kernel,
        grid=(m // tm, D_MODEL // tn),
        in_specs=[act_spec(MOBA_WIDTH), act_spec(NSA_WIDTH), act_spec(GLA_WIDTH),
                  w_spec(MOBA_WIDTH), w_spec(NSA_WIDTH), w_spec(GLA_WIDTH),
                  gate_spec(0), gate_spec(1), gate_spec(2)],
        out_specs=pl.BlockSpec((tm, tn), lambda i, j: (i, j)),
        out_shape=jax.ShapeDtypeStruct((m, D_MODEL), BF16),
        compiler_params=_params("parallel", "parallel"),
        name="merge",
    )(o_m, o_n, o_g, w_m, w_n, w_g, proj2d, proj2d, proj2d)


def _first_max(vals, lane):
    top = jnp.max(vals, axis=-1, keepdims=True)
    idx = jnp.min(jnp.where(vals == top, lane, float(ROUTER_LANES)), axis=-1, keepdims=True)
    return top, idx


def _router_kernel(x_ref, g_ref, w_ref, b_ref, h_ref, route_ref):
    h = _rms(x_ref[...], g_ref[...])
    h_ref[...] = h.astype(h_ref.dtype)
    h1 = h.astype(BF16)
    r1 = h - h1.astype(F32)
    h2 = r1.astype(BF16)
    h3 = (r1 - h2.astype(F32)).astype(BF16)
    hi, lo = slice(0, ROUTER_LANES), slice(ROUTER_LANES, 2 * ROUTER_LANES)
    a = _dot(h1, w_ref[...])
    b = _dot(h2, w_ref[...])
    logits = (a[:, hi] + (a[:, lo] + b[:, hi]) + (b[:, lo] + _dot(h3, w_ref[:, hi]))) + b_ref[...]

    lane = lax.broadcasted_iota(I32, logits.shape, 1).astype(F32)
    g_logits = jnp.where(lane < MOE_GROUPS, logits, -jnp.inf)
    g_top, grp = _first_max(g_logits, lane)
    p_grp = 1.0 / jnp.sum(jnp.exp(g_logits - g_top), axis=-1, keepdims=True)
    first = MOE_GROUPS + grp * MOE_EXPERTS_PER_GROUP
    e_logits = jnp.where((lane >= first) & (lane < first + MOE_EXPERTS_PER_GROUP), logits, -jnp.inf)
    top1, lane1 = _first_max(e_logits, lane)
    top2, lane2 = _first_max(jnp.where(lane == lane1, -jnp.inf, e_logits), lane)
    ratio = jnp.exp(top2 - top1)
    w_first = p_grp / (1.0 + ratio)
    route = jnp.where(lane == 0, lane1 - MOE_GROUPS, jnp.where(lane == 1, lane2 - MOE_GROUPS,
                      jnp.where(lane == 2, w_first, jnp.where(lane == 3, w_first * ratio, 0.0))))
    route_ref[...] = route


def router(x2d, norm_g, rg_w, rg_b, re_w, re_b, tm=512):
    m, d = x2d.shape
    tm = min(tm, m)
    n_real = MOE_GROUPS + MOE_EXPERTS
    w = jnp.zeros((d, ROUTER_LANES), F32).at[:, :n_real].set(jnp.concatenate([rg_w, re_w], axis=1))
    w_hi = w.astype(BF16)
    w_lo = (w - w_hi.astype(F32)).astype(BF16)
    bias = jnp.zeros((1, ROUTER_LANES), F32).at[0, :n_real].set(jnp.concatenate([rg_b, re_b]))
    return pl.pallas_call(
        _router_kernel,
        grid=(m // tm,),
        in_specs=[pl.BlockSpec((tm, d), lambda i: (i, 0)), pl.BlockSpec((1, d), lambda i: (0, 0)),
                  pl.BlockSpec((d, 2 * ROUTER_LANES), lambda i: (0, 0)),
                  pl.BlockSpec((1, ROUTER_LANES), lambda i: (0, 0))],
        out_specs=[pl.BlockSpec((tm, d), lambda i: (i, 0)), pl.BlockSpec((tm, ROUTER_LANES), lambda i: (i, 0))],
        out_shape=[jax.ShapeDtypeStruct((m, d), BF16), jax.ShapeDtypeStruct((m, ROUTER_LANES), F32)],
        compiler_params=_params("parallel"),
        name="router",
    )(x2d, norm_g.reshape(1, d).astype(F32), jnp.concatenate([w_hi, w_lo], axis=1), bias)


def _expert_kernel(blk_e_ref, n_used_ref, x_ref, wg_ref, wu_ref, wd_ref, o_ref, wg_bf, wu_bf, wd_bf):
    i = pl.program_id(0)
    new_expert = (i == 0) | (blk_e_ref[i] != blk_e_ref[jnp.maximum(i - 1, 0)])

    @pl.when(new_expert)
    def _():
        wg_bf[...] = wg_ref[...].astype(BF16)
        wu_bf[...] = wu_ref[...].astype(BF16)
        wd_bf[...] = wd_ref[...].astype(BF16)

    @pl.when(i < n_used_ref[0])
    def _():
        x = x_ref[...]
        gate = _dot(x, wg_bf[...])
        hid = gate * jax.nn.sigmoid(gate) * _dot(x, wu_bf[...])
        o_ref[...] = _dot(hid.astype(BF16), wd_bf[...]).astype(o_ref.dtype)

    @pl.when(i >= n_used_ref[0])
    def _():
        o_ref[...] = jnp.zeros_like(o_ref)


def expert_blocks(xs, blk_e, n_used, layer, w_gate, w_up, w_down):
    p, d = xs.shape
    ff = w_gate.shape[3]
    n_blk = p // MOE_ROWS
    grid_spec = pltpu.PrefetchScalarGridSpec(
        num_scalar_prefetch=2,
        grid=(n_blk,),
        in_specs=[pl.BlockSpec((MOE_ROWS, d), lambda i, e, n: (i, 0)),
                  pl.BlockSpec((None, None, d, ff), lambda i, e, n: (layer, e[i], 0, 0)),
                  pl.BlockSpec((None, None, d, ff), lambda i, e, n: (layer, e[i], 0, 0)),
                  pl.BlockSpec((None, None, ff, d), lambda i, e, n: (layer, e[i], 0, 0))],
        out_specs=pl.BlockSpec((MOE_ROWS, d), lambda i, e, n: (i, 0)),
        scratch_shapes=[pltpu.VMEM((d, ff), BF16), pltpu.VMEM((d, ff), BF16), pltpu.VMEM((ff, d), BF16)],
    )
    return pl.pallas_call(
        _expert_kernel,
        grid_spec=grid_spec,
        out_shape=jax.ShapeDtypeStruct((p, d), BF16),
        compiler_params=_params("arbitrary"),
        name="experts",
    )(blk_e, n_used, xs, w_gate, w_up, w_down)


def _combine_kernel(x_ref, y0_ref, y1_ref, w_ref, g_ref, o_ref, *, final_norm):
    w = w_ref[...]
    x = x_ref[...] + (w[:, 0:1] * y0_ref[...].astype(F32) + w[:, 1:2] * y1_ref[...].astype(F32))
    o_ref[...] = _rms(x, g_ref[...]) if final_norm else x


def combine(x2d, y0, y1, w, norm_g, final_norm, tm=512):
    m, d = x2d.shape
    tm = min(tm, m)
    row = pl.BlockSpec((tm, d), lambda i: (i, 0))
    return pl.pallas_call(
        functools.partial(_combine_kernel, final_norm=final_norm),
        grid=(m // tm,),
        in_specs=[row, row, row, pl.BlockSpec((tm, MOE_TOPK), lambda i: (i, 0)),
                  pl.BlockSpec((1, d), lambda i: (0, 0))],
        out_specs=row,
        out_shape=jax.ShapeDtypeStruct((m, d), F32),
        compiler_params=_params("parallel"),
        name="combine",
    )(x2d, y0, y1, w, norm_g.reshape(1, d).astype(F32))


def hier_moe(x2d, norm_g, rg_w, rg_b, re_w, re_b, layer, w_gate, w_up, w_down, out_norm_g, final_norm):
    t = x2d.shape[0]
    h, route = router(x2d, norm_g, rg_w, rg_b, re_w, re_b)
    expert = route[:, 0:MOE_TOPK].astype(I32)
    w = route[:, MOE_TOPK:2 * MOE_TOPK]

    a = t * MOE_TOPK
    e_flat = expert.reshape(a)
    onehot = (e_flat[:, None] == jnp.arange(MOE_EXPERTS, dtype=I32)[None, :]).astype(I32)
    running = jnp.cumsum(onehot, axis=0)
    counts = running[-1]
    rank = jnp.take_along_axis(running, e_flat[:, None], axis=1)[:, 0] - 1
    padded = (counts + MOE_ROWS - 1) // MOE_ROWS * MOE_ROWS
    pends = jnp.cumsum(padded)
    dest = (pends - padded)[e_flat] + rank
    p_rows = (a + MOE_EXPERTS * (MOE_ROWS - 1)) // MOE_ROWS * MOE_ROWS
    n_blk = p_rows // MOE_ROWS
    row_tok = (jnp.arange(p_rows, dtype=I32) % t).at[dest].set(jnp.arange(a, dtype=I32) // MOE_TOPK)
    blk_start = jnp.arange(n_blk, dtype=I32) * MOE_ROWS
    blk_e = jnp.minimum(jnp.sum((pends[None, :] <= blk_start[:, None]).astype(I32), axis=1), MOE_EXPERTS - 1)
    n_used = (pends[-1] // MOE_ROWS).astype(I32).reshape(1)

    xs = jnp.take(h, row_tok, axis=0)
    y_rows = expert_blocks(xs, blk_e, n_used, layer, w_gate, w_up, w_down)
    dest2 = dest.reshape(t, MOE_TOPK)
    y0 = jnp.take(y_rows, dest2[:, 0], axis=0)
    y1 = jnp.take(y_rows, dest2[:, 1], axis=0)
    return combine(x2d, y0, y1, w, out_norm_g, final_norm)


PERM_TILE = 512
PERM_REGIONS = ((0, 0, C_GQ // PERM_TILE),
                (_SRC_GQ - C_GQ, C_GQ // PERM_TILE, C_GG // PERM_TILE),
                (_SRC_GG - C_GG, C_GG // PERM_TILE, C_SMALL // PERM_TILE))


def _permute_kernel(src_ref, small_ref, o_ref):
    @pl.when(pl.program_id(1) < C_SMALL // PERM_TILE)
    def _():
        o_ref[...] = src_ref[0].T.astype(o_ref.dtype)

    @pl.when(pl.program_id(1) == C_SMALL // PERM_TILE)
    def _():
        o_ref[:, 0:LANES] = small_ref[...].astype(o_ref.dtype)


def permute_w_in(w_in, layer, tk=D_MODEL):
    assert C_GQ % PERM_TILE == 0 and C_GG % PERM_TILE == 0 and C_SMALL % PERM_TILE == 0
    assert all(shift % SUBLANES == 0 for shift, _, _ in PERM_REGIONS)
    d = w_in.shape[1]
    w_t = jnp.swapaxes(w_in, 1, 2)
    small = jnp.concatenate([w_in[layer, :, _SRC_GA:_SRC_GG], w_in[layer, :, _SRC_NG:_SRC_GQ],
                             jnp.zeros((d, LANES - GLA_RANK - 3 * NSA_HEADS), w_in.dtype)], axis=1)

    def src_row(j):
        shift = sum(jnp.where((j >= j0) & (j < j1), s, 0) for s, j0, j1 in PERM_REGIONS)
        return pl.multiple_of(jnp.minimum(j * PERM_TILE + shift, _SRC_END - PERM_TILE), SUBLANES)

    return pl.pallas_call(
        _permute_kernel,
        grid=(d // tk, pl.cdiv(PROJ_WIDTH, PERM_TILE)),
        in_specs=[pl.BlockSpec((pl.Element(1), pl.Element(PERM_TILE), pl.Element(tk)),
                               lambda k, j: (layer, src_row(j), k * tk)),
                  pl.BlockSpec((tk, LANES), lambda k, j: (k, 0))],
        out_specs=pl.BlockSpec((tk, PERM_TILE), lambda k, j: (k, j)),
        out_shape=jax.ShapeDtypeStruct((d, PROJ_WIDTH), BF16),
        compiler_params=_params("parallel", "parallel"),
        name="permute_w_in",
    )(w_t, small)


def hybrid_layer(x, norm1_g, w_in, nsa_cmp_pe, nsa_cmp_w1, nsa_cmp_w2, gla_wa, gla_ba, gla_norm_g,
                 w_br_moba, w_br_nsa, w_br_gla, w_out, norm2_g, router_group_w, router_group_b,
                 router_expert_w, router_expert_b, layer, expert_w_gate, expert_w_up, expert_w_down,
                 out_norm_g, final_norm):
    batch, seq, d = x.shape
    t = batch * seq
    x2d = x.reshape(t, d)
    proj2d = norm_matmul(x2d, norm1_g, permute_w_in(w_in, layer), BF16)
    proj = proj2d.reshape(batch, seq, PROJ_WIDTH)
    o_m = moba_attention(proj, batch, seq)
    kv_c, kv_ct = nsa_compress(proj, nsa_cmp_pe, nsa_cmp_w1, nsa_cmp_w2, batch, seq)
    o_n = nsa_attention(proj, kv_c, kv_ct, batch, seq)
    o_g = gla_attention(proj, gla_wa, gla_ba, gla_norm_g, batch, seq)
    merged = merge_branches(o_m.reshape(t, -1), o_n.reshape(t, -1), o_g.reshape(t, -1),
                            w_br_moba.astype(BF16), w_br_nsa.astype(BF16), w_br_gla.astype(BF16), proj2d)
    x2d = matmul_residual(merged, w_out.astype(BF16), x2d)
    x2d = hier_moe(x2d, norm2_g, router_group_w, router_group_b, router_expert_w, router_expert_b,
                   layer, expert_w_gate, expert_w_up, expert_w_down, out_norm_g, final_norm)
    return x2d.reshape(batch, seq, d)


def kernel(x, norm1_g, w_in, nsa_cmp_pe, nsa_cmp_w1, nsa_cmp_w2, gla_wa, gla_ba, gla_norm_g, w_br_moba,
           w_br_nsa, w_br_gla, w_out, norm2_g, router_group_w, router_group_b, router_expert_w,
           router_expert_b, expert_w_gate, expert_w_up, expert_w_down, final_norm_g):
    for l in range(DEPTH):
        x = hybrid_layer(x, norm1_g[l], w_in, nsa_cmp_pe[l], nsa_cmp_w1[l], nsa_cmp_w2[l], gla_wa[l],
                         gla_ba[l], gla_norm_g[l], w_br_moba[l], w_br_nsa[l], w_br_gla[l], w_out[l],
                         norm2_g[l], router_group_w[l], router_group_b[l], router_expert_w[l],
                         router_expert_b[l], l, expert_w_gate, expert_w_up, expert_w_down,
                         final_norm_g, l == DEPTH - 1)
    return x
```

```python
import functools

import jax
import jax.numpy as jnp
from jax import lax
from jax.experimental import pallas as pl
from jax.experimental.pallas import tpu as pltpu

F32 = jnp.float32
BF16 = jnp.bfloat16
I32 = jnp.int32

D_MODEL = 2048
DEPTH = 2
HEAD_DIM = 128
NEG = -1e30
FORCE = 1e9
EPS = 1e-6
LOG2E = 1.4426950408889634

MOBA_HEADS = 8
MOBA_BLOCK = 256
MOBA_TOPK = 3
MOBA_WIDTH = MOBA_HEADS * HEAD_DIM

NSA_HEADS = 8
NSA_KV_HEADS = 2
NSA_GROUP = NSA_HEADS // NSA_KV_HEADS
NSA_CMP_LEN = 32
NSA_CMP_STRIDE = 16
NSA_SLC_BLOCK = 64
NSA_TOPN = 8
NSA_WINDOW = 512
NSA_WIDTH = NSA_HEADS * HEAD_DIM
NSA_KV_WIDTH = NSA_KV_HEADS * HEAD_DIM

GLA_HEADS = 4
GLA_DK = 128
GLA_DV = 256
GLA_RANK = 16
GLA_TAU = 16.0
GLA_KEY_WIDTH = GLA_HEADS * GLA_DK
GLA_WIDTH = GLA_HEADS * GLA_DV

N_BRANCH = 3
MOE_GROUPS = 4
MOE_EXPERTS_PER_GROUP = 8
MOE_EXPERTS = MOE_GROUPS * MOE_EXPERTS_PER_GROUP
MOE_TOPK = 2
MOE_FF = D_MODEL // 4

LANES = 128
SUBLANES = 8
BF16_SUBLANES = 16
VMEM_LIMIT_BYTES = 56 * 1024 * 1024

_SRC_NG = MOBA_WIDTH * 3 + NSA_WIDTH + 6 * NSA_KV_WIDTH
_SRC_GQ = _SRC_NG + 3 * NSA_HEADS
_SRC_GA = _SRC_GQ + 2 * GLA_KEY_WIDTH + GLA_WIDTH
_SRC_GG = _SRC_GA + GLA_RANK
_SRC_END = _SRC_GG + GLA_WIDTH + N_BRANCH * D_MODEL

C_MQ = 0
C_MK = C_MQ + MOBA_WIDTH
C_MV = C_MK + MOBA_WIDTH
C_NQ = C_MV + MOBA_WIDTH
C_NKV = C_NQ + NSA_WIDTH
C_GQ = C_NKV + 6 * NSA_KV_WIDTH
C_GK = C_GQ + GLA_KEY_WIDTH
C_GV = C_GK + GLA_KEY_WIDTH
C_GG = C_GV + GLA_WIDTH
C_MG = C_GG + GLA_WIDTH
C_SMALL = C_MG + N_BRANCH * D_MODEL
PROJ_WIDTH = C_SMALL + LANES
SMALL_GA_LANE = 0
SMALL_NG_LANE = GLA_RANK

ROUTER_LANES = LANES
MOE_ROWS = 256


def _params(*semantics):
    return pltpu.CompilerParams(dimension_semantics=semantics, vmem_limit_bytes=VMEM_LIMIT_BYTES)


def _dot(a, b):
    return jnp.dot(a, b, preferred_element_type=F32)


def _dot_nt(a, b):
    return lax.dot_general(a, b, (((1,), (1,)), ((), ())), preferred_element_type=F32)


def _dot_split3(a01, x):
    x1 = x.astype(BF16)
    r1 = x - x1.astype(F32)
    x2 = r1.astype(BF16)
    x3 = (r1 - x2.astype(F32)).astype(BF16)
    return _dot(a01, x1) + _dot(a01, x2) + _dot(a01, x3)


def _rms(x, g):
    return x * lax.rsqrt(jnp.mean(x * x, axis=-1, keepdims=True) + EPS) * g


def _norm_matmul_kernel(x_ref, g_ref, b_ref, o_ref, h_ref):
    @pl.when(pl.program_id(1) == 0)
    def _():
        h_ref[...] = _rms(x_ref[...], g_ref[...]).astype(h_ref.dtype)

    o_ref[...] = _dot(h_ref[...], b_ref[...]).astype(o_ref.dtype)


def norm_matmul(x, g, b, out_dtype, tm=1024, tn=1152):
    m, k = x.shape
    n = b.shape[1]
    tm, tn = min(tm, m), min(tn, n)
    return pl.pallas_call(
        _norm_matmul_kernel,
        grid=(m // tm, n // tn),
        in_specs=[pl.BlockSpec((tm, k), lambda i, j: (i, 0)), pl.BlockSpec((1, k), lambda i, j: (0, 0)),
                  pl.BlockSpec((k, tn), lambda i, j: (0, j))],
        out_specs=pl.BlockSpec((tm, tn), lambda i, j: (i, j)),
        out_shape=jax.ShapeDtypeStruct((m, n), out_dtype),
        scratch_shapes=[pltpu.VMEM((tm, k), BF16)],
        compiler_params=_params("parallel", "arbitrary"),
        name="norm_matmul",
    )(x, g.reshape(1, k).astype(F32), b)


def _matmul_residual_kernel(a_ref, b_ref, r_ref, o_ref):
    o_ref[...] = r_ref[...] + _dot(a_ref[...], b_ref[...])


def matmul_residual(a, b, res, tm=1024, tn=1024):
    m, k = a.shape
    n = b.shape[1]
    tm, tn = min(tm, m), min(tn, n)
    return pl.pallas_call(
        _matmul_residual_kernel,
        grid=(m // tm, n // tn),
        in_specs=[pl.BlockSpec((tm, k), lambda i, j: (i, 0)), pl.BlockSpec((k, tn), lambda i, j: (0, j)),
                  pl.BlockSpec((tm, tn), lambda i, j: (i, j))],
        out_specs=pl.BlockSpec((tm, tn), lambda i, j: (i, j)),
        out_shape=jax.ShapeDtypeStruct((m, n), F32),
        compiler_params=_params("parallel", "parallel"),
        name="matmul_residual",
    )(a, b, res)


def _rank_desc_rows(vals, n_candidates):
    row = lax.broadcasted_iota(I32, vals.shape, 0)
    rank = jnp.zeros(vals.shape, F32)
    for m in range(n_candidates):
        vm = vals[m:m + 1, :]
        beats = (vm > vals) | ((vm == vals) & (row > m))
        rank = rank + beats.astype(F32)
    return rank


def _moba_kernel(q_ref, k_ref, vt_ref, o_ref, kmean_ref, bias_ref, s_ref, p_ref, acc_ref, bmax_ref, m_ref, *,
                 n_blocks, topk, heads):
    blk = MOBA_BLOCK
    cur = pl.program_id(2)
    scale = HEAD_DIM ** -0.5
    cols = [slice(h * HEAD_DIM, (h + 1) * HEAD_DIM) for h in range(heads)]

    @pl.when(cur == 0)
    def _():
        kmean_ref[...] = jnp.zeros_like(kmean_ref)
        for h in range(heads):
            for n in range(n_blocks):
                k_blk = k_ref[n * blk:(n + 1) * blk, cols[h]].astype(F32)
                kmean_ref[h, n:n + 1, :] = jnp.sum(k_blk, axis=0, keepdims=True) * (1.0 / blk)

    for h in range(heads):
        gate = _dot_nt(kmean_ref[h].astype(BF16), q_ref[:, cols[h]])
        row = lax.broadcasted_iota(I32, gate.shape, 0)
        valid = row < cur
        gate = jnp.where(valid, gate, NEG)
        chosen = valid & (_rank_desc_rows(gate, n_blocks) < topk)
        bias_ref[h] = jnp.where(chosen, 0.0, NEG)

    seq = k_ref.shape[0]
    lo_blocks = (n_blocks + 1) // 2
    lo = lo_blocks * blk
    need_hi = cur >= lo_blocks

    def score_blocks(b0, b1):
        for h in range(heads):
            s = _dot_nt(k_ref[b0 * blk:b1 * blk, cols[h]], q_ref[:, cols[h]])
            s_ref[h, b0 * blk:b1 * blk, :] = s
            for n in range(b0, b1):
                s_blk = s[(n - b0) * blk:(n - b0 + 1) * blk]
                bmax_ref[h, n] = jnp.max(s_blk.reshape(blk // SUBLANES, SUBLANES, blk), axis=0)

    def past_max(h, b0, b1):
        return functools.reduce(jnp.maximum, [bmax_ref[h, n] * scale + bias_ref[h, n:n + 1, :]
                                              for n in range(b0, b1)])

    score_blocks(0, lo_blocks)
    if lo < seq:
        pl.when(need_hi)(lambda: score_blocks(lo_blocks, n_blocks))

    sub = LANES
    base = pl.multiple_of(cur * blk, blk)
    qry_i = lax.broadcasted_iota(I32, (sub, blk), 1)
    key_i = [j * sub + lax.broadcasted_iota(I32, (sub, blk), 0) for j in range(blk // sub)]
    exp_scale = scale * LOG2E

    def fold_max(x):
        return jnp.max(x.reshape(sub // SUBLANES, SUBLANES, blk), axis=0)

    def tiles(h, start):
        return [s_ref[h, pl.ds(pl.multiple_of(start + j * sub, sub), sub), :] for j in range(blk // sub)]

    for h in range(heads):
        own = [fold_max(jnp.where(key_i[j] <= qry_i, t, NEG)) for j, t in enumerate(tiles(h, base))]
        m_ref[h] = jnp.maximum(functools.reduce(jnp.maximum, own) * scale, past_max(h, 0, lo_blocks))

    if lo < seq:
        @pl.when(need_hi)
        def _():
            for h in range(heads):
                m_ref[h] = jnp.maximum(m_ref[h], past_max(h, lo_blocks, n_blocks))

    m_log2 = [jnp.max(m_ref[h], axis=0, keepdims=True) * LOG2E for h in range(heads)]

    def fold_sum(x):
        return jnp.sum(x.reshape(sub // SUBLANES, SUBLANES, blk), axis=0)

    def store_p(h, start, j, p):
        p_ref[h, pl.ds(pl.multiple_of(start + j * sub, sub), sub), :] = p.astype(p_ref.dtype)

    def prob_body(n, l8):
        out = []
        for h in range(heads):
            shift = bias_ref[h, pl.ds(n, 1), :] * LOG2E - m_log2[h]
            acc = l8[h]
            for j, t in enumerate(tiles(h, n * blk)):
                p = jnp.exp2(t * exp_scale + shift)
                store_p(h, n * blk, j, p)
                acc = acc + fold_sum(p)
            out.append(acc)
        return tuple(out)

    l8 = []
    for h in range(heads):
        acc = jnp.zeros((SUBLANES, blk), F32)
        for j, t in enumerate(tiles(h, base)):
            p = jnp.where(key_i[j] <= qry_i, jnp.exp2(t * exp_scale - m_log2[h]), 0.0)
            store_p(h, base, j, p)
            acc = acc + fold_sum(p)
        l8.append(acc)
    l8 = lax.fori_loop(0, cur, prob_body, tuple(l8))

    def zero_body(n, carry):
        for h in range(heads):
            for j in range(blk // sub):
                store_p(h, n * blk, j, jnp.zeros((sub, blk), F32))
        return carry

    lax.fori_loop(cur + 1, jnp.where(need_hi, n_blocks, lo_blocks), zero_body, 0)
    for h in range(heads):
        acc_ref[h] = _dot(vt_ref[h, :, 0:lo], p_ref[h, 0:lo, :])

    if lo < seq:
        @pl.when(need_hi)
        def _():
            for h in range(heads):
                acc_ref[h] += _dot(vt_ref[h, :, lo:seq], p_ref[h, lo:seq, :])

    for h in range(heads):
        l_sum = jnp.sum(l8[h], axis=0, keepdims=True)
        o_ref[:, cols[h]] = (acc_ref[h] / l_sum).T.astype(o_ref.dtype)


def moba_attention(proj, batch, seq, heads=8):
    assert seq % MOBA_BLOCK == 0 and MOBA_HEADS % heads == 0
    n_blocks = seq // MOBA_BLOCK
    rows = -(-n_blocks // BF16_SUBLANES) * BF16_SUBLANES
    width = heads * HEAD_DIM
    qb, kb = C_MQ // width, C_MK // width
    v_t = proj[:, :, C_MV:C_MV + MOBA_WIDTH].reshape(batch, seq, MOBA_HEADS, HEAD_DIM).transpose(0, 2, 3, 1)
    kern = functools.partial(_moba_kernel, n_blocks=n_blocks, topk=min(MOBA_TOPK, n_blocks), heads=heads)
    return pl.pallas_call(
        kern,
        grid=(batch, MOBA_HEADS // heads, n_blocks),
        in_specs=[pl.BlockSpec((None, MOBA_BLOCK, width), lambda b, h, i: (b, i, qb + h)),
                  pl.BlockSpec((None, seq, width), lambda b, h, i: (b, 0, kb + h)),
                  pl.BlockSpec((None, heads, HEAD_DIM, seq), lambda b, h, i: (b, h, 0, 0))],
        out_specs=pl.BlockSpec((None, MOBA_BLOCK, width), lambda b, h, i: (b, i, h)),
        out_shape=jax.ShapeDtypeStruct((batch, seq, MOBA_WIDTH), BF16),
        scratch_shapes=[pltpu.VMEM((heads, rows, HEAD_DIM), F32), pltpu.VMEM((heads, rows, MOBA_BLOCK), F32),
                        pltpu.VMEM((heads, seq, MOBA_BLOCK), F32), pltpu.VMEM((heads, seq, MOBA_BLOCK), BF16),
                        pltpu.VMEM((heads, HEAD_DIM, MOBA_BLOCK), F32),
                        pltpu.VMEM((heads, n_blocks, SUBLANES, MOBA_BLOCK), F32),
                        pltpu.VMEM((heads, SUBLANES, MOBA_BLOCK), F32)],
        compiler_params=_params("parallel", "parallel", "arbitrary"),
        name="moba",
    )(proj, proj, v_t)


def _nsa_compress_kernel(x_ref, pe_ref, w1_ref, w2_ref, o_ref, ot_ref):
    x = x_ref[...].astype(F32)
    half = x.shape[1]
    lo = _dot((x + pe_ref[:, :half]).astype(BF16), w1_ref[:half, :])
    hi = _dot((x + pe_ref[:, half:]).astype(BF16), w1_ref[half:, :])
    pre = lo + pltpu.roll(hi, hi.shape[0] - 1, 0)
    hid = pre * jax.nn.sigmoid(pre)
    out = _dot(hid.astype(BF16), w2_ref[...])
    o_ref[...] = out.astype(o_ref.dtype)
    ot_ref[...] = out.T.astype(ot_ref.dtype)


def nsa_compress(proj, cmp_pe, cmp_w1, cmp_w2, batch, seq):
    n16 = seq // NSA_CMP_STRIDE
    width = NSA_CMP_STRIDE * HEAD_DIM
    x = proj[:, :, C_NKV:C_NKV + 2 * NSA_KV_WIDTH].reshape(batch, seq, 2, NSA_KV_HEADS, HEAD_DIM)
    x = x.transpose(0, 2, 3, 1, 4).reshape(batch, 2, NSA_KV_HEADS, n16, width)
    pe = cmp_pe.reshape(2, 1, NSA_CMP_LEN * HEAD_DIM).astype(F32)
    return pl.pallas_call(
        _nsa_compress_kernel,
        grid=(batch, 2, NSA_KV_HEADS),
        in_specs=[pl.BlockSpec((None, None, None, n16, width), lambda b, c, g: (b, c, g, 0, 0)),
                  pl.BlockSpec((None, 1, 2 * width), lambda b, c, g: (c, 0, 0)),
                  pl.BlockSpec((None, 2 * width, HEAD_DIM), lambda b, c, g: (c, 0, 0)),
                  pl.BlockSpec((None, HEAD_DIM, HEAD_DIM), lambda b, c, g: (c, 0, 0))],
        out_specs=[pl.BlockSpec((None, None, None, n16, HEAD_DIM), lambda b, c, g: (b, c, g, 0, 0)),
                   pl.BlockSpec((None, None, None, HEAD_DIM, n16), lambda b, c, g: (b, c, g, 0, 0))],
        out_shape=[jax.ShapeDtypeStruct((batch, 2, NSA_KV_HEADS, n16, HEAD_DIM), BF16),
                   jax.ShapeDtypeStruct((batch, 2, NSA_KV_HEADS, HEAD_DIM, n16), BF16)],
        compiler_params=_params("parallel", "parallel", "parallel"),
        name="nsa_compress",
    )(x, pe, cmp_w1.astype(BF16), cmp_w2.astype(BF16))


def _nsa_group(g, qi, q_ref, kc_ref, vct_ref, ks_ref, vst_ref, kw_ref, vwt_ref, ngt_ref, o_ref,
               bias_ref, s_ref, p_ref, sw_ref, pw_ref, out_ref, bmax_ref, *, tq, span, n_cmp, n_slc, topn,
               win_len):
    seq = ks_ref.shape[0]
    n16 = kc_ref.shape[0]
    lanes = NSA_GROUP * tq
    blk = NSA_SLC_BLOCK
    scale = HEAD_DIM ** -0.5
    exp_scale = scale * LOG2E
    start = qi * tq
    heads = [slice(r * tq, (r + 1) * tq) for r in range(NSA_GROUP)]
    q = [q_ref[:, r * HEAD_DIM:(r + 1) * HEAD_DIM] for r in range(NSA_GROUP)]

    def per_group(row):
        return jnp.concatenate([row] * NSA_GROUP, axis=1)

    def scores(keys):
        return jnp.concatenate([_dot_nt(keys, q[r]) for r in range(NSA_GROUP)], axis=1)

    def fold(op, x):
        return op(x.reshape(x.shape[0] // SUBLANES, SUBLANES, lanes), axis=0)

    pos1 = start + lax.broadcasted_iota(I32, (1, tq), 1)
    pos = per_group(pos1)

    n_idx = lax.broadcasted_iota(I32, (n16, lanes), 0)
    in_range = n_idx < n_cmp
    cmask = (n_idx * NSA_CMP_STRIDE + (NSA_CMP_LEN - 1) <= pos) & in_range
    s_c = jnp.where(cmask, scores(kc_ref[...]) * scale, NEG)
    e_c = jnp.where(in_range, jnp.exp(s_c - jnp.max(s_c, axis=0, keepdims=True)), 0.0)
    p_c = jnp.where(cmask, e_c / jnp.sum(e_c, axis=0, keepdims=True), 0.0)
    out_ref[0] = _dot(vct_ref[...], p_c.astype(BF16))

    p_sum = functools.reduce(lambda a, b: a + b, [p_c[:, h] for h in heads])
    rows = bias_ref.shape[0]
    oj = lax.broadcasted_iota(I32, (rows, n16), 0)
    on = lax.broadcasted_iota(I32, (rows, n16), 1)
    overlap_t = ((on * NSA_CMP_STRIDE < (oj + 1) * blk) & (on * NSA_CMP_STRIDE + (NSA_CMP_LEN - 1) >= oj * blk)
                 & (on < n_cmp) & (oj < n_slc)).astype(BF16)
    p_hi = p_sum.astype(BF16)
    p_lo = (p_sum - p_hi.astype(F32)).astype(BF16)
    imp = _dot(overlap_t, p_hi) + _dot(overlap_t, p_lo)
    j_idx = lax.broadcasted_iota(I32, (rows, tq), 0)
    cur_blk = pos1 // blk
    forced = (j_idx == 0) | (j_idx == cur_blk) | (j_idx == cur_blk - 1)
    imp = jnp.where(forced, FORCE, imp)
    imp = jnp.where(j_idx > cur_blk, NEG, imp)
    chosen = (_rank_desc_rows(imp, n_slc) < topn) & (j_idx <= cur_blk)
    bias_ref[...] = jnp.where(chosen, 0.0, NEG)

    n_spans = seq // span
    span_blocks = span // blk

    def span_scores(k):
        s = scores(ks_ref[k * span:(k + 1) * span, :])
        s_ref[k * span:(k + 1) * span, :] = s
        for j in range(span_blocks):
            bmax_ref[k * span_blocks + j] = fold(jnp.max, s[j * blk:(j + 1) * blk])

    span_scores(0)
    w0 = pl.multiple_of(jnp.maximum(start + tq - win_len, 0), tq)
    sw_ref[...] = scores(kw_ref[pl.ds(w0, win_len), :])
    for k in range(1, n_spans):
        pl.when(start >= k * span)(functools.partial(span_scores, k))

    per_tile = tq // blk
    first_own = qi * per_tile

    def block_bias(j):
        return per_group(bias_ref[pl.ds(j, 1), :])

    def block_rows(j):
        return pl.ds(pl.multiple_of(j * blk, blk), blk)

    def causal_mask(j):
        key = j * blk + lax.broadcasted_iota(I32, (blk, lanes), 0)
        return key <= pos

    def max_body(i, m8):
        for d in range(per_tile):
            j = i * per_tile + d
            m8 = jnp.maximum(m8, bmax_ref[j] * scale + block_bias(j))
        return m8

    m8 = lax.fori_loop(0, qi, max_body, jnp.full((SUBLANES, lanes), NEG, F32))
    for d in range(per_tile):
        j = first_own + d
        own = jnp.where(causal_mask(j), s_ref[block_rows(j), :], NEG)
        m8 = jnp.maximum(m8, fold(jnp.max, own) * scale + block_bias(j))
    m_log2 = jnp.max(m8, axis=0, keepdims=True) * LOG2E

    def prob_body(i, l8):
        for d in range(per_tile):
            j = i * per_tile + d
            p = jnp.exp2(s_ref[block_rows(j), :] * exp_scale + (block_bias(j) * LOG2E - m_log2))
            p_ref[block_rows(j), :] = p.astype(p_ref.dtype)
            l8 = l8 + fold(jnp.sum, p)
        return l8

    l8 = lax.fori_loop(0, qi, prob_body, jnp.zeros((SUBLANES, lanes), F32))
    for d in range(per_tile):
        j = first_own + d
        p = jnp.exp2(s_ref[block_rows(j), :] * exp_scale + (block_bias(j) * LOG2E - m_log2))
        p = jnp.where(causal_mask(j), p, 0.0)
        p_ref[block_rows(j), :] = p.astype(p_ref.dtype)
        l8 = l8 + fold(jnp.sum, p)

    def zero_body(i, carry):
        p_ref[pl.ds(pl.multiple_of(i * tq, tq), tq), :] = jnp.zeros((tq, lanes), p_ref.dtype)
        return carry

    visible_tiles = (start // span + 1) * (span // tq)
    lax.fori_loop(qi + 1, visible_tiles, zero_body, 0)

    out_ref[1] = _dot(vst_ref[:, 0:span], p_ref[0:span, :])

    n_tiles = win_len // blk

    def win_tile(t):
        key = w0 + t * blk + lax.broadcasted_iota(I32, (blk, lanes), 0)
        return sw_ref[t * blk:(t + 1) * blk, :], (key <= pos) & (key > pos - NSA_WINDOW)

    wm8 = jnp.full((SUBLANES, lanes), NEG, F32)
    for t in range(n_tiles):
        tile, mask = win_tile(t)
        wm8 = jnp.maximum(wm8, fold(jnp.max, jnp.where(mask, tile, NEG)))
    wm_log2 = jnp.max(wm8, axis=0, keepdims=True) * exp_scale
    wl8 = jnp.zeros((SUBLANES, lanes), F32)
    for t in range(n_tiles):
        tile, mask = win_tile(t)
        p = jnp.where(mask, jnp.exp2(tile * exp_scale - wm_log2), 0.0)
        pw_ref[t * blk:(t + 1) * blk, :] = p.astype(pw_ref.dtype)
        wl8 = wl8 + fold(jnp.sum, p)
    w_tile0 = w0 // tq
    pv = _dot(vwt_ref[w_tile0], pw_ref[0:tq, :])
    for c in range(1, win_len // tq):
        pv = pv + _dot(vwt_ref[w_tile0 + c], pw_ref[c * tq:(c + 1) * tq, :])
    out_ref[2] = pv / jnp.sum(wl8, axis=0, keepdims=True)

    for k in range(1, n_spans):
        def span_pv(k=k):
            out_ref[1] += _dot(vst_ref[:, k * span:(k + 1) * span], p_ref[k * span:(k + 1) * span, :])
        pl.when(start >= k * span)(span_pv)
    out_ref[1] = out_ref[1] / jnp.sum(l8, axis=0, keepdims=True)

    for r in range(NSA_GROUP):
        gate_row = (g * NSA_GROUP + r) * 3
        mix = jnp.zeros((HEAD_DIM, tq), F32)
        for c in range(3):
            mix = mix + jax.nn.sigmoid(ngt_ref[pl.ds(gate_row + c, 1), :]) * out_ref[c, :, heads[r]]
        head = g * NSA_GROUP + r
        o_ref[:, head * HEAD_DIM:(head + 1) * HEAD_DIM] = mix.T.astype(o_ref.dtype)


NSA_GROUP_OPERANDS = 7


def _nsa_kernel(*refs, **params):
    n_in = NSA_KV_HEADS * NSA_GROUP_OPERANDS
    ngt_ref, o_ref = refs[n_in:n_in + 2]
    for g in range(NSA_KV_HEADS):
        _nsa_group(g, pl.program_id(1), *refs[g * NSA_GROUP_OPERANDS:(g + 1) * NSA_GROUP_OPERANDS],
                   ngt_ref, o_ref, *refs[n_in + 2:], **params)


def nsa_attention(proj, kv_c, kv_ct, batch, seq, tq=128):
    assert tq == LANES and seq % tq == 0 and tq % NSA_SLC_BLOCK == 0
    n_cmp = (seq - NSA_CMP_LEN) // NSA_CMP_STRIDE + 1
    n_slc = seq // NSA_SLC_BLOCK
    rows = -(-n_slc // BF16_SUBLANES) * BF16_SUBLANES
    span = max(tq, seq // 4)
    win_len = min(NSA_WINDOW + tq, seq)
    n16 = seq // NSA_CMP_STRIDE
    gw = NSA_GROUP * HEAD_DIM
    lanes = NSA_GROUP * tq
    nkv = C_NKV // HEAD_DIM

    def kv_cols(slot):
        c0 = C_NKV + slot * NSA_KV_WIDTH
        return proj[:, :, c0:c0 + NSA_KV_WIDTH].reshape(batch, seq, NSA_KV_HEADS, HEAD_DIM)

    vs_t = kv_cols(3).transpose(0, 2, 3, 1)
    vw_t = kv_cols(5).reshape(batch, seq // tq, tq, NSA_KV_HEADS, HEAD_DIM).transpose(0, 3, 1, 4, 2)
    n_gates = 3 * NSA_HEADS
    ng_t = proj[:, :, C_SMALL + SMALL_NG_LANE:C_SMALL + SMALL_NG_LANE + n_gates].astype(F32).transpose(0, 2, 1)

    def group_specs(g):
        def k_spec(slot):
            return pl.BlockSpec((None, seq, HEAD_DIM), lambda b, i: (b, 0, nkv + slot * NSA_KV_HEADS + g))

        return [pl.BlockSpec((None, tq, gw), lambda b, i: (b, i, C_NQ // gw + g)),
                pl.BlockSpec((None, None, None, n16, HEAD_DIM), lambda b, i: (b, 0, g, 0, 0)),
                pl.BlockSpec((None, None, None, HEAD_DIM, n16), lambda b, i: (b, 1, g, 0, 0)),
                k_spec(2),
                pl.BlockSpec((None, None, HEAD_DIM, seq), lambda b, i: (b, g, 0, 0)),
                k_spec(4),
                pl.BlockSpec((None, None, seq // tq, HEAD_DIM, tq), lambda b, i: (b, g, 0, 0, 0))]

    kern = functools.partial(_nsa_kernel, tq=tq, span=span, n_cmp=n_cmp, n_slc=n_slc,
                             topn=min(NSA_TOPN, n_slc), win_len=win_len)
    return pl.pallas_call(
        kern,
        grid=(batch, seq // tq),
        in_specs=[spec for g in range(NSA_KV_HEADS) for spec in group_specs(g)]
                 + [pl.BlockSpec((None, n_gates, tq), lambda b, i: (b, 0, i))],
        out_specs=pl.BlockSpec((None, tq, NSA_WIDTH), lambda b, i: (b, i, 0)),
        out_shape=jax.ShapeDtypeStruct((batch, seq, NSA_WIDTH), BF16),
        scratch_shapes=[pltpu.VMEM((rows, tq), F32),
                        pltpu.VMEM((seq, lanes), F32), pltpu.VMEM((seq, lanes), BF16),
                        pltpu.VMEM((win_len, lanes), F32), pltpu.VMEM((win_len, lanes), BF16),
                        pltpu.VMEM((3, HEAD_DIM, lanes), F32), pltpu.VMEM((n_slc, SUBLANES, lanes), F32)],
        compiler_params=_params("parallel", "arbitrary"),
        name="nsa",
    )(*([proj, kv_c, kv_ct, proj, vs_t, proj, vw_t] * NSA_KV_HEADS), ng_t)


GLA_SUB = 8


def _gla_kernel(*refs, chunk):
    q_ref, k_ref = refs[0:2]
    v_refs = refs[2:2 + GLA_HEADS]
    gg_refs = refs[2 + GLA_HEADS:2 + 2 * GLA_HEADS]
    small_ref, wa_ref, ba_ref, ng_ref, o_ref, state_ref, attn_ref = refs[2 + 2 * GLA_HEADS:]
    hs = range(GLA_HEADS)
    keys = [slice(h * GLA_DK, (h + 1) * GLA_DK) for h in hs]

    @pl.when(pl.program_id(1) == 0)
    def _():
        state_ref[...] = jnp.zeros_like(state_ref)

    q = q_ref[...].astype(F32) * (GLA_DK ** -0.5)
    k = k_ref[...].astype(F32)
    z = _dot(small_ref[...], wa_ref[...]) + ba_ref[...]
    log_a = (jnp.minimum(z, 0.0) - jnp.log(1.0 + jnp.exp(-jnp.abs(z)))) * (1.0 / GLA_TAU)
    tri = (lax.broadcasted_iota(I32, (chunk, chunk), 1)
           <= lax.broadcasted_iota(I32, (chunk, chunk), 0)).astype(BF16)
    b = _dot_split3(tri, log_a)

    attn_ref[...] = jnp.zeros_like(attn_ref)
    t_idx = lax.broadcasted_iota(I32, (GLA_SUB, 1), 0)
    s_lane = lax.broadcasted_iota(I32, (GLA_SUB, GLA_SUB), 1)
    for i in range(chunk // GLA_SUB):
        r0 = i * GLA_SUB
        bi, qi, ki = b[r0:r0 + GLA_SUB], q[r0:r0 + GLA_SUB], k[r0:r0 + GLA_SUB]
        diag = [jnp.zeros((GLA_SUB, GLA_SUB), F32) for _ in hs]
        for s in range(GLA_SUB):
            decay = jnp.exp(jnp.where(t_idx >= s, bi - bi[s:s + 1], NEG))
            prod = qi * ki[s:s + 1] * decay
            for h in hs:
                col = jnp.sum(prod[:, keys[h]], axis=-1, keepdims=True)
                diag[h] = jnp.where(s_lane == s, col, diag[h])
        for h in hs:
            attn_ref[h, r0:r0 + GLA_SUB, r0:r0 + GLA_SUB] = diag[h]
        if i > 0:
            ref_b = b[r0:r0 + 1]
            q_dec = (qi * jnp.exp(bi - ref_b)).astype(BF16)
            k_dec = (k[:r0] * jnp.exp(ref_b - b[:r0])).astype(BF16)
            for h in hs:
                attn_ref[h, r0:r0 + GLA_SUB, 0:r0] = _dot_nt(q_dec[:, keys[h]], k_dec[:, keys[h]])

    q_in = (q * jnp.exp(b)).astype(BF16)
    b_last = b[chunk - 1:chunk]
    k_out = (k * jnp.exp(b_last - b)).astype(BF16)
    carry = jnp.exp(b_last)
    for h in hs:
        v = v_refs[h][...]
        state_t = state_ref[h]
        o = _dot(attn_ref[h].astype(BF16), v) + _dot_nt(q_in[:, keys[h]], state_t.astype(BF16))
        state_ref[h] = state_t * carry[:, keys[h]] + _dot(v.astype(F32).T.astype(BF16), k_out[:, keys[h]])
        gate = gg_refs[h][...].astype(F32)
        out = _rms(o, ng_ref[...]) * (gate * jax.nn.sigmoid(gate))
        o_ref[:, h * GLA_DV:(h + 1) * GLA_DV] = out.astype(o_ref.dtype)


def gla_attention(proj, gla_wa, gla_ba, gla_norm_g, batch, seq, chunk=128):
    chunk = min(chunk, seq)
    assert seq % chunk == 0 and chunk % GLA_SUB == 0
    wa = jnp.zeros((LANES, GLA_KEY_WIDTH), BF16).at[SMALL_GA_LANE:SMALL_GA_LANE + GLA_RANK].set(gla_wa.astype(BF16))

    def head_spec(c0, h):
        return pl.BlockSpec((None, chunk, GLA_DV), lambda b, c, h=h: (b, c, c0 // GLA_DV + h))

    const = lambda b, c: (0, 0)
    return pl.pallas_call(
        functools.partial(_gla_kernel, chunk=chunk),
        grid=(batch, seq // chunk),
        in_specs=[pl.BlockSpec((None, chunk, GLA_KEY_WIDTH), lambda b, c: (b, c, C_GQ // GLA_KEY_WIDTH)),
                  pl.BlockSpec((None, chunk, GLA_KEY_WIDTH), lambda b, c: (b, c, C_GK // GLA_KEY_WIDTH)),
                  *[head_spec(C_GV, h) for h in range(GLA_HEADS)],
                  *[head_spec(C_GG, h) for h in range(GLA_HEADS)],
                  pl.BlockSpec((None, chunk, LANES), lambda b, c: (b, c, C_SMALL // LANES)),
                  pl.BlockSpec((LANES, GLA_KEY_WIDTH), const),
                  pl.BlockSpec((1, GLA_KEY_WIDTH), const),
                  pl.BlockSpec((1, GLA_DV), const)],
        out_specs=pl.BlockSpec((None, chunk, GLA_WIDTH), lambda b, c: (b, c, 0)),
        out_shape=jax.ShapeDtypeStruct((batch, seq, GLA_WIDTH), BF16),
        scratch_shapes=[pltpu.VMEM((GLA_HEADS, GLA_DV, GLA_DK), F32), pltpu.VMEM((GLA_HEADS, chunk, chunk), F32)],
        compiler_params=_params("parallel", "arbitrary"),
        name="gla",
    )(proj, proj, *([proj] * (2 * GLA_HEADS)), proj, wa, gla_ba.reshape(1, GLA_KEY_WIDTH).astype(F32),
      gla_norm_g.reshape(1, GLA_DV).astype(F32))


def _merge_kernel(om_ref, on_ref, og_ref, wm_ref, wn_ref, wg_ref, gm_ref, gn_ref, gl_ref, o_ref):
    def gated(gate_ref, a_ref, w_ref):
        return jax.nn.sigmoid(gate_ref[...].astype(F32)) * _dot(a_ref[...], w_ref[...])

    o_ref[...] = (gated(gm_ref, om_ref, wm_ref) + gated(gn_ref, on_ref, wn_ref)
                  + gated(gl_ref, og_ref, wg_ref)).astype(o_ref.dtype)


def merge_branches(o_m, o_n, o_g, w_m, w_n, w_g, proj2d, tm=1024, tn=512):
    m = o_m.shape[0]
    tm = min(tm, m)
    assert C_MG % tn == 0 and D_MODEL % tn == 0

    def gate_spec(c):
        return pl.BlockSpec((tm, tn), lambda i, j, c=c: (i, (C_MG + c * D_MODEL) // tn + j))

    def act_spec(width):
        return pl.BlockSpec((tm, width), lambda i, j: (i, 0))

    def w_spec(width):
        return pl.BlockSpec((width, tn), lambda i, j: (0, j))

    return pl.pallas_call(
        _merge_kernel,
        grid=(m // tm, D_MODEL // tn),
        in_specs=[act_spec(MOBA_WIDTH), act_spec(NSA_WIDTH), act_spec(GLA_WIDTH),
                  w_spec(MOBA_WIDTH), w_spec(NSA_WIDTH), w_spec(GLA_WIDTH),
                  gate_spec(0), gate_spec(1), gate_spec(2)],
        out_specs=pl.BlockSpec((tm, tn), lambda i, j: (i, j)),
        out_shape=jax.ShapeDtypeStruct((m, D_MODEL), BF16),
        compiler_params=_params("parallel", "parallel"),
        name="merge",
    )(o_m, o_n, o_g, w_m, w_n, w_g, proj2d, proj2d, proj2d)


def _first_max(vals, lane):
    top = jnp.max(vals, axis=-1, keepdims=True)
    idx = jnp.min(jnp.where(vals == top, lane, float(ROUTER_LANES)), axis=-1, keepdims=True)
    return top, idx


def _router_kernel(x_ref, g_ref, w_ref, b_ref, h_ref, route_ref):
    h = _rms(x_ref[...], g_ref[...])
    h_ref[...] = h.astype(h_ref.dtype)
    h1 = h.astype(BF16)
    r1 = h - h1.astype(F32)
    h2 = r1.astype(BF16)
    h3 = (r1 - h2.astype(F32)).astype(BF16)
    hi, lo = slice(0, ROUTER_LANES), slice(ROUTER_LANES, 2 * ROUTER_LANES)
    a = _dot(h1, w_ref[...])
    b = _dot(h2, w_ref[...])
    logits = (a[:, hi] + (a[:, lo] + b[:, hi]) + (b[:, lo] + _dot(h3, w_ref[:, hi]))) + b_ref[...]

    lane = lax.broadcasted_iota(I32, logits.shape, 1).astype(F32)
    g_logits = jnp.where(lane < MOE_GROUPS, logits, -jnp.inf)
    g_top, grp = _first_max(g_logits, lane)
    p_grp = 1.0 / jnp.sum(jnp.exp(g_logits - g_top), axis=-1, keepdims=True)
    first = MOE_GROUPS + grp * MOE_EXPERTS_PER_GROUP
    e_logits = jnp.where((lane >= first) & (lane < first + MOE_EXPERTS_PER_GROUP), logits, -jnp.inf)
    top1, lane1 = _first_max(e_logits, lane)
    top2, lane2 = _first_max(jnp.where(lane == lane1, -jnp.inf, e_logits), lane)
    ratio = jnp.exp(top2 - top1)
    w_first = p_grp / (1.0 + ratio)
    route = jnp.where(lane == 0, lane1 - MOE_GROUPS, jnp.where(lane == 1, lane2 - MOE_GROUPS,
                      jnp.where(lane == 2, w_first, jnp.where(lane == 3, w_first * ratio, 0.0))))
    route_ref[...] = route


def router(x2d, norm_g, rg_w, rg_b, re_w, re_b, tm=512):
    m, d = x2d.shape
    tm = min(tm, m)
    n_real = MOE_GROUPS + MOE_EXPERTS
    w = jnp.zeros((d, ROUTER_LANES), F32).at[:, :n_real].set(jnp.concatenate([rg_w, re_w], axis=1))
    w_hi = w.astype(BF16)
    w_lo = (w - w_hi.astype(F32)).astype(BF16)
    bias = jnp.zeros((1, ROUTER_LANES), F32).at[0, :n_real].set(jnp.concatenate([rg_b, re_b]))
    return pl.pallas_call(
        _router_kernel,
        grid=(m // tm,),
        in_specs=[pl.BlockSpec((tm, d), lambda i: (i, 0)), pl.BlockSpec((1, d), lambda i: (0, 0)),
                  pl.BlockSpec((d, 2 * ROUTER_LANES), lambda i: (0, 0)),
                  pl.BlockSpec((1, ROUTER_LANES), lambda i: (0, 0))],
        out_specs=[pl.BlockSpec((tm, d), lambda i: (i, 0)), pl.BlockSpec((tm, ROUTER_LANES), lambda i: (i, 0))],
        out_shape=[jax.ShapeDtypeStruct((m, d), BF16), jax.ShapeDtypeStruct((m, ROUTER_LANES), F32)],
        compiler_params=_params("parallel"),
        name="router",
    )(x2d, norm_g.reshape(1, d).astype(F32), jnp.concatenate([w_hi, w_lo], axis=1), bias)


def _expert_kernel(blk_e_ref, n_used_ref, x_ref, wg_ref, wu_ref, wd_ref, o_ref, wg_bf, wu_bf, wd_bf):
    i = pl.program_id(0)
    new_expert = (i == 0) | (blk_e_ref[i] != blk_e_ref[jnp.maximum(i - 1, 0)])

    @pl.when(new_expert)
    def _():
        wg_bf[...] = wg_ref[...].astype(BF16)
        wu_bf[...] = wu_ref[...].astype(BF16)
        wd_bf[...] = wd_ref[...].astype(BF16)

    @pl.when(i < n_used_ref[0])
    def _():
        x = x_ref[...]
        gate = _dot(x, wg_bf[...])
        hid = gate * jax.nn.sigmoid(gate) * _dot(x, wu_bf[...])
        o_ref[...] = _dot(hid.astype(BF16), wd_bf[...]).astype(o_ref.dtype)

    @pl.when(i >= n_used_ref[0])
    def _():
        o_ref[...] = jnp.zeros_like(o_ref)


def expert_blocks(xs, blk_e, n_used, layer, w_gate, w_up, w_down):
    p, d = xs.shape
    ff = w_gate.shape[3]
    n_blk = p // MOE_ROWS
    grid_spec = pltpu.PrefetchScalarGridSpec(
        num_scalar_prefetch=2,
        grid=(n_blk,),
        in_specs=[pl.BlockSpec((MOE_ROWS, d), lambda i, e, n: (i, 0)),
                  pl.BlockSpec((None, None, d, ff), lambda i, e, n: (layer, e[i], 0, 0)),
                  pl.BlockSpec((None, None, d, ff), lambda i, e, n: (layer, e[i], 0, 0)),
                  pl.BlockSpec((None, None, ff, d), lambda i, e, n: (layer, e[i], 0, 0))],
        out_specs=pl.BlockSpec((MOE_ROWS, d), lambda i, e, n: (i, 0)),
        scratch_shapes=[pltpu.VMEM((d, ff), BF16), pltpu.VMEM((d, ff), BF16), pltpu.VMEM((ff, d), BF16)],
    )
    return pl.pallas_call(
        _expert_kernel,
        grid_spec=grid_spec,
        out_shape=jax.ShapeDtypeStruct((p, d), BF16),
        compiler_params=_params("arbitrary"),
        name="experts",
    )(blk_e, n_used, xs, w_gate, w_up, w_down)


def _combine_kernel(x_ref, y0_ref, y1_ref, w_ref, g_ref, o_ref, *, final_norm):
    w = w_ref[...]
    x = x_ref[...] + (w[:, 0:1] * y0_ref[...].astype(F32) + w[:, 1:2] * y1_ref[...].astype(F32))
    o_ref[...] = _rms(x, g_ref[...]) if final_norm else x


def combine(x2d, y0, y1, w, norm_g, final_norm, tm=512):
    m, d = x2d.shape
    tm = min(tm, m)
    row = pl.BlockSpec((tm, d), lambda i: (i, 0))
    return pl.pallas_call(
        functools.partial(_combine_kernel, final_norm=final_norm),
        grid=(m // tm,),
        in_specs=[row, row, row, pl.BlockSpec((tm, MOE_TOPK), lambda i: (i, 0)),
                  pl.BlockSpec((1, d), lambda i: (0, 0))],
        out_specs=row,
        out_shape=jax.ShapeDtypeStruct((m, d), F32),
        compiler_params=_params("parallel"),
        name="combine",
    )(x2d, y0, y1, w, norm_g.reshape(1, d).astype(F32))


def hier_moe(x2d, norm_g, rg_w, rg_b, re_w, re_b, layer, w_gate, w_up, w_down, out_norm_g, final_norm):
    t = x2d.shape[0]
    h, route = router(x2d, norm_g, rg_w, rg_b, re_w, re_b)
    expert = route[:, 0:MOE_TOPK].astype(I32)
    w = route[:, MOE_TOPK:2 * MOE_TOPK]

    a = t * MOE_TOPK
    e_flat = expert.reshape(a)
    onehot = (e_flat[:, None] == jnp.arange(MOE_EXPERTS, dtype=I32)[None, :]).astype(I32)
    running = jnp.cumsum(onehot, axis=0)
    counts = running[-1]
    rank = jnp.take_along_axis(running, e_flat[:, None], axis=1)[:, 0] - 1
    padded = (counts + MOE_ROWS - 1) // MOE_ROWS * MOE_ROWS
    pends = jnp.cumsum(padded)
    dest = (pends - padded)[e_flat] + rank
    p_rows = (a + MOE_EXPERTS * (MOE_ROWS - 1)) // MOE_ROWS * MOE_ROWS
    n_blk = p_rows // MOE_ROWS
    row_tok = (jnp.arange(p_rows, dtype=I32) % t).at[dest].set(jnp.arange(a, dtype=I32) // MOE_TOPK)
    blk_start = jnp.arange(n_blk, dtype=I32) * MOE_ROWS
    blk_e = jnp.minimum(jnp.sum((pends[None, :] <= blk_start[:, None]).astype(I32), axis=1), MOE_EXPERTS - 1)
    n_used = (pends[-1] // MOE_ROWS).astype(I32).reshape(1)

    xs = jnp.take(h, row_tok, axis=0)
    y_rows = expert_blocks(xs, blk_e, n_used, layer, w_gate, w_up, w_down)
    dest2 = dest.reshape(t, MOE_TOPK)
    y0 = jnp.take(y_rows, dest2[:, 0], axis=0)
    y1 = jnp.take(y_rows, dest2[:, 1], axis=0)
    return combine(x2d, y0, y1, w, out_norm_g, final_norm)


PERM_TILE = 512
PERM_REGIONS = ((0, 0, C_GQ // PERM_TILE),
                (_SRC_GQ - C_GQ, C_GQ // PERM_TILE, C_GG // PERM_TILE),
                (_SRC_GG - C_GG, C_GG // PERM_TILE, C_SMALL // PERM_TILE))


def _permute_kernel(src_ref, small_ref, o_ref):
    @pl.when(pl.program_id(1) < C_SMALL // PERM_TILE)
    def _():
        o_ref[...] = src_ref[0].T.astype(o_ref.dtype)

    @pl.when(pl.program_id(1) == C_SMALL // PERM_TILE)
    def _():
        o_ref[:, 0:LANES] = small_ref[...].astype(o_ref.dtype)


def permute_w_in(w_in, layer, tk=D_MODEL):
    assert C_GQ % PERM_TILE == 0 and C_GG % PERM_TILE == 0 and C_SMALL % PERM_TILE == 0
    assert all(shift % SUBLANES == 0 for shift, _, _ in PERM_REGIONS)
    d = w_in.shape[1]
    w_t = jnp.swapaxes(w_in, 1, 2)
    small = jnp.concatenate([w_in[layer, :, _SRC_GA:_SRC_GG], w_in[layer, :, _SRC_NG:_SRC_GQ],
                             jnp.zeros((d, LANES - GLA_RANK - 3 * NSA_HEADS), w_in.dtype)], axis=1)

    def src_row(j):
        shift = sum(jnp.where((j >= j0) & (j < j1), s, 0) for s, j0, j1 in PERM_REGIONS)
        return pl.multiple_of(jnp.minimum(j * PERM_TILE + shift, _SRC_END - PERM_TILE), SUBLANES)

    return pl.pallas_call(
        _permute_kernel,
        grid=(d // tk, pl.cdiv(PROJ_WIDTH, PERM_TILE)),
        in_specs=[pl.BlockSpec((pl.Element(1), pl.Element(PERM_TILE), pl.Element(tk)),
                               lambda k, j: (layer, src_row(j), k * tk)),
                  pl.BlockSpec((tk, LANES), lambda k, j: (k, 0))],
        out_specs=pl.BlockSpec((tk, PERM_TILE), lambda k, j: (k, j)),
        out_shape=jax.ShapeDtypeStruct((d, PROJ_WIDTH), BF16),
        compiler_params=_params("parallel", "parallel"),
        name="permute_w_in",
    )(w_t, small)


def hybrid_layer(x, norm1_g, w_in, nsa_cmp_pe, nsa_cmp_w1, nsa_cmp_w2, gla_wa, gla_ba, gla_norm_g,
                 w_br_moba, w_br_nsa, w_br_gla, w_out, norm2_g, router_group_w, router_group_b,
                 router_expert_w, router_expert_b, layer, expert_w_gate, expert_w_up, expert_w_down,
                 out_norm_g, final_norm):
    batch, seq, d = x.shape
    t = batch * seq
    x2d = x.reshape(t, d)
    proj2d = norm_matmul(x2d, norm1_g, permute_w_in(w_in, layer), BF16)
    proj = proj2d.reshape(batch, seq, PROJ_WIDTH)
    o_m = moba_attention(proj, batch, seq)
    kv_c, kv_ct = nsa_compress(proj, nsa_cmp_pe, nsa_cmp_w1, nsa_cmp_w2, batch, seq)
    o_n = nsa_attention(proj, kv_c, kv_ct, batch, seq)
    o_g = gla_attention(proj, gla_wa, gla_ba, gla_norm_g, batch, seq)
    merged = merge_branches(o_m.reshape(t, -1), o_n.reshape(t, -1), o_g.reshape(t, -1),
                            w_br_moba.astype(BF16), w_br_nsa.astype(BF16), w_br_gla.astype(BF16), proj2d)
    x2d = matmul_residual(merged, w_out.astype(BF16), x2d)
    x2d = hier_moe(x2d, norm2_g, router_group_w, router_group_b, router_expert_w, router_expert_b,
                   layer, expert_w_gate, expert_w_up, expert_w_down, out_norm_g, final_norm)
    return x2d.reshape(batch, seq, d)


def kernel(x, norm1_g, w_in, nsa_cmp_pe, nsa_cmp_w1, nsa_cmp_w2, gla_wa, gla_ba, gla_norm_g, w_br_moba,
           w_br_nsa, w_br_gla, w_out, norm2_g, router_group_w, router_group_b, router_expert_w,
           router_expert_b, expert_w_gate, expert_w_up, expert_w_down, final_norm_g):
    for l in range(DEPTH):
        x = hybrid_layer(x, norm1_g[l], w_in, nsa_cmp_pe[l], nsa_cmp_w1[l], nsa_cmp_w2[l], gla_wa[l],
                         gla_ba[l], gla_norm_g[l], w_br_moba[l], w_br_nsa[l], w_br_gla[l], w_out[l],
                         norm2_g[l], router_group_w[l], router_group_b[l], router_expert_w[l],
                         router_expert_b[l], l, expert_w_gate, expert_w_up, expert_w_down,
                         final_norm_g, l == DEPTH - 1)
    return x
```

```python
import functools

import jax
import jax.numpy as jnp
from jax import lax
from jax.experimental import pallas as pl
from jax.experimental.pallas import tpu as pltpu

F32 = jnp.float32
BF16 = jnp.bfloat16
I32 = jnp.int32

D_MODEL = 2048
DEPTH = 2
HEAD_DIM = 128
NEG = -1e30
FORCE = 1e9
EPS = 1e-6
LOG2E = 1.4426950408889634

MOBA_HEADS = 8
MOBA_BLOCK = 256
MOBA_TOPK = 3
MOBA_WIDTH = MOBA_HEADS * HEAD_DIM

NSA_HEADS = 8
NSA_KV_HEADS = 2
NSA_GROUP = NSA_HEADS // NSA_KV_HEADS
NSA_CMP_LEN = 32
NSA_CMP_STRIDE = 16
NSA_SLC_BLOCK = 64
NSA_TOPN = 8
NSA_WINDOW = 512
NSA_WIDTH = NSA_HEADS * HEAD_DIM
NSA_KV_WIDTH = NSA_KV_HEADS * HEAD_DIM

GLA_HEADS = 4
GLA_DK = 128
GLA_DV = 256
GLA_RANK = 16
GLA_TAU = 16.0
GLA_KEY_WIDTH = GLA_HEADS * GLA_DK
GLA_WIDTH = GLA_HEADS * GLA_DV

N_BRANCH = 3
MOE_GROUPS = 4
MOE_EXPERTS_PER_GROUP = 8
MOE_EXPERTS = MOE_GROUPS * MOE_EXPERTS_PER_GROUP
MOE_TOPK = 2
MOE_FF = D_MODEL // 4

LANES = 128
SUBLANES = 8
BF16_SUBLANES = 16
VMEM_LIMIT_BYTES = 56 * 1024 * 1024

_SRC_NG = MOBA_WIDTH * 3 + NSA_WIDTH + 6 * NSA_KV_WIDTH
_SRC_GQ = _SRC_NG + 3 * NSA_HEADS
_SRC_GA = _SRC_GQ + 2 * GLA_KEY_WIDTH + GLA_WIDTH
_SRC_GG = _SRC_GA + GLA_RANK
_SRC_END = _SRC_GG + GLA_WIDTH + N_BRANCH * D_MODEL

C_MQ = 0
C_MK = C_MQ + MOBA_WIDTH
C_MV = C_MK + MOBA_WIDTH
C_NQ = C_MV + MOBA_WIDTH
C_NKV = C_NQ + NSA_WIDTH
C_GQ = C_NKV + 6 * NSA_KV_WIDTH
C_GK = C_GQ + GLA_KEY_WIDTH
C_GV = C_GK + GLA_KEY_WIDTH
C_GG = C_GV + GLA_WIDTH
C_MG = C_GG + GLA_WIDTH
C_SMALL = C_MG + N_BRANCH * D_MODEL
PROJ_WIDTH = C_SMALL + LANES
SMALL_GA_LANE = 0
SMALL_NG_LANE = GLA_RANK

ROUTER_LANES = LANES
MOE_ROWS = 256


def _params(*semantics):
    return pltpu.CompilerParams(dimension_semantics=semantics, vmem_limit_bytes=VMEM_LIMIT_BYTES)


def _dot(a, b):
    return jnp.dot(a, b, preferred_element_type=F32)


def _dot_nt(a, b):
    return lax.dot_general(a, b, (((1,), (1,)), ((), ())), preferred_element_type=F32)


def _dot_split3(a01, x):
    x1 = x.astype(BF16)
    r1 = x - x1.astype(F32)
    x2 = r1.astype(BF16)
    x3 = (r1 - x2.astype(F32)).astype(BF16)
    return _dot(a01, x1) + _dot(a01, x2) + _dot(a01, x3)


def _rms(x, g):
    return x * lax.rsqrt(jnp.mean(x * x, axis=-1, keepdims=True) + EPS) * g


def _norm_matmul_kernel(x_ref, g_ref, b_ref, o_ref, h_ref):
    @pl.when(pl.program_id(1) == 0)
    def _():
        h_ref[...] = _rms(x_ref[...], g_ref[...]).astype(h_ref.dtype)

    o_ref[...] = _dot(h_ref[...], b_ref[...]).astype(o_ref.dtype)


def norm_matmul(x, g, b, out_dtype, tm=1024, tn=1152):
    m, k = x.shape
    n = b.shape[1]
    tm, tn = min(tm, m), min(tn, n)
    return pl.pallas_call(
        _norm_matmul_kernel,
        grid=(m // tm, n // tn),
        in_specs=[pl.BlockSpec((tm, k), lambda i, j: (i, 0)), pl.BlockSpec((1, k), lambda i, j: (0, 0)),
                  pl.BlockSpec((k, tn), lambda i, j: (0, j))],
        out_specs=pl.BlockSpec((tm, tn), lambda i, j: (i, j)),
        out_shape=jax.ShapeDtypeStruct((m, n), out_dtype),
        scratch_shapes=[pltpu.VMEM((tm, k), BF16)],
        compiler_params=_params("parallel", "arbitrary"),
        name="norm_matmul",
    )(x, g.reshape(1, k).astype(F32), b)


def _matmul_residual_kernel(a_ref, b_ref, r_ref, o_ref):
    o_ref[...] = r_ref[...] + _dot(a_ref[...], b_ref[...])


def matmul_residual(a, b, res, tm=1024, tn=1024):
    m, k = a.shape
    n = b.shape[1]
    tm, tn = min(tm, m), min(tn, n)
    return pl.pallas_call(
        _matmul_residual_kernel,
        grid=(m // tm, n // tn),
        in_specs=[pl.BlockSpec((tm, k), lambda i, j: (i, 0)), pl.BlockSpec((k, tn), lambda i, j: (0, j)),
                  pl.BlockSpec((tm, tn), lambda i, j: (i, j))],
        out_specs=pl.BlockSpec((tm, tn), lambda i, j: (i, j)),
        out_shape=jax.ShapeDtypeStruct((m, n), F32),
        compiler_params=_params("parallel", "parallel"),
        name="matmul_residual",
    )(a, b, res)


def _rank_desc_rows(vals, n_candidates):
    row = lax.broadcasted_iota(I32, vals.shape, 0)
    rank = jnp.zeros(vals.shape, F32)
    for m in range(n_candidates):
        vm = vals[m:m + 1, :]
        beats = (vm > vals) | ((vm == vals) & (row > m))
        rank = rank + beats.astype(F32)
    return rank


def _moba_kernel(q_ref, k_ref, vt_ref, o_ref, kmean_ref, bias_ref, s_ref, p_ref, acc_ref, bmax_ref, m_ref, *,
                 n_blocks, topk, heads):
    blk = MOBA_BLOCK
    cur = pl.program_id(2)
    scale = HEAD_DIM ** -0.5
    cols = [slice(h * HEAD_DIM, (h + 1) * HEAD_DIM) for h in range(heads)]

    @pl.when(cur == 0)
    def _():
        kmean_ref[...] = jnp.zeros_like(kmean_ref)
        for h in range(heads):
            for n in range(n_blocks):
                k_blk = k_ref[n * blk:(n + 1) * blk, cols[h]].astype(F32)
                kmean_ref[h, n:n + 1, :] = jnp.sum(k_blk, axis=0, keepdims=True) * (1.0 / blk)

    for h in range(heads):
        gate = _dot_nt(kmean_ref[h].astype(BF16), q_ref[:, cols[h]])
        row = lax.broadcasted_iota(I32, gate.shape, 0)
        valid = row < cur
        gate = jnp.where(valid, gate, NEG)
        chosen = valid & (_rank_desc_rows(gate, n_blocks) < topk)
        bias_ref[h] = jnp.where(chosen, 0.0, NEG)

    seq = k_ref.shape[0]
    lo_blocks = (n_blocks + 1) // 2
    lo = lo_blocks * blk
    need_hi = cur >= lo_blocks

    def score_blocks(b0, b1):
        for h in range(heads):
            s = _dot_nt(k_ref[b0 * blk:b1 * blk, cols[h]], q_ref[:, cols[h]])
            s_ref[h, b0 * blk:b1 * blk, :] = s
            for n in range(b0, b1):
                s_blk = s[(n - b0) * blk:(n - b0 + 1) * blk]
                bmax_ref[h, n] = jnp.max(s_blk.reshape(blk // SUBLANES, SUBLANES, blk), axis=0)

    def past_max(h, b0, b1):
        return functools.reduce(jnp.maximum, [bmax_ref[h, n] * scale + bias_ref[h, n:n + 1, :]
                                              for n in range(b0, b1)])

    score_blocks(0, lo_blocks)
    if lo < seq:
        pl.when(need_hi)(lambda: score_blocks(lo_blocks, n_blocks))

    sub = LANES
    base = pl.multiple_of(cur * blk, blk)
    qry_i = lax.broadcasted_iota(I32, (sub, blk), 1)
    key_i = [j * sub + lax.broadcasted_iota(I32, (sub, blk), 0) for j in range(blk // sub)]
    exp_scale = scale * LOG2E

    def fold_max(x):
        return jnp.max(x.reshape(sub // SUBLANES, SUBLANES, blk), axis=0)

    def tiles(h, start):
        return [s_ref[h, pl.ds(pl.multiple_of(start + j * sub, sub), sub), :] for j in range(blk // sub)]

    for h in range(heads):
        own = [fold_max(jnp.where(key_i[j] <= qry_i, t, NEG)) for j, t in enumerate(tiles(h, base))]
        m_ref[h] = jnp.maximum(functools.reduce(jnp.maximum, own) * scale, past_max(h, 0, lo_blocks))

    if lo < seq:
        @pl.when(need_hi)
        def _():
            for h in range(heads):
                m_ref[h] = jnp.maximum(m_ref[h], past_max(h, lo_blocks, n_blocks))

    m_log2 = [jnp.max(m_ref[h], axis=0, keepdims=True) * LOG2E for h in range(heads)]

    def fold_sum(x):
        return jnp.sum(x.reshape(sub // SUBLANES, SUBLANES, blk), axis=0)

    def store_p(h, start, j, p):
        p_ref[h, pl.ds(pl.multiple_of(start + j * sub, sub), sub), :] = p.astype(p_ref.dtype)

    def prob_body(n, l8):
        out = []
        for h in range(heads):
            shift = bias_ref[h, pl.ds(n, 1), :] * LOG2E - m_log2[h]
            acc = l8[h]
            for j, t in enumerate(tiles(h, n * blk)):
                p = jnp.exp2(t * exp_scale + shift)
                store_p(h, n * blk, j, p)
                acc = acc + fold_sum(p)
            out.append(acc)
        return tuple(out)

    l8 = []
    for h in range(heads):
        acc = jnp.zeros((SUBLANES, blk), F32)
        for j, t in enumerate(tiles(h, base)):
            p = jnp.where(key_i[j] <= qry_i, jnp.exp2(t * exp_scale - m_log2[h]), 0.0)
            store_p(h, base, j, p)
            acc = acc + fold_sum(p)
        l8.append(acc)
    l8 = lax.fori_loop(0, cur, prob_body, tuple(l8))

    def zero_body(n, carry):
        for h in range(heads):
            for j in range(blk // sub):
                store_p(h, n * blk, j, jnp.zeros((sub, blk), F32))
        return carry

    lax.fori_loop(cur + 1, jnp.where(need_hi, n_blocks, lo_blocks), zero_body, 0)
    for h in range(heads):
        acc_ref[h] = _dot(vt_ref[h, :, 0:lo], p_ref[h, 0:lo, :])

    if lo < seq:
        @pl.when(need_hi)
        def _():
            for h in range(heads):
                acc_ref[h] += _dot(vt_ref[h, :, lo:seq], p_ref[h, lo:seq, :])

    for h in range(heads):
        l_sum = jnp.sum(l8[h], axis=0, keepdims=True)
        o_ref[:, cols[h]] = (acc_ref[h] / l_sum).T.astype(o_ref.dtype)


def moba_attention(proj, batch, seq, heads=8):
    assert seq % MOBA_BLOCK == 0 and MOBA_HEADS % heads == 0
    n_blocks = seq // MOBA_BLOCK
    rows = -(-n_blocks // BF16_SUBLANES) * BF16_SUBLANES
    width = heads * HEAD_DIM
    qb, kb = C_MQ // width, C_MK // width
    v_t = proj[:, :, C_MV:C_MV + MOBA_WIDTH].reshape(batch, seq, MOBA_HEADS, HEAD_DIM).transpose(0, 2, 3, 1)
    kern = functools.partial(_moba_kernel, n_blocks=n_blocks, topk=min(MOBA_TOPK, n_blocks), heads=heads)
    return pl.pallas_call(
        kern,
        grid=(batch, MOBA_HEADS // heads, n_blocks),
        in_specs=[pl.BlockSpec((None, MOBA_BLOCK, width), lambda b, h, i: (b, i, qb + h)),
                  pl.BlockSpec((None, seq, width), lambda b, h, i: (b, 0, kb + h)),
                  pl.BlockSpec((None, heads, HEAD_DIM, seq), lambda b, h, i: (b, h, 0, 0))],
        out_specs=pl.BlockSpec((None, MOBA_BLOCK, width), lambda b, h, i: (b, i, h)),
        out_shape=jax.ShapeDtypeStruct((batch, seq, MOBA_WIDTH), BF16),
        scratch_shapes=[pltpu.VMEM((heads, rows, HEAD_DIM), F32), pltpu.VMEM((heads, rows, MOBA_BLOCK), F32),
                        pltpu.VMEM((heads, seq, MOBA_BLOCK), F32), pltpu.VMEM((heads, seq, MOBA_BLOCK), BF16),
                        pltpu.VMEM((heads, HEAD_DIM, MOBA_BLOCK), F32),
                        pltpu.VMEM((heads, n_blocks, SUBLANES, MOBA_BLOCK), F32),
                        pltpu.VMEM((heads, SUBLANES, MOBA_BLOCK), F32)],
        compiler_params=_params("parallel", "parallel", "arbitrary"),
        name="moba",
    )(proj, proj, v_t)


def _nsa_compress_kernel(x_ref, pe_ref, w1_ref, w2_ref, o_ref, ot_ref):
    x = x_ref[...].astype(F32)
    half = x.shape[1]
    lo = _dot((x + pe_ref[:, :half]).astype(BF16), w1_ref[:half, :])
    hi = _dot((x + pe_ref[:, half:]).astype(BF16), w1_ref[half:, :])
    pre = lo + pltpu.roll(hi, hi.shape[0] - 1, 0)
    hid = pre * jax.nn.sigmoid(pre)
    out = _dot(hid.astype(BF16), w2_ref[...])
    o_ref[...] = out.astype(o_ref.dtype)
    ot_ref[...] = out.T.astype(ot_ref.dtype)


def nsa_compress(proj, cmp_pe, cmp_w1, cmp_w2, batch, seq):
    n16 = seq // NSA_CMP_STRIDE
    width = NSA_CMP_STRIDE * HEAD_DIM
    x = proj[:, :, C_NKV:C_NKV + 2 * NSA_KV_WIDTH].reshape(batch, seq, 2, NSA_KV_HEADS, HEAD_DIM)
    x = x.transpose(0, 2, 3, 1, 4).reshape(batch, 2, NSA_KV_HEADS, n16, width)
    pe = cmp_pe.reshape(2, 1, NSA_CMP_LEN * HEAD_DIM).astype(F32)
    return pl.pallas_call(
        _nsa_compress_kernel,
        grid=(batch, 2, NSA_KV_HEADS),
        in_specs=[pl.BlockSpec((None, None, None, n16, width), lambda b, c, g: (b, c, g, 0, 0)),
                  pl.BlockSpec((None, 1, 2 * width), lambda b, c, g: (c, 0, 0)),
                  pl.BlockSpec((None, 2 * width, HEAD_DIM), lambda b, c, g: (c, 0, 0)),
                  pl.BlockSpec((None, HEAD_DIM, HEAD_DIM), lambda b, c, g: (c, 0, 0))],
        out_specs=[pl.BlockSpec((None, None, None, n16, HEAD_DIM), lambda b, c, g: (b, c, g, 0, 0)),
                   pl.BlockSpec((None, None, None, HEAD_DIM, n16), lambda b, c, g: (b, c, g, 0, 0))],
        out_shape=[jax.ShapeDtypeStruct((batch, 2, NSA_KV_HEADS, n16, HEAD_DIM), BF16),
                   jax.ShapeDtypeStruct((batch, 2, NSA_KV_HEADS, HEAD_DIM, n16), BF16)],
        compiler_params=_params("parallel", "parallel", "parallel"),
        name="nsa_compress",
    )(x, pe, cmp_w1.astype(BF16), cmp_w2.astype(BF16))


def _nsa_group(g, qi, q_ref, kc_ref, vct_ref, ks_ref, vst_ref, kw_ref, vwt_ref, ngt_ref, o_ref,
               bias_ref, s_ref, p_ref, sw_ref, pw_ref, out_ref, bmax_ref, *, tq, span, n_cmp, n_slc, topn,
               win_len):
    seq = ks_ref.shape[0]
    n16 = kc_ref.shape[0]
    lanes = NSA_GROUP * tq
    blk = NSA_SLC_BLOCK
    scale = HEAD_DIM ** -0.5
    exp_scale = scale * LOG2E
    start = qi * tq
    heads = [slice(r * tq, (r + 1) * tq) for r in range(NSA_GROUP)]
    q = [q_ref[:, r * HEAD_DIM:(r + 1) * HEAD_DIM] for r in range(NSA_GROUP)]

    def per_group(row):
        return jnp.concatenate([row] * NSA_GROUP, axis=1)

    def scores(keys):
        return jnp.concatenate([_dot_nt(keys, q[r]) for r in range(NSA_GROUP)], axis=1)

    def fold(op, x):
        return op(x.reshape(x.shape[0] // SUBLANES, SUBLANES, lanes), axis=0)

    pos1 = start + lax.broadcasted_iota(I32, (1, tq), 1)
    pos = per_group(pos1)

    n_idx = lax.broadcasted_iota(I32, (n16, lanes), 0)
    in_range = n_idx < n_cmp
    cmask = (n_idx * NSA_CMP_STRIDE + (NSA_CMP_LEN - 1) <= pos) & in_range
    s_c = jnp.where(cmask, scores(kc_ref[...]) * scale, NEG)
    e_c = jnp.where(in_range, jnp.exp(s_c - jnp.max(s_c, axis=0, keepdims=True)), 0.0)
    p_c = jnp.where(cmask, e_c / jnp.sum(e_c, axis=0, keepdims=True), 0.0)
    out_ref[0] = _dot(vct_ref[...], p_c.astype(BF16))

    p_sum = functools.reduce(lambda a, b: a + b, [p_c[:, h] for h in heads])
    rows = bias_ref.shape[0]
    oj = lax.broadcasted_iota(I32, (rows, n16), 0)
    on = lax.broadcasted_iota(I32, (rows, n16), 1)
    overlap_t = ((on * NSA_CMP_STRIDE < (oj + 1) * blk) & (on * NSA_CMP_STRIDE + (NSA_CMP_LEN - 1) >= oj * blk)
                 & (on < n_cmp) & (oj < n_slc)).astype(BF16)
    p_hi = p_sum.astype(BF16)
    p_lo = (p_sum - p_hi.astype(F32)).astype(BF16)
    imp = _dot(overlap_t, p_hi) + _dot(overlap_t, p_lo)
    j_idx = lax.broadcasted_iota(I32, (rows, tq), 0)
    cur_blk = pos1 // blk
    forced = (j_idx == 0) | (j_idx == cur_blk) | (j_idx == cur_blk - 1)
    imp = jnp.where(forced, FORCE, imp)
    imp = jnp.where(j_idx > cur_blk, NEG, imp)
    chosen = (_rank_desc_rows(imp, n_slc) < topn) & (j_idx <= cur_blk)
    bias_ref[...] = jnp.where(chosen, 0.0, NEG)

    n_spans = seq // span
    span_blocks = span // blk

    def span_scores(k):
        s = scores(ks_ref[k * span:(k + 1) * span, :])
        s_ref[k * span:(k + 1) * span, :] = s
        for j in range(span_blocks):
            bmax_ref[k * span_blocks + j] = fold(jnp.max, s[j * blk:(j + 1) * blk])

    span_scores(0)
    w0 = pl.multiple_of(jnp.maximum(start + tq - win_len, 0), tq)
    sw_ref[...] = scores(kw_ref[pl.ds(w0, win_len), :])
    for k in range(1, n_spans):
        pl.when(start >= k * span)(functools.partial(span_scores, k))

    per_tile = tq // blk
    first_own = qi * per_tile

    def block_bias(j):
        return per_group(bias_ref[pl.ds(j, 1), :])

    def block_rows(j):
        return pl.ds(pl.multiple_of(j * blk, blk), blk)

    def causal_mask(j):
        key = j * blk + lax.broadcasted_iota(I32, (blk, lanes), 0)
        return key <= pos

    def max_body(i, m8):
        for d in range(per_tile):
            j = i * per_tile + d
            m8 = jnp.maximum(m8, bmax_ref[j] * scale + block_bias(j))
        return m8

    m8 = lax.fori_loop(0, qi, max_body, jnp.full((SUBLANES, lanes), NEG, F32))
    for d in range(per_tile):
        j = first_own + d
        own = jnp.where(causal_mask(j), s_ref[block_rows(j), :], NEG)
        m8 = jnp.maximum(m8, fold(jnp.max, own) * scale + block_bias(j))
    m_log2 = jnp.max(m8, axis=0, keepdims=True) * LOG2E

    def prob_body(i, l8):
        for d in range(per_tile):
            j = i * per_tile + d
            p = jnp.exp2(s_ref[block_rows(j), :] * exp_scale + (block_bias(j) * LOG2E - m_log2))
            p_ref[block_rows(j), :] = p.astype(p_ref.dtype)
            l8 = l8 + fold(jnp.sum, p)
        return l8

    l8 = lax.fori_loop(0, qi, prob_body, jnp.zeros((SUBLANES, lanes), F32))
    for d in range(per_tile):
        j = first_own + d
        p = jnp.exp2(s_ref[block_rows(j), :] * exp_scale + (block_bias(j) * LOG2E - m_log2))
        p = jnp.where(causal_mask(j), p, 0.0)
        p_ref[block_rows(j), :] = p.astype(p_ref.dtype)
        l8 = l8 + fold(jnp.sum, p)

    def zero_body(i, carry):
        p_ref[pl.ds(pl.multiple_of(i * tq, tq), tq), :] = jnp.zeros((tq, lanes), p_ref.dtype)
        return carry

    visible_tiles = (start // span + 1) * (span // tq)
    lax.fori_loop(qi + 1, visible_tiles, zero_body, 0)

    out_ref[1] = _dot(vst_ref[:, 0:span], p_ref[0:span, :])

    n_tiles = win_len // blk

    def win_tile(t):
        key = w0 + t * blk + lax.broadcasted_iota(I32, (blk, lanes), 0)
        return sw_ref[t * blk:(t + 1) * blk, :], (key <= pos) & (key > pos - NSA_WINDOW)

    wm8 = jnp.full((SUBLANES, lanes), NEG, F32)
    for t in range(n_tiles):
        tile, mask = win_tile(t)
        wm8 = jnp.maximum(wm8, fold(jnp.max, jnp.where(mask, tile, NEG)))
    wm_log2 = jnp.max(wm8, axis=0, keepdims=True) * exp_scale
    wl8 = jnp.zeros((SUBLANES, lanes), F32)
    for t in range(n_tiles):
        tile, mask = win_tile(t)
        p = jnp.where(mask, jnp.exp2(tile * exp_scale - wm_log2), 0.0)
        pw_ref[t * blk:(t + 1) * blk, :] = p.astype(pw_ref.dtype)
        wl8 = wl8 + fold(jnp.sum, p)
    w_tile0 = w0 // tq
    pv = _dot(vwt_ref[w_tile0], pw_ref[0:tq, :])
    for c in range(1, win_len // tq):
        pv = pv + _dot(vwt_ref[w_tile0 + c], pw_ref[c * tq:(c + 1) * tq, :])
    out_ref[2] = pv / jnp.sum(wl8, axis=0, keepdims=True)

    for k in range(1, n_spans):
        def span_pv(k=k):
            out_ref[1] += _dot(vst_ref[:, k * span:(k + 1) * span], p_ref[k * span:(k + 1) * span, :])
        pl.when(start >= k * span)(span_pv)
    out_ref[1] = out_ref[1] / jnp.sum(l8, axis=0, keepdims=True)

    for r in range(NSA_GROUP):
        gate_row = (g * NSA_GROUP + r) * 3
        mix = jnp.zeros((HEAD_DIM, tq), F32)
        for c in range(3):
            mix = mix + jax.nn.sigmoid(ngt_ref[pl.ds(gate_row + c, 1), :]) * out_ref[c, :, heads[r]]
        head = g * NSA_GROUP + r
        o_ref[:, head * HEAD_DIM:(head + 1) * HEAD_DIM] = mix.T.astype(o_ref.dtype)


NSA_GROUP_OPERANDS = 7


def _nsa_kernel(*refs, **params):
    n_in = NSA_KV_HEADS * NSA_GROUP_OPERANDS
    ngt_ref, o_ref = refs[n_in:n_in + 2]
    for g in range(NSA_KV_HEADS):
        _nsa_group(g, pl.program_id(1), *refs[g * NSA_GROUP_OPERANDS:(g + 1) * NSA_GROUP_OPERANDS],
                   ngt_ref, o_ref, *refs[n_in + 2:], **params)


def nsa_attention(proj, kv_c, kv_ct, batch, seq, tq=128):
    assert tq == LANES and seq % tq == 0 and tq % NSA_SLC_BLOCK == 0
    n_cmp = (seq - NSA_CMP_LEN) // NSA_CMP_STRIDE + 1
    n_slc = seq // NSA_SLC_BLOCK
    rows = -(-n_slc // BF16_SUBLANES) * BF16_SUBLANES
    span = max(tq, seq // 4)
    win_len = min(NSA_WINDOW + tq, seq)
    n16 = seq // NSA_CMP_STRIDE
    gw = NSA_GROUP * HEAD_DIM
    lanes = NSA_GROUP * tq
    nkv = C_NKV // HEAD_DIM

    def kv_cols(slot):
        c0 = C_NKV + slot * NSA_KV_WIDTH
        return proj[:, :, c0:c0 + NSA_KV_WIDTH].reshape(batch, seq, NSA_KV_HEADS, HEAD_DIM)

    vs_t = kv_cols(3).transpose(0, 2, 3, 1)
    vw_t = kv_cols(5).reshape(batch, seq // tq, tq, NSA_KV_HEADS, HEAD_DIM).transpose(0, 3, 1, 4, 2)
    n_gates = 3 * NSA_HEADS
    ng_t = proj[:, :, C_SMALL + SMALL_NG_LANE:C_SMALL + SMALL_NG_LANE + n_gates].astype(F32).transpose(0, 2, 1)

    def group_specs(g):
        def k_spec(slot):
            return pl.BlockSpec((None, seq, HEAD_DIM), lambda b, i: (b, 0, nkv + slot * NSA_KV_HEADS + g))

        return [pl.BlockSpec((None, tq, gw), lambda b, i: (b, i, C_NQ // gw + g)),
                pl.BlockSpec((None, None, None, n16, HEAD_DIM), lambda b, i: (b, 0, g, 0, 0)),
                pl.BlockSpec((None, None, None, HEAD_DIM, n16), lambda b, i: (b, 1, g, 0, 0)),
                k_spec(2),
                pl.BlockSpec((None, None, HEAD_DIM, seq), lambda b, i: (b, g, 0, 0)),
                k_spec(4),
                pl.BlockSpec((None, None, seq // tq, HEAD_DIM, tq), lambda b, i: (b, g, 0, 0, 0))]

    kern = functools.partial(_nsa_kernel, tq=tq, span=span, n_cmp=n_cmp, n_slc=n_slc,
                             topn=min(NSA_TOPN, n_slc), win_len=win_len)
    return pl.pallas_call(
        kern,
        grid=(batch, seq // tq),
        in_specs=[spec for g in range(NSA_KV_HEADS) for spec in group_specs(g)]
                 + [pl.BlockSpec((None, n_gates, tq), lambda b, i: (b, 0, i))],
        out_specs=pl.BlockSpec((None, tq, NSA_WIDTH), lambda b, i: (b, i, 0)),
        out_shape=jax.ShapeDtypeStruct((batch, seq, NSA_WIDTH), BF16),
        scratch_shapes=[pltpu.VMEM((rows, tq), F32),
                        pltpu.VMEM((seq, lanes), F32), pltpu.VMEM((seq, lanes), BF16),
                        pltpu.VMEM((win_len, lanes), F32), pltpu.VMEM((win_len, lanes), BF16),
                        pltpu.VMEM((3, HEAD_DIM, lanes), F32), pltpu.VMEM((n_slc, SUBLANES, lanes), F32)],
        compiler_params=_params("parallel", "arbitrary"),
        name="nsa",
    )(*([proj, kv_c, kv_ct, proj, vs_t, proj, vw_t] * NSA_KV_HEADS), ng_t)


GLA_SUB = 8


def _gla_kernel(*refs, chunk, per_step):
    state_ref = refs[-2]

    @pl.when(pl.program_id(1) == 0)
    def _():
        state_ref[...] = jnp.zeros_like(state_ref)

    for c in range(per_step):
        _gla_chunk(slice(c * chunk, (c + 1) * chunk), *refs, chunk=chunk)


def _gla_chunk(rows, *refs, chunk):
    q_ref, k_ref = refs[0:2]
    v_refs = refs[2:2 + GLA_HEADS]
    gg_refs = refs[2 + GLA_HEADS:2 + 2 * GLA_HEADS]
    small_ref, wa_ref, ba_ref, ng_ref, o_ref, state_ref, attn_ref = refs[2 + 2 * GLA_HEADS:]
    hs = range(GLA_HEADS)
    keys = [slice(h * GLA_DK, (h + 1) * GLA_DK) for h in hs]

    q = q_ref[rows, :].astype(F32) * (GLA_DK ** -0.5)
    k = k_ref[rows, :].astype(F32)
    z = _dot(small_ref[rows, :], wa_ref[...]) + ba_ref[...]
    log_a = (jnp.minimum(z, 0.0) - jnp.log(1.0 + jnp.exp(-jnp.abs(z)))) * (1.0 / GLA_TAU)
    tri = (lax.broadcasted_iota(I32, (chunk, chunk), 1)
           <= lax.broadcasted_iota(I32, (chunk, chunk), 0)).astype(BF16)
    b = _dot_split3(tri, log_a)

    attn_ref[...] = jnp.zeros_like(attn_ref)
    t_idx = lax.broadcasted_iota(I32, (GLA_SUB, 1), 0)
    s_lane = lax.broadcasted_iota(I32, (GLA_SUB, GLA_SUB), 1)
    for i in range(chunk // GLA_SUB):
        r0 = i * GLA_SUB
        bi, qi, ki = b[r0:r0 + GLA_SUB], q[r0:r0 + GLA_SUB], k[r0:r0 + GLA_SUB]
        diag = [jnp.zeros((GLA_SUB, GLA_SUB), F32) for _ in hs]
        for s in range(GLA_SUB):
            decay = jnp.exp(jnp.where(t_idx >= s, bi - bi[s:s + 1], NEG))
            prod = qi * ki[s:s + 1] * decay
            for h in hs:
                col = jnp.sum(prod[:, keys[h]], axis=-1, keepdims=True)
                diag[h] = jnp.where(s_lane == s, col, diag[h])
        for h in hs:
            attn_ref[h, r0:r0 + GLA_SUB, r0:r0 + GLA_SUB] = diag[h]
        if i > 0:
            ref_b = b[r0:r0 + 1]
            q_dec = (qi * jnp.exp(bi - ref_b)).astype(BF16)
            k_dec = (k[:r0] * jnp.exp(ref_b - b[:r0])).astype(BF16)
            for h in hs:
                attn_ref[h, r0:r0 + GLA_SUB, 0:r0] = _dot_nt(q_dec[:, keys[h]], k_dec[:, keys[h]])

    q_in = (q * jnp.exp(b)).astype(BF16)
    b_last = b[chunk - 1:chunk]
    k_out = (k * jnp.exp(b_last - b)).astype(BF16)
    carry = jnp.exp(b_last)
    for h in hs:
        v = v_refs[h][rows, :]
        state_t = state_ref[h]
        o = _dot(attn_ref[h].astype(BF16), v) + _dot_nt(q_in[:, keys[h]], state_t.astype(BF16))
        state_ref[h] = state_t * carry[:, keys[h]] + _dot(v.astype(F32).T.astype(BF16), k_out[:, keys[h]])
        gate = gg_refs[h][rows, :].astype(F32)
        out = _rms(o, ng_ref[...]) * (gate * jax.nn.sigmoid(gate))
        o_ref[rows, h * GLA_DV:(h + 1) * GLA_DV] = out.astype(o_ref.dtype)


def gla_attention(proj, gla_wa, gla_ba, gla_norm_g, batch, seq, chunk=128, per_step=2):
    chunk = min(chunk, seq)
    per_step = min(per_step, seq // chunk)
    rows = chunk * per_step
    assert seq % rows == 0 and chunk % GLA_SUB == 0
    wa = jnp.zeros((LANES, GLA_KEY_WIDTH), BF16).at[SMALL_GA_LANE:SMALL_GA_LANE + GLA_RANK].set(gla_wa.astype(BF16))

    def head_spec(c0, h):
        return pl.BlockSpec((None, rows, GLA_DV), lambda b, c, h=h: (b, c, c0 // GLA_DV + h))

    const = lambda b, c: (0, 0)
    return pl.pallas_call(
        functools.partial(_gla_kernel, chunk=chunk, per_step=per_step),
        grid=(batch, seq // rows),
        in_specs=[pl.BlockSpec((None, rows, GLA_KEY_WIDTH), lambda b, c: (b, c, C_GQ // GLA_KEY_WIDTH)),
                  pl.BlockSpec((None, rows, GLA_KEY_WIDTH), lambda b, c: (b, c, C_GK // GLA_KEY_WIDTH)),
                  *[head_spec(C_GV, h) for h in range(GLA_HEADS)],
                  *[head_spec(C_GG, h) for h in range(GLA_HEADS)],
                  pl.BlockSpec((None, rows, LANES), lambda b, c: (b, c, C_SMALL // LANES)),
                  pl.BlockSpec((LANES, GLA_KEY_WIDTH), const),
                  pl.BlockSpec((1, GLA_KEY_WIDTH), const),
                  pl.BlockSpec((1, GLA_DV), const)],
        out_specs=pl.BlockSpec((None, rows, GLA_WIDTH), lambda b, c: (b, c, 0)),
        out_shape=jax.ShapeDtypeStruct((batch, seq, GLA_WIDTH), BF16),
        scratch_shapes=[pltpu.VMEM((GLA_HEADS, GLA_DV, GLA_DK), F32), pltpu.VMEM((GLA_HEADS, chunk, chunk), F32)],
        compiler_params=_params("parallel", "arbitrary"),
        name="gla",
    )(proj, proj, *([proj] * (2 * GLA_HEADS)), proj, wa, gla_ba.reshape(1, GLA_KEY_WIDTH).astype(F32),
      gla_norm_g.reshape(1, GLA_DV).astype(F32))


def _merge_kernel(om_ref, on_ref, og_ref, wm_ref, wn_ref, wg_ref, gm_ref, gn_ref, gl_ref, o_ref):
    def gated(gate_ref, a_ref, w_ref):
        return jax.nn.sigmoid(gate_ref[...].astype(F32)) * _dot(a_ref[...], w_ref[...])

    o_ref[...] = (gated(gm_ref, om_ref, wm_ref) + gated(gn_ref, on_ref, wn_ref)
                  + gated(gl_ref, og_ref, wg_ref)).astype(o_ref.dtype)


def merge_branches(o_m, o_n, o_g, w_m, w_n, w_g, proj2d, tm=1024, tn=512):
    m = o_m.shape[0]
    tm = min(tm, m)
    assert C_MG % tn == 0 and D_MODEL % tn == 0

    def gate_spec(c):
        return pl.BlockSpec((tm, tn), lambda i, j, c=c: (i, (C_MG + c * D_MODEL) // tn + j))

    def act_spec(width):
        return pl.BlockSpec((tm, width), lambda i, j: (i, 0))

    def w_spec(width):
        return pl.BlockSpec((width, tn), lambda i, j: (0, j))

    return pl.pallas_call(
        _merge_kernel,
        grid=(m // tm, D_MODEL // tn),
        in_specs=[act_spec(MOBA_WIDTH), act_spec(NSA_WIDTH), act_spec(GLA_WIDTH),
                  w_spec(MOBA_WIDTH), w_spec(NSA_WIDTH), w_spec(GLA_WIDTH),
                  gate_spec(0), gate_spec(1), gate_spec(2)],
        out_specs=pl.BlockSpec((tm, tn), lambda i, j: (i, j)),
        out_shape=jax.ShapeDtypeStruct((m, D_MODEL), BF16),
        compiler_params=_params("parallel", "parallel"),
        name="merge",
    )(o_m, o_n, o_g, w_m, w_n, w_g, proj2d, proj2d, proj2d)


def _first_max(vals, lane):
    top = jnp.max(vals, axis=-1, keepdims=True)
    idx = jnp.min(jnp.where(vals == top, lane, float(ROUTER_LANES)), axis=-1, keepdims=True)
    return top, idx


def _router_kernel(x_ref, g_ref, w_ref, b_ref, h_ref, route_ref):
    h = _rms(x_ref[...], g_ref[...])
    h_ref[...] = h.astype(h_ref.dtype)
    h1 = h.astype(BF16)
    r1 = h - h1.astype(F32)
    h2 = r1.astype(BF16)
    h3 = (r1 - h2.astype(F32)).astype(BF16)
    hi, lo = slice(0, ROUTER_LANES), slice(ROUTER_LANES, 2 * ROUTER_LANES)
    a = _dot(h1, w_ref[...])
    b = _dot(h2, w_ref[...])
    logits = (a[:, hi] + (a[:, lo] + b[:, hi]) + (b[:, lo] + _dot(h3, w_ref[:, hi]))) + b_ref[...]

    lane = lax.broadcasted_iota(I32, logits.shape, 1).astype(F32)
    g_logits = jnp.where(lane < MOE_GROUPS, logits, -jnp.inf)
    g_top, grp = _first_max(g_logits, lane)
    p_grp = 1.0 / jnp.sum(jnp.exp(g_logits - g_top), axis=-1, keepdims=True)
    first = MOE_GROUPS + grp * MOE_EXPERTS_PER_GROUP
    e_logits = jnp.where((lane >= first) & (lane < first + MOE_EXPERTS_PER_GROUP), logits, -jnp.inf)
    top1, lane1 = _first_max(e_logits, lane)
    top2, lane2 = _first_max(jnp.where(lane == lane1, -jnp.inf, e_logits), lane)
    ratio = jnp.exp(top2 - top1)
    w_first = p_grp / (1.0 + ratio)
    route = jnp.where(lane == 0, lane1 - MOE_GROUPS, jnp.where(lane == 1, lane2 - MOE_GROUPS,
                      jnp.where(lane == 2, w_first, jnp.where(lane == 3, w_first * ratio, 0.0))))
    route_ref[...] = route


def router(x2d, norm_g, rg_w, rg_b, re_w, re_b, tm=512):
    m, d = x2d.shape
    tm = min(tm, m)
    n_real = MOE_GROUPS + MOE_EXPERTS
    w = jnp.zeros((d, ROUTER_LANES), F32).at[:, :n_real].set(jnp.concatenate([rg_w, re_w], axis=1))
    w_hi = w.astype(BF16)
    w_lo = (w - w_hi.astype(F32)).astype(BF16)
    bias = jnp.zeros((1, ROUTER_LANES), F32).at[0, :n_real].set(jnp.concatenate([rg_b, re_b]))
    return pl.pallas_call(
        _router_kernel,
        grid=(m // tm,),
        in_specs=[pl.BlockSpec((tm, d), lambda i: (i, 0)), pl.BlockSpec((1, d), lambda i: (0, 0)),
                  pl.BlockSpec((d, 2 * ROUTER_LANES), lambda i: (0, 0)),
                  pl.BlockSpec((1, ROUTER_LANES), lambda i: (0, 0))],
        out_specs=[pl.BlockSpec((tm, d), lambda i: (i, 0)), pl.BlockSpec((tm, ROUTER_LANES), lambda i: (i, 0))],
        out_shape=[jax.ShapeDtypeStruct((m, d), BF16), jax.ShapeDtypeStruct((m, ROUTER_LANES), F32)],
        compiler_params=_params("parallel"),
        name="router",
    )(x2d, norm_g.reshape(1, d).astype(F32), jnp.concatenate([w_hi, w_lo], axis=1), bias)


def _expert_kernel(blk_e_ref, n_used_ref, x_ref, wg_ref, wu_ref, wd_ref, o_ref, wg_bf, wu_bf, wd_bf):
    i = pl.program_id(0)
    new_expert = (i == 0) | (blk_e_ref[i] != blk_e_ref[jnp.maximum(i - 1, 0)])

    @pl.when(new_expert)
    def _():
        wg_bf[...] = wg_ref[...].astype(BF16)
        wu_bf[...] = wu_ref[...].astype(BF16)
        wd_bf[...] = wd_ref[...].astype(BF16)

    @pl.when(i < n_used_ref[0])
    def _():
        x = x_ref[...]
        gate = _dot(x, wg_bf[...])
        hid = gate * jax.nn.sigmoid(gate) * _dot(x, wu_bf[...])
        o_ref[...] = _dot(hid.astype(BF16), wd_bf[...]).astype(o_ref.dtype)

    @pl.when(i >= n_used_ref[0])
    def _():
        o_ref[...] = jnp.zeros_like(o_ref)


def expert_blocks(xs, blk_e, n_used, layer, w_gate, w_up, w_down):
    p, d = xs.shape
    ff = w_gate.shape[3]
    n_blk = p // MOE_ROWS
    grid_spec = pltpu.PrefetchScalarGridSpec(
        num_scalar_prefetch=2,
        grid=(n_blk,),
        in_specs=[pl.BlockSpec((MOE_ROWS, d), lambda i, e, n: (i, 0)),
                  pl.BlockSpec((None, None, d, ff), lambda i, e, n: (layer, e[i], 0, 0)),
                  pl.BlockSpec((None, None, d, ff), lambda i, e, n: (layer, e[i], 0, 0)),
                  pl.BlockSpec((None, None, ff, d), lambda i, e, n: (layer, e[i], 0, 0))],
        out_specs=pl.BlockSpec((MOE_ROWS, d), lambda i, e, n: (i, 0)),
        scratch_shapes=[pltpu.VMEM((d, ff), BF16), pltpu.VMEM((d, ff), BF16), pltpu.VMEM((ff, d), BF16)],
    )
    return pl.pallas_call(
        _expert_kernel,
        grid_spec=grid_spec,
        out_shape=jax.ShapeDtypeStruct((p, d), BF16),
        compiler_params=_params("arbitrary"),
        name="experts",
    )(blk_e, n_used, xs, w_gate, w_up, w_down)


def _combine_kernel(x_ref, y0_ref, y1_ref, w_ref, g_ref, o_ref, *, final_norm):
    w = w_ref[...]
    x = x_ref[...] + (w[:, 0:1] * y0_ref[...].astype(F32) + w[:, 1:2] * y1_ref[...].astype(F32))
    o_ref[...] = _rms(x, g_ref[...]) if final_norm else x


def combine(x2d, y0, y1, w, norm_g, final_norm, tm=512):
    m, d = x2d.shape
    tm = min(tm, m)
    row = pl.BlockSpec((tm, d), lambda i: (i, 0))
    return pl.pallas_call(
        functools.partial(_combine_kernel, final_norm=final_norm),
        grid=(m // tm,),
        in_specs=[row, row, row, pl.BlockSpec((tm, MOE_TOPK), lambda i: (i, 0)),
                  pl.BlockSpec((1, d), lambda i: (0, 0))],
        out_specs=row,
        out_shape=jax.ShapeDtypeStruct((m, d), F32),
        compiler_params=_params("parallel"),
        name="combine",
    )(x2d, y0, y1, w, norm_g.reshape(1, d).astype(F32))


def hier_moe(x2d, norm_g, rg_w, rg_b, re_w, re_b, layer, w_gate, w_up, w_down, out_norm_g, final_norm):
    t = x2d.shape[0]
    h, route = router(x2d, norm_g, rg_w, rg_b, re_w, re_b)
    expert = route[:, 0:MOE_TOPK].astype(I32)
    w = route[:, MOE_TOPK:2 * MOE_TOPK]

    a = t * MOE_TOPK
    e_flat = expert.reshape(a)
    onehot = (e_flat[:, None] == jnp.arange(MOE_EXPERTS, dtype=I32)[None, :]).astype(I32)
    running = jnp.cumsum(onehot, axis=0)
    counts = running[-1]
    rank = jnp.take_along_axis(running, e_flat[:, None], axis=1)[:, 0] - 1
    padded = (counts + MOE_ROWS - 1) // MOE_ROWS * MOE_ROWS
    pends = jnp.cumsum(padded)
    dest = (pends - padded)[e_flat] + rank
    p_rows = (a + MOE_EXPERTS * (MOE_ROWS - 1)) // MOE_ROWS * MOE_ROWS
    n_blk = p_rows // MOE_ROWS
    row_tok = (jnp.arange(p_rows, dtype=I32) % t).at[dest].set(jnp.arange(a, dtype=I32) // MOE_TOPK)
    blk_start = jnp.arange(n_blk, dtype=I32) * MOE_ROWS
    blk_e = jnp.minimum(jnp.sum((pends[None, :] <= blk_start[:, None]).astype(I32), axis=1), MOE_EXPERTS - 1)
    n_used = (pends[-1] // MOE_ROWS).astype(I32).reshape(1)

    xs = jnp.take(h, row_tok, axis=0)
    y_rows = expert_blocks(xs, blk_e, n_used, layer, w_gate, w_up, w_down)
    dest2 = dest.reshape(t, MOE_TOPK)
    y0 = jnp.take(y_rows, dest2[:, 0], axis=0)
    y1 = jnp.take(y_rows, dest2[:, 1], axis=0)
    return combine(x2d, y0, y1, w, out_norm_g, final_norm)


PERM_TILE = 512
PERM_REGIONS = ((0, 0, C_GQ // PERM_TILE),
                (_SRC_GQ - C_GQ, C_GQ // PERM_TILE, C_GG // PERM_TILE),
                (_SRC_GG - C_GG, C_GG // PERM_TILE, C_SMALL // PERM_TILE))


def _permute_kernel(src_ref, small_ref, o_ref):
    @pl.when(pl.program_id(1) < C_SMALL // PERM_TILE)
    def _():
        o_ref[...] = src_ref[0].T.astype(o_ref.dtype)

    @pl.when(pl.program_id(1) == C_SMALL // PERM_TILE)
    def _():
        o_ref[:, 0:LANES] = small_ref[...].astype(o_ref.dtype)


def permute_w_in(w_in, layer, tk=D_MODEL):
    assert C_GQ % PERM_TILE == 0 and C_GG % PERM_TILE == 0 and C_SMALL % PERM_TILE == 0
    assert all(shift % SUBLANES == 0 for shift, _, _ in PERM_REGIONS)
    d = w_in.shape[1]
    w_t = jnp.swapaxes(w_in, 1, 2)
    small = jnp.concatenate([w_in[layer, :, _SRC_GA:_SRC_GG], w_in[layer, :, _SRC_NG:_SRC_GQ],
                             jnp.zeros((d, LANES - GLA_RANK - 3 * NSA_HEADS), w_in.dtype)], axis=1)

    def src_row(j):
        shift = sum(jnp.where((j >= j0) & (j < j1), s, 0) for s, j0, j1 in PERM_REGIONS)
        return pl.multiple_of(jnp.minimum(j * PERM_TILE + shift, _SRC_END - PERM_TILE), SUBLANES)

    return pl.pallas_call(
        _permute_kernel,
        grid=(d // tk, pl.cdiv(PROJ_WIDTH, PERM_TILE)),
        in_specs=[pl.BlockSpec((pl.Element(1), pl.Element(PERM_TILE), pl.Element(tk)),
                               lambda k, j: (layer, src_row(j), k * tk)),
                  pl.BlockSpec((tk, LANES), lambda k, j: (k, 0))],
        out_specs=pl.BlockSpec((tk, PERM_TILE), lambda k, j: (k, j)),
        out_shape=jax.ShapeDtypeStruct((d, PROJ_WIDTH), BF16),
        compiler_params=_params("parallel", "parallel"),
        name="permute_w_in",
    )(w_t, small)


def hybrid_layer(x, norm1_g, w_in, nsa_cmp_pe, nsa_cmp_w1, nsa_cmp_w2, gla_wa, gla_ba, gla_norm_g,
                 w_br_moba, w_br_nsa, w_br_gla, w_out, norm2_g, router_group_w, router_group_b,
                 router_expert_w, router_expert_b, layer, expert_w_gate, expert_w_up, expert_w_down,
                 out_norm_g, final_norm):
    batch, seq, d = x.shape
    t = batch * seq
    x2d = x.reshape(t, d)
    proj2d = norm_matmul(x2d, norm1_g, permute_w_in(w_in, layer), BF16)
    proj = proj2d.reshape(batch, seq, PROJ_WIDTH)
    o_m = moba_attention(proj, batch, seq)
    kv_c, kv_ct = nsa_compress(proj, nsa_cmp_pe, nsa_cmp_w1, nsa_cmp_w2, batch, seq)
    o_n = nsa_attention(proj, kv_c, kv_ct, batch, seq)
    o_g = gla_attention(proj, gla_wa, gla_ba, gla_norm_g, batch, seq)
    merged = merge_branches(o_m.reshape(t, -1), o_n.reshape(t, -1), o_g.reshape(t, -1),
                            w_br_moba.astype(BF16), w_br_nsa.astype(BF16), w_br_gla.astype(BF16), proj2d)
    x2d = matmul_residual(merged, w_out.astype(BF16), x2d)
    x2d = hier_moe(x2d, norm2_g, router_group_w, router_group_b, router_expert_w, router_expert_b,
                   layer, expert_w_gate, expert_w_up, expert_w_down, out_norm_g, final_norm)
    return x2d.reshape(batch, seq, d)


def kernel(x, norm1_g, w_in, nsa_cmp_pe, nsa_cmp_w1, nsa_cmp_w2, gla_wa, gla_ba, gla_norm_g, w_br_moba,
           w_br_nsa, w_br_gla, w_out, norm2_g, router_group_w, router_group_b, router_expert_w,
           router_expert_b, expert_w_gate, expert_w_up, expert_w_down, final_norm_g):
    for l in range(DEPTH):
        x = hybrid_layer(x, norm1_g[l], w_in, nsa_cmp_pe[l], nsa_cmp_w1[l], nsa_cmp_w2[l], gla_wa[l],
                         gla_ba[l], gla_norm_g[l], w_br_moba[l], w_br_nsa[l], w_br_gla[l], w_out[l],
                         norm2_g[l], router_group_w[l], router_group_b[l], router_expert_w[l],
                         router_expert_b[l], l, expert_w_gate, expert_w_up, expert_w_down,
                         final_norm_g, l == DEPTH - 1)
    return x
```
